```python
import jax, jax.numpy as jnp
from jax import lax
import numpy as np

D_MODEL = 2048
BATCH = 8
SEQ = 8192
DEPTH = 1

N_HEADS = 16
N_KV_HEADS = 4
HEAD_DIM = D_MODEL // N_HEADS
GROUP = N_HEADS // N_KV_HEADS
Q_BLOCK = 128
ROPE_THETA = 10000.0
AXIS_DIM = HEAD_DIM // 2
GRID_W = 64
CONV_WIDTH = D_MODEL // 2
CONV_KERNEL = 31
D_FF = 4 * D_MODEL
PLE_DIM = 256
N_BRANCHES = 2
EPS = 1e-6

Q_W = N_HEADS * HEAD_DIM
KV_W = N_KV_HEADS * HEAD_DIM
IN_W = 2 * CONV_WIDTH + Q_W + 2 * KV_W + N_BRANCHES * D_MODEL

kernel_name = "hybrid_conformer_gqa_axial_gated_encoder_block"


def rms_norm(x, g):
    xf = x.astype(jnp.float32)
    y = xf * lax.rsqrt(jnp.mean(xf * xf, axis=-1, keepdims=True) + EPS)
    return (y * g.astype(jnp.float32)).astype(x.dtype)


def layer_norm(x, g, b):
    xf = x.astype(jnp.float32)
    mu = jnp.mean(xf, axis=-1, keepdims=True)
    xc = xf - mu
    y = xc * lax.rsqrt(jnp.mean(xc * xc, axis=-1, keepdims=True) + EPS)
    return (y * g.astype(jnp.float32) + b.astype(jnp.float32)).astype(x.dtype)


def rope_half(x, ang):
    n = ang.shape[-1]
    cos = jnp.cos(ang)[None, :, None, :].astype(x.dtype)
    sin = jnp.sin(ang)[None, :, None, :].astype(x.dtype)
    x1, x2 = x[..., :n], x[..., n:]
    return jnp.concatenate([x1 * cos - x2 * sin, x2 * cos + x1 * sin], axis=-1)


def axial_rope(x, ang_row, ang_col):
    return jnp.concatenate([rope_half(x[..., :AXIS_DIM], ang_row),
                            rope_half(x[..., AXIS_DIM:], ang_col)], axis=-1)


def conformer_branch(u_a, u_b, w_dw, ln_g, ln_b, w_proj):
    u = u_a * jax.nn.sigmoid(u_b)
    u = lax.conv_general_dilated(
        u, w_dw[:, None, :].astype(u.dtype), window_strides=(1,),
        padding=[(CONV_KERNEL // 2, CONV_KERNEL // 2)],
        dimension_numbers=("NWC", "WIO", "NWC"),
        feature_group_count=CONV_WIDTH)
    u = jax.nn.silu(layer_norm(u, ln_g, ln_b))
    return u @ w_proj


def attention_branch(q, k, v, q_g, k_g, w_proj):
    B, S, _ = q.shape
    rows = S // GRID_W
    row = jnp.repeat(jnp.arange(rows, dtype=jnp.int32), GRID_W)
    col = jnp.tile(jnp.arange(GRID_W, dtype=jnp.int32), rows)
    inv_freq = ROPE_THETA ** (-jnp.arange(0, AXIS_DIM, 2, dtype=jnp.float32) / AXIS_DIM)
    ang_row = row.astype(jnp.float32)[:, None] * inv_freq[None, :]
    ang_col = col.astype(jnp.float32)[:, None] * inv_freq[None, :]

    q = q.reshape(B, S, N_HEADS, HEAD_DIM)
    k = k.reshape(B, S, N_KV_HEADS, HEAD_DIM)
    v = v.reshape(B, S, N_KV_HEADS, HEAD_DIM)
    q = axial_rope(rms_norm(q, q_g), ang_row, ang_col)
    k = axial_rope(rms_norm(k, k_g), ang_row, ang_col)

    n_blk = S // Q_BLOCK
    qb = q.reshape(B, n_blk, Q_BLOCK, N_KV_HEADS, GROUP, HEAD_DIM)
    qb = qb.transpose(1, 0, 3, 4, 2, 5)
    kt = k.transpose(0, 2, 1, 3)
    vt = v.transpose(0, 2, 1, 3)
    scale = HEAD_DIM ** -0.5

    def one_block(qblk):
        s = jnp.einsum("bkgqd,bksd->bkgqs", qblk, kt).astype(jnp.float32) * scale
        pr = jax.nn.softmax(s, axis=-1).astype(vt.dtype)
        return jnp.einsum("bkgqs,bksd->bkgqd", pr, vt)

    o = lax.map(one_block, qb)
    o = o.transpose(1, 0, 4, 2, 3, 5).reshape(B, S, Q_W)
    return o @ w_proj


def _fwd_setup_inputs(seed: int = 0) -> dict:
    key = jax.random.key(seed)
    ks = jax.random.split(key, 24)
    f32 = jnp.float32

    def nrm(k, shape, scale):
        return jax.random.normal(k, shape, f32) * scale

    def gain(k, shape):
        return 1.0 + 0.02 * jax.random.normal(k, shape, f32)

    return {
        "x": nrm(ks[0], (BATCH, SEQ, D_MODEL), 1.0),
        "p": nrm(ks[1], (DEPTH, BATCH, SEQ, PLE_DIM), 1.0),
        "norm_mix": gain(ks[2], (DEPTH, D_MODEL)),
        "w_in": nrm(ks[3], (DEPTH, D_MODEL, IN_W), D_MODEL ** -0.5),
        "w_dw": nrm(ks[4], (DEPTH, CONV_KERNEL, CONV_WIDTH), CONV_KERNEL ** -0.5),
        "conv_ln_g": gain(ks[5], (DEPTH, CONV_WIDTH)),
        "conv_ln_b": nrm(ks[6], (DEPTH, CONV_WIDTH), 0.02),
        "w_conv_proj": nrm(ks[7], (DEPTH, CONV_WIDTH, D_MODEL), CONV_WIDTH ** -0.5),
        "q_norm": gain(ks[8], (DEPTH, HEAD_DIM)),
        "k_norm": gain(ks[9], (DEPTH, HEAD_DIM)),
        "w_attn_proj": nrm(ks[10], (DEPTH, Q_W, D_MODEL), Q_W ** -0.5),
        "w_out": nrm(ks[11], (DEPTH, D_MODEL, D_MODEL), D_MODEL ** -0.5),
        "norm_ffn": gain(ks[12], (DEPTH, D_MODEL)),
        "w_ff1": nrm(ks[13], (DEPTH, D_MODEL, D_FF), D_MODEL ** -0.5),
        "w_ff2": nrm(ks[14], (DEPTH, D_FF, D_MODEL), D_FF ** -0.5),
        "norm_ple": gain(ks[15], (DEPTH, D_MODEL)),
        "w_ple_gate": nrm(ks[16], (DEPTH, D_MODEL, D_MODEL), D_MODEL ** -0.5),
        "w_ple_proj": nrm(ks[17], (DEPTH, PLE_DIM, D_MODEL), PLE_DIM ** -0.5),
        "norm_final": gain(ks[18], (D_MODEL,)),
    }


def _fwd_reference(x, p, norm_mix, w_in, w_dw, conv_ln_g, conv_ln_b, w_conv_proj,
              q_norm, k_norm, w_attn_proj, w_out, norm_ffn, w_ff1, w_ff2,
              norm_ple, w_ple_gate, w_ple_proj, norm_final):
    split_at = list(np.cumsum([CONV_WIDTH, CONV_WIDTH, Q_W, KV_W, KV_W, D_MODEL]))
    for i in range(DEPTH):
        h = rms_norm(x, norm_mix[i])
        z = h @ w_in[i]
        c_a, c_b, q, k, v, g_c, g_a = jnp.split(z, split_at, axis=-1)
        y_c = conformer_branch(c_a, c_b, w_dw[i], conv_ln_g[i], conv_ln_b[i], w_conv_proj[i])
        y_a = attention_branch(q, k, v, q_norm[i], k_norm[i], w_attn_proj[i])
        merged = jax.nn.sigmoid(g_c) * y_c + jax.nn.sigmoid(g_a) * y_a
        x = x + merged @ w_out[i]
        h = rms_norm(x, norm_ffn[i])
        x = x + jnp.square(jax.nn.relu(h @ w_ff1[i])) @ w_ff2[i]
        gate = jax.nn.sigmoid(rms_norm(x, norm_ple[i]) @ w_ple_gate[i])
        x = x + gate * (p[i] @ w_ple_proj[i])
    return rms_norm(x, norm_final)


import jax as _jax
import jax.numpy as _jnp

TWIN_FORMAT = 'train_step'
FWD_PARAMS = ['x', 'p', 'norm_mix', 'w_in', 'w_dw', 'conv_ln_g', 'conv_ln_b', 'w_conv_proj', 'q_norm', 'k_norm', 'w_attn_proj', 'w_out', 'norm_ffn', 'w_ff1', 'w_ff2', 'norm_ple', 'w_ple_gate', 'w_ple_proj', 'norm_final']
TWIN_WEIGHTS = ['norm_mix', 'w_in', 'w_dw', 'conv_ln_g', 'conv_ln_b', 'w_conv_proj', 'q_norm', 'k_norm', 'w_attn_proj', 'w_out', 'norm_ffn', 'w_ff1', 'w_ff2', 'norm_ple', 'w_ple_gate', 'w_ple_proj', 'norm_final']
TWIN_DIFF_INPUT = 'x'
TWIN_INPUTS = ['x', 'p', 'norm_mix', 'w_in', 'w_dw', 'conv_ln_g', 'conv_ln_b', 'w_conv_proj', 'q_norm', 'k_norm', 'w_attn_proj', 'w_out', 'norm_ffn', 'w_ff1', 'w_ff2', 'norm_ple', 'w_ple_gate', 'w_ple_proj', 'norm_final', 'loss_target', 'm_norm_mix', 'm_w_in', 'm_w_dw', 'm_conv_ln_g', 'm_conv_ln_b', 'm_w_conv_proj', 'm_q_norm', 'm_k_norm', 'm_w_attn_proj', 'm_w_out', 'm_norm_ffn', 'm_w_ff1', 'm_w_ff2', 'm_norm_ple', 'm_w_ple_gate', 'm_w_ple_proj', 'm_norm_final', 'v_norm_mix', 'v_w_in', 'v_w_dw', 'v_conv_ln_g', 'v_conv_ln_b', 'v_w_conv_proj', 'v_q_norm', 'v_k_norm', 'v_w_attn_proj', 'v_w_out', 'v_norm_ffn', 'v_w_ff1', 'v_w_ff2', 'v_norm_ple', 'v_w_ple_gate', 'v_w_ple_proj', 'v_norm_final']
TWIN_OUTPUTS = ['loss', 'grad_x', 'grad_norm_mix', 'grad_w_in', 'grad_w_dw', 'grad_conv_ln_g', 'grad_conv_ln_b', 'grad_w_conv_proj', 'grad_q_norm', 'grad_k_norm', 'grad_w_attn_proj', 'grad_w_out', 'grad_norm_ffn', 'grad_w_ff1', 'grad_w_ff2', 'grad_norm_ple', 'grad_w_ple_gate', 'grad_w_ple_proj', 'grad_norm_final', 'delta_norm_mix', 'delta_w_in', 'delta_w_dw', 'delta_conv_ln_g', 'delta_conv_ln_b', 'delta_w_conv_proj', 'delta_q_norm', 'delta_k_norm', 'delta_w_attn_proj', 'delta_w_out', 'delta_norm_ffn', 'delta_w_ff1', 'delta_w_ff2', 'delta_norm_ple', 'delta_w_ple_gate', 'delta_w_ple_proj', 'delta_norm_final', 'new_m_norm_mix', 'new_m_w_in', 'new_m_w_dw', 'new_m_conv_ln_g', 'new_m_conv_ln_b', 'new_m_w_conv_proj', 'new_m_q_norm', 'new_m_k_norm', 'new_m_w_attn_proj', 'new_m_w_out', 'new_m_norm_ffn', 'new_m_w_ff1', 'new_m_w_ff2', 'new_m_norm_ple', 'new_m_w_ple_gate', 'new_m_w_ple_proj', 'new_m_norm_final', 'new_v_norm_mix', 'new_v_w_in', 'new_v_w_dw', 'new_v_conv_ln_g', 'new_v_conv_ln_b', 'new_v_w_conv_proj', 'new_v_q_norm', 'new_v_k_norm', 'new_v_w_attn_proj', 'new_v_w_out', 'new_v_norm_ffn', 'new_v_w_ff1', 'new_v_w_ff2', 'new_v_norm_ple', 'new_v_w_ple_gate', 'new_v_w_ple_proj', 'new_v_norm_final']
TWIN_LEAF_KINDS = {'loss': 'loss', 'grad_x': 'grad_x', 'grad_norm_mix': 'grad_w', 'grad_w_in': 'grad_w', 'grad_w_dw': 'grad_w', 'grad_conv_ln_g': 'grad_w', 'grad_conv_ln_b': 'grad_w', 'grad_w_conv_proj': 'grad_w', 'grad_q_norm': 'grad_w', 'grad_k_norm': 'grad_w', 'grad_w_attn_proj': 'grad_w', 'grad_w_out': 'grad_w', 'grad_norm_ffn': 'grad_w', 'grad_w_ff1': 'grad_w', 'grad_w_ff2': 'grad_w', 'grad_norm_ple': 'grad_w', 'grad_w_ple_gate': 'grad_w', 'grad_w_ple_proj': 'grad_w', 'grad_norm_final': 'grad_w', 'delta_norm_mix': 'delta_w', 'delta_w_in': 'delta_w', 'delta_w_dw': 'delta_w', 'delta_conv_ln_g': 'delta_w', 'delta_conv_ln_b': 'delta_w', 'delta_w_conv_proj': 'delta_w', 'delta_q_norm': 'delta_w', 'delta_k_norm': 'delta_w', 'delta_w_attn_proj': 'delta_w', 'delta_w_out': 'delta_w', 'delta_norm_ffn': 'delta_w', 'delta_w_ff1': 'delta_w', 'delta_w_ff2': 'delta_w', 'delta_norm_ple': 'delta_w', 'delta_w_ple_gate': 'delta_w', 'delta_w_ple_proj': 'delta_w', 'delta_norm_final': 'delta_w', 'new_m_norm_mix': 'new_m', 'new_m_w_in': 'new_m', 'new_m_w_dw': 'new_m', 'new_m_conv_ln_g': 'new_m', 'new_m_conv_ln_b': 'new_m', 'new_m_w_conv_proj': 'new_m', 'new_m_q_norm': 'new_m', 'new_m_k_norm': 'new_m', 'new_m_w_attn_proj': 'new_m', 'new_m_w_out': 'new_m', 'new_m_norm_ffn': 'new_m', 'new_m_w_ff1': 'new_m', 'new_m_w_ff2': 'new_m', 'new_m_norm_ple': 'new_m', 'new_m_w_ple_gate': 'new_m', 'new_m_w_ple_proj': 'new_m', 'new_m_norm_final': 'new_m', 'new_v_norm_mix': 'new_v', 'new_v_w_in': 'new_v', 'new_v_w_dw': 'new_v', 'new_v_conv_ln_g': 'new_v', 'new_v_conv_ln_b': 'new_v', 'new_v_w_conv_proj': 'new_v', 'new_v_q_norm': 'new_v', 'new_v_k_norm': 'new_v', 'new_v_w_attn_proj': 'new_v', 'new_v_w_out': 'new_v', 'new_v_norm_ffn': 'new_v', 'new_v_w_ff1': 'new_v', 'new_v_w_ff2': 'new_v', 'new_v_norm_ple': 'new_v', 'new_v_w_ple_gate': 'new_v', 'new_v_w_ple_proj': 'new_v', 'new_v_norm_final': 'new_v'}


def _forward(args):
    return _fwd_reference(*[args[k] for k in FWD_PARAMS])


def _output_shape():
    def fwd():
        inp = _fwd_setup_inputs(0)
        return _fwd_reference(*[inp[k] for k in FWD_PARAMS])
    out = _jax.eval_shape(fwd)
    return out.shape, out.dtype

N_MICROBATCH = 1
ADAM_LR = 0.001
ADAM_B1 = 0.9
ADAM_B2 = 0.999
ADAM_EPS = 1e-08
ADAM_WD = 0.01
ADAM_STEP = 10
PER_EXAMPLE_BATCH_AXIS = {'x': 0, 'p': 1, 'loss_target': 0}
SHARED_INPUTS = []
_WEIGHT_DTYPES = {'norm_mix': _jnp.float32, 'w_in': _jnp.float32, 'w_dw': _jnp.float32, 'conv_ln_g': _jnp.float32, 'conv_ln_b': _jnp.float32, 'w_conv_proj': _jnp.float32, 'q_norm': _jnp.float32, 'k_norm': _jnp.float32, 'w_attn_proj': _jnp.float32, 'w_out': _jnp.float32, 'norm_ffn': _jnp.float32, 'w_ff1': _jnp.float32, 'w_ff2': _jnp.float32, 'norm_ple': _jnp.float32, 'w_ple_gate': _jnp.float32, 'w_ple_proj': _jnp.float32, 'norm_final': _jnp.float32}
MOMENT_SCALE = {'norm_mix': 4.714480e-02, 'w_in': 2.307816e-02, 'w_dw': 5.902798e-02, 'conv_ln_g': 7.881891e-02, 'conv_ln_b': 8.691570e-02, 'w_conv_proj': 4.339057e-02, 'q_norm': 2.723852e-02, 'k_norm': 2.847287e-02, 'w_attn_proj': 6.926912e-03, 'w_out': 4.268847e-02, 'norm_ffn': 1.085847e-01, 'w_ff1': 5.271002e-02, 'w_ff2': 1.121523e-01, 'norm_ple': 1.834143e-02, 'w_ple_gate': 1.744267e-02, 'w_ple_proj': 4.015897e-02, 'norm_final': 3.231134e+01}


def _to_microbatches(a, axis):
    t = _jnp.moveaxis(a, axis, 0)
    t = t.reshape((N_MICROBATCH, t.shape[0] // N_MICROBATCH) + t.shape[1:])
    return _jnp.moveaxis(t, 1, axis + 1)


def setup_inputs(seed: int = 0) -> dict:
    inp = _fwd_setup_inputs(seed)
    key = _jax.random.fold_in(_jax.random.key(seed), 7919)
    shape, _ = _output_shape()
    out = dict(inp)
    out["loss_target"] = _jax.random.normal(_jax.random.fold_in(key, 0), shape, _jnp.float32)
    for i, name in enumerate(TWIN_WEIGHTS):
        w = inp[name].astype(_jnp.float32)
        if MOMENT_SCALE is None:
            s = _jnp.sqrt(_jnp.mean(_jnp.square(w)) + 1e-30)
        else:
            s = MOMENT_SCALE[name]
        km, kv = _jax.random.split(_jax.random.fold_in(key, i + 1))
        out[name] = w
        out["m_" + name] = s * _jax.random.normal(km, w.shape, _jnp.float32)
        out["v_" + name] = (s * s) * _jax.random.uniform(kv, w.shape, _jnp.float32, 0.5, 1.5)
    if N_MICROBATCH > 1:
        for name, axis in PER_EXAMPLE_BATCH_AXIS.items():
            out[name] = _to_microbatches(out[name], axis)
    return {'x': out['x'], 'p': out['p'], 'norm_mix': out['norm_mix'], 'w_in': out['w_in'], 'w_dw': out['w_dw'], 'conv_ln_g': out['conv_ln_g'], 'conv_ln_b': out['conv_ln_b'], 'w_conv_proj': out['w_conv_proj'], 'q_norm': out['q_norm'], 'k_norm': out['k_norm'], 'w_attn_proj': out['w_attn_proj'], 'w_out': out['w_out'], 'norm_ffn': out['norm_ffn'], 'w_ff1': out['w_ff1'], 'w_ff2': out['w_ff2'], 'norm_ple': out['norm_ple'], 'w_ple_gate': out['w_ple_gate'], 'w_ple_proj': out['w_ple_proj'], 'norm_final': out['norm_final'], 'loss_target': out['loss_target'], 'm_norm_mix': out['m_norm_mix'], 'm_w_in': out['m_w_in'], 'm_w_dw': out['m_w_dw'], 'm_conv_ln_g': out['m_conv_ln_g'], 'm_conv_ln_b': out['m_conv_ln_b'], 'm_w_conv_proj': out['m_w_conv_proj'], 'm_q_norm': out['m_q_norm'], 'm_k_norm': out['m_k_norm'], 'm_w_attn_proj': out['m_w_attn_proj'], 'm_w_out': out['m_w_out'], 'm_norm_ffn': out['m_norm_ffn'], 'm_w_ff1': out['m_w_ff1'], 'm_w_ff2': out['m_w_ff2'], 'm_norm_ple': out['m_norm_ple'], 'm_w_ple_gate': out['m_w_ple_gate'], 'm_w_ple_proj': out['m_w_ple_proj'], 'm_norm_final': out['m_norm_final'], 'v_norm_mix': out['v_norm_mix'], 'v_w_in': out['v_w_in'], 'v_w_dw': out['v_w_dw'], 'v_conv_ln_g': out['v_conv_ln_g'], 'v_conv_ln_b': out['v_conv_ln_b'], 'v_w_conv_proj': out['v_w_conv_proj'], 'v_q_norm': out['v_q_norm'], 'v_k_norm': out['v_k_norm'], 'v_w_attn_proj': out['v_w_attn_proj'], 'v_w_out': out['v_w_out'], 'v_norm_ffn': out['v_norm_ffn'], 'v_w_ff1': out['v_w_ff1'], 'v_w_ff2': out['v_w_ff2'], 'v_norm_ple': out['v_norm_ple'], 'v_w_ple_gate': out['v_w_ple_gate'], 'v_w_ple_proj': out['v_w_ple_proj'], 'v_norm_final': out['v_norm_final']}


def _loss(weights, diff, rest, loss_target):
    with _jax.named_scope("forward"):
        args = {**rest, TWIN_DIFF_INPUT: diff, **{k: w.astype(_WEIGHT_DTYPES[k]) for k, w in weights.items()}}
        y = _forward(args)
    with _jax.named_scope("loss_head"):
        err = _jnp.square(y.astype(_jnp.float32) - loss_target)
        return 0.5 * _jnp.sum(_jnp.mean(err, axis=-1)) if err.ndim else 0.5 * err


def _adamw(w, g, m, v):
    m = ADAM_B1 * m + (1.0 - ADAM_B1) * g
    v = ADAM_B2 * v + (1.0 - ADAM_B2) * _jnp.square(g)
    m_hat = m / (1.0 - ADAM_B1 ** ADAM_STEP)
    v_hat = v / (1.0 - ADAM_B2 ** ADAM_STEP)
    delta = -ADAM_LR * (m_hat / (_jnp.sqrt(v_hat) + ADAM_EPS) + ADAM_WD * w)
    return delta, m, v


def reference(x, p, norm_mix, w_in, w_dw, conv_ln_g, conv_ln_b, w_conv_proj, q_norm, k_norm, w_attn_proj, w_out, norm_ffn, w_ff1, w_ff2, norm_ple, w_ple_gate, w_ple_proj, norm_final, loss_target, m_norm_mix, m_w_in, m_w_dw, m_conv_ln_g, m_conv_ln_b, m_w_conv_proj, m_q_norm, m_k_norm, m_w_attn_proj, m_w_out, m_norm_ffn, m_w_ff1, m_w_ff2, m_norm_ple, m_w_ple_gate, m_w_ple_proj, m_norm_final, v_norm_mix, v_w_in, v_w_dw, v_conv_ln_g, v_conv_ln_b, v_w_conv_proj, v_q_norm, v_k_norm, v_w_attn_proj, v_w_out, v_norm_ffn, v_w_ff1, v_w_ff2, v_norm_ple, v_w_ple_gate, v_w_ple_proj, v_norm_final):
    given = dict(x=x, p=p, norm_mix=norm_mix, w_in=w_in, w_dw=w_dw, conv_ln_g=conv_ln_g, conv_ln_b=conv_ln_b, w_conv_proj=w_conv_proj, q_norm=q_norm, k_norm=k_norm, w_attn_proj=w_attn_proj, w_out=w_out, norm_ffn=norm_ffn, w_ff1=w_ff1, w_ff2=w_ff2, norm_ple=norm_ple, w_ple_gate=w_ple_gate, w_ple_proj=w_ple_proj, norm_final=norm_final, loss_target=loss_target, m_norm_mix=m_norm_mix, m_w_in=m_w_in, m_w_dw=m_w_dw, m_conv_ln_g=m_conv_ln_g, m_conv_ln_b=m_conv_ln_b, m_w_conv_proj=m_w_conv_proj, m_q_norm=m_q_norm, m_k_norm=m_k_norm, m_w_attn_proj=m_w_attn_proj, m_w_out=m_w_out, m_norm_ffn=m_norm_ffn, m_w_ff1=m_w_ff1, m_w_ff2=m_w_ff2, m_norm_ple=m_norm_ple, m_w_ple_gate=m_w_ple_gate, m_w_ple_proj=m_w_ple_proj, m_norm_final=m_norm_final, v_norm_mix=v_norm_mix, v_w_in=v_w_in, v_w_dw=v_w_dw, v_conv_ln_g=v_conv_ln_g, v_conv_ln_b=v_conv_ln_b, v_w_conv_proj=v_w_conv_proj, v_q_norm=v_q_norm, v_k_norm=v_k_norm, v_w_attn_proj=v_w_attn_proj, v_w_out=v_w_out, v_norm_ffn=v_norm_ffn, v_w_ff1=v_w_ff1, v_w_ff2=v_w_ff2, v_norm_ple=v_norm_ple, v_w_ple_gate=v_w_ple_gate, v_w_ple_proj=v_w_ple_proj, v_norm_final=v_norm_final)
    weights = {n: given[n] for n in TWIN_WEIGHTS}
    shared = {n: given[n] for n in SHARED_INPUTS}
    per_example = {n: given[n] for n in ['x', 'p']}
    grad_fn = _jax.value_and_grad(_loss, argnums=(0, 1))

    def one_microbatch(ex, loss_target):
        ex = dict(ex)
        diff = ex.pop(TWIN_DIFF_INPUT)
        return grad_fn(weights, diff, {**shared, **ex}, loss_target)

    if N_MICROBATCH == 1:
        loss, (grad_w, grad_x) = one_microbatch(per_example, given["loss_target"])
    else:
        def body(carry, xs):
            loss_sum, grad_sum = carry
            l_k, (gw_k, gx_k) = one_microbatch(xs[0], xs[1])
            with _jax.named_scope("update"):
                return (loss_sum + l_k, _jax.tree.map(_jnp.add, grad_sum, gw_k)), gx_k

        init = (_jnp.zeros((), _jnp.float32), _jax.tree.map(_jnp.zeros_like, weights))
        (loss, grad_w), grad_x = _jax.lax.scan(body, init, (per_example, given["loss_target"]))
    with _jax.named_scope("update"):
        delta_w, new_m, new_v = {}, {}, {}
        for n in TWIN_WEIGHTS:
            delta_w[n], new_m[n], new_v[n] = _adamw(weights[n], grad_w[n], given["m_" + n], given["v_" + n])
    return (loss, grad_x, *[grad_w[n] for n in TWIN_WEIGHTS], *[delta_w[n] for n in TWIN_WEIGHTS],
            *[new_m[n] for n in TWIN_WEIGHTS], *[new_v[n] for n in TWIN_WEIGHTS])
```

```python
import functools

import jax
import jax.numpy as jnp
from jax import lax
from jax.experimental import pallas as pl
from jax.experimental.pallas import tpu as pltpu

F32 = jnp.float32
BF = jnp.bfloat16

EPS = 1e-6
HEAD_DIM = 128
GROUP = 4
GRID_W = 64
ROPE_THETA = 10000.0
CONV_KERNEL = 31
HALO = 16
N_CHIPS = 4
N_DEV = 8
LANES = 128

ADAM_LR = 0.001
ADAM_B1 = 0.9
ADAM_B2 = 0.999
ADAM_EPS = 1e-08
ADAM_WD = 0.01
ADAM_STEP = 10

VMEM_LIMIT = 56 * 2 ** 20
ROW_TILE = 256
FLASH_TQ = 256
FLASH_TK = 512
MM_TM = 1024
MM_TM_EPI = 512
MESH = pl.DeviceIdType.MESH
ANY = pl.BlockSpec(memory_space=pl.ANY)


def _params(sem):
    return pltpu.CompilerParams(dimension_semantics=sem, vmem_limit_bytes=VMEM_LIMIT)


def _sigmoid(x):
    return 1.0 / (1.0 + jnp.exp(-x))


def _mm(name, a, b, *, tm, tn, tk, ta=False, tb=False, b_cm=False, out_cm=False,
        a_fn=None, extras=(), epi=None, out_dtypes=(BF,), epi_rows=256):
    if ta:
        kc, m = a.shape
    else:
        m, kc = a.shape
    if b_cm:
        nc, r, c = b.shape
        n, per = (r, c) if tb else (nc * c, c)
    else:
        n = b.shape[0] if tb else b.shape[1]
    tm, tn, tk = min(tm, m), min(tn, n), min(tk, kc)
    assert m % tm == 0 and n % tn == 0 and kc % tk == 0, (name, m, n, kc, tm, tn, tk)
    nk = kc // tk
    a_spec = pl.BlockSpec((tk, tm), lambda i, j, k: (k, i)) if ta else pl.BlockSpec((tm, tk), lambda i, j, k: (i, k))
    if b_cm and not tb:
        assert per % tn == 0
        npj = per // tn
        b_spec = pl.BlockSpec((None, tk, tn), lambda i, j, k: (j // npj, k, j % npj))
    elif b_cm:
        assert per % tk == 0
        npk = per // tk
        b_spec = pl.BlockSpec((None, tn, tk), lambda i, j, k: (k // npk, j, k % npk))
    elif tb:
        b_spec = pl.BlockSpec((tn, tk), lambda i, j, k: (j, k))
    else:
        b_spec = pl.BlockSpec((tk, tn), lambda i, j, k: (k, j))
    if out_cm:
        assert (n // N_CHIPS) % tn == 0
        npo = (n // N_CHIPS) // tn
        o_spec = pl.BlockSpec((None, tm, tn), lambda i, j, k: (j // npo, i, j % npo))
        o_shape = (N_CHIPS, m, n // N_CHIPS)
    else:
        o_spec = pl.BlockSpec((tm, tn), lambda i, j, k: (i, j))
        o_shape = (m, n)
    ne, no = len(extras), len(out_dtypes)
    dims = (((0 if ta else 1,), (1 if tb else 0,)), ((), ()))
    use_acc = nk > 1 or epi is not None
    er = min(epi_rows, tm)

    def body(*refs):
        a_ref, b_ref = refs[0], refs[1]
        ex = refs[2:2 + ne]
        outs = refs[2 + ne:2 + ne + no]
        at = a_ref[...]
        if a_fn is not None:
            at = a_fn(at)
        d = lax.dot_general(at, b_ref[...], dims, preferred_element_type=F32)
        if not use_acc:
            outs[0][...] = d.astype(out_dtypes[0])
            return
        acc = refs[-1]
        k = pl.program_id(2)

        @pl.when(k == 0)
        def _():
            acc[...] = d

        if nk > 1:
            @pl.when(k > 0)
            def _():
                acc[...] += d

        @pl.when(k == nk - 1)
        def _():
            for r0 in range(0, tm, er):
                rows = slice(r0, r0 + er)
                if epi is None:
                    vals = (acc[rows, :],)
                else:
                    vals = epi(acc[rows, :], *[e[rows, :] for e in ex])
                for o, v, dt in zip(outs, vals, out_dtypes):
                    o[rows, :] = v.astype(dt)

    return pl.pallas_call(
        body, name=name,
        grid=(m // tm, n // tn, nk),
        in_specs=[a_spec, b_spec] + [pl.BlockSpec(bs, im) for _, bs, im in extras],
        out_specs=[o_spec] * no,
        out_shape=[jax.ShapeDtypeStruct(o_shape, dt) for dt in out_dtypes],
        scratch_shapes=[pltpu.VMEM((tm, tn), F32)] if use_acc else [],
        compiler_params=_params(("parallel", "parallel", "arbitrary")),
    )(a, b, *[e for e, _, _ in extras])


def _tile_extra(arr, tm, tn, col_block0=0):
    return (arr, (tm, tn), lambda i, j, k: (i, j + col_block0))


def _rms_fwd(name, x, g, ts=None):
    s, d = x.shape
    ts = ts or ROW_TILE

    def body(x_ref, g_ref, h_ref):
        xv = x_ref[...]
        r = lax.rsqrt(jnp.mean(xv * xv, axis=-1, keepdims=True) + EPS)
        h_ref[...] = (xv * r * g_ref[...]).astype(BF)

    return pl.pallas_call(
        body, name=name, grid=(s // ts,),
        in_specs=[pl.BlockSpec((ts, d), lambda i: (i, 0)), pl.BlockSpec((1, d), lambda i: (0, 0))],
        out_specs=pl.BlockSpec((ts, d), lambda i: (i, 0)),
        out_shape=jax.ShapeDtypeStruct((s, d), BF),
        compiler_params=_params(("parallel",)),
    )(x, g)


def _rms_bwd(name, dh, x, g, dres, ts=None):
    s, d = x.shape
    ts = ts or ROW_TILE

    def body(dh_ref, x_ref, g_ref, dres_ref, dx_ref, dxb_ref, dg_ref):
        xv = x_ref[...]
        dhv = dh_ref[...].astype(F32)
        r = lax.rsqrt(jnp.mean(xv * xv, axis=-1, keepdims=True) + EPS)
        nrm = xv * r
        dn = dhv * g_ref[...]
        dx = dres_ref[...] + r * (dn - nrm * jnp.mean(dn * nrm, axis=-1, keepdims=True))
        dx_ref[...] = dx
        dxb_ref[...] = dx.astype(BF)
        part = jnp.sum(dhv * nrm, axis=0, keepdims=True)

        @pl.when(pl.program_id(0) == 0)
        def _():
            dg_ref[...] = part

        @pl.when(pl.program_id(0) > 0)
        def _():
            dg_ref[...] += part

    row = pl.BlockSpec((ts, d), lambda i: (i, 0))
    vec = pl.BlockSpec((1, d), lambda i: (0, 0))
    return pl.pallas_call(
        body, name=name, grid=(s // ts,),
        in_specs=[row, row, vec, row],
        out_specs=[row, row, vec],
        out_shape=[jax.ShapeDtypeStruct((s, d), F32), jax.ShapeDtypeStruct((s, d), BF), jax.ShapeDtypeStruct((1, d), F32)],
        compiler_params=_params(("arbitrary",)),
    )(dh, x, g, dres)


def _loss_bwd(x3, tgt, gfin, e, gate, ts=None):
    s, d = x3.shape
    ts = ts or ROW_TILE

    def body(x_ref, t_ref, g_ref, e_ref, gate_ref, dx_ref, de_ref, dgp_ref, sq_ref, dg_ref):
        xv = x_ref[...]
        gv = g_ref[...]
        r = lax.rsqrt(jnp.mean(xv * xv, axis=-1, keepdims=True) + EPS)
        nrm = xv * r
        err = nrm * gv - t_ref[...]
        dy = err * (1.0 / d)
        dn = dy * gv
        dx = r * (dn - nrm * jnp.mean(dn * nrm, axis=-1, keepdims=True))
        dx_ref[...] = dx
        ev = e_ref[...].astype(F32)
        gt = gate_ref[...].astype(F32)
        de_ref[...] = (dx * gt).astype(BF)
        dgp_ref[...] = (dx * ev * gt * (1.0 - gt)).astype(BF)
        sq = jnp.full((8, LANES), jnp.sum(err * err), F32)
        part = jnp.sum(dy * nrm, axis=0, keepdims=True)

        @pl.when(pl.program_id(0) == 0)
        def _():
            sq_ref[...] = sq
            dg_ref[...] = part

        @pl.when(pl.program_id(0) > 0)
        def _():
            sq_ref[...] += sq
            dg_ref[...] += part

    row = pl.BlockSpec((ts, d), lambda i: (i, 0))
    vec = pl.BlockSpec((1, d), lambda i: (0, 0))
    return pl.pallas_call(
        body, name="loss_bwd", grid=(s // ts,),
        in_specs=[row, row, vec, row, row],
        out_specs=[row, row, row, pl.BlockSpec((8, LANES), lambda i: (0, 0)), vec],
        out_shape=[jax.ShapeDtypeStruct((s, d), F32), jax.ShapeDtypeStruct((s, d), BF), jax.ShapeDtypeStruct((s, d), BF),
                   jax.ShapeDtypeStruct((8, LANES), F32), jax.ShapeDtypeStruct((1, d), F32)],
        compiler_params=_params(("arbitrary",)),
    )(x3, tgt, gfin, e, gate)


def _halo_specs(ts, s, width, col_block):
    per = ts // HALO
    last = s // HALO - 1
    return [
        pl.BlockSpec((HALO, width), lambda i: (jnp.maximum(i * per - 1, 0), col_block)),
        pl.BlockSpec((ts, width), lambda i: (i, col_block)),
        pl.BlockSpec((HALO, width), lambda i: (jnp.minimum((i + 1) * per, last), col_block)),
    ]


def _glu_ext(zp, zc, zn, ext, cw, ts, i, n_tiles):
    def glu(zr):
        zv = zr[...].astype(F32)
        return zv[:, :cw] * _sigmoid(zv[:, cw:])

    ext[0:HALO, :] = jnp.where(i > 0, glu(zp), 0.0)
    ext[HALO:HALO + ts, :] = glu(zc)
    ext[HALO + ts:, :] = jnp.where(i < n_tiles - 1, glu(zn), 0.0)


def _ln_stats(uc):
    mu = jnp.mean(uc, axis=-1, keepdims=True)
    xc = uc - mu
    rstd = lax.rsqrt(jnp.mean(xc * xc, axis=-1, keepdims=True) + EPS)
    return xc * rstd, rstd


def _conv_fwd(z, wdw, ln_g, ln_b, cw, ts=None):
    s = z.shape[0]
    ts = ts or ROW_TILE
    n_tiles = s // ts
    pad = CONV_KERNEL // 2

    def body(zp, zc, zn, w_ref, g_ref, b_ref, uc_ref, act_ref, ext):
        i = pl.program_id(0)
        _glu_ext(zp, zc, zn, ext, cw, ts, i, n_tiles)
        acc = jnp.zeros((ts, cw), F32)
        for j in range(CONV_KERNEL):
            off = HALO - pad + j
            acc = acc + ext[off:off + ts, :] * w_ref[j:j + 1, :]
        uc_ref[...] = acc
        xhat, _ = _ln_stats(acc)
        ln = xhat * g_ref[...] + b_ref[...]
        act_ref[...] = (ln * _sigmoid(ln)).astype(BF)

    vec = pl.BlockSpec((1, cw), lambda i: (0, 0))
    row = pl.BlockSpec((ts, cw), lambda i: (i, 0))
    return pl.pallas_call(
        body, name="conv_fwd", grid=(n_tiles,),
        in_specs=_halo_specs(ts, s, 2 * cw, 0) + [pl.BlockSpec((32, cw), lambda i: (0, 0)), vec, vec],
        out_specs=[row, row],
        out_shape=[jax.ShapeDtypeStruct((s, cw), F32), jax.ShapeDtypeStruct((s, cw), BF)],
        scratch_shapes=[pltpu.VMEM((ts + 2 * HALO, cw), F32)],
        compiler_params=_params(("parallel",)),
    )(z, z, z, wdw, ln_g, ln_b)


def _conv_bwd(ds, uc, z, wdw, ln_g, ln_b, cw, ts=None):
    s = z.shape[0]
    ts = ts or ROW_TILE
    n_tiles = s // ts
    pad = CONV_KERNEL // 2

    def body(zp, zc, zn, dsp, dsc, dsn, ucp, ucc, ucn, w_ref, g_ref, b_ref,
             dz_ref, dw_ref, dg_ref, db_ref, ext, dext):
        i = pl.program_id(0)
        gv, bv = g_ref[...], b_ref[...]

        def ln_bwd(ds_r, uc_r):
            xhat, rstd = _ln_stats(uc_r[...])
            ln = xhat * gv + bv
            sg = _sigmoid(ln)
            dln = ds_r[...].astype(F32) * (sg * (1.0 + ln * (1.0 - sg)))
            dxh = dln * gv
            duc = rstd * (dxh - jnp.mean(dxh, axis=-1, keepdims=True) - xhat * jnp.mean(dxh * xhat, axis=-1, keepdims=True))
            return duc, dln, xhat

        duc_p, _, _ = ln_bwd(dsp, ucp)
        duc_c, dln_c, xhat_c = ln_bwd(dsc, ucc)
        duc_n, _, _ = ln_bwd(dsn, ucn)
        dext[0:HALO, :] = jnp.where(i > 0, duc_p, 0.0)
        dext[HALO:HALO + ts, :] = duc_c
        dext[HALO + ts:, :] = jnp.where(i < n_tiles - 1, duc_n, 0.0)
        _glu_ext(zp, zc, zn, ext, cw, ts, i, n_tiles)

        du = jnp.zeros((ts, cw), F32)
        for j in range(CONV_KERNEL):
            off = HALO + pad - j
            du = du + dext[off:off + ts, :] * w_ref[j:j + 1, :]
        zv = zc[...].astype(F32)
        ca, sb = zv[:, :cw], _sigmoid(zv[:, cw:])
        dz_ref[:, :cw] = (du * sb).astype(BF)
        dz_ref[:, cw:] = (du * ca * sb * (1.0 - sb)).astype(BF)

        dg_part = jnp.sum(dln_c * xhat_c, axis=0, keepdims=True)
        db_part = jnp.sum(dln_c, axis=0, keepdims=True)

        @pl.when(i == 0)
        def _():
            dw_ref[...] = jnp.zeros_like(dw_ref)
            dg_ref[...] = dg_part
            db_ref[...] = db_part

        @pl.when(i > 0)
        def _():
            dg_ref[...] += dg_part
            db_ref[...] += db_part

        for j in range(CONV_KERNEL):
            off = HALO - pad + j
            dw_ref[j:j + 1, :] += jnp.sum(ext[off:off + ts, :] * duc_c, axis=0, keepdims=True)

    vec = pl.BlockSpec((1, cw), lambda i: (0, 0))
    wsp = pl.BlockSpec((32, cw), lambda i: (0, 0))
    return pl.pallas_call(
        body, name="conv_bwd", grid=(n_tiles,),
        in_specs=_halo_specs(ts, s, 2 * cw, 0) + _halo_specs(ts, s, cw, 0) + _halo_specs(ts, s, cw, 0) + [wsp, vec, vec],
        out_specs=[pl.BlockSpec((ts, 2 * cw), lambda i: (i, 0)), wsp, vec, vec],
        out_shape=[jax.ShapeDtypeStruct((s, 2 * cw), BF), jax.ShapeDtypeStruct((32, cw), F32),
                   jax.ShapeDtypeStruct((1, cw), F32), jax.ShapeDtypeStruct((1, cw), F32)],
        scratch_shapes=[pltpu.VMEM((ts + 2 * HALO, cw), F32), pltpu.VMEM((ts + 2 * HALO, cw), F32)],
        compiler_params=_params(("arbitrary",)),
    )(z, z, z, ds, ds, ds, uc, uc, uc, wdw, ln_g, ln_b)


def _rope_tables(s):
    axis_dim = HEAD_DIM // 2
    t = jnp.arange(s, dtype=jnp.int32)
    row = (t // GRID_W).astype(F32)[:, None]
    col = (t % GRID_W).astype(F32)[:, None]
    inv_freq = ROPE_THETA ** (-jnp.arange(0, axis_dim, 2, dtype=F32) / axis_dim)[None, :]
    ar, ac = row * inv_freq, col * inv_freq
    cos = jnp.concatenate([jnp.cos(ar), jnp.cos(ar), jnp.cos(ac), jnp.cos(ac)], axis=-1)
    sin = jnp.concatenate([-jnp.sin(ar), jnp.sin(ar), -jnp.sin(ac), jnp.sin(ac)], axis=-1)
    return cos, sin


def _swap_quarters(x):
    q = HEAD_DIM // 4
    lane = lax.broadcasted_iota(jnp.int32, x.shape, 1)
    return jnp.where((lane % (2 * q)) < q, pltpu.roll(x, HEAD_DIM - q, 1), pltpu.roll(x, q, 1))


def _qk_fwd(z, cos, sin, qg, kg, d, ts=None):
    s = z.shape[0]
    ts = ts or ROW_TILE
    kvw = d // GROUP
    scale = HEAD_DIM ** -0.5

    def body(q_ref, k_ref, c_ref, s_ref, qg_ref, kg_ref, qo_ref, ko_ref):
        cv, sv = c_ref[...], s_ref[...]

        def head(x_ref, g_ref, o_ref, h, mul):
            xv = x_ref[:, h * HEAD_DIM:(h + 1) * HEAD_DIM].astype(F32)
            r = lax.rsqrt(jnp.mean(xv * xv, axis=-1, keepdims=True) + EPS)
            nrm = xv * r * g_ref[...]
            out = nrm * cv + _swap_quarters(nrm) * sv
            o_ref[:, h * HEAD_DIM:(h + 1) * HEAD_DIM] = (out * mul).astype(BF)

        for h in range(d // HEAD_DIM):
            head(q_ref, qg_ref, qo_ref, h, scale)
        for h in range(kvw // HEAD_DIM):
            head(k_ref, kg_ref, ko_ref, h, 1.0)

    cw2 = d
    tab = pl.BlockSpec((ts, HEAD_DIM), lambda i: (i, 0))
    vec = pl.BlockSpec((1, HEAD_DIM), lambda i: (0, 0))
    return pl.pallas_call(
        body, name="qk_fwd", grid=(s // ts,),
        in_specs=[pl.BlockSpec((ts, d), lambda i: (i, cw2 // d)),
                  pl.BlockSpec((ts, kvw), lambda i: (i, (cw2 + d) // kvw)), tab, tab, vec, vec],
        out_specs=[pl.BlockSpec((ts, d), lambda i: (i, 0)), pl.BlockSpec((ts, kvw), lambda i: (i, 0))],
        out_shape=[jax.ShapeDtypeStruct((s, d), BF), jax.ShapeDtypeStruct((s, kvw), BF)],
        compiler_params=_params(("parallel",)),
    )(z, z, cos, sin, qg, kg)


def _qk_bwd(dqt, dkt, z, cos, sin, qg, kg, d, ts=None):
    s = z.shape[0]
    ts = ts or ROW_TILE
    kvw = d // GROUP
    scale = HEAD_DIM ** -0.5

    def body(dq_ref, dk_ref, q_ref, k_ref, c_ref, s_ref, qg_ref, kg_ref, dqo_ref, dko_ref, dqg_ref, dkg_ref):
        cv, sv = c_ref[...], s_ref[...]

        def head(dy_ref, x_ref, g_ref, o_ref, h, mul):
            dout = dy_ref[:, h * HEAD_DIM:(h + 1) * HEAD_DIM].astype(F32) * mul
            dn = dout * cv + _swap_quarters(dout * sv)
            xv = x_ref[:, h * HEAD_DIM:(h + 1) * HEAD_DIM].astype(F32)
            r = lax.rsqrt(jnp.mean(xv * xv, axis=-1, keepdims=True) + EPS)
            nh = xv * r
            dnh = dn * g_ref[...]
            o_ref[:, h * HEAD_DIM:(h + 1) * HEAD_DIM] = (r * (dnh - nh * jnp.mean(dnh * nh, axis=-1, keepdims=True))).astype(BF)
            return jnp.sum(dn * nh, axis=0, keepdims=True)

        dqg = jnp.zeros((1, HEAD_DIM), F32)
        for h in range(d // HEAD_DIM):
            dqg = dqg + head(dq_ref, q_ref, qg_ref, dqo_ref, h, scale)
        dkg = jnp.zeros((1, HEAD_DIM), F32)
        for h in range(kvw // HEAD_DIM):
            dkg = dkg + head(dk_ref, k_ref, kg_ref, dko_ref, h, 1.0)

        @pl.when(pl.program_id(0) == 0)
        def _():
            dqg_ref[...] = dqg
            dkg_ref[...] = dkg

        @pl.when(pl.program_id(0) > 0)
        def _():
            dqg_ref[...] += dqg
            dkg_ref[...] += dkg

    cw2 = d
    tab = pl.BlockSpec((ts, HEAD_DIM), lambda i: (i, 0))
    vec = pl.BlockSpec((1, HEAD_DIM), lambda i: (0, 0))
    qrow = pl.BlockSpec((ts, d), lambda i: (i, 0))
    krow = pl.BlockSpec((ts, kvw), lambda i: (i, 0))
    return pl.pallas_call(
        body, name="qk_bwd", grid=(s // ts,),
        in_specs=[qrow, krow, pl.BlockSpec((ts, d), lambda i: (i, cw2 // d)),
                  pl.BlockSpec((ts, kvw), lambda i: (i, (cw2 + d) // kvw)), tab, tab, vec, vec],
        out_specs=[qrow, krow, vec, vec],
        out_shape=[jax.ShapeDtypeStruct((s, d), BF), jax.ShapeDtypeStruct((s, kvw), BF),
                   jax.ShapeDtypeStruct((1, HEAD_DIM), F32), jax.ShapeDtypeStruct((1, HEAD_DIM), F32)],
        compiler_params=_params(("arbitrary",)),
    )(dqt, dkt, z, z, cos, sin, qg, kg)


_NT = (((1,), (1,)), ((), ()))
_TN = (((0,), (0,)), ((), ()))


def _v_col_block(d):
    return (2 * d + d // GROUP) // HEAD_DIM


def _flash_fwd(qt, kt, z, d, tq=None, tk=None):
    s = qt.shape[0]
    tq, tk = min(tq or FLASH_TQ, s), min(tk or FLASH_TK, s)
    ng, nq, nk = d // (GROUP * HEAD_DIM), s // tq, s // tk
    gw = GROUP * HEAD_DIM
    rows = GROUP * tq

    def body(q_ref, k_ref, v_ref, o_ref, lse_ref, qs, m_s, l_s, acc_s):
        for h in range(GROUP):
            qs[h * tq:(h + 1) * tq, :] = q_ref[:, h * HEAD_DIM:(h + 1) * HEAD_DIM]
        m_s[...] = jnp.full((rows, 1), -1e30, F32)
        l_s[...] = jnp.zeros((rows, 1), F32)
        acc_s[...] = jnp.zeros((rows, HEAD_DIM), F32)

        def step(j, carry):
            kv_rows = pl.ds(pl.multiple_of(j * tk, tk), tk)
            sc = lax.dot_general(qs[...], k_ref[kv_rows, :], _NT, preferred_element_type=F32)
            m_old = m_s[...]
            m_new = jnp.maximum(m_old, jnp.max(sc, axis=-1, keepdims=True))
            alpha = jnp.exp(m_old - m_new)
            p = jnp.exp(sc - m_new)
            l_s[...] = alpha * l_s[...] + jnp.sum(p, axis=-1, keepdims=True)
            acc_s[...] = alpha * acc_s[...] + jnp.dot(p.astype(BF), v_ref[kv_rows, :], preferred_element_type=F32)
            m_s[...] = m_new
            return carry

        lax.fori_loop(0, nk, step, 0)
        o = acc_s[...] / l_s[...]
        for h in range(GROUP):
            o_ref[:, h * HEAD_DIM:(h + 1) * HEAD_DIM] = o[h * tq:(h + 1) * tq, :].astype(BF)
        lse_ref[...] = jnp.broadcast_to(m_s[...] + jnp.log(l_s[...]), (rows, LANES))

    vb = _v_col_block(d)
    return pl.pallas_call(
        body, name="flash_fwd", grid=(ng, nq),
        in_specs=[pl.BlockSpec((tq, gw), lambda g, i: (i, g)),
                  pl.BlockSpec((s, HEAD_DIM), lambda g, i: (0, g)),
                  pl.BlockSpec((s, HEAD_DIM), lambda g, i: (0, vb + g))],
        out_specs=[pl.BlockSpec((tq, gw), lambda g, i: (i, g)),
                   pl.BlockSpec((rows, LANES), lambda g, i: (g * nq + i, 0))],
        out_shape=[jax.ShapeDtypeStruct((s, d), BF), jax.ShapeDtypeStruct((ng * nq * rows, LANES), F32)],
        scratch_shapes=[pltpu.VMEM((rows, HEAD_DIM), BF), pltpu.VMEM((rows, 1), F32), pltpu.VMEM((rows, 1), F32),
                        pltpu.VMEM((rows, HEAD_DIM), F32)],
        compiler_params=_params(("parallel", "parallel")),
    )(qt, kt, z)


def _flash_bwd(qt, kt, z, o, do, lse, d, tq=None, tk=None):
    s = qt.shape[0]
    tq, tk = min(tq or FLASH_TQ, s), min(tk or FLASH_TK, s)
    ng, nq, nk = d // (GROUP * HEAD_DIM), s // tq, s // tk
    gw = GROUP * HEAD_DIM
    rows = GROUP * tq

    def body(q_ref, k_ref, v_ref, o_ref, do_ref, lse_ref, dq_ref, dk_ref, dv_ref, qs, dos, delta_s, dq_s):
        i = pl.program_id(1)
        for h in range(GROUP):
            cols = slice(h * HEAD_DIM, (h + 1) * HEAD_DIM)
            qs[h * tq:(h + 1) * tq, :] = q_ref[:, cols]
            dov = do_ref[:, cols]
            dos[h * tq:(h + 1) * tq, :] = dov
            delta_s[h * tq:(h + 1) * tq, :] = jnp.sum(dov.astype(F32) * o_ref[:, cols].astype(F32), axis=-1, keepdims=True)
        dq_s[...] = jnp.zeros((rows, HEAD_DIM), F32)

        @pl.when(i == 0)
        def _():
            dk_ref[...] = jnp.zeros_like(dk_ref)
            dv_ref[...] = jnp.zeros_like(dv_ref)

        def step(j, carry):
            kv_rows = pl.ds(pl.multiple_of(j * tk, tk), tk)
            kv, vv = k_ref[kv_rows, :], v_ref[kv_rows, :]
            sc = lax.dot_general(qs[...], kv, _NT, preferred_element_type=F32)
            p = jnp.exp(sc - lse_ref[:, 0:1])
            dp = lax.dot_general(dos[...], vv, _NT, preferred_element_type=F32)
            dsc = (p * (dp - delta_s[...])).astype(BF)
            dv_ref[kv_rows, :] += lax.dot_general(p.astype(BF), dos[...], _TN, preferred_element_type=F32)
            dk_ref[kv_rows, :] += lax.dot_general(dsc, qs[...], _TN, preferred_element_type=F32)
            dq_s[...] += jnp.dot(dsc, kv, preferred_element_type=F32)
            return carry

        lax.fori_loop(0, nk, step, 0)
        for h in range(GROUP):
            dq_ref[:, h * HEAD_DIM:(h + 1) * HEAD_DIM] = dq_s[h * tq:(h + 1) * tq, :].astype(BF)

    vb = _v_col_block(d)
    qspec = pl.BlockSpec((tq, gw), lambda g, i: (i, g))
    kspec = pl.BlockSpec((s, HEAD_DIM), lambda g, i: (0, g))
    return pl.pallas_call(
        body, name="flash_bwd", grid=(ng, nq),
        in_specs=[qspec, kspec, pl.BlockSpec((s, HEAD_DIM), lambda g, i: (0, vb + g)), qspec, qspec,
                  pl.BlockSpec((rows, LANES), lambda g, i: (g * nq + i, 0))],
        out_specs=[qspec, kspec, kspec],
        out_shape=[jax.ShapeDtypeStruct((s, d), BF), jax.ShapeDtypeStruct((s, d // GROUP), F32),
                   jax.ShapeDtypeStruct((s, d // GROUP), F32)],
        scratch_shapes=[pltpu.VMEM((rows, HEAD_DIM), BF), pltpu.VMEM((rows, HEAD_DIM), BF), pltpu.VMEM((rows, 1), F32),
                        pltpu.VMEM((rows, HEAD_DIM), F32)],
        compiler_params=_params(("parallel", "arbitrary")),
    )(qt, kt, z, o, do, lse)


def _place():
    x, y, c = lax.axis_index("x"), lax.axis_index("y"), lax.axis_index("c")
    other_chips = [(1 - x, y), (x, 1 - y), (1 - x, 1 - y)]
    return x, y, c, other_chips


def _gather_weights(shards):
    n = len(shards)

    def body(*refs):
        src, dst = refs[:n], refs[n:2 * n]
        send, recv, fsend, frecv, loc = refs[2 * n:]
        x, y, c, chips = _place()
        me = 2 * x + y
        local = [pltpu.make_async_copy(src[w], dst[w].at[me], loc.at[w]) for w in range(n)]
        for cp in local:
            cp.start()

        def half(w, core):
            h = shards[w].shape[0] // 2
            return pl.ds(core * h, h)

        def ici(w, j, chip_from, core_to):
            return pltpu.make_async_remote_copy(
                src_ref=src[w].at[half(w, c)], dst_ref=dst[w].at[2 * chip_from[0] + chip_from[1], half(w, c)],
                send_sem=send.at[3 * w + j], recv_sem=recv.at[3 * w + j], device_id=core_to, device_id_type=MESH)

        def d2d(w, j, chip, core_half):
            slab = dst[w].at[2 * chip[0] + chip[1], half(w, core_half)]
            return pltpu.make_async_remote_copy(
                src_ref=slab, dst_ref=slab, send_sem=fsend.at[3 * w + j], recv_sem=frecv.at[3 * w + j],
                device_id=(x, y, 1 - c), device_id_type=MESH)

        first = [ici(w, j, (x, y), (*chip, c)) for w in range(n) for j, chip in enumerate(chips)]
        for cp in first:
            cp.start()
        passed = []
        for w in range(n):
            for j, chip in enumerate(chips):
                ici(w, j, chip, (x, y, c)).wait_recv()
                cp = d2d(w, j, chip, c)
                cp.start()
                passed.append(cp)
        for w in range(n):
            for j, chip in enumerate(chips):
                d2d(w, j, chip, 1 - c).wait_recv()
        for cp in first + passed:
            cp.wait_send()
        for cp in local:
            cp.wait()

    return pl.pallas_call(
        body, name="gather_weights",
        in_specs=[ANY] * n, out_specs=[ANY] * n,
        out_shape=[jax.ShapeDtypeStruct((N_CHIPS,) + s.shape, s.dtype) for s in shards],
        scratch_shapes=[pltpu.SemaphoreType.DMA((3 * n,))] * 4 + [pltpu.SemaphoreType.DMA((n,))],
    )(*shards)


def _pair_exchange(grads):
    n = len(grads)

    def body(*refs):
        src, dst = refs[:n], refs[n:2 * n]
        send, recv = refs[2 * n:]
        x, y, c, _ = _place()
        copies = []
        for w in range(n):
            h = grads[w].shape[1] // 2
            copies.append(pltpu.make_async_remote_copy(
                src_ref=src[w].at[:, pl.ds((1 - c) * h, h), :], dst_ref=dst[w],
                send_sem=send.at[w], recv_sem=recv.at[w], device_id=(x, y, 1 - c), device_id_type=MESH))
        for cp in copies:
            cp.start()
        for cp in copies:
            cp.wait()

    return pl.pallas_call(
        body, name="grad_pair_exchange",
        in_specs=[ANY] * n, out_specs=[ANY] * n,
        out_shape=[jax.ShapeDtypeStruct((N_CHIPS, g.shape[1] // 2, g.shape[2]), g.dtype) for g in grads],
        scratch_shapes=[pltpu.SemaphoreType.DMA((n,))] * 2,
    )(*grads)


def _pair_sum(name, own, got, c_arr, tr=256):
    nc, r, cc = own.shape
    h = r // 2
    tr = min(tr, h)
    nb = h // tr

    def body(c_ref, a_ref, b_ref, o_ref):
        o_ref[...] = (a_ref[...].astype(F32) + b_ref[...].astype(F32)).astype(BF)

    return pl.pallas_call(
        body, name=name,
        grid_spec=pltpu.PrefetchScalarGridSpec(
            num_scalar_prefetch=1, grid=(nc, nb),
            in_specs=[pl.BlockSpec((None, tr, cc), lambda s, i, c_ref: (s, c_ref[0] * nb + i, 0)),
                      pl.BlockSpec((None, tr, cc), lambda s, i, c_ref: (s, i, 0))],
            out_specs=pl.BlockSpec((None, tr, cc), lambda s, i, c_ref: (s, i, 0))),
        out_shape=jax.ShapeDtypeStruct((nc, h, cc), BF),
        compiler_params=_params(("parallel", "parallel")),
    )(c_arr, own, got)


def _chip_exchange(parts):
    n = len(parts)

    def body(*refs):
        src, dst = refs[:n], refs[n:2 * n]
        send, recv, loc = refs[2 * n:]
        x, y, c, chips = _place()
        me = 2 * x + y
        local = [pltpu.make_async_copy(src[w].at[me], dst[w].at[me], loc.at[w]) for w in range(n)]
        for cp in local:
            cp.start()
        copies = []
        for w in range(n):
            for j, chip in enumerate(chips):
                copies.append(pltpu.make_async_remote_copy(
                    src_ref=src[w].at[2 * chip[0] + chip[1]], dst_ref=dst[w].at[me],
                    send_sem=send.at[3 * w + j], recv_sem=recv.at[3 * w + j], device_id=(*chip, c), device_id_type=MESH))
        for cp in copies:
            cp.start()
        for cp in copies:
            cp.wait()
        for cp in local:
            cp.wait()

    return pl.pallas_call(
        body, name="grad_chip_exchange",
        in_specs=[ANY] * n, out_specs=[ANY] * n,
        out_shape=[jax.ShapeDtypeStruct(p.shape, p.dtype) for p in parts],
        scratch_shapes=[pltpu.SemaphoreType.DMA((3 * n,))] * 2 + [pltpu.SemaphoreType.DMA((n,))],
    )(*parts)


def _chip_sum(name, slots, tr=256):
    nc, h, cc = slots.shape
    tr = min(tr, h)

    def body(s_ref, o_ref):
        acc = s_ref[0].astype(F32)
        for s in range(1, nc):
            acc = acc + s_ref[s].astype(F32)
        o_ref[...] = acc

    return pl.pallas_call(
        body, name=name, grid=(h // tr,),
        in_specs=[pl.BlockSpec((nc, tr, cc), lambda i: (0, i, 0))],
        out_specs=pl.BlockSpec((tr, cc), lambda i: (i, 0)),
        out_shape=jax.ShapeDtypeStruct((h, cc), F32),
        compiler_params=_params(("parallel",)),
    )(slots)


def _pair_gather(halves):
    n = len(halves)

    def body(*refs):
        src, dst = refs[:n], refs[n:2 * n]
        send, recv, loc = refs[2 * n:]
        x, y, c, _ = _place()
        local, copies = [], []
        for w in range(n):
            h = halves[w].shape[0]
            mine = dst[w].at[pl.ds(c * h, h)]
            local.append(pltpu.make_async_copy(src[w], mine, loc.at[w]))
            copies.append(pltpu.make_async_remote_copy(
                src_ref=src[w], dst_ref=mine, send_sem=send.at[w], recv_sem=recv.at[w],
                device_id=(x, y, 1 - c), device_id_type=MESH))
        for cp in local + copies:
            cp.start()
        for w in range(n):
            h = halves[w].shape[0]
            theirs = dst[w].at[pl.ds((1 - c) * h, h)]
            pltpu.make_async_remote_copy(src_ref=src[w], dst_ref=theirs, send_sem=send.at[w], recv_sem=recv.at[w],
                                         device_id=(x, y, 1 - c), device_id_type=MESH).wait_recv()
        for cp in copies:
            cp.wait_send()
        for cp in local:
            cp.wait()

    return pl.pallas_call(
        body, name="grad_pair_gather",
        in_specs=[ANY] * n, out_specs=[ANY] * n,
        out_shape=[jax.ShapeDtypeStruct((2 * hv.shape[0], hv.shape[1]), hv.dtype) for hv in halves],
        scratch_shapes=[pltpu.SemaphoreType.DMA((n,))] * 3,
    )(*halves)


def _all_sum_small(name, v):
    p = v.shape[0]

    def body(v_ref, o_ref, slots, send, recv):
        x, y, c, _ = _place()
        me = 4 * x + 2 * y + c
        copies = []
        for k in range(1, N_DEV):
            peer = (x ^ (k >> 2), y ^ ((k >> 1) & 1), c ^ (k & 1))
            copies.append(pltpu.make_async_remote_copy(
                src_ref=v_ref, dst_ref=slots.at[me], send_sem=send.at[k - 1], recv_sem=recv.at[k - 1],
                device_id=peer, device_id_type=MESH))
        for cp in copies:
            cp.start()
        slots[me] = v_ref[...]
        for cp in copies:
            cp.wait()
        acc = slots[0]
        for s in range(1, N_DEV):
            acc = acc + slots[s]
        o_ref[...] = acc

    vm = pl.BlockSpec(memory_space=pltpu.VMEM)
    return pl.pallas_call(
        body, name=name,
        in_specs=[vm], out_specs=vm,
        out_shape=jax.ShapeDtypeStruct(v.shape, F32),
        scratch_shapes=[pltpu.VMEM((N_DEV, p, LANES), F32), pltpu.SemaphoreType.DMA((N_DEV - 1,)),
                        pltpu.SemaphoreType.DMA((N_DEV - 1,))],
    )(v)


def _adamw(name, w, g, m, v, tr=256):
    r, c = w.shape
    tr = min(tr, r)
    assert r % tr == 0
    bc1 = 1.0 - ADAM_B1 ** ADAM_STEP
    bc2 = 1.0 - ADAM_B2 ** ADAM_STEP

    def body(w_ref, g_ref, m_ref, v_ref, d_ref, nm_ref, nv_ref):
        gv = g_ref[...]
        nm = ADAM_B1 * m_ref[...] + (1.0 - ADAM_B1) * gv
        nv = ADAM_B2 * v_ref[...] + (1.0 - ADAM_B2) * (gv * gv)
        nm_ref[...] = nm
        nv_ref[...] = nv
        d_ref[...] = -ADAM_LR * ((nm / bc1) / (jnp.sqrt(nv / bc2) + ADAM_EPS) + ADAM_WD * w_ref[...])

    blk = pl.BlockSpec((tr, c), lambda i: (i, 0))
    return pl.pallas_call(
        body, name=name, grid=(r // tr,),
        in_specs=[blk] * 4, out_specs=[blk] * 3,
        out_shape=[jax.ShapeDtypeStruct((r, c), F32)] * 3,
        compiler_params=_params(("parallel",)),
    )(w, g, m, v)


def _pack_small(parts):
    flat = jnp.concatenate([a.reshape(-1) for a in parts])
    n = flat.shape[0]
    p = -(-n // (8 * LANES)) * 8
    packed = jnp.pad(flat, (0, p * LANES - n)).reshape(p, LANES)

    def unpack(q):
        out, off = [], 0
        f = q.reshape(-1)
        for a in parts:
            out.append(f[off:off + a.size].reshape(a.shape))
            off += a.size
        return out

    return packed, unpack


def kernel(x, p, norm_mix, w_in, w_dw, conv_ln_g, conv_ln_b, w_conv_proj, q_norm, k_norm, w_attn_proj, w_out, norm_ffn, w_ff1, w_ff2, norm_ple, w_ple_gate, w_ple_proj, norm_final, loss_target, m_norm_mix, m_w_in, m_w_dw, m_conv_ln_g, m_conv_ln_b, m_w_conv_proj, m_q_norm, m_k_norm, m_w_attn_proj, m_w_out, m_norm_ffn, m_w_ff1, m_w_ff2, m_norm_ple, m_w_ple_gate, m_w_ple_proj, m_norm_final, v_norm_mix, v_w_in, v_w_dw, v_conv_ln_g, v_conv_ln_b, v_w_conv_proj, v_q_norm, v_k_norm, v_w_attn_proj, v_w_out, v_norm_ffn, v_w_ff1, v_w_ff2, v_norm_ple, v_w_ple_gate, v_w_ple_proj, v_norm_final):
    s, d = x.shape[1], x.shape[2]
    cw = d // 2
    kvw = d // GROUP
    dff = 4 * d
    xs, ps, tgt = x[0], p[0, 0], loss_target[0]
    cx, cy, cc = lax.axis_index("x"), lax.axis_index("y"), lax.axis_index("c")
    chip = 2 * cx + cy
    c_arr = jnp.reshape(cc, (1,)).astype(jnp.int32)
    tm, tme = min(MM_TM, s), min(MM_TM_EPI, s)

    big = [w_in, w_conv_proj, w_attn_proj, w_out, w_ff1, w_ff2, w_ple_gate, w_ple_proj]
    win, wcp, wap, wout, w1, w2, wpg, wple = _gather_weights([w[0].astype(BF) for w in big])
    wap, wout, w2, wpg = (t.reshape(-1, t.shape[-1]) for t in (wap, wout, w2, wpg))
    cpc = cw // N_CHIPS
    taps = jnp.zeros((32, N_CHIPS, cpc), F32).at[:CONV_KERNEL].set(
        jnp.where(lax.broadcasted_iota(jnp.int32, (1, N_CHIPS, 1), 1) == chip, w_dw[0][:, None, :], 0.0))
    taps = jnp.where(cc == 0, taps, 0.0).reshape(32 * cw // LANES, LANES)
    wdw = _all_sum_small("gather_taps", taps).reshape(32, cw)

    cos, sin = _rope_tables(s)
    h0 = _rms_fwd("rms_mix", xs, norm_mix)
    (z,) = _mm("z_proj", h0, win, b_cm=True, tm=tm, tn=win.shape[2] // 3, tk=d)
    uc, act = _conv_fwd(z, wdw, conv_ln_g, conv_ln_b, cw)
    (y_c,) = _mm("conv_proj", act, wcp, b_cm=True, tm=tm, tn=wcp.shape[2], tk=cw, out_dtypes=(F32,))
    qt, kt = _qk_fwd(z, cos, sin, q_norm, k_norm, d)
    o, lse = _flash_fwd(qt, kt, z, d)
    tn = d // 2
    gcb = (2 * d + 2 * kvw) // tn

    def merge_epi(acc, yc, gc, ga):
        return acc, _sigmoid(gc.astype(F32)) * yc + _sigmoid(ga.astype(F32)) * acc

    y_a, merged = _mm("attn_proj", o, wap, tm=tme, tn=tn, tk=d, epi=merge_epi, out_dtypes=(BF, BF),
                      extras=[_tile_extra(y_c, tme, tn), _tile_extra(z, tme, tn, gcb), _tile_extra(z, tme, tn, gcb + 2)])
    (x1,) = _mm("out_proj", merged, wout, tm=tm, tn=tn, tk=d, epi=lambda acc, r: (r + acc,), out_dtypes=(F32,),
                extras=[_tile_extra(xs, tm, tn)])
    h1 = _rms_fwd("rms_ffn", x1, norm_ffn)
    (a,) = _mm("ff1", h1, w1, b_cm=True, tm=tm, tn=tn, tk=d)

    def relu2(t):
        return jnp.square(jnp.maximum(t, 0.0))

    (x2,) = _mm("ff2", a, w2, tm=tme, tn=tn, tk=d, a_fn=relu2, epi=lambda acc, r: (r + acc,), out_dtypes=(F32,),
                extras=[_tile_extra(x1, tme, tn)])
    h2 = _rms_fwd("rms_ple", x2, norm_ple)
    to_bf = lambda t: t.astype(BF)
    (e,) = _mm("ple_proj", ps, wple, b_cm=True, tm=tm, tn=wple.shape[2], tk=ps.shape[1], a_fn=to_bf)

    def ple_epi(acc, ev, r):
        gt = _sigmoid(acc)
        return r + gt * ev.astype(F32), gt

    x3, gate = _mm("ple_gate", h2, wpg, tm=tme, tn=tn, tk=d, epi=ple_epi, out_dtypes=(F32, BF),
                   extras=[_tile_extra(e, tme, tn), _tile_extra(x2, tme, tn)])

    dx3, de, dgp, sq, d_fin = _loss_bwd(x3, tgt, norm_final.reshape(1, d), e, gate)
    tkt = min(2048, s)
    (g_wple,) = _mm("d_wple", ps, de, ta=True, out_cm=True, tm=ps.shape[1], tn=wple.shape[2], tk=tkt, a_fn=to_bf)
    (g_wpg,) = _mm("d_wpg", h2, dgp, ta=True, tm=tm, tn=tn, tk=tkt)
    (dh2,) = _mm("d_h2", dgp, wpg, tb=True, tm=tm, tn=tn, tk=d)
    dx2, dx2b, d_ple = _rms_bwd("rms_ple_bwd", dh2, x2, norm_ple, dx3)

    (da,) = _mm("d_a", dx2b, w2, tb=True, tm=tm, tn=tn, tk=d, out_dtypes=(BF,),
                epi=lambda acc, av: (acc * (2.0 * jnp.maximum(av.astype(F32), 0.0)),), extras=[_tile_extra(a, tm, tn)])
    (g_w2,) = _mm("d_w2", a, dx2b, ta=True, tm=tm, tn=tn, tk=min(1024, s), a_fn=relu2)
    (g_w1,) = _mm("d_w1", h1, da, ta=True, out_cm=True, tm=tm, tn=tn, tk=tkt)
    (dh1,) = _mm("d_h1", da, w1, tb=True, b_cm=True, tm=tm, tn=tn, tk=w1.shape[2])
    dx1, dx1b, d_ffn = _rms_bwd("rms_ffn_bwd", dh1, x1, norm_ffn, dx2)

    def merge_bwd(acc, gc, ga, yc, ya):
        sc, sa = _sigmoid(gc.astype(F32)), _sigmoid(ga.astype(F32))
        return acc * sc, acc * sa, acc * yc * sc * (1.0 - sc), acc * ya.astype(F32) * sa * (1.0 - sa)

    dy_c, dy_a, dg_c, dg_a = _mm(
        "d_merged", dx1b, wout, tb=True, tm=tme, tn=tn, tk=d, epi=merge_bwd, out_dtypes=(BF, BF, BF, BF),
        extras=[_tile_extra(z, tme, tn, gcb), _tile_extra(z, tme, tn, gcb + 2), _tile_extra(y_c, tme, tn),
                _tile_extra(y_a, tme, tn)])
    (g_wout,) = _mm("d_wout", merged, dx1b, ta=True, tm=tm, tn=tn, tk=tkt)
    (g_wap,) = _mm("d_wap", o, dy_a, ta=True, tm=tm, tn=tn, tk=tkt)
    (do,) = _mm("d_o", dy_a, wap, tb=True, tm=tm, tn=tn, tk=d)
    dqt, dkt, dv = _flash_bwd(qt, kt, z, o, do, lse, d)
    dq, dk, d_qn, d_kn = _qk_bwd(dqt, dkt, z, cos, sin, q_norm, k_norm, d)
    (g_wcp,) = _mm("d_wcp", act, dy_c, ta=True, out_cm=True, tm=cw, tn=wcp.shape[2], tk=tkt)
    (dact,) = _mm("d_act", dy_c, wcp, tb=True, b_cm=True, tm=tm, tn=cw, tk=wcp.shape[2])
    dcab, d_taps, d_lng, d_lnb = _conv_bwd(dact, uc, z, wdw, conv_ln_g, conv_ln_b, cw)
    dz = jnp.concatenate([dcab, dq, dk, dv.astype(BF), dg_c, dg_a], axis=1)
    (g_win,) = _mm("d_win", h0, dz, ta=True, out_cm=True, tm=tm, tn=win.shape[2] // 3, tk=tkt)
    (dh0,) = _mm("d_h0", dz, win, tb=True, b_cm=True, tm=tm, tn=tn, tk=win.shape[2])
    dx, _, d_mix = _rms_bwd("rms_mix_bwd", dh0, xs, norm_mix, dx1)

    names = ["w_in", "w_conv_proj", "w_attn_proj", "w_out", "w_ff1", "w_ff2", "w_ple_gate", "w_ple_proj"]
    partials = [g_win, g_wcp, g_wap, g_wout, g_w1, g_w2, g_wpg, g_wple]
    partials = [g if g.ndim == 3 else g.reshape(N_CHIPS, g.shape[0] // N_CHIPS, g.shape[1]) for g in partials]
    got = _pair_exchange(partials)
    chip_parts = [_pair_sum("pair_sum_" + nm, g, r, c_arr) for nm, g, r in zip(names, partials, got)]
    slots = _chip_exchange(chip_parts)
    halves = [_chip_sum("chip_sum_" + nm, sl) for nm, sl in zip(names, slots)]
    big_grads = _pair_gather(halves)

    small = [d_mix, d_taps[:CONV_KERNEL], d_lng, d_lnb, d_qn, d_kn, d_ffn, d_ple, d_fin]
    packed, unpack = _pack_small(small)
    g_mix, g_taps, g_lng, g_lnb, g_qn, g_kn, g_ffn, g_ple, g_fin = unpack(_all_sum_small("reduce_small", packed))
    g_dw = lax.dynamic_slice_in_dim(g_taps.reshape(CONV_KERNEL, N_CHIPS, cpc), chip, 1, axis=1).reshape(1, CONV_KERNEL, cpc)

    sq_local = lax.reduce_precision(sq[0, 0], 8, 23)
    loss = (0.5 / d) * lax.psum(sq_local, ("x", "y", "c"))

    grads = {
        "norm_mix": g_mix, "w_in": big_grads[0][None], "w_dw": g_dw, "conv_ln_g": g_lng, "conv_ln_b": g_lnb,
        "w_conv_proj": big_grads[1][None], "q_norm": g_qn, "k_norm": g_kn, "w_attn_proj": big_grads[2][None],
        "w_out": big_grads[3][None], "norm_ffn": g_ffn, "w_ff1": big_grads[4][None], "w_ff2": big_grads[5][None],
        "norm_ple": g_ple, "w_ple_gate": big_grads[6][None], "w_ple_proj": big_grads[7][None],
        "norm_final": g_fin.reshape(d),
    }
    weights = dict(norm_mix=norm_mix, w_in=w_in, w_dw=w_dw, conv_ln_g=conv_ln_g, conv_ln_b=conv_ln_b, w_conv_proj=w_conv_proj,
                   q_norm=q_norm, k_norm=k_norm, w_attn_proj=w_attn_proj, w_out=w_out, norm_ffn=norm_ffn, w_ff1=w_ff1,
                   w_ff2=w_ff2, norm_ple=norm_ple, w_ple_gate=w_ple_gate, w_ple_proj=w_ple_proj, norm_final=norm_final)
    m_in = dict(norm_mix=m_norm_mix, w_in=m_w_in, w_dw=m_w_dw, conv_ln_g=m_conv_ln_g, conv_ln_b=m_conv_ln_b,
                w_conv_proj=m_w_conv_proj, q_norm=m_q_norm, k_norm=m_k_norm, w_attn_proj=m_w_attn_proj, w_out=m_w_out,
                norm_ffn=m_norm_ffn, w_ff1=m_w_ff1, w_ff2=m_w_ff2, norm_ple=m_norm_ple, w_ple_gate=m_w_ple_gate,
                w_ple_proj=m_w_ple_proj, norm_final=m_norm_final)
    v_in = dict(norm_mix=v_norm_mix, w_in=v_w_in, w_dw=v_w_dw, conv_ln_g=v_conv_ln_g, conv_ln_b=v_conv_ln_b,
                w_conv_proj=v_w_conv_proj, q_norm=v_q_norm, k_norm=v_k_norm, w_attn_proj=v_w_attn_proj, w_out=v_w_out,
                norm_ffn=v_norm_ffn, w_ff1=v_w_ff1, w_ff2=v_w_ff2, norm_ple=v_norm_ple, w_ple_gate=v_w_ple_gate,
                w_ple_proj=v_w_ple_proj, norm_final=v_norm_final)
    order = list(weights)
    deltas, new_m, new_v, g_out = [], [], [], []
    for nm in order:
        w = weights[nm]
        shape = w.shape
        two_d = (-1, shape[-1])
        dl, mm_, vv_ = _adamw("adamw_" + nm, w.reshape(two_d), grads[nm].reshape(two_d), m_in[nm].reshape(two_d),
                              v_in[nm].reshape(two_d))
        g_out.append(grads[nm].reshape(shape))
        deltas.append(dl.reshape(shape))
        new_m.append(mm_.reshape(shape))
        new_v.append(vv_.reshape(shape))
    return (loss, dx[None], *g_out, *deltas, *new_m, *new_v)
```

```python
import jax
import jax.numpy as jnp
from jax import lax
from jax.experimental import pallas as pl
from jax.experimental.pallas import tpu as pltpu

F32 = jnp.float32
BF = jnp.bfloat16

EPS = 1e-6
HEAD_DIM = 128
GROUP = 4
GRID_W = 64
ROPE_THETA = 10000.0
CONV_KERNEL = 31
HALO = 16
N_CHIPS = 4
N_DEV = 8
LANES = 128

ADAM_LR = 0.001
ADAM_B1 = 0.9
ADAM_B2 = 0.999
ADAM_EPS = 1e-08
ADAM_WD = 0.01
ADAM_STEP = 10

VMEM_LIMIT = 56 * 2 ** 20
LOG2E = 1.4426950408889634
LN2 = 0.6931471805599453
Q_SCALE = HEAD_DIM ** -0.5 * LOG2E
ROW_TILE = 256
FLASH_TQ = 256
FLASH_TK = 512
MM_TM = 1024
MM_TM_EPI = 512
MESH = pl.DeviceIdType.MESH
ANY = pl.BlockSpec(memory_space=pl.ANY)


def _params(sem):
    return pltpu.CompilerParams(dimension_semantics=sem, vmem_limit_bytes=VMEM_LIMIT)


def _sigmoid(x):
    return 1.0 / (1.0 + jnp.exp(-x))


def _mm(name, a, b, *, tm, tn, tk, ta=False, tb=False, b_cm=False, out_cm=False,
        a_fn=None, extras=(), epi=None, out_dtypes=(BF,), epi_rows=256):
    if ta:
        kc, m = a.shape
    else:
        m, kc = a.shape
    if b_cm:
        nc, r, c = b.shape
        n, per = (r, c) if tb else (nc * c, c)
    else:
        n = b.shape[0] if tb else b.shape[1]
    tm, tn, tk = min(tm, m), min(tn, n), min(tk, kc)
    assert m % tm == 0 and n % tn == 0 and kc % tk == 0, (name, m, n, kc, tm, tn, tk)
    nk = kc // tk
    a_spec = pl.BlockSpec((tk, tm), lambda i, j, k: (k, i)) if ta else pl.BlockSpec((tm, tk), lambda i, j, k: (i, k))
    if b_cm and not tb:
        assert per % tn == 0
        npj = per // tn
        b_spec = pl.BlockSpec((None, tk, tn), lambda i, j, k: (j // npj, k, j % npj))
    elif b_cm:
        assert per % tk == 0
        npk = per // tk
        b_spec = pl.BlockSpec((None, tn, tk), lambda i, j, k: (k // npk, j, k % npk))
    elif tb:
        b_spec = pl.BlockSpec((tn, tk), lambda i, j, k: (j, k))
    else:
        b_spec = pl.BlockSpec((tk, tn), lambda i, j, k: (k, j))
    if out_cm:
        assert (n // N_CHIPS) % tn == 0
        npo = (n // N_CHIPS) // tn
        o_spec = pl.BlockSpec((None, tm, tn), lambda i, j, k: (j // npo, i, j % npo))
        o_shape = (N_CHIPS, m, n // N_CHIPS)
    else:
        o_spec = pl.BlockSpec((tm, tn), lambda i, j, k: (i, j))
        o_shape = (m, n)
    ne, no = len(extras), len(out_dtypes)
    dims = (((0 if ta else 1,), (1 if tb else 0,)), ((), ()))
    use_acc = nk > 1 or epi is not None
    er = min(epi_rows, tm)

    def body(*refs):
        a_ref, b_ref = refs[0], refs[1]
        ex = refs[2:2 + ne]
        outs = refs[2 + ne:2 + ne + no]
        at = a_ref[...]
        if a_fn is not None:
            at = a_fn(at)
        d = lax.dot_general(at, b_ref[...], dims, preferred_element_type=F32)
        if not use_acc:
            outs[0][...] = d.astype(out_dtypes[0])
            return
        acc = refs[-1]
        k = pl.program_id(2)

        @pl.when(k == 0)
        def _():
            acc[...] = d

        if nk > 1:
            @pl.when(k > 0)
            def _():
                acc[...] += d

        @pl.when(k == nk - 1)
        def _():
            for r0 in range(0, tm, er):
                rows = slice(r0, r0 + er)
                if epi is None:
                    vals = (acc[rows, :],)
                else:
                    vals = epi(acc[rows, :], *[e[rows, :] for e in ex])
                for o, v, dt in zip(outs, vals, out_dtypes):
                    o[rows, :] = v.astype(dt)

    return pl.pallas_call(
        body, name=name,
        grid=(m // tm, n // tn, nk),
        in_specs=[a_spec, b_spec] + [pl.BlockSpec(bs, im) for _, bs, im in extras],
        out_specs=[o_spec] * no,
        out_shape=[jax.ShapeDtypeStruct(o_shape, dt) for dt in out_dtypes],
        scratch_shapes=[pltpu.VMEM((tm, tn), F32)] if use_acc else [],
        compiler_params=_params(("parallel", "parallel", "arbitrary")),
    )(a, b, *[e for e, _, _ in extras])


def _tile_extra(arr, tm, tn, col_block0=0):
    return (arr, (tm, tn), lambda i, j, k: (i, j + col_block0))


def _rms_fwd(name, x, g, ts=None):
    s, d = x.shape
    ts = ts or ROW_TILE

    def body(x_ref, g_ref, h_ref):
        xv = x_ref[...]
        r = lax.rsqrt(jnp.mean(xv * xv, axis=-1, keepdims=True) + EPS)
        h_ref[...] = (xv * r * g_ref[...]).astype(BF)

    return pl.pallas_call(
        body, name=name, grid=(s // ts,),
        in_specs=[pl.BlockSpec((ts, d), lambda i: (i, 0)), pl.BlockSpec((1, d), lambda i: (0, 0))],
        out_specs=pl.BlockSpec((ts, d), lambda i: (i, 0)),
        out_shape=jax.ShapeDtypeStruct((s, d), BF),
        compiler_params=_params(("parallel",)),
    )(x, g)


def _rms_bwd(name, dh, x, g, dres, ts=None):
    s, d = x.shape
    ts = ts or ROW_TILE

    def body(dh_ref, x_ref, g_ref, dres_ref, dx_ref, dxb_ref, dg_ref):
        xv = x_ref[...]
        dhv = dh_ref[...].astype(F32)
        r = lax.rsqrt(jnp.mean(xv * xv, axis=-1, keepdims=True) + EPS)
        nrm = xv * r
        dn = dhv * g_ref[...]
        dx = dres_ref[...] + r * (dn - nrm * jnp.mean(dn * nrm, axis=-1, keepdims=True))
        dx_ref[...] = dx
        dxb_ref[...] = dx.astype(BF)
        part = jnp.sum(dhv * nrm, axis=0, keepdims=True)

        @pl.when(pl.program_id(0) == 0)
        def _():
            dg_ref[...] = part

        @pl.when(pl.program_id(0) > 0)
        def _():
            dg_ref[...] += part

    row = pl.BlockSpec((ts, d), lambda i: (i, 0))
    vec = pl.BlockSpec((1, d), lambda i: (0, 0))
    return pl.pallas_call(
        body, name=name, grid=(s // ts,),
        in_specs=[row, row, vec, row],
        out_specs=[row, row, vec],
        out_shape=[jax.ShapeDtypeStruct((s, d), F32), jax.ShapeDtypeStruct((s, d), BF), jax.ShapeDtypeStruct((1, d), F32)],
        compiler_params=_params(("arbitrary",)),
    )(dh, x, g, dres)


def _loss_bwd(x3, tgt, gfin, e, gate, ts=None):
    s, d = x3.shape
    ts = ts or ROW_TILE

    def body(x_ref, t_ref, g_ref, e_ref, gate_ref, dx_ref, de_ref, dgp_ref, sq_ref, dg_ref):
        xv = x_ref[...]
        gv = g_ref[...]
        r = lax.rsqrt(jnp.mean(xv * xv, axis=-1, keepdims=True) + EPS)
        nrm = xv * r
        err = nrm * gv - t_ref[...]
        dy = err * (1.0 / d)
        dn = dy * gv
        dx = r * (dn - nrm * jnp.mean(dn * nrm, axis=-1, keepdims=True))
        dx_ref[...] = dx
        ev = e_ref[...].astype(F32)
        gt = gate_ref[...].astype(F32)
        de_ref[...] = (dx * gt).astype(BF)
        dgp_ref[...] = (dx * ev * gt * (1.0 - gt)).astype(BF)
        sq = jnp.full((8, LANES), jnp.sum(err * err), F32)
        part = jnp.sum(dy * nrm, axis=0, keepdims=True)

        @pl.when(pl.program_id(0) == 0)
        def _():
            sq_ref[...] = sq
            dg_ref[...] = part

        @pl.when(pl.program_id(0) > 0)
        def _():
            sq_ref[...] += sq
            dg_ref[...] += part

    row = pl.BlockSpec((ts, d), lambda i: (i, 0))
    vec = pl.BlockSpec((1, d), lambda i: (0, 0))
    return pl.pallas_call(
        body, name="loss_bwd", grid=(s // ts,),
        in_specs=[row, row, vec, row, row],
        out_specs=[row, row, row, pl.BlockSpec((8, LANES), lambda i: (0, 0)), vec],
        out_shape=[jax.ShapeDtypeStruct((s, d), F32), jax.ShapeDtypeStruct((s, d), BF), jax.ShapeDtypeStruct((s, d), BF),
                   jax.ShapeDtypeStruct((8, LANES), F32), jax.ShapeDtypeStruct((1, d), F32)],
        compiler_params=_params(("arbitrary",)),
    )(x3, tgt, gfin, e, gate)


def _halo_specs(ts, s, width, col_block):
    per = ts // HALO
    last = s // HALO - 1
    return [
        pl.BlockSpec((HALO, width), lambda i: (jnp.maximum(i * per - 1, 0), col_block)),
        pl.BlockSpec((ts, width), lambda i: (i, col_block)),
        pl.BlockSpec((HALO, width), lambda i: (jnp.minimum((i + 1) * per, last), col_block)),
    ]


def _glu_ext(zp, zc, zn, ext, cw, ts, i, n_tiles):
    def glu(zr):
        zv = zr[...].astype(F32)
        return zv[:, :cw] * _sigmoid(zv[:, cw:])

    ext[0:HALO, :] = jnp.where(i > 0, glu(zp), 0.0)
    ext[HALO:HALO + ts, :] = glu(zc)
    ext[HALO + ts:, :] = jnp.where(i < n_tiles - 1, glu(zn), 0.0)


SUBLANES = 8


def _shift_scratch(ts):
    return pltpu.VMEM((SUBLANES, ts + 2 * HALO - SUBLANES, LANES), F32)


def _shifted_copies(ext, sh, cols, ts):
    n = ts + 2 * HALO - SUBLANES
    for r in range(SUBLANES):
        sh[r] = ext[r:r + n, cols]


def _tap_rows(sh, off, ts):
    q, r = divmod(off, SUBLANES)
    return sh[r, q * SUBLANES:q * SUBLANES + ts, :]


def _ln_stats(uc):
    mu = jnp.mean(uc, axis=-1, keepdims=True)
    xc = uc - mu
    rstd = lax.rsqrt(jnp.mean(xc * xc, axis=-1, keepdims=True) + EPS)
    return xc * rstd, rstd


def _conv_fwd(z, wdw, ln_g, ln_b, cw, ts=None):
    s = z.shape[0]
    ts = ts or ROW_TILE
    n_tiles = s // ts
    pad = CONV_KERNEL // 2

    def body(zp, zc, zn, w_ref, g_ref, b_ref, uc_ref, act_ref, ext, sh):
        i = pl.program_id(0)
        _glu_ext(zp, zc, zn, ext, cw, ts, i, n_tiles)

        def col_block(cb, carry):
            cols = pl.ds(pl.multiple_of(cb * LANES, LANES), LANES)
            _shifted_copies(ext, sh, cols, ts)
            acc = jnp.zeros((ts, LANES), F32)
            for j in range(CONV_KERNEL):
                acc = acc + _tap_rows(sh, HALO - pad + j, ts) * w_ref[j:j + 1, cols]
            uc_ref[:, cols] = acc
            return carry

        lax.fori_loop(0, cw // LANES, col_block, 0)
        xhat, _ = _ln_stats(uc_ref[...])
        ln = xhat * g_ref[...] + b_ref[...]
        act_ref[...] = (ln * _sigmoid(ln)).astype(BF)

    vec = pl.BlockSpec((1, cw), lambda i: (0, 0))
    row = pl.BlockSpec((ts, cw), lambda i: (i, 0))
    return pl.pallas_call(
        body, name="conv_fwd", grid=(n_tiles,),
        in_specs=_halo_specs(ts, s, 2 * cw, 0) + [pl.BlockSpec((32, cw), lambda i: (0, 0)), vec, vec],
        out_specs=[row, row],
        out_shape=[jax.ShapeDtypeStruct((s, cw), F32), jax.ShapeDtypeStruct((s, cw), BF)],
        scratch_shapes=[pltpu.VMEM((ts + 2 * HALO, cw), F32), _shift_scratch(ts)],
        compiler_params=_params(("parallel",)),
    )(z, z, z, wdw, ln_g, ln_b)


def _conv_bwd(ds, uc, z, wdw, ln_g, ln_b, cw, ts=None):
    s = z.shape[0]
    ts = ts or ROW_TILE
    n_tiles = s // ts
    pad = CONV_KERNEL // 2

    def body(zp, zc, zn, dsp, dsc, dsn, ucp, ucc, ucn, w_ref, g_ref, b_ref,
             dz_ref, dw_ref, dg_ref, db_ref, ext, dext, sh, dsh):
        i = pl.program_id(0)
        gv, bv = g_ref[...], b_ref[...]

        def ln_bwd(ds_r, uc_r):
            xhat, rstd = _ln_stats(uc_r[...])
            ln = xhat * gv + bv
            sg = _sigmoid(ln)
            dln = ds_r[...].astype(F32) * (sg * (1.0 + ln * (1.0 - sg)))
            dxh = dln * gv
            duc = rstd * (dxh - jnp.mean(dxh, axis=-1, keepdims=True) - xhat * jnp.mean(dxh * xhat, axis=-1, keepdims=True))
            return duc, dln, xhat

        duc_p, _, _ = ln_bwd(dsp, ucp)
        duc_c, dln_c, xhat_c = ln_bwd(dsc, ucc)
        duc_n, _, _ = ln_bwd(dsn, ucn)
        dext[0:HALO, :] = jnp.where(i > 0, duc_p, 0.0)
        dext[HALO:HALO + ts, :] = duc_c
        dext[HALO + ts:, :] = jnp.where(i < n_tiles - 1, duc_n, 0.0)
        _glu_ext(zp, zc, zn, ext, cw, ts, i, n_tiles)

        dg_part = jnp.sum(dln_c * xhat_c, axis=0, keepdims=True)
        db_part = jnp.sum(dln_c, axis=0, keepdims=True)

        @pl.when(i == 0)
        def _():
            dw_ref[...] = jnp.zeros_like(dw_ref)
            dg_ref[...] = dg_part
            db_ref[...] = db_part

        @pl.when(i > 0)
        def _():
            dg_ref[...] += dg_part
            db_ref[...] += db_part

        def col_block(cb, carry):
            c0 = pl.multiple_of(cb * LANES, LANES)
            cols, gate_cols = pl.ds(c0, LANES), pl.ds(cw + c0, LANES)
            _shifted_copies(dext, dsh, cols, ts)
            _shifted_copies(ext, sh, cols, ts)
            du = jnp.zeros((ts, LANES), F32)
            for j in range(CONV_KERNEL):
                du = du + _tap_rows(dsh, HALO + pad - j, ts) * w_ref[j:j + 1, cols]
            ca, sb = zc[:, cols].astype(F32), _sigmoid(zc[:, gate_cols].astype(F32))
            dz_ref[:, cols] = (du * sb).astype(BF)
            dz_ref[:, gate_cols] = (du * ca * sb * (1.0 - sb)).astype(BF)
            duc_blk = _tap_rows(dsh, HALO, ts)
            for j in range(CONV_KERNEL):
                dw_ref[j:j + 1, cols] += jnp.sum(_tap_rows(sh, HALO - pad + j, ts) * duc_blk, axis=0, keepdims=True)
            return carry

        lax.fori_loop(0, cw // LANES, col_block, 0)

    vec = pl.BlockSpec((1, cw), lambda i: (0, 0))
    wsp = pl.BlockSpec((32, cw), lambda i: (0, 0))
    return pl.pallas_call(
        body, name="conv_bwd", grid=(n_tiles,),
        in_specs=_halo_specs(ts, s, 2 * cw, 0) + _halo_specs(ts, s, cw, 0) + _halo_specs(ts, s, cw, 0) + [wsp, vec, vec],
        out_specs=[pl.BlockSpec((ts, 2 * cw), lambda i: (i, 0)), wsp, vec, vec],
        out_shape=[jax.ShapeDtypeStruct((s, 2 * cw), BF), jax.ShapeDtypeStruct((32, cw), F32),
                   jax.ShapeDtypeStruct((1, cw), F32), jax.ShapeDtypeStruct((1, cw), F32)],
        scratch_shapes=[pltpu.VMEM((ts + 2 * HALO, cw), F32), pltpu.VMEM((ts + 2 * HALO, cw), F32),
                        _shift_scratch(ts), _shift_scratch(ts)],
        compiler_params=_params(("arbitrary",)),
    )(z, z, z, ds, ds, ds, uc, uc, uc, wdw, ln_g, ln_b)


def _rope_tables(s):
    axis_dim = HEAD_DIM // 2
    t = jnp.arange(s, dtype=jnp.int32)
    row = (t // GRID_W).astype(F32)[:, None]
    col = (t % GRID_W).astype(F32)[:, None]
    inv_freq = ROPE_THETA ** (-jnp.arange(0, axis_dim, 2, dtype=F32) / axis_dim)[None, :]
    ar, ac = row * inv_freq, col * inv_freq
    cos = jnp.concatenate([jnp.cos(ar), jnp.cos(ar), jnp.cos(ac), jnp.cos(ac)], axis=-1)
    sin = jnp.concatenate([-jnp.sin(ar), jnp.sin(ar), -jnp.sin(ac), jnp.sin(ac)], axis=-1)
    return cos, sin


def _swap_quarters(x):
    q = HEAD_DIM // 4
    lane = lax.broadcasted_iota(jnp.int32, x.shape, 1)
    return jnp.where((lane % (2 * q)) < q, pltpu.roll(x, HEAD_DIM - q, 1), pltpu.roll(x, q, 1))


def _qk_fwd(z, cos, sin, qg, kg, d, ts=None):
    s = z.shape[0]
    ts = ts or ROW_TILE
    kvw = d // GROUP
    scale = Q_SCALE

    def body(q_ref, k_ref, c_ref, s_ref, qg_ref, kg_ref, qo_ref, ko_ref):
        cv, sv = c_ref[...], s_ref[...]

        def head(x_ref, g_ref, o_ref, h, mul):
            xv = x_ref[:, h * HEAD_DIM:(h + 1) * HEAD_DIM].astype(F32)
            r = lax.rsqrt(jnp.mean(xv * xv, axis=-1, keepdims=True) + EPS)
            nrm = xv * r * g_ref[...]
            out = nrm * cv + _swap_quarters(nrm) * sv
            o_ref[:, h * HEAD_DIM:(h + 1) * HEAD_DIM] = (out * mul).astype(BF)

        for h in range(d // HEAD_DIM):
            head(q_ref, qg_ref, qo_ref, h, scale)
        for h in range(kvw // HEAD_DIM):
            head(k_ref, kg_ref, ko_ref, h, 1.0)

    cw2 = d
    tab = pl.BlockSpec((ts, HEAD_DIM), lambda i: (i, 0))
    vec = pl.BlockSpec((1, HEAD_DIM), lambda i: (0, 0))
    return pl.pallas_call(
        body, name="qk_fwd", grid=(s // ts,),
        in_specs=[pl.BlockSpec((ts, d), lambda i: (i, cw2 // d)),
                  pl.BlockSpec((ts, kvw), lambda i: (i, (cw2 + d) // kvw)), tab, tab, vec, vec],
        out_specs=[pl.BlockSpec((ts, d), lambda i: (i, 0)), pl.BlockSpec((ts, kvw), lambda i: (i, 0))],
        out_shape=[jax.ShapeDtypeStruct((s, d), BF), jax.ShapeDtypeStruct((s, kvw), BF)],
        compiler_params=_params(("parallel",)),
    )(z, z, cos, sin, qg, kg)


def _qk_bwd(dqt, dkt, z, cos, sin, qg, kg, d, ts=None):
    s = z.shape[0]
    ts = ts or ROW_TILE
    kvw = d // GROUP
    scale = HEAD_DIM ** -0.5

    def body(dq_ref, dk_ref, q_ref, k_ref, c_ref, s_ref, qg_ref, kg_ref, dqo_ref, dko_ref, dqg_ref, dkg_ref):
        cv, sv = c_ref[...], s_ref[...]

        def head(dy_ref, x_ref, g_ref, o_ref, h, mul):
            dout = dy_ref[:, h * HEAD_DIM:(h + 1) * HEAD_DIM].astype(F32) * mul
            dn = dout * cv + _swap_quarters(dout * sv)
            xv = x_ref[:, h * HEAD_DIM:(h + 1) * HEAD_DIM].astype(F32)
            r = lax.rsqrt(jnp.mean(xv * xv, axis=-1, keepdims=True) + EPS)
            nh = xv * r
            dnh = dn * g_ref[...]
            o_ref[:, h * HEAD_DIM:(h + 1) * HEAD_DIM] = (r * (dnh - nh * jnp.mean(dnh * nh, axis=-1, keepdims=True))).astype(BF)
            return jnp.sum(dn * nh, axis=0, keepdims=True)

        dqg = jnp.zeros((1, HEAD_DIM), F32)
        for h in range(d // HEAD_DIM):
            dqg = dqg + head(dq_ref, q_ref, qg_ref, dqo_ref, h, scale)
        dkg = jnp.zeros((1, HEAD_DIM), F32)
        for h in range(kvw // HEAD_DIM):
            dkg = dkg + head(dk_ref, k_ref, kg_ref, dko_ref, h, LN2)

        @pl.when(pl.program_id(0) == 0)
        def _():
            dqg_ref[...] = dqg
            dkg_ref[...] = dkg

        @pl.when(pl.program_id(0) > 0)
        def _():
            dqg_ref[...] += dqg
            dkg_ref[...] += dkg

    cw2 = d
    tab = pl.BlockSpec((ts, HEAD_DIM), lambda i: (i, 0))
    vec = pl.BlockSpec((1, HEAD_DIM), lambda i: (0, 0))
    qrow = pl.BlockSpec((ts, d), lambda i: (i, 0))
    krow = pl.BlockSpec((ts, kvw), lambda i: (i, 0))
    return pl.pallas_call(
        body, name="qk_bwd", grid=(s // ts,),
        in_specs=[qrow, krow, pl.BlockSpec((ts, d), lambda i: (i, cw2 // d)),
                  pl.BlockSpec((ts, kvw), lambda i: (i, (cw2 + d) // kvw)), tab, tab, vec, vec],
        out_specs=[qrow, krow, vec, vec],
        out_shape=[jax.ShapeDtypeStruct((s, d), BF), jax.ShapeDtypeStruct((s, kvw), BF),
                   jax.ShapeDtypeStruct((1, HEAD_DIM), F32), jax.ShapeDtypeStruct((1, HEAD_DIM), F32)],
        compiler_params=_params(("arbitrary",)),
    )(dqt, dkt, z, z, cos, sin, qg, kg)


_NT = (((1,), (1,)), ((), ()))
_TN = (((0,), (0,)), ((), ()))


def _v_col_block(d):
    return (2 * d + d // GROUP) // HEAD_DIM


def _flash_fwd(qt, kt, z, d, tq=None, tk=None):
    s = qt.shape[0]
    tq, tk = min(tq or FLASH_TQ, s), min(tk or FLASH_TK, s)
    ng, nq, nk = d // (GROUP * HEAD_DIM), s // tq, s // tk
    gw = GROUP * HEAD_DIM
    rows = GROUP * tq

    nt = tk // LANES
    assert nk % 2 == 0, (s, tk)

    def body(q_ref, k_ref, v_ref, o_ref, lse_ref, qs, v1, p_s, m_s, acc_s, sc_s):
        @pl.when(pl.program_id(1) == 0)
        def _():
            v1[:, :HEAD_DIM] = v_ref[...]
            v1[:, HEAD_DIM:] = jnp.ones((s, HEAD_DIM), BF)

        for h in range(GROUP):
            qs[h * tq:(h + 1) * tq, :] = q_ref[:, h * HEAD_DIM:(h + 1) * HEAD_DIM]
        m_s[...] = jnp.full((rows, LANES), -1e30, F32)
        acc_s[...] = jnp.zeros((rows, 2 * HEAD_DIM), F32)

        def scores(j):
            return lax.dot_general(qs[...], k_ref[pl.ds(pl.multiple_of(j * tk, tk), tk), :], _NT, preferred_element_type=F32)

        def softmax_pv(j, sc):
            kv_rows = pl.ds(pl.multiple_of(j * tk, tk), tk)
            mt = sc[:, :LANES]
            for c in range(1, nt):
                mt = jnp.maximum(mt, sc[:, c * LANES:(c + 1) * LANES])
            m_old = m_s[...]
            m_new = jnp.maximum(m_old, jnp.max(mt, axis=-1, keepdims=True))
            alpha = jnp.exp2(m_old - m_new)
            for c in range(nt):
                cs = slice(c * LANES, (c + 1) * LANES)
                p_s[:, cs] = jnp.exp2(sc[:, cs] - m_new).astype(BF)
            pv = jnp.dot(p_s[...], v1[kv_rows, :], preferred_element_type=F32)
            acc_s[:, :HEAD_DIM] = alpha * acc_s[:, :HEAD_DIM] + pv[:, :HEAD_DIM]
            acc_s[:, HEAD_DIM:] = alpha * acc_s[:, HEAD_DIM:] + pv[:, HEAD_DIM:]
            m_s[...] = m_new

        sc_s[0] = scores(0)

        def step(jj, carry):
            j = 2 * jj
            sc_s[1] = scores(j + 1)
            softmax_pv(j, sc_s[0])
            sc_s[0] = scores(jnp.minimum(j + 2, nk - 1))
            softmax_pv(j + 1, sc_s[1])
            return carry

        lax.fori_loop(0, nk // 2, step, 0)
        l = acc_s[:, HEAD_DIM:]
        o = acc_s[:, :HEAD_DIM] / l
        for h in range(GROUP):
            o_ref[:, h * HEAD_DIM:(h + 1) * HEAD_DIM] = o[h * tq:(h + 1) * tq, :].astype(BF)
        lse_ref[...] = m_s[...] + jnp.log2(l)

    vb = _v_col_block(d)
    return pl.pallas_call(
        body, name="flash_fwd", grid=(ng, nq),
        in_specs=[pl.BlockSpec((tq, gw), lambda g, i: (i, g)),
                  pl.BlockSpec((s, HEAD_DIM), lambda g, i: (0, g)),
                  pl.BlockSpec((s, HEAD_DIM), lambda g, i: (0, vb + g))],
        out_specs=[pl.BlockSpec((tq, gw), lambda g, i: (i, g)),
                   pl.BlockSpec((rows, LANES), lambda g, i: (g * nq + i, 0))],
        out_shape=[jax.ShapeDtypeStruct((s, d), BF), jax.ShapeDtypeStruct((ng * nq * rows, LANES), F32)],
        scratch_shapes=[pltpu.VMEM((rows, HEAD_DIM), BF), pltpu.VMEM((s, 2 * HEAD_DIM), BF), pltpu.VMEM((rows, tk), BF),
                        pltpu.VMEM((rows, LANES), F32), pltpu.VMEM((rows, 2 * HEAD_DIM), F32), pltpu.VMEM((2, rows, tk), F32)],
        compiler_params=_params(("parallel", "arbitrary")),
    )(qt, kt, z)


def _flash_bwd(qt, kt, z, o, do, lse, d, tq=None, tk=None):
    s = qt.shape[0]
    tq, tk = min(tq or FLASH_TQ, s), min(tk or FLASH_TK, s)
    ng, nq, nk = d // (GROUP * HEAD_DIM), s // tq, s // tk
    gw = GROUP * HEAD_DIM
    rows = GROUP * tq

    nt = tk // LANES

    def body(q_ref, k_ref, v_ref, o_ref, do_ref, lse_ref, dq_ref, dk_ref, dv_ref, qs, dos, delta_s, dq_s, p_s, ds_s):
        i = pl.program_id(1)
        for h in range(GROUP):
            cols = slice(h * HEAD_DIM, (h + 1) * HEAD_DIM)
            qs[h * tq:(h + 1) * tq, :] = q_ref[:, cols]
            dov = do_ref[:, cols]
            dos[h * tq:(h + 1) * tq, :] = dov
            delta = jnp.sum(dov.astype(F32) * o_ref[:, cols].astype(F32), axis=-1, keepdims=True)
            delta_s[h * tq:(h + 1) * tq, :] = jnp.broadcast_to(delta, (tq, LANES))
        dq_s[...] = jnp.zeros((rows, HEAD_DIM), F32)

        @pl.when(i == 0)
        def _():
            dk_ref[...] = jnp.zeros_like(dk_ref)
            dv_ref[...] = jnp.zeros_like(dv_ref)

        def step(j, carry):
            kv_rows = pl.ds(pl.multiple_of(j * tk, tk), tk)
            kv, vv = k_ref[kv_rows, :], v_ref[kv_rows, :]
            sc = lax.dot_general(qs[...], kv, _NT, preferred_element_type=F32)
            dp = lax.dot_general(dos[...], vv, _NT, preferred_element_type=F32)
            lse, delta = lse_ref[...], delta_s[...]
            for c in range(nt):
                cs = slice(c * LANES, (c + 1) * LANES)
                p = jnp.exp2(sc[:, cs] - lse)
                p_s[:, cs] = p.astype(BF)
                ds_s[:, cs] = (p * (dp[:, cs] - delta)).astype(BF)
            dv_ref[kv_rows, :] += lax.dot_general(p_s[...], dos[...], _TN, preferred_element_type=F32)
            dk_ref[kv_rows, :] += lax.dot_general(ds_s[...], qs[...], _TN, preferred_element_type=F32)
            dq_s[...] += jnp.dot(ds_s[...], kv, preferred_element_type=F32)
            return carry

        lax.fori_loop(0, nk, step, 0)
        for h in range(GROUP):
            dq_ref[:, h * HEAD_DIM:(h + 1) * HEAD_DIM] = dq_s[h * tq:(h + 1) * tq, :].astype(BF)

    vb = _v_col_block(d)
    qspec = pl.BlockSpec((tq, gw), lambda g, i: (i, g))
    kspec = pl.BlockSpec((s, HEAD_DIM), lambda g, i: (0, g))
    return pl.pallas_call(
        body, name="flash_bwd", grid=(ng, nq),
        in_specs=[qspec, kspec, pl.BlockSpec((s, HEAD_DIM), lambda g, i: (0, vb + g)), qspec, qspec,
                  pl.BlockSpec((rows, LANES), lambda g, i: (g * nq + i, 0))],
        out_specs=[qspec, kspec, kspec],
        out_shape=[jax.ShapeDtypeStruct((s, d), BF), jax.ShapeDtypeStruct((s, d // GROUP), F32),
                   jax.ShapeDtypeStruct((s, d // GROUP), F32)],
        scratch_shapes=[pltpu.VMEM((rows, HEAD_DIM), BF), pltpu.VMEM((rows, HEAD_DIM), BF), pltpu.VMEM((rows, LANES), F32),
                        pltpu.VMEM((rows, HEAD_DIM), F32), pltpu.VMEM((rows, tk), BF), pltpu.VMEM((rows, tk), BF)],
        compiler_params=_params(("parallel", "arbitrary")),
    )(qt, kt, z, o, do, lse)


def _place():
    x, y, c = lax.axis_index("x"), lax.axis_index("y"), lax.axis_index("c")
    other_chips = [(1 - x, y), (x, 1 - y), (1 - x, 1 - y)]
    return x, y, c, other_chips


def _cast_place(name, w, chip_arr, tr=256):
    r, cc = w.shape
    tr = min(tr, r)

    def body(p_ref, w_ref, o_ref):
        o_ref[...] = w_ref[...].astype(BF)

    return pl.pallas_call(
        body, name=name,
        grid_spec=pltpu.PrefetchScalarGridSpec(
            num_scalar_prefetch=1, grid=(r // tr,),
            in_specs=[pl.BlockSpec((tr, cc), lambda i, p_ref: (i, 0))],
            out_specs=pl.BlockSpec((None, tr, cc), lambda i, p_ref: (p_ref[0], i, 0))),
        out_shape=jax.ShapeDtypeStruct((N_CHIPS, r, cc), BF),
        compiler_params=_params(("parallel",)),
    )(chip_arr, w)


def _gather_weights(bufs):
    n = len(bufs)

    def body(*refs):
        dst = refs[n:2 * n]
        send, recv, fsend, frecv = refs[2 * n:]
        x, y, c, chips = _place()

        def part(w, chip, core_half):
            h = bufs[w].shape[1] // 2
            return dst[w].at[2 * chip[0] + chip[1], pl.ds(core_half * h, h)]

        def ici(w, j, chip_from, core_to):
            slab = part(w, chip_from, c)
            return pltpu.make_async_remote_copy(
                src_ref=slab, dst_ref=slab, send_sem=send.at[3 * w + j], recv_sem=recv.at[3 * w + j],
                device_id=core_to, device_id_type=MESH)

        def d2d(w, j, chip, core_half):
            slab = part(w, chip, core_half)
            return pltpu.make_async_remote_copy(
                src_ref=slab, dst_ref=slab, send_sem=fsend.at[3 * w + j], recv_sem=frecv.at[3 * w + j],
                device_id=(x, y, 1 - c), device_id_type=MESH)

        first = [ici(w, j, (x, y), (*chip, c)) for w in range(n) for j, chip in enumerate(chips)]
        for cp in first:
            cp.start()
        passed = []
        for w in range(n):
            for j, chip in enumerate(chips):
                ici(w, j, chip, (x, y, c)).wait_recv()
                cp = d2d(w, j, chip, c)
                cp.start()
                passed.append(cp)
        for w in range(n):
            for j, chip in enumerate(chips):
                d2d(w, j, chip, 1 - c).wait_recv()
        for cp in first + passed:
            cp.wait_send()

    return pl.pallas_call(
        body, name="gather_weights",
        in_specs=[ANY] * n, out_specs=[ANY] * n,
        out_shape=[jax.ShapeDtypeStruct(b.shape, b.dtype) for b in bufs],
        input_output_aliases={w: w for w in range(n)},
        scratch_shapes=[pltpu.SemaphoreType.DMA((3 * n,))] * 4,
    )(*bufs)


def _pair_exchange(grads):
    n = len(grads)

    def body(*refs):
        src, dst = refs[:n], refs[n:2 * n]
        send, recv = refs[2 * n:]
        x, y, c, _ = _place()
        copies = []
        for w in range(n):
            h = grads[w].shape[1] // 2
            copies.append(pltpu.make_async_remote_copy(
                src_ref=src[w].at[:, pl.ds((1 - c) * h, h), :], dst_ref=dst[w],
                send_sem=send.at[w], recv_sem=recv.at[w], device_id=(x, y, 1 - c), device_id_type=MESH))
        for cp in copies:
            cp.start()
        for cp in copies:
            cp.wait()

    return pl.pallas_call(
        body, name="grad_pair_exchange",
        in_specs=[ANY] * n, out_specs=[ANY] * n,
        out_shape=[jax.ShapeDtypeStruct((N_CHIPS, g.shape[1] // 2, g.shape[2]), g.dtype) for g in grads],
        scratch_shapes=[pltpu.SemaphoreType.DMA((n,))] * 2,
    )(*grads)


def _pair_sum(name, own, got, c_arr, tr=256):
    nc, r, cc = own.shape
    h = r // 2
    tr = min(tr, h)
    nb = h // tr

    def body(c_ref, a_ref, b_ref, o_ref):
        o_ref[...] = (a_ref[...].astype(F32) + b_ref[...].astype(F32)).astype(BF)

    return pl.pallas_call(
        body, name=name,
        grid_spec=pltpu.PrefetchScalarGridSpec(
            num_scalar_prefetch=1, grid=(nc, nb),
            in_specs=[pl.BlockSpec((None, tr, cc), lambda s, i, c_ref: (s, c_ref[0] * nb + i, 0)),
                      pl.BlockSpec((None, tr, cc), lambda s, i, c_ref: (s, i, 0))],
            out_specs=pl.BlockSpec((None, tr, cc), lambda s, i, c_ref: (s, i, 0))),
        out_shape=jax.ShapeDtypeStruct((nc, h, cc), BF),
        compiler_params=_params(("parallel", "parallel")),
    )(c_arr, own, got)


def _chip_exchange(parts):
    n = len(parts)

    def body(*refs):
        src, dst = refs[:n], refs[n:2 * n]
        send, recv = refs[2 * n:]
        x, y, c, chips = _place()
        copies = []
        for w in range(n):
            for j, chip in enumerate(chips):
                copies.append(pltpu.make_async_remote_copy(
                    src_ref=src[w].at[2 * chip[0] + chip[1]], dst_ref=dst[w].at[j],
                    send_sem=send.at[3 * w + j], recv_sem=recv.at[3 * w + j], device_id=(*chip, c), device_id_type=MESH))
        for cp in copies:
            cp.start()
        for cp in copies:
            cp.wait()

    return pl.pallas_call(
        body, name="grad_chip_exchange",
        in_specs=[ANY] * n, out_specs=[ANY] * n,
        out_shape=[jax.ShapeDtypeStruct((N_CHIPS - 1,) + p.shape[1:], p.dtype) for p in parts],
        scratch_shapes=[pltpu.SemaphoreType.DMA((3 * n,))] * 2,
    )(*parts)


def _chip_sum(name, parts, got, chip_arr, c_arr, tr=256):
    _, h, cc = parts.shape
    tr = min(tr, h)
    nb = h // tr

    def body(chip_ref, c_ref, own_ref, got_ref, o_ref):
        acc = own_ref[...].astype(F32)
        for k in range(N_CHIPS - 1):
            acc = acc + got_ref[k].astype(F32)
        o_ref[...] = acc

    return pl.pallas_call(
        body, name=name,
        grid_spec=pltpu.PrefetchScalarGridSpec(
            num_scalar_prefetch=2, grid=(nb,),
            in_specs=[pl.BlockSpec((None, tr, cc), lambda i, chip_ref, c_ref: (chip_ref[0], i, 0)),
                      pl.BlockSpec((N_CHIPS - 1, tr, cc), lambda i, chip_ref, c_ref: (0, i, 0))],
            out_specs=pl.BlockSpec((tr, cc), lambda i, chip_ref, c_ref: (c_ref[0] * nb + i, 0))),
        out_shape=jax.ShapeDtypeStruct((2 * h, cc), F32),
        compiler_params=_params(("parallel",)),
    )(chip_arr, c_arr, parts, got)


def _pair_gather(bufs):
    n = len(bufs)

    def body(*refs):
        dst = refs[n:2 * n]
        send, recv = refs[2 * n:]
        x, y, c, _ = _place()

        def copy(w, core_half):
            h = bufs[w].shape[0] // 2
            rows = dst[w].at[pl.ds(core_half * h, h)]
            return pltpu.make_async_remote_copy(src_ref=rows, dst_ref=rows, send_sem=send.at[w], recv_sem=recv.at[w],
                                                device_id=(x, y, 1 - c), device_id_type=MESH)

        sends = [copy(w, c) for w in range(n)]
        for cp in sends:
            cp.start()
        for w in range(n):
            copy(w, 1 - c).wait_recv()
        for cp in sends:
            cp.wait_send()

    return pl.pallas_call(
        body, name="grad_pair_gather",
        in_specs=[ANY] * n, out_specs=[ANY] * n,
        out_shape=[jax.ShapeDtypeStruct(b.shape, b.dtype) for b in bufs],
        input_output_aliases={w: w for w in range(n)},
        scratch_shapes=[pltpu.SemaphoreType.DMA((n,))] * 2,
    )(*bufs)


def _all_sum_small(name, v):
    p = v.shape[0]

    def body(v_ref, o_ref, slots, send, recv):
        x, y, c, _ = _place()
        me = 4 * x + 2 * y + c
        copies = []
        for k in range(1, N_DEV):
            peer = (x ^ (k >> 2), y ^ ((k >> 1) & 1), c ^ (k & 1))
            copies.append(pltpu.make_async_remote_copy(
                src_ref=v_ref, dst_ref=slots.at[me], send_sem=send.at[k - 1], recv_sem=recv.at[k - 1],
                device_id=peer, device_id_type=MESH))
        for cp in copies:
            cp.start()
        slots[me] = v_ref[...]
        for cp in copies:
            cp.wait()
        acc = slots[0]
        for s in range(1, N_DEV):
            acc = acc + slots[s]
        o_ref[...] = acc

    vm = pl.BlockSpec(memory_space=pltpu.VMEM)
    return pl.pallas_call(
        body, name=name,
        in_specs=[vm], out_specs=vm,
        out_shape=jax.ShapeDtypeStruct(v.shape, F32),
        scratch_shapes=[pltpu.VMEM((N_DEV, p, LANES), F32), pltpu.SemaphoreType.DMA((N_DEV - 1,)),
                        pltpu.SemaphoreType.DMA((N_DEV - 1,))],
    )(v)


def _adamw(name, w, g, m, v, tr=256):
    r, c = w.shape
    tr = min(tr, r)
    assert r % tr == 0
    bc1 = 1.0 - ADAM_B1 ** ADAM_STEP
    bc2 = 1.0 - ADAM_B2 ** ADAM_STEP

    def body(w_ref, g_ref, m_ref, v_ref, d_ref, nm_ref, nv_ref):
        gv = g_ref[...]
        nm = ADAM_B1 * m_ref[...] + (1.0 - ADAM_B1) * gv
        nv = ADAM_B2 * v_ref[...] + (1.0 - ADAM_B2) * (gv * gv)
        nm_ref[...] = nm
        nv_ref[...] = nv
        d_ref[...] = -ADAM_LR * ((nm / bc1) / (jnp.sqrt(nv / bc2) + ADAM_EPS) + ADAM_WD * w_ref[...])

    blk = pl.BlockSpec((tr, c), lambda i: (i, 0))
    return pl.pallas_call(
        body, name=name, grid=(r // tr,),
        in_specs=[blk] * 4, out_specs=[blk] * 3,
        out_shape=[jax.ShapeDtypeStruct((r, c), F32)] * 3,
        compiler_params=_params(("parallel",)),
    )(w, g, m, v)


def _pack_small(parts):
    flat = jnp.concatenate([a.reshape(-1) for a in parts])
    n = flat.shape[0]
    p = -(-n // (8 * LANES)) * 8
    packed = jnp.pad(flat, (0, p * LANES - n)).reshape(p, LANES)

    def unpack(q):
        out, off = [], 0
        f = q.reshape(-1)
        for a in parts:
            out.append(f[off:off + a.size].reshape(a.shape))
            off += a.size
        return out

    return packed, unpack


def kernel(x, p, norm_mix, w_in, w_dw, conv_ln_g, conv_ln_b, w_conv_proj, q_norm, k_norm, w_attn_proj, w_out, norm_ffn, w_ff1, w_ff2, norm_ple, w_ple_gate, w_ple_proj, norm_final, loss_target, m_norm_mix, m_w_in, m_w_dw, m_conv_ln_g, m_conv_ln_b, m_w_conv_proj, m_q_norm, m_k_norm, m_w_attn_proj, m_w_out, m_norm_ffn, m_w_ff1, m_w_ff2, m_norm_ple, m_w_ple_gate, m_w_ple_proj, m_norm_final, v_norm_mix, v_w_in, v_w_dw, v_conv_ln_g, v_conv_ln_b, v_w_conv_proj, v_q_norm, v_k_norm, v_w_attn_proj, v_w_out, v_norm_ffn, v_w_ff1, v_w_ff2, v_norm_ple, v_w_ple_gate, v_w_ple_proj, v_norm_final):
    s, d = x.shape[1], x.shape[2]
    cw = d // 2
    kvw = d // GROUP
    xs, ps, tgt = x[0], p[0, 0], loss_target[0]
    cx, cy, cc = lax.axis_index("x"), lax.axis_index("y"), lax.axis_index("c")
    chip = 2 * cx + cy
    c_arr = jnp.reshape(cc, (1,)).astype(jnp.int32)
    tm, tme = min(MM_TM, s), min(MM_TM_EPI, s)

    names = ["w_in", "w_conv_proj", "w_attn_proj", "w_out", "w_ff1", "w_ff2", "w_ple_gate", "w_ple_proj"]
    big = [w_in, w_conv_proj, w_attn_proj, w_out, w_ff1, w_ff2, w_ple_gate, w_ple_proj]
    chip_arr = jnp.reshape(chip, (1,)).astype(jnp.int32)
    win, wcp, wap, wout, w1, w2, wpg, wple = _gather_weights(
        [_cast_place("cast_" + nm, w[0], chip_arr) for nm, w in zip(names, big)])
    wap, wout, w2, wpg = (t.reshape(-1, t.shape[-1]) for t in (wap, wout, w2, wpg))
    cpc = cw // N_CHIPS
    taps = jnp.zeros((32, N_CHIPS, cpc), F32).at[:CONV_KERNEL].set(
        jnp.where(lax.broadcasted_iota(jnp.int32, (1, N_CHIPS, 1), 1) == chip, w_dw[0][:, None, :], 0.0))
    taps = jnp.where(cc == 0, taps, 0.0).reshape(32 * cw // LANES, LANES)
    wdw = _all_sum_small("gather_taps", taps).reshape(32, cw)

    cos, sin = _rope_tables(s)
    h0 = _rms_fwd("rms_mix", xs, norm_mix)
    (z,) = _mm("z_proj", h0, win, b_cm=True, tm=tm, tn=win.shape[2] // 3, tk=d)
    uc, act = _conv_fwd(z, wdw, conv_ln_g, conv_ln_b, cw)
    (y_c,) = _mm("conv_proj", act, wcp, b_cm=True, tm=tm, tn=wcp.shape[2], tk=cw, out_dtypes=(F32,))
    qt, kt = _qk_fwd(z, cos, sin, q_norm, k_norm, d)
    o, lse = _flash_fwd(qt, kt, z, d)
    tn = d // 2
    gcb = (2 * d + 2 * kvw) // tn

    def merge_epi(acc, yc, gc, ga):
        return acc, _sigmoid(gc.astype(F32)) * yc + _sigmoid(ga.astype(F32)) * acc

    y_a, merged = _mm("attn_proj", o, wap, tm=tme, tn=tn, tk=d, epi=merge_epi, out_dtypes=(BF, BF),
                      extras=[_tile_extra(y_c, tme, tn), _tile_extra(z, tme, tn, gcb), _tile_extra(z, tme, tn, gcb + 2)])
    (x1,) = _mm("out_proj", merged, wout, tm=tm, tn=tn, tk=d, epi=lambda acc, r: (r + acc,), out_dtypes=(F32,),
                extras=[_tile_extra(xs, tm, tn)])
    h1 = _rms_fwd("rms_ffn", x1, norm_ffn)
    (a,) = _mm("ff1", h1, w1, b_cm=True, tm=tm, tn=tn, tk=d)

    def relu2(t):
        return jnp.square(jnp.maximum(t, 0.0))

    (x2,) = _mm("ff2", a, w2, tm=tme, tn=tn, tk=d, a_fn=relu2, epi=lambda acc, r: (r + acc,), out_dtypes=(F32,),
                extras=[_tile_extra(x1, tme, tn)])
    h2 = _rms_fwd("rms_ple", x2, norm_ple)
    to_bf = lambda t: t.astype(BF)
    (e,) = _mm("ple_proj", ps, wple, b_cm=True, tm=tm, tn=wple.shape[2], tk=ps.shape[1], a_fn=to_bf)

    def ple_epi(acc, ev, r):
        gt = _sigmoid(acc)
        return r + gt * ev.astype(F32), gt

    x3, gate = _mm("ple_gate", h2, wpg, tm=tme, tn=tn, tk=d, epi=ple_epi, out_dtypes=(F32, BF),
                   extras=[_tile_extra(e, tme, tn), _tile_extra(x2, tme, tn)])

    dx3, de, dgp, sq, d_fin = _loss_bwd(x3, tgt, norm_final.reshape(1, d), e, gate)
    tkt = min(2048, s)
    (g_wple,) = _mm("d_wple", ps, de, ta=True, out_cm=True, tm=ps.shape[1], tn=wple.shape[2], tk=tkt, a_fn=to_bf)
    (g_wpg,) = _mm("d_wpg", h2, dgp, ta=True, tm=tm, tn=tn, tk=tkt)
    (dh2,) = _mm("d_h2", dgp, wpg, tb=True, tm=tm, tn=tn, tk=d)
    dx2, dx2b, d_ple = _rms_bwd("rms_ple_bwd", dh2, x2, norm_ple, dx3)

    (da,) = _mm("d_a", dx2b, w2, tb=True, tm=tm, tn=tn, tk=d, out_dtypes=(BF,),
                epi=lambda acc, av: (acc * (2.0 * jnp.maximum(av.astype(F32), 0.0)),), extras=[_tile_extra(a, tm, tn)])
    (g_w2,) = _mm("d_w2", a, dx2b, ta=True, tm=tm, tn=tn, tk=min(1024, s), a_fn=relu2)
    (g_w1,) = _mm("d_w1", h1, da, ta=True, out_cm=True, tm=tm, tn=tn, tk=tkt)
    (dh1,) = _mm("d_h1", da, w1, tb=True, b_cm=True, tm=tm, tn=tn, tk=w1.shape[2])
    dx1, dx1b, d_ffn = _rms_bwd("rms_ffn_bwd", dh1, x1, norm_ffn, dx2)

    def merge_bwd(acc, gc, ga, yc, ya):
        sc, sa = _sigmoid(gc.astype(F32)), _sigmoid(ga.astype(F32))
        return acc * sc, acc * sa, acc * yc * sc * (1.0 - sc), acc * ya.astype(F32) * sa * (1.0 - sa)

    dy_c, dy_a, dg_c, dg_a = _mm(
        "d_merged", dx1b, wout, tb=True, tm=tme, tn=tn, tk=d, epi=merge_bwd, out_dtypes=(BF, BF, BF, BF),
        extras=[_tile_extra(z, tme, tn, gcb), _tile_extra(z, tme, tn, gcb + 2), _tile_extra(y_c, tme, tn),
                _tile_extra(y_a, tme, tn)])
    (g_wout,) = _mm("d_wout", merged, dx1b, ta=True, tm=tm, tn=tn, tk=tkt)
    (g_wap,) = _mm("d_wap", o, dy_a, ta=True, tm=tm, tn=tn, tk=tkt)
    (do,) = _mm("d_o", dy_a, wap, tb=True, tm=tm, tn=tn, tk=d)
    dqt, dkt, dv = _flash_bwd(qt, kt, z, o, do, lse, d)
    dq, dk, d_qn, d_kn = _qk_bwd(dqt, dkt, z, cos, sin, q_norm, k_norm, d)
    (g_wcp,) = _mm("d_wcp", act, dy_c, ta=True, out_cm=True, tm=cw, tn=wcp.shape[2], tk=tkt)
    (dact,) = _mm("d_act", dy_c, wcp, tb=True, b_cm=True, tm=tm, tn=cw, tk=wcp.shape[2])
    dcab, d_taps, d_lng, d_lnb = _conv_bwd(dact, uc, z, wdw, conv_ln_g, conv_ln_b, cw)
    dz = jnp.concatenate([dcab, dq, dk, dv.astype(BF), dg_c, dg_a], axis=1)
    (g_win,) = _mm("d_win", h0, dz, ta=True, out_cm=True, tm=tm, tn=win.shape[2] // 3, tk=tkt)
    (dh0,) = _mm("d_h0", dz, win, tb=True, b_cm=True, tm=tm, tn=tn, tk=win.shape[2])
    dx, _, d_mix = _rms_bwd("rms_mix_bwd", dh0, xs, norm_mix, dx1)

    partials = [g_win, g_wcp, g_wap, g_wout, g_w1, g_w2, g_wpg, g_wple]
    partials = [g if g.ndim == 3 else g.reshape(N_CHIPS, g.shape[0] // N_CHIPS, g.shape[1]) for g in partials]
    got = _pair_exchange(partials)
    chip_parts = [_pair_sum("pair_sum_" + nm, g, r, c_arr) for nm, g, r in zip(names, partials, got)]
    slots = _chip_exchange(chip_parts)
    big_grads = _pair_gather(
        [_chip_sum("chip_sum_" + nm, cp, sl, chip_arr, c_arr) for nm, cp, sl in zip(names, chip_parts, slots)])

    small = [d_mix, d_taps[:CONV_KERNEL], d_lng, d_lnb, d_qn, d_kn, d_ffn, d_ple, d_fin]
    packed, unpack = _pack_small(small)
    g_mix, g_taps, g_lng, g_lnb, g_qn, g_kn, g_ffn, g_ple, g_fin = unpack(_all_sum_small("reduce_small", packed))
    g_dw = lax.dynamic_slice_in_dim(g_taps.reshape(CONV_KERNEL, N_CHIPS, cpc), chip, 1, axis=1).reshape(1, CONV_KERNEL, cpc)

    sq_local = lax.reduce_precision(sq[0, 0], 8, 23)
    loss = (0.5 / d) * lax.psum(sq_local, ("x", "y", "c"))

    grads = {
        "norm_mix": g_mix, "w_in": big_grads[0][None], "w_dw": g_dw, "conv_ln_g": g_lng, "conv_ln_b": g_lnb,
        "w_conv_proj": big_grads[1][None], "q_norm": g_qn, "k_norm": g_kn, "w_attn_proj": big_grads[2][None],
        "w_out": big_grads[3][None], "norm_ffn": g_ffn, "w_ff1": big_grads[4][None], "w_ff2": big_grads[5][None],
        "norm_ple": g_ple, "w_ple_gate": big_grads[6][None], "w_ple_proj": big_grads[7][None],
        "norm_final": g_fin.reshape(d),
    }
    weights = dict(norm_mix=norm_mix, w_in=w_in, w_dw=w_dw, conv_ln_g=conv_ln_g, conv_ln_b=conv_ln_b, w_conv_proj=w_conv_proj,
                   q_norm=q_norm, k_norm=k_norm, w_attn_proj=w_attn_proj, w_out=w_out, norm_ffn=norm_ffn, w_ff1=w_ff1,
                   w_ff2=w_ff2, norm_ple=norm_ple, w_ple_gate=w_ple_gate, w_ple_proj=w_ple_proj, norm_final=norm_final)
    m_in = dict(norm_mix=m_norm_mix, w_in=m_w_in, w_dw=m_w_dw, conv_ln_g=m_conv_ln_g, conv_ln_b=m_conv_ln_b,
                w_conv_proj=m_w_conv_proj, q_norm=m_q_norm, k_norm=m_k_norm, w_attn_proj=m_w_attn_proj, w_out=m_w_out,
                norm_ffn=m_norm_ffn, w_ff1=m_w_ff1, w_ff2=m_w_ff2, norm_ple=m_norm_ple, w_ple_gate=m_w_ple_gate,
                w_ple_proj=m_w_ple_proj, norm_final=m_norm_final)
    v_in = dict(norm_mix=v_norm_mix, w_in=v_w_in, w_dw=v_w_dw, conv_ln_g=v_conv_ln_g, conv_ln_b=v_conv_ln_b,
                w_conv_proj=v_w_conv_proj, q_norm=v_q_norm, k_norm=v_k_norm, w_attn_proj=v_w_attn_proj, w_out=v_w_out,
                norm_ffn=v_norm_ffn, w_ff1=v_w_ff1, w_ff2=v_w_ff2, norm_ple=v_norm_ple, w_ple_gate=v_w_ple_gate,
                w_ple_proj=v_w_ple_proj, norm_final=v_norm_final)
    order = list(weights)
    deltas, new_m, new_v, g_out = [], [], [], []
    for nm in order:
        w = weights[nm]
        shape = w.shape
        two_d = (-1, shape[-1])
        dl, mm_, vv_ = _adamw("adamw_" + nm, w.reshape(two_d), grads[nm].reshape(two_d), m_in[nm].reshape(two_d),
                              v_in[nm].reshape(two_d))
        g_out.append(grads[nm].reshape(shape))
        deltas.append(dl.reshape(shape))
        new_m.append(mm_.reshape(shape))
        new_v.append(vv_.reshape(shape))
    return (loss, dx[None], *g_out, *deltas, *new_m, *new_v)
```

```python
from typing import NamedTuple

import jax
import jax.numpy as jnp
from jax import lax
from jax.experimental import pallas as pl
from jax.experimental.pallas import tpu as pltpu

F32 = jnp.float32
BF = jnp.bfloat16

EPS = 1e-6
HEAD_DIM = 128
GROUP = 4
GRID_W = 64
ROPE_THETA = 10000.0
CONV_KERNEL = 31
HALO = 16
N_CHIPS = 4
N_DEV = 8
LANES = 128

ADAM_LR = 0.001
ADAM_B1 = 0.9
ADAM_B2 = 0.999
ADAM_EPS = 1e-08
ADAM_WD = 0.01
ADAM_STEP = 10

VMEM_LIMIT = 56 * 2 ** 20
LOG2E = 1.4426950408889634
LN2 = 0.6931471805599453
Q_SCALE = HEAD_DIM ** -0.5 * LOG2E
ROW_TILE = 256
FLASH_TQ = 256
FLASH_TK = 512
MM_TM = 1024
MM_TM_EPI = 512
MESH = pl.DeviceIdType.MESH
ANY = pl.BlockSpec(memory_space=pl.ANY)


def _params(sem):
    return pltpu.CompilerParams(dimension_semantics=sem, vmem_limit_bytes=VMEM_LIMIT)


def _sigmoid(x):
    return 1.0 / (1.0 + jnp.exp(-x))


class _Comm(NamedTuple):
    arrays: list
    out_shapes: list
    aliases: dict
    sems: list
    phases: tuple


def _call(body, *, name, grid, in_specs, out_specs, out_shape, scratch_shapes, semantics, args, comm=None):
    n_in, n_out = len(in_specs), len(out_specs)
    if comm is None:
        res = pl.pallas_call(body, name=name, grid=grid, in_specs=in_specs, out_specs=out_specs, out_shape=out_shape,
                             scratch_shapes=scratch_shapes, compiler_params=_params(semantics))(*args)
        return res, []
    nci, nco, ncs = len(comm.arrays), len(comm.out_shapes), len(comm.sems)
    n_steps = 1
    for g in grid:
        n_steps *= g
    first, middle, last = comm.phases

    def hosted(*refs):
        ins, cin = refs[:n_in], refs[n_in:n_in + nci]
        outs = refs[n_in + nci:n_in + nci + n_out]
        cout = refs[n_in + nci + n_out:n_in + nci + n_out + nco]
        rest = refs[n_in + nci + n_out + nco:]
        scratch, sems = rest[:len(rest) - ncs], rest[len(rest) - ncs:]
        step = 0
        for ax, g in enumerate(grid):
            step = step * g + pl.program_id(ax)
        for at, fn in ((0, first), (n_steps // 2, middle)):
            if fn is not None:
                pl.when(step == at)(lambda fn=fn: fn(cin, cout, sems))
        body(*ins, *outs, *scratch)
        if last is not None:
            pl.when(step == n_steps - 1)(lambda: last(cin, cout, sems))

    res = pl.pallas_call(
        hosted, name=name, grid=grid,
        in_specs=list(in_specs) + [ANY] * nci, out_specs=list(out_specs) + [ANY] * nco,
        out_shape=list(out_shape) + list(comm.out_shapes),
        input_output_aliases={n_in + a: n_out + b for a, b in comm.aliases.items()},
        scratch_shapes=list(scratch_shapes) + list(comm.sems),
        compiler_params=_params(("arbitrary",) * len(grid)),
    )(*args, *comm.arrays)
    return res[:n_out], res[n_out:]


def _mm(name, a, b, *, tm, tn, tk, ta=False, tb=False, b_cm=False, out_cm=False,
        a_fn=None, extras=(), epi=None, out_dtypes=(BF,), epi_rows=256, comm=None):
    if ta:
        kc, m = a.shape
    else:
        m, kc = a.shape
    if b_cm:
        nc, r, c = b.shape
        n, per = (r, c) if tb else (nc * c, c)
    else:
        n = b.shape[0] if tb else b.shape[1]
    tm, tn, tk = min(tm, m), min(tn, n), min(tk, kc)
    assert m % tm == 0 and n % tn == 0 and kc % tk == 0, (name, m, n, kc, tm, tn, tk)
    nk = kc // tk
    a_spec = pl.BlockSpec((tk, tm), lambda i, j, k: (k, i)) if ta else pl.BlockSpec((tm, tk), lambda i, j, k: (i, k))
    if b_cm and not tb:
        assert per % tn == 0
        npj = per // tn
        b_spec = pl.BlockSpec((None, tk, tn), lambda i, j, k: (j // npj, k, j % npj))
    elif b_cm:
        assert per % tk == 0
        npk = per // tk
        b_spec = pl.BlockSpec((None, tn, tk), lambda i, j, k: (k // npk, j, k % npk))
    elif tb:
        b_spec = pl.BlockSpec((tn, tk), lambda i, j, k: (j, k))
    else:
        b_spec = pl.BlockSpec((tk, tn), lambda i, j, k: (k, j))
    if out_cm:
        assert (n // N_CHIPS) % tn == 0
        npo = (n // N_CHIPS) // tn
        o_spec = pl.BlockSpec((None, tm, tn), lambda i, j, k: (j // npo, i, j % npo))
        o_shape = (N_CHIPS, m, n // N_CHIPS)
    else:
        o_spec = pl.BlockSpec((tm, tn), lambda i, j, k: (i, j))
        o_shape = (m, n)
    ne, no = len(extras), len(out_dtypes)
    dims = (((0 if ta else 1,), (1 if tb else 0,)), ((), ()))
    use_acc = nk > 1 or epi is not None
    er = min(epi_rows, tm)

    def body(*refs):
        a_ref, b_ref = refs[0], refs[1]
        ex = refs[2:2 + ne]
        outs = refs[2 + ne:2 + ne + no]
        at = a_ref[...]
        if a_fn is not None:
            at = a_fn(at)
        d = lax.dot_general(at, b_ref[...], dims, preferred_element_type=F32)
        if not use_acc:
            outs[0][...] = d.astype(out_dtypes[0])
            return
        acc = refs[-1]
        k = pl.program_id(2)

        @pl.when(k == 0)
        def _():
            acc[...] = d

        if nk > 1:
            @pl.when(k > 0)
            def _():
                acc[...] += d

        @pl.when(k == nk - 1)
        def _():
            for r0 in range(0, tm, er):
                rows = slice(r0, r0 + er)
                if epi is None:
                    vals = (acc[rows, :],)
                else:
                    vals = epi(acc[rows, :], *[e[rows, :] for e in ex])
                for o, v, dt in zip(outs, vals, out_dtypes):
                    o[rows, :] = v.astype(dt)

    res, cres = _call(
        body, name=name, grid=(m // tm, n // tn, nk),
        in_specs=[a_spec, b_spec] + [pl.BlockSpec(bs, im) for _, bs, im in extras],
        out_specs=[o_spec] * no,
        out_shape=[jax.ShapeDtypeStruct(o_shape, dt) for dt in out_dtypes],
        scratch_shapes=[pltpu.VMEM((tm, tn), F32)] if use_acc else [],
        semantics=("parallel", "parallel", "arbitrary"),
        args=[a, b] + [e for e, _, _ in extras], comm=comm)
    return res if comm is None else (res, cres)


def _tile_extra(arr, tm, tn, col_block0=0):
    return (arr, (tm, tn), lambda i, j, k: (i, j + col_block0))


def _rms_fwd(name, x, g, ts=None):
    s, d = x.shape
    ts = ts or ROW_TILE

    def body(x_ref, g_ref, h_ref):
        xv = x_ref[...]
        r = lax.rsqrt(jnp.mean(xv * xv, axis=-1, keepdims=True) + EPS)
        h_ref[...] = (xv * r * g_ref[...]).astype(BF)

    return pl.pallas_call(
        body, name=name, grid=(s // ts,),
        in_specs=[pl.BlockSpec((ts, d), lambda i: (i, 0)), pl.BlockSpec((1, d), lambda i: (0, 0))],
        out_specs=pl.BlockSpec((ts, d), lambda i: (i, 0)),
        out_shape=jax.ShapeDtypeStruct((s, d), BF),
        compiler_params=_params(("parallel",)),
    )(x, g)


def _rms_bwd(name, dh, x, g, dres, ts=None):
    s, d = x.shape
    ts = ts or ROW_TILE

    def body(dh_ref, x_ref, g_ref, dres_ref, dx_ref, dxb_ref, dg_ref):
        xv = x_ref[...]
        dhv = dh_ref[...].astype(F32)
        r = lax.rsqrt(jnp.mean(xv * xv, axis=-1, keepdims=True) + EPS)
        nrm = xv * r
        dn = dhv * g_ref[...]
        dx = dres_ref[...] + r * (dn - nrm * jnp.mean(dn * nrm, axis=-1, keepdims=True))
        dx_ref[...] = dx
        dxb_ref[...] = dx.astype(BF)
        part = jnp.sum(dhv * nrm, axis=0, keepdims=True)

        @pl.when(pl.program_id(0) == 0)
        def _():
            dg_ref[...] = part

        @pl.when(pl.program_id(0) > 0)
        def _():
            dg_ref[...] += part

    row = pl.BlockSpec((ts, d), lambda i: (i, 0))
    vec = pl.BlockSpec((1, d), lambda i: (0, 0))
    return pl.pallas_call(
        body, name=name, grid=(s // ts,),
        in_specs=[row, row, vec, row],
        out_specs=[row, row, vec],
        out_shape=[jax.ShapeDtypeStruct((s, d), F32), jax.ShapeDtypeStruct((s, d), BF), jax.ShapeDtypeStruct((1, d), F32)],
        compiler_params=_params(("arbitrary",)),
    )(dh, x, g, dres)


def _loss_bwd(x3, tgt, gfin, e, gate, ts=None):
    s, d = x3.shape
    ts = ts or ROW_TILE

    def body(x_ref, t_ref, g_ref, e_ref, gate_ref, dx_ref, de_ref, dgp_ref, sq_ref, dg_ref):
        xv = x_ref[...]
        gv = g_ref[...]
        r = lax.rsqrt(jnp.mean(xv * xv, axis=-1, keepdims=True) + EPS)
        nrm = xv * r
        err = nrm * gv - t_ref[...]
        dy = err * (1.0 / d)
        dn = dy * gv
        dx = r * (dn - nrm * jnp.mean(dn * nrm, axis=-1, keepdims=True))
        dx_ref[...] = dx
        ev = e_ref[...].astype(F32)
        gt = gate_ref[...].astype(F32)
        de_ref[...] = (dx * gt).astype(BF)
        dgp_ref[...] = (dx * ev * gt * (1.0 - gt)).astype(BF)
        sq = jnp.full((8, LANES), jnp.sum(err * err), F32)
        part = jnp.sum(dy * nrm, axis=0, keepdims=True)

        @pl.when(pl.program_id(0) == 0)
        def _():
            sq_ref[...] = sq
            dg_ref[...] = part

        @pl.when(pl.program_id(0) > 0)
        def _():
            sq_ref[...] += sq
            dg_ref[...] += part

    row = pl.BlockSpec((ts, d), lambda i: (i, 0))
    vec = pl.BlockSpec((1, d), lambda i: (0, 0))
    return pl.pallas_call(
        body, name="loss_bwd", grid=(s // ts,),
        in_specs=[row, row, vec, row, row],
        out_specs=[row, row, row, pl.BlockSpec((8, LANES), lambda i: (0, 0)), vec],
        out_shape=[jax.ShapeDtypeStruct((s, d), F32), jax.ShapeDtypeStruct((s, d), BF), jax.ShapeDtypeStruct((s, d), BF),
                   jax.ShapeDtypeStruct((8, LANES), F32), jax.ShapeDtypeStruct((1, d), F32)],
        compiler_params=_params(("arbitrary",)),
    )(x3, tgt, gfin, e, gate)


def _halo_specs(ts, s, width, col_block):
    per = ts // HALO
    last = s // HALO - 1
    return [
        pl.BlockSpec((HALO, width), lambda i: (jnp.maximum(i * per - 1, 0), col_block)),
        pl.BlockSpec((ts, width), lambda i: (i, col_block)),
        pl.BlockSpec((HALO, width), lambda i: (jnp.minimum((i + 1) * per, last), col_block)),
    ]


def _glu_ext(zp, zc, zn, ext, cw, ts, i, n_tiles):
    def glu(zr):
        zv = zr[...].astype(F32)
        return zv[:, :cw] * _sigmoid(zv[:, cw:])

    ext[0:HALO, :] = jnp.where(i > 0, glu(zp), 0.0)
    ext[HALO:HALO + ts, :] = glu(zc)
    ext[HALO + ts:, :] = jnp.where(i < n_tiles - 1, glu(zn), 0.0)


SUBLANES = 8


def _shift_scratch(ts):
    return pltpu.VMEM((SUBLANES, ts + 2 * HALO - SUBLANES, LANES), F32)


def _shifted_copies(ext, sh, cols, ts):
    n = ts + 2 * HALO - SUBLANES
    for r in range(SUBLANES):
        sh[r] = ext[r:r + n, cols]


def _tap_rows(sh, off, ts):
    q, r = divmod(off, SUBLANES)
    return sh[r, q * SUBLANES:q * SUBLANES + ts, :]


def _ln_stats(uc):
    mu = jnp.mean(uc, axis=-1, keepdims=True)
    xc = uc - mu
    rstd = lax.rsqrt(jnp.mean(xc * xc, axis=-1, keepdims=True) + EPS)
    return xc * rstd, rstd


def _conv_fwd(z, wdw, ln_g, ln_b, cw, ts=None):
    s = z.shape[0]
    ts = ts or ROW_TILE
    n_tiles = s // ts
    pad = CONV_KERNEL // 2

    def body(zp, zc, zn, w_ref, g_ref, b_ref, uc_ref, act_ref, ext, sh):
        i = pl.program_id(0)
        _glu_ext(zp, zc, zn, ext, cw, ts, i, n_tiles)

        def col_block(cb, carry):
            cols = pl.ds(pl.multiple_of(cb * LANES, LANES), LANES)
            _shifted_copies(ext, sh, cols, ts)
            acc = jnp.zeros((ts, LANES), F32)
            for j in range(CONV_KERNEL):
                acc = acc + _tap_rows(sh, HALO - pad + j, ts) * w_ref[j:j + 1, cols]
            uc_ref[:, cols] = acc
            return carry

        lax.fori_loop(0, cw // LANES, col_block, 0)
        xhat, _ = _ln_stats(uc_ref[...])
        ln = xhat * g_ref[...] + b_ref[...]
        act_ref[...] = (ln * _sigmoid(ln)).astype(BF)

    vec = pl.BlockSpec((1, cw), lambda i: (0, 0))
    row = pl.BlockSpec((ts, cw), lambda i: (i, 0))
    return pl.pallas_call(
        body, name="conv_fwd", grid=(n_tiles,),
        in_specs=_halo_specs(ts, s, 2 * cw, 0) + [pl.BlockSpec((32, cw), lambda i: (0, 0)), vec, vec],
        out_specs=[row, row],
        out_shape=[jax.ShapeDtypeStruct((s, cw), F32), jax.ShapeDtypeStruct((s, cw), BF)],
        scratch_shapes=[pltpu.VMEM((ts + 2 * HALO, cw), F32), _shift_scratch(ts)],
        compiler_params=_params(("parallel",)),
    )(z, z, z, wdw, ln_g, ln_b)


def _conv_bwd(ds, uc, z, wdw, ln_g, ln_b, cw, ts=None):
    s = z.shape[0]
    ts = ts or ROW_TILE
    n_tiles = s // ts
    pad = CONV_KERNEL // 2

    def body(zp, zc, zn, dsp, dsc, dsn, ucp, ucc, ucn, w_ref, g_ref, b_ref,
             dz_ref, dw_ref, dg_ref, db_ref, ext, dext, sh, dsh):
        i = pl.program_id(0)
        gv, bv = g_ref[...], b_ref[...]

        def ln_bwd(ds_r, uc_r):
            xhat, rstd = _ln_stats(uc_r[...])
            ln = xhat * gv + bv
            sg = _sigmoid(ln)
            dln = ds_r[...].astype(F32) * (sg * (1.0 + ln * (1.0 - sg)))
            dxh = dln * gv
            duc = rstd * (dxh - jnp.mean(dxh, axis=-1, keepdims=True) - xhat * jnp.mean(dxh * xhat, axis=-1, keepdims=True))
            return duc, dln, xhat

        duc_p, _, _ = ln_bwd(dsp, ucp)
        duc_c, dln_c, xhat_c = ln_bwd(dsc, ucc)
        duc_n, _, _ = ln_bwd(dsn, ucn)
        dext[0:HALO, :] = jnp.where(i > 0, duc_p, 0.0)
        dext[HALO:HALO + ts, :] = duc_c
        dext[HALO + ts:, :] = jnp.where(i < n_tiles - 1, duc_n, 0.0)
        _glu_ext(zp, zc, zn, ext, cw, ts, i, n_tiles)

        dg_part = jnp.sum(dln_c * xhat_c, axis=0, keepdims=True)
        db_part = jnp.sum(dln_c, axis=0, keepdims=True)

        @pl.when(i == 0)
        def _():
            dw_ref[...] = jnp.zeros_like(dw_ref)
            dg_ref[...] = dg_part
            db_ref[...] = db_part

        @pl.when(i > 0)
        def _():
            dg_ref[...] += dg_part
            db_ref[...] += db_part

        def col_block(cb, carry):
            c0 = pl.multiple_of(cb * LANES, LANES)
            cols, gate_cols = pl.ds(c0, LANES), pl.ds(cw + c0, LANES)
            _shifted_copies(dext, dsh, cols, ts)
            _shifted_copies(ext, sh, cols, ts)
            du = jnp.zeros((ts, LANES), F32)
            for j in range(CONV_KERNEL):
                du = du + _tap_rows(dsh, HALO + pad - j, ts) * w_ref[j:j + 1, cols]
            ca, sb = zc[:, cols].astype(F32), _sigmoid(zc[:, gate_cols].astype(F32))
            dz_ref[:, cols] = (du * sb).astype(BF)
            dz_ref[:, gate_cols] = (du * ca * sb * (1.0 - sb)).astype(BF)
            duc_blk = _tap_rows(dsh, HALO, ts)
            for j in range(CONV_KERNEL):
                dw_ref[j:j + 1, cols] += jnp.sum(_tap_rows(sh, HALO - pad + j, ts) * duc_blk, axis=0, keepdims=True)
            return carry

        lax.fori_loop(0, cw // LANES, col_block, 0)

    vec = pl.BlockSpec((1, cw), lambda i: (0, 0))
    wsp = pl.BlockSpec((32, cw), lambda i: (0, 0))
    return pl.pallas_call(
        body, name="conv_bwd", grid=(n_tiles,),
        in_specs=_halo_specs(ts, s, 2 * cw, 0) + _halo_specs(ts, s, cw, 0) + _halo_specs(ts, s, cw, 0) + [wsp, vec, vec],
        out_specs=[pl.BlockSpec((ts, 2 * cw), lambda i: (i, 0)), wsp, vec, vec],
        out_shape=[jax.ShapeDtypeStruct((s, 2 * cw), BF), jax.ShapeDtypeStruct((32, cw), F32),
                   jax.ShapeDtypeStruct((1, cw), F32), jax.ShapeDtypeStruct((1, cw), F32)],
        scratch_shapes=[pltpu.VMEM((ts + 2 * HALO, cw), F32), pltpu.VMEM((ts + 2 * HALO, cw), F32),
                        _shift_scratch(ts), _shift_scratch(ts)],
        compiler_params=_params(("arbitrary",)),
    )(z, z, z, ds, ds, ds, uc, uc, uc, wdw, ln_g, ln_b)


def _rope_tables(s):
    axis_dim = HEAD_DIM // 2
    t = jnp.arange(s, dtype=jnp.int32)
    row = (t // GRID_W).astype(F32)[:, None]
    col = (t % GRID_W).astype(F32)[:, None]
    inv_freq = ROPE_THETA ** (-jnp.arange(0, axis_dim, 2, dtype=F32) / axis_dim)[None, :]
    ar, ac = row * inv_freq, col * inv_freq
    cos = jnp.concatenate([jnp.cos(ar), jnp.cos(ar), jnp.cos(ac), jnp.cos(ac)], axis=-1)
    sin = jnp.concatenate([-jnp.sin(ar), jnp.sin(ar), -jnp.sin(ac), jnp.sin(ac)], axis=-1)
    return cos, sin


def _swap_quarters(x):
    q = HEAD_DIM // 4
    lane = lax.broadcasted_iota(jnp.int32, x.shape, 1)
    return jnp.where((lane % (2 * q)) < q, pltpu.roll(x, HEAD_DIM - q, 1), pltpu.roll(x, q, 1))


def _qk_fwd(z, cos, sin, qg, kg, d, ts=None):
    s = z.shape[0]
    ts = ts or ROW_TILE
    kvw = d // GROUP
    scale = Q_SCALE

    def body(q_ref, k_ref, c_ref, s_ref, qg_ref, kg_ref, qo_ref, ko_ref):
        cv, sv = c_ref[...], s_ref[...]

        def head(x_ref, g_ref, o_ref, h, mul):
            xv = x_ref[:, h * HEAD_DIM:(h + 1) * HEAD_DIM].astype(F32)
            r = lax.rsqrt(jnp.mean(xv * xv, axis=-1, keepdims=True) + EPS)
            nrm = xv * r * g_ref[...]
            out = nrm * cv + _swap_quarters(nrm) * sv
            o_ref[:, h * HEAD_DIM:(h + 1) * HEAD_DIM] = (out * mul).astype(BF)

        for h in range(d // HEAD_DIM):
            head(q_ref, qg_ref, qo_ref, h, scale)
        for h in range(kvw // HEAD_DIM):
            head(k_ref, kg_ref, ko_ref, h, 1.0)

    cw2 = d
    tab = pl.BlockSpec((ts, HEAD_DIM), lambda i: (i, 0))
    vec = pl.BlockSpec((1, HEAD_DIM), lambda i: (0, 0))
    return pl.pallas_call(
        body, name="qk_fwd", grid=(s // ts,),
        in_specs=[pl.BlockSpec((ts, d), lambda i: (i, cw2 // d)),
                  pl.BlockSpec((ts, kvw), lambda i: (i, (cw2 + d) // kvw)), tab, tab, vec, vec],
        out_specs=[pl.BlockSpec((ts, d), lambda i: (i, 0)), pl.BlockSpec((ts, kvw), lambda i: (i, 0))],
        out_shape=[jax.ShapeDtypeStruct((s, d), BF), jax.ShapeDtypeStruct((s, kvw), BF)],
        compiler_params=_params(("parallel",)),
    )(z, z, cos, sin, qg, kg)


def _qk_bwd(dqt, dkt, z, cos, sin, qg, kg, d, ts=None):
    s = z.shape[0]
    ts = ts or ROW_TILE
    kvw = d // GROUP
    scale = HEAD_DIM ** -0.5

    def body(dq_ref, dk_ref, q_ref, k_ref, c_ref, s_ref, qg_ref, kg_ref, dqo_ref, dko_ref, dqg_ref, dkg_ref):
        cv, sv = c_ref[...], s_ref[...]

        def head(dy_ref, x_ref, g_ref, o_ref, h, mul):
            dout = dy_ref[:, h * HEAD_DIM:(h + 1) * HEAD_DIM].astype(F32) * mul
            dn = dout * cv + _swap_quarters(dout * sv)
            xv = x_ref[:, h * HEAD_DIM:(h + 1) * HEAD_DIM].astype(F32)
            r = lax.rsqrt(jnp.mean(xv * xv, axis=-1, keepdims=True) + EPS)
            nh = xv * r
            dnh = dn * g_ref[...]
            o_ref[:, h * HEAD_DIM:(h + 1) * HEAD_DIM] = (r * (dnh - nh * jnp.mean(dnh * nh, axis=-1, keepdims=True))).astype(BF)
            return jnp.sum(dn * nh, axis=0, keepdims=True)

        dqg = jnp.zeros((1, HEAD_DIM), F32)
        for h in range(d // HEAD_DIM):
            dqg = dqg + head(dq_ref, q_ref, qg_ref, dqo_ref, h, scale)
        dkg = jnp.zeros((1, HEAD_DIM), F32)
        for h in range(kvw // HEAD_DIM):
            dkg = dkg + head(dk_ref, k_ref, kg_ref, dko_ref, h, LN2)

        @pl.when(pl.program_id(0) == 0)
        def _():
            dqg_ref[...] = dqg
            dkg_ref[...] = dkg

        @pl.when(pl.program_id(0) > 0)
        def _():
            dqg_ref[...] += dqg
            dkg_ref[...] += dkg

    cw2 = d
    tab = pl.BlockSpec((ts, HEAD_DIM), lambda i: (i, 0))
    vec = pl.BlockSpec((1, HEAD_DIM), lambda i: (0, 0))
    qrow = pl.BlockSpec((ts, d), lambda i: (i, 0))
    krow = pl.BlockSpec((ts, kvw), lambda i: (i, 0))
    return pl.pallas_call(
        body, name="qk_bwd", grid=(s // ts,),
        in_specs=[qrow, krow, pl.BlockSpec((ts, d), lambda i: (i, cw2 // d)),
                  pl.BlockSpec((ts, kvw), lambda i: (i, (cw2 + d) // kvw)), tab, tab, vec, vec],
        out_specs=[qrow, krow, vec, vec],
        out_shape=[jax.ShapeDtypeStruct((s, d), BF), jax.ShapeDtypeStruct((s, kvw), BF),
                   jax.ShapeDtypeStruct((1, HEAD_DIM), F32), jax.ShapeDtypeStruct((1, HEAD_DIM), F32)],
        compiler_params=_params(("arbitrary",)),
    )(dqt, dkt, z, z, cos, sin, qg, kg)


_NT = (((1,), (1,)), ((), ()))
_TN = (((0,), (0,)), ((), ()))


def _v_col_block(d):
    return (2 * d + d // GROUP) // HEAD_DIM


def _flash_fwd(qt, kt, z, d, tq=None, tk=None, comm=None):
    s = qt.shape[0]
    tq, tk = min(tq or FLASH_TQ, s), min(tk or FLASH_TK, s)
    ng, nq, nk = d // (GROUP * HEAD_DIM), s // tq, s // tk
    gw = GROUP * HEAD_DIM
    rows = GROUP * tq

    nt = tk // LANES
    assert nk % 2 == 0, (s, tk)

    def body(q_ref, k_ref, v_ref, o_ref, lse_ref, qs, v1, p_s, m_s, acc_s, sc_s):
        @pl.when(pl.program_id(1) == 0)
        def _():
            v1[:, :HEAD_DIM] = v_ref[...]
            v1[:, HEAD_DIM:] = jnp.ones((s, HEAD_DIM), BF)

        for h in range(GROUP):
            qs[h * tq:(h + 1) * tq, :] = q_ref[:, h * HEAD_DIM:(h + 1) * HEAD_DIM]
        m_s[...] = jnp.full((rows, LANES), -1e30, F32)
        acc_s[...] = jnp.zeros((rows, 2 * HEAD_DIM), F32)

        def scores(j):
            return lax.dot_general(qs[...], k_ref[pl.ds(pl.multiple_of(j * tk, tk), tk), :], _NT, preferred_element_type=F32)

        def softmax_pv(j, sc):
            kv_rows = pl.ds(pl.multiple_of(j * tk, tk), tk)
            mt = sc[:, :LANES]
            for c in range(1, nt):
                mt = jnp.maximum(mt, sc[:, c * LANES:(c + 1) * LANES])
            m_old = m_s[...]
            m_new = jnp.maximum(m_old, jnp.max(mt, axis=-1, keepdims=True))
            alpha = jnp.exp2(m_old - m_new)
            for c in range(nt):
                cs = slice(c * LANES, (c + 1) * LANES)
                p_s[:, cs] = jnp.exp2(sc[:, cs] - m_new).astype(BF)
            pv = jnp.dot(p_s[...], v1[kv_rows, :], preferred_element_type=F32)
            acc_s[:, :HEAD_DIM] = alpha * acc_s[:, :HEAD_DIM] + pv[:, :HEAD_DIM]
            acc_s[:, HEAD_DIM:] = alpha * acc_s[:, HEAD_DIM:] + pv[:, HEAD_DIM:]
            m_s[...] = m_new

        sc_s[0] = scores(0)

        def step(jj, carry):
            j = 2 * jj
            sc_s[1] = scores(j + 1)
            softmax_pv(j, sc_s[0])
            sc_s[0] = scores(jnp.minimum(j + 2, nk - 1))
            softmax_pv(j + 1, sc_s[1])
            return carry

        lax.fori_loop(0, nk // 2, step, 0)
        l = acc_s[:, HEAD_DIM:]
        o = acc_s[:, :HEAD_DIM] / l
        for h in range(GROUP):
            o_ref[:, h * HEAD_DIM:(h + 1) * HEAD_DIM] = o[h * tq:(h + 1) * tq, :].astype(BF)
        lse_ref[...] = m_s[...] + jnp.log2(l)

    vb = _v_col_block(d)
    (o, lse), cres = _call(
        body, name="flash_fwd", grid=(ng, nq),
        in_specs=[pl.BlockSpec((tq, gw), lambda g, i: (i, g)),
                  pl.BlockSpec((s, HEAD_DIM), lambda g, i: (0, g)),
                  pl.BlockSpec((s, HEAD_DIM), lambda g, i: (0, vb + g))],
        out_specs=[pl.BlockSpec((tq, gw), lambda g, i: (i, g)),
                   pl.BlockSpec((rows, LANES), lambda g, i: (g * nq + i, 0))],
        out_shape=[jax.ShapeDtypeStruct((s, d), BF), jax.ShapeDtypeStruct((ng * nq * rows, LANES), F32)],
        scratch_shapes=[pltpu.VMEM((rows, HEAD_DIM), BF), pltpu.VMEM((s, 2 * HEAD_DIM), BF), pltpu.VMEM((rows, tk), BF),
                        pltpu.VMEM((rows, LANES), F32), pltpu.VMEM((rows, 2 * HEAD_DIM), F32), pltpu.VMEM((2, rows, tk), F32)],
        semantics=("parallel", "arbitrary"), args=[qt, kt, z], comm=comm)
    return o, lse, cres


def _flash_bwd(qt, kt, z, o, do, lse, d, tq=None, tk=None, comm=None):
    s = qt.shape[0]
    tq, tk = min(tq or FLASH_TQ, s), min(tk or FLASH_TK, s)
    ng, nq, nk = d // (GROUP * HEAD_DIM), s // tq, s // tk
    gw = GROUP * HEAD_DIM
    rows = GROUP * tq

    nt = tk // LANES

    def body(q_ref, k_ref, v_ref, o_ref, do_ref, lse_ref, dq_ref, dk_ref, dv_ref, qs, dos, delta_s, dq_s, p_s, ds_s):
        i = pl.program_id(1)
        for h in range(GROUP):
            cols = slice(h * HEAD_DIM, (h + 1) * HEAD_DIM)
            qs[h * tq:(h + 1) * tq, :] = q_ref[:, cols]
            dov = do_ref[:, cols]
            dos[h * tq:(h + 1) * tq, :] = dov
            delta = jnp.sum(dov.astype(F32) * o_ref[:, cols].astype(F32), axis=-1, keepdims=True)
            delta_s[h * tq:(h + 1) * tq, :] = jnp.broadcast_to(delta, (tq, LANES))
        dq_s[...] = jnp.zeros((rows, HEAD_DIM), F32)

        @pl.when(i == 0)
        def _():
            dk_ref[...] = jnp.zeros_like(dk_ref)
            dv_ref[...] = jnp.zeros_like(dv_ref)

        def step(j, carry):
            kv_rows = pl.ds(pl.multiple_of(j * tk, tk), tk)
            kv, vv = k_ref[kv_rows, :], v_ref[kv_rows, :]
            sc = lax.dot_general(qs[...], kv, _NT, preferred_element_type=F32)
            dp = lax.dot_general(dos[...], vv, _NT, preferred_element_type=F32)
            lse, delta = lse_ref[...], delta_s[...]
            for c in range(nt):
                cs = slice(c * LANES, (c + 1) * LANES)
                p = jnp.exp2(sc[:, cs] - lse)
                p_s[:, cs] = p.astype(BF)
                ds_s[:, cs] = (p * (dp[:, cs] - delta)).astype(BF)
            dv_ref[kv_rows, :] += lax.dot_general(p_s[...], dos[...], _TN, preferred_element_type=F32)
            dk_ref[kv_rows, :] += lax.dot_general(ds_s[...], qs[...], _TN, preferred_element_type=F32)
            dq_s[...] += jnp.dot(ds_s[...], kv, preferred_element_type=F32)
            return carry

        lax.fori_loop(0, nk, step, 0)
        for h in range(GROUP):
            dq_ref[:, h * HEAD_DIM:(h + 1) * HEAD_DIM] = dq_s[h * tq:(h + 1) * tq, :].astype(BF)

    vb = _v_col_block(d)
    qspec = pl.BlockSpec((tq, gw), lambda g, i: (i, g))
    kspec = pl.BlockSpec((s, HEAD_DIM), lambda g, i: (0, g))
    (dq, dk, dv), cres = _call(
        body, name="flash_bwd", grid=(ng, nq),
        in_specs=[qspec, kspec, pl.BlockSpec((s, HEAD_DIM), lambda g, i: (0, vb + g)), qspec, qspec,
                  pl.BlockSpec((rows, LANES), lambda g, i: (g * nq + i, 0))],
        out_specs=[qspec, kspec, kspec],
        out_shape=[jax.ShapeDtypeStruct((s, d), BF), jax.ShapeDtypeStruct((s, d // GROUP), F32),
                   jax.ShapeDtypeStruct((s, d // GROUP), F32)],
        scratch_shapes=[pltpu.VMEM((rows, HEAD_DIM), BF), pltpu.VMEM((rows, HEAD_DIM), BF), pltpu.VMEM((rows, LANES), F32),
                        pltpu.VMEM((rows, HEAD_DIM), F32), pltpu.VMEM((rows, tk), BF), pltpu.VMEM((rows, tk), BF)],
        semantics=("parallel", "arbitrary"), args=[qt, kt, z, o, do, lse], comm=comm)
    return dq, dk, dv, cres


def _place():
    x, y, c = lax.axis_index("x"), lax.axis_index("y"), lax.axis_index("c")
    other_chips = [(1 - x, y), (x, 1 - y), (1 - x, 1 - y)]
    return x, y, c, other_chips


def _cast_place(name, w, chip_arr, tr=256):
    r, cc = w.shape
    tr = min(tr, r)

    def body(p_ref, w_ref, o_ref):
        o_ref[...] = w_ref[...].astype(BF)

    return pl.pallas_call(
        body, name=name,
        grid_spec=pltpu.PrefetchScalarGridSpec(
            num_scalar_prefetch=1, grid=(r // tr,),
            in_specs=[pl.BlockSpec((tr, cc), lambda i, p_ref: (i, 0))],
            out_specs=pl.BlockSpec((None, tr, cc), lambda i, p_ref: (p_ref[0], i, 0))),
        out_shape=jax.ShapeDtypeStruct((N_CHIPS, r, cc), BF),
        compiler_params=_params(("parallel",)),
    )(chip_arr, w)


def _gather_comm(bufs):
    n = len(bufs)
    pairs = [(w, j) for w in range(n) for j in range(N_CHIPS - 1)]

    def copies(dst, sems):
        send, recv, fsend, frecv = sems
        x, y, c, chips = _place()

        def part(w, chip, core_half):
            h = bufs[w].shape[1] // 2
            return dst[w].at[2 * chip[0] + chip[1], pl.ds(core_half * h, h)]

        def ici(w, j, incoming):
            slab = part(w, chips[j] if incoming else (x, y), c)
            return pltpu.make_async_remote_copy(
                src_ref=slab, dst_ref=slab, send_sem=send.at[3 * w + j], recv_sem=recv.at[3 * w + j],
                device_id=(*chips[j], c), device_id_type=MESH)

        def d2d(w, j, incoming):
            slab = part(w, chips[j], 1 - c if incoming else c)
            return pltpu.make_async_remote_copy(
                src_ref=slab, dst_ref=slab, send_sem=fsend.at[3 * w + j], recv_sem=frecv.at[3 * w + j],
                device_id=(x, y, 1 - c), device_id_type=MESH)

        return ici, d2d

    def first(_, dst, sems):
        ici, _d = copies(dst, sems)
        for w, j in pairs:
            ici(w, j, False).start()

    def middle(_, dst, sems):
        ici, d2d = copies(dst, sems)
        for w, j in pairs:
            ici(w, j, True).wait_recv()
            d2d(w, j, False).start()

    def last(_, dst, sems):
        ici, d2d = copies(dst, sems)
        for w, j in pairs:
            d2d(w, j, True).wait_recv()
        for w, j in pairs:
            ici(w, j, False).wait_send()
            d2d(w, j, False).wait_send()

    return _Comm(arrays=list(bufs), out_shapes=[jax.ShapeDtypeStruct(b.shape, b.dtype) for b in bufs],
                 aliases={w: w for w in range(n)}, sems=[pltpu.SemaphoreType.DMA((3 * n,))] * 4,
                 phases=(first, middle, last))


def _run_comm(name, comm):
    nci, nco = len(comm.arrays), len(comm.out_shapes)

    def body(*refs):
        cin, cout, sems = refs[:nci], refs[nci:nci + nco], refs[nci + nco:]
        for fn in comm.phases:
            if fn is not None:
                fn(cin, cout, sems)

    return pl.pallas_call(
        body, name=name, in_specs=[ANY] * nci, out_specs=[ANY] * nco, out_shape=list(comm.out_shapes),
        input_output_aliases=dict(comm.aliases), scratch_shapes=list(comm.sems),
    )(*comm.arrays)


def _pair_exchange(name, grads):
    n = len(grads)

    def body(*refs):
        src, dst = refs[:n], refs[n:2 * n]
        send, recv = refs[2 * n:]
        x, y, c, _ = _place()
        copies = []
        for w in range(n):
            h = grads[w].shape[1] // 2
            copies.append(pltpu.make_async_remote_copy(
                src_ref=src[w].at[:, pl.ds((1 - c) * h, h), :], dst_ref=dst[w],
                send_sem=send.at[w], recv_sem=recv.at[w], device_id=(x, y, 1 - c), device_id_type=MESH))
        for cp in copies:
            cp.start()
        for cp in copies:
            cp.wait()

    return pl.pallas_call(
        body, name=name,
        in_specs=[ANY] * n, out_specs=[ANY] * n,
        out_shape=[jax.ShapeDtypeStruct((N_CHIPS, g.shape[1] // 2, g.shape[2]), g.dtype) for g in grads],
        scratch_shapes=[pltpu.SemaphoreType.DMA((n,))] * 2,
    )(*grads)


def _pair_sum(name, own, got, c_arr, tr=256):
    nc, r, cc = own.shape
    h = r // 2
    tr = min(tr, h)
    nb = h // tr

    def body(c_ref, a_ref, b_ref, o_ref):
        o_ref[...] = (a_ref[...].astype(F32) + b_ref[...].astype(F32)).astype(BF)

    return pl.pallas_call(
        body, name=name,
        grid_spec=pltpu.PrefetchScalarGridSpec(
            num_scalar_prefetch=1, grid=(nc, nb),
            in_specs=[pl.BlockSpec((None, tr, cc), lambda s, i, c_ref: (s, c_ref[0] * nb + i, 0)),
                      pl.BlockSpec((None, tr, cc), lambda s, i, c_ref: (s, i, 0))],
            out_specs=pl.BlockSpec((None, tr, cc), lambda s, i, c_ref: (s, i, 0))),
        out_shape=jax.ShapeDtypeStruct((nc, h, cc), BF),
        compiler_params=_params(("parallel", "parallel")),
    )(c_arr, own, got)


def _chip_comm(parts):
    n = len(parts)

    def copies(src, dst, sems):
        send, recv = sems
        _, _, c, chips = _place()
        return [pltpu.make_async_remote_copy(
            src_ref=src[w].at[2 * chip[0] + chip[1]], dst_ref=dst[w].at[j],
            send_sem=send.at[3 * w + j], recv_sem=recv.at[3 * w + j], device_id=(*chip, c), device_id_type=MESH)
            for w in range(n) for j, chip in enumerate(chips)]

    def first(src, dst, sems):
        for cp in copies(src, dst, sems):
            cp.start()

    def last(src, dst, sems):
        for cp in copies(src, dst, sems):
            cp.wait()

    return _Comm(arrays=list(parts), out_shapes=[jax.ShapeDtypeStruct((N_CHIPS - 1,) + p.shape[1:], p.dtype) for p in parts],
                 aliases={}, sems=[pltpu.SemaphoreType.DMA((3 * n,))] * 2, phases=(first, None, last))


def _chip_sum(name, parts, got, chip_arr, c_arr, tr=256):
    _, h, cc = parts.shape
    tr = min(tr, h)
    nb = h // tr

    def body(chip_ref, c_ref, own_ref, got_ref, o_ref):
        acc = own_ref[...].astype(F32)
        for k in range(N_CHIPS - 1):
            acc = acc + got_ref[k].astype(F32)
        o_ref[...] = acc

    return pl.pallas_call(
        body, name=name,
        grid_spec=pltpu.PrefetchScalarGridSpec(
            num_scalar_prefetch=2, grid=(nb,),
            in_specs=[pl.BlockSpec((None, tr, cc), lambda i, chip_ref, c_ref: (chip_ref[0], i, 0)),
                      pl.BlockSpec((N_CHIPS - 1, tr, cc), lambda i, chip_ref, c_ref: (0, i, 0))],
            out_specs=pl.BlockSpec((tr, cc), lambda i, chip_ref, c_ref: (c_ref[0] * nb + i, 0))),
        out_shape=jax.ShapeDtypeStruct((2 * h, cc), F32),
        compiler_params=_params(("parallel",)),
    )(chip_arr, c_arr, parts, got)


def _pair_gather(bufs):
    n = len(bufs)

    def body(*refs):
        dst = refs[n:2 * n]
        send, recv = refs[2 * n:]
        x, y, c, _ = _place()

        def copy(w, core_half):
            h = bufs[w].shape[0] // 2
            rows = dst[w].at[pl.ds(core_half * h, h)]
            return pltpu.make_async_remote_copy(src_ref=rows, dst_ref=rows, send_sem=send.at[w], recv_sem=recv.at[w],
                                                device_id=(x, y, 1 - c), device_id_type=MESH)

        sends = [copy(w, c) for w in range(n)]
        for cp in sends:
            cp.start()
        for w in range(n):
            copy(w, 1 - c).wait_recv()
        for cp in sends:
            cp.wait_send()

    return pl.pallas_call(
        body, name="grad_pair_gather",
        in_specs=[ANY] * n, out_specs=[ANY] * n,
        out_shape=[jax.ShapeDtypeStruct(b.shape, b.dtype) for b in bufs],
        input_output_aliases={w: w for w in range(n)},
        scratch_shapes=[pltpu.SemaphoreType.DMA((n,))] * 2,
    )(*bufs)


def _all_sum_small(name, v):
    p = v.shape[0]

    def body(v_ref, o_ref, slots, send, recv):
        x, y, c, _ = _place()
        me = 4 * x + 2 * y + c
        copies = []
        for k in range(1, N_DEV):
            peer = (x ^ (k >> 2), y ^ ((k >> 1) & 1), c ^ (k & 1))
            copies.append(pltpu.make_async_remote_copy(
                src_ref=v_ref, dst_ref=slots.at[me], send_sem=send.at[k - 1], recv_sem=recv.at[k - 1],
                device_id=peer, device_id_type=MESH))
        for cp in copies:
            cp.start()
        slots[me] = v_ref[...]
        for cp in copies:
            cp.wait()
        acc = slots[0]
        for s in range(1, N_DEV):
            acc = acc + slots[s]
        o_ref[...] = acc

    vm = pl.BlockSpec(memory_space=pltpu.VMEM)
    return pl.pallas_call(
        body, name=name,
        in_specs=[vm], out_specs=vm,
        out_shape=jax.ShapeDtypeStruct(v.shape, F32),
        scratch_shapes=[pltpu.VMEM((N_DEV, p, LANES), F32), pltpu.SemaphoreType.DMA((N_DEV - 1,)),
                        pltpu.SemaphoreType.DMA((N_DEV - 1,))],
    )(v)


def _adamw(name, w, g, m, v, tr=256):
    r, c = w.shape
    tr = min(tr, r)
    assert r % tr == 0
    bc1 = 1.0 - ADAM_B1 ** ADAM_STEP
    bc2 = 1.0 - ADAM_B2 ** ADAM_STEP

    def body(w_ref, g_ref, m_ref, v_ref, d_ref, nm_ref, nv_ref):
        gv = g_ref[...]
        nm = ADAM_B1 * m_ref[...] + (1.0 - ADAM_B1) * gv
        nv = ADAM_B2 * v_ref[...] + (1.0 - ADAM_B2) * (gv * gv)
        nm_ref[...] = nm
        nv_ref[...] = nv
        d_ref[...] = -ADAM_LR * ((nm / bc1) / (jnp.sqrt(nv / bc2) + ADAM_EPS) + ADAM_WD * w_ref[...])

    blk = pl.BlockSpec((tr, c), lambda i: (i, 0))
    return pl.pallas_call(
        body, name=name, grid=(r // tr,),
        in_specs=[blk] * 4, out_specs=[blk] * 3,
        out_shape=[jax.ShapeDtypeStruct((r, c), F32)] * 3,
        compiler_params=_params(("parallel",)),
    )(w, g, m, v)


def _pack_small(parts):
    flat = jnp.concatenate([a.reshape(-1) for a in parts])
    n = flat.shape[0]
    p = -(-n // (8 * LANES)) * 8
    packed = jnp.pad(flat, (0, p * LANES - n)).reshape(p, LANES)

    def unpack(q):
        out, off = [], 0
        f = q.reshape(-1)
        for a in parts:
            out.append(f[off:off + a.size].reshape(a.shape))
            off += a.size
        return out

    return packed, unpack


def kernel(x, p, norm_mix, w_in, w_dw, conv_ln_g, conv_ln_b, w_conv_proj, q_norm, k_norm, w_attn_proj, w_out, norm_ffn, w_ff1, w_ff2, norm_ple, w_ple_gate, w_ple_proj, norm_final, loss_target, m_norm_mix, m_w_in, m_w_dw, m_conv_ln_g, m_conv_ln_b, m_w_conv_proj, m_q_norm, m_k_norm, m_w_attn_proj, m_w_out, m_norm_ffn, m_w_ff1, m_w_ff2, m_norm_ple, m_w_ple_gate, m_w_ple_proj, m_norm_final, v_norm_mix, v_w_in, v_w_dw, v_conv_ln_g, v_conv_ln_b, v_w_conv_proj, v_q_norm, v_k_norm, v_w_attn_proj, v_w_out, v_norm_ffn, v_w_ff1, v_w_ff2, v_norm_ple, v_w_ple_gate, v_w_ple_proj, v_norm_final):
    s, d = x.shape[1], x.shape[2]
    cw = d // 2
    kvw = d // GROUP
    xs, ps, tgt = x[0], p[0, 0], loss_target[0]
    cx, cy, cc = lax.axis_index("x"), lax.axis_index("y"), lax.axis_index("c")
    chip = 2 * cx + cy
    c_arr = jnp.reshape(cc, (1,)).astype(jnp.int32)
    tm, tme = min(MM_TM, s), min(MM_TM_EPI, s)

    names = ["w_in", "w_conv_proj", "w_attn_proj", "w_out", "w_ff1", "w_ff2", "w_ple_gate", "w_ple_proj"]
    big = [w_in, w_conv_proj, w_attn_proj, w_out, w_ff1, w_ff2, w_ple_gate, w_ple_proj]
    chip_arr = jnp.reshape(chip, (1,)).astype(jnp.int32)
    placed = [_cast_place("cast_" + nm, w[0], chip_arr) for nm, w in zip(names, big)]
    (win,) = _run_comm("gather_w_in", _gather_comm(placed[:1]))
    cpc = cw // N_CHIPS
    taps = jnp.zeros((32, N_CHIPS, cpc), F32).at[:CONV_KERNEL].set(
        jnp.where(lax.broadcasted_iota(jnp.int32, (1, N_CHIPS, 1), 1) == chip, w_dw[0][:, None, :], 0.0))
    taps = jnp.where(cc == 0, taps, 0.0).reshape(32 * cw // LANES, LANES)
    wdw = _all_sum_small("gather_taps", taps).reshape(32, cw)

    cos, sin = _rope_tables(s)
    h0 = _rms_fwd("rms_mix", xs, norm_mix)
    (z,) = _mm("z_proj", h0, win, b_cm=True, tm=tm, tn=win.shape[2] // 3, tk=d)
    uc, act = _conv_fwd(z, wdw, conv_ln_g, conv_ln_b, cw)
    qt, kt = _qk_fwd(z, cos, sin, q_norm, k_norm, d)
    o, lse, (wcp, wap, wout, w1, w2, wpg, wple) = _flash_fwd(qt, kt, z, d, comm=_gather_comm(placed[1:]))
    wap, wout, w2, wpg = (t.reshape(-1, t.shape[-1]) for t in (wap, wout, w2, wpg))
    (y_c,) = _mm("conv_proj", act, wcp, b_cm=True, tm=tm, tn=wcp.shape[2], tk=cw, out_dtypes=(F32,))
    tn = d // 2
    gcb = (2 * d + 2 * kvw) // tn

    def merge_epi(acc, yc, gc, ga):
        return acc, _sigmoid(gc.astype(F32)) * yc + _sigmoid(ga.astype(F32)) * acc

    y_a, merged = _mm("attn_proj", o, wap, tm=tme, tn=tn, tk=d, epi=merge_epi, out_dtypes=(BF, BF),
                      extras=[_tile_extra(y_c, tme, tn), _tile_extra(z, tme, tn, gcb), _tile_extra(z, tme, tn, gcb + 2)])
    (x1,) = _mm("out_proj", merged, wout, tm=tm, tn=tn, tk=d, epi=lambda acc, r: (r + acc,), out_dtypes=(F32,),
                extras=[_tile_extra(xs, tm, tn)])
    h1 = _rms_fwd("rms_ffn", x1, norm_ffn)
    (a,) = _mm("ff1", h1, w1, b_cm=True, tm=tm, tn=tn, tk=d)

    def relu2(t):
        return jnp.square(jnp.maximum(t, 0.0))

    (x2,) = _mm("ff2", a, w2, tm=tme, tn=tn, tk=d, a_fn=relu2, epi=lambda acc, r: (r + acc,), out_dtypes=(F32,),
                extras=[_tile_extra(x1, tme, tn)])
    h2 = _rms_fwd("rms_ple", x2, norm_ple)
    to_bf = lambda t: t.astype(BF)
    (e,) = _mm("ple_proj", ps, wple, b_cm=True, tm=tm, tn=wple.shape[2], tk=ps.shape[1], a_fn=to_bf)

    def ple_epi(acc, ev, r):
        gt = _sigmoid(acc)
        return r + gt * ev.astype(F32), gt

    x3, gate = _mm("ple_gate", h2, wpg, tm=tme, tn=tn, tk=d, epi=ple_epi, out_dtypes=(F32, BF),
                   extras=[_tile_extra(e, tme, tn), _tile_extra(x2, tme, tn)])

    dx3, de, dgp, sq, d_fin = _loss_bwd(x3, tgt, norm_final.reshape(1, d), e, gate)
    tkt = min(2048, s)
    (g_wple,) = _mm("d_wple", ps, de, ta=True, out_cm=True, tm=ps.shape[1], tn=wple.shape[2], tk=tkt, a_fn=to_bf)
    (g_wpg,) = _mm("d_wpg", h2, dgp, ta=True, tm=tm, tn=tn, tk=tkt)
    (dh2,) = _mm("d_h2", dgp, wpg, tb=True, tm=tm, tn=tn, tk=d)
    dx2, dx2b, d_ple = _rms_bwd("rms_ple_bwd", dh2, x2, norm_ple, dx3)

    (da,) = _mm("d_a", dx2b, w2, tb=True, tm=tm, tn=tn, tk=d, out_dtypes=(BF,),
                epi=lambda acc, av: (acc * (2.0 * jnp.maximum(av.astype(F32), 0.0)),), extras=[_tile_extra(a, tm, tn)])
    (g_w2,) = _mm("d_w2", a, dx2b, ta=True, tm=tm, tn=tn, tk=min(1024, s), a_fn=relu2)
    (g_w1,) = _mm("d_w1", h1, da, ta=True, out_cm=True, tm=tm, tn=tn, tk=tkt)
    (dh1,) = _mm("d_h1", da, w1, tb=True, b_cm=True, tm=tm, tn=tn, tk=w1.shape[2])
    dx1, dx1b, d_ffn = _rms_bwd("rms_ffn_bwd", dh1, x1, norm_ffn, dx2)

    def merge_bwd(acc, gc, ga, yc, ya):
        sc, sa = _sigmoid(gc.astype(F32)), _sigmoid(ga.astype(F32))
        return acc * sc, acc * sa, acc * yc * sc * (1.0 - sc), acc * ya.astype(F32) * sa * (1.0 - sa)

    dy_c, dy_a, dg_c, dg_a = _mm(
        "d_merged", dx1b, wout, tb=True, tm=tme, tn=tn, tk=d, epi=merge_bwd, out_dtypes=(BF, BF, BF, BF),
        extras=[_tile_extra(z, tme, tn, gcb), _tile_extra(z, tme, tn, gcb + 2), _tile_extra(y_c, tme, tn),
                _tile_extra(y_a, tme, tn)])
    (g_wout,) = _mm("d_wout", merged, dx1b, ta=True, tm=tm, tn=tn, tk=tkt)
    (g_wap,) = _mm("d_wap", o, dy_a, ta=True, tm=tm, tn=tn, tk=tkt)
    (do,) = _mm("d_o", dy_a, wap, tb=True, tm=tm, tn=tn, tk=d)

    def chip_parts_of(tag, nms, grads_):
        grads_ = [g if g.ndim == 3 else g.reshape(N_CHIPS, g.shape[0] // N_CHIPS, g.shape[1]) for g in grads_]
        got = _pair_exchange("grad_pair_exchange_" + tag, grads_)
        return [_pair_sum("pair_sum_" + nm, g, r, c_arr) for nm, g, r in zip(nms, grads_, got)]

    parts_a = chip_parts_of("a", names[2:], [g_wap, g_wout, g_w1, g_w2, g_wpg, g_wple])
    dqt, dkt, dv, _ = _flash_bwd(qt, kt, z, o, do, lse, d)
    dq, dk, d_qn, d_kn = _qk_bwd(dqt, dkt, z, cos, sin, q_norm, k_norm, d)
    (g_wcp,) = _mm("d_wcp", act, dy_c, ta=True, out_cm=True, tm=cw, tn=wcp.shape[2], tk=tkt)
    (dact,) = _mm("d_act", dy_c, wcp, tb=True, b_cm=True, tm=tm, tn=cw, tk=wcp.shape[2])
    dcab, d_taps, d_lng, d_lnb = _conv_bwd(dact, uc, z, wdw, conv_ln_g, conv_ln_b, cw)
    dz = jnp.concatenate([dcab, dq, dk, dv.astype(BF), dg_c, dg_a], axis=1)
    (g_win,), slots_a = _mm("d_win", h0, dz, ta=True, out_cm=True, tm=tm, tn=win.shape[2] // 3, tk=tkt,
                            comm=_chip_comm(parts_a))
    parts_b = chip_parts_of("b", names[:2], [g_win, g_wcp])
    (dh0,), slots_b = _mm("d_h0", dz, win, tb=True, b_cm=True, tm=tm, tn=tn, tk=win.shape[2], comm=_chip_comm(parts_b))
    dx, _, d_mix = _rms_bwd("rms_mix_bwd", dh0, xs, norm_mix, dx1)
    big_grads = _pair_gather(
        [_chip_sum("chip_sum_" + nm, cp, sl, chip_arr, c_arr)
         for nm, cp, sl in zip(names, parts_b + parts_a, list(slots_b) + list(slots_a))])

    small = [d_mix, d_taps[:CONV_KERNEL], d_lng, d_lnb, d_qn, d_kn, d_ffn, d_ple, d_fin]
    packed, unpack = _pack_small(small)
    g_mix, g_taps, g_lng, g_lnb, g_qn, g_kn, g_ffn, g_ple, g_fin = unpack(_all_sum_small("reduce_small", packed))
    g_dw = lax.dynamic_slice_in_dim(g_taps.reshape(CONV_KERNEL, N_CHIPS, cpc), chip, 1, axis=1).reshape(1, CONV_KERNEL, cpc)

    sq_local = lax.reduce_precision(sq[0, 0], 8, 23)
    loss = (0.5 / d) * lax.psum(sq_local, ("x", "y", "c"))

    grads = {
        "norm_mix": g_mix, "w_in": big_grads[0][None], "w_dw": g_dw, "conv_ln_g": g_lng, "conv_ln_b": g_lnb,
        "w_conv_proj": big_grads[1][None], "q_norm": g_qn, "k_norm": g_kn, "w_attn_proj": big_grads[2][None],
        "w_out": big_grads[3][None], "norm_ffn": g_ffn, "w_ff1": big_grads[4][None], "w_ff2": big_grads[5][None],
        "norm_ple": g_ple, "w_ple_gate": big_grads[6][None], "w_ple_proj": big_grads[7][None],
        "norm_final": g_fin.reshape(d),
    }
    weights = dict(norm_mix=norm_mix, w_in=w_in, w_dw=w_dw, conv_ln_g=conv_ln_g, conv_ln_b=conv_ln_b, w_conv_proj=w_conv_proj,
                   q_norm=q_norm, k_norm=k_norm, w_attn_proj=w_attn_proj, w_out=w_out, norm_ffn=norm_ffn, w_ff1=w_ff1,
                   w_ff2=w_ff2, norm_ple=norm_ple, w_ple_gate=w_ple_gate, w_ple_proj=w_ple_proj, norm_final=norm_final)
    m_in = dict(norm_mix=m_norm_mix, w_in=m_w_in, w_dw=m_w_dw, conv_ln_g=m_conv_ln_g, conv_ln_b=m_conv_ln_b,
                w_conv_proj=m_w_conv_proj, q_norm=m_q_norm, k_norm=m_k_norm, w_attn_proj=m_w_attn_proj, w_out=m_w_out,
                norm_ffn=m_norm_ffn, w_ff1=m_w_ff1, w_ff2=m_w_ff2, norm_ple=m_norm_ple, w_ple_gate=m_w_ple_gate,
                w_ple_proj=m_w_ple_proj, norm_final=m_norm_final)
    v_in = dict(norm_mix=v_norm_mix, w_in=v_w_in, w_dw=v_w_dw, conv_ln_g=v_conv_ln_g, conv_ln_b=v_conv_ln_b,
                w_conv_proj=v_w_conv_proj, q_norm=v_q_norm, k_norm=v_k_norm, w_attn_proj=v_w_attn_proj, w_out=v_w_out,
                norm_ffn=v_norm_ffn, w_ff1=v_w_ff1, w_ff2=v_w_ff2, norm_ple=v_norm_ple, w_ple_gate=v_w_ple_gate,
                w_ple_proj=v_w_ple_proj, norm_final=v_norm_final)
    order = list(weights)
    deltas, new_m, new_v, g_out = [], [], [], []
    for nm in order:
        w = weights[nm]
        shape = w.shape
        two_d = (-1, shape[-1])
        dl, mm_, vv_ = _adamw("adamw_" + nm, w.reshape(two_d), grads[nm].reshape(two_d), m_in[nm].reshape(two_d),
                              v_in[nm].reshape(two_d))
        g_out.append(grads[nm].reshape(shape))
        deltas.append(dl.reshape(shape))
        new_m.append(mm_.reshape(shape))
        new_v.append(vv_.reshape(shape))
    return (loss, dx[None], *g_out, *deltas, *new_m, *new_v)
```

```python
from typing import NamedTuple

import jax
import jax.numpy as jnp
from jax import lax
from jax.experimental import pallas as pl
from jax.experimental.pallas import tpu as pltpu

F32 = jnp.float32
BF = jnp.bfloat16

EPS = 1e-6
HEAD_DIM = 128
GROUP = 4
GRID_W = 64
ROPE_THETA = 10000.0
CONV_KERNEL = 31
HALO = 16
N_CHIPS = 4
N_DEV = 8
LANES = 128

ADAM_LR = 0.001
ADAM_B1 = 0.9
ADAM_B2 = 0.999
ADAM_EPS = 1e-08
ADAM_WD = 0.01
ADAM_STEP = 10

VMEM_LIMIT = 56 * 2 ** 20
LOG2E = 1.4426950408889634
LN2 = 0.6931471805599453
Q_SCALE = HEAD_DIM ** -0.5 * LOG2E
ROW_TILE = 256
FLASH_TQ_FWD = 512
FLASH_TQ_BWD = 256
FLASH_TK = 512
MM_TM = 1024
MM_TM_EPI = 512
MESH = pl.DeviceIdType.MESH
ANY = pl.BlockSpec(memory_space=pl.ANY)


def _params(sem):
    return pltpu.CompilerParams(dimension_semantics=sem, vmem_limit_bytes=VMEM_LIMIT)


def _sigmoid(x):
    return 1.0 / (1.0 + jnp.exp(-x))


class _Comm(NamedTuple):
    arrays: list
    out_shapes: list
    aliases: dict
    sems: list
    phases: tuple


def _call(body, *, name, grid, in_specs, out_specs, out_shape, scratch_shapes, semantics, args, comm=None):
    n_in, n_out = len(in_specs), len(out_specs)
    if comm is None:
        res = pl.pallas_call(body, name=name, grid=grid, in_specs=in_specs, out_specs=out_specs, out_shape=out_shape,
                             scratch_shapes=scratch_shapes, compiler_params=_params(semantics))(*args)
        return res, []
    nci, nco, ncs = len(comm.arrays), len(comm.out_shapes), len(comm.sems)
    n_steps = 1
    for g in grid:
        n_steps *= g
    first, middle, last = comm.phases

    def hosted(*refs):
        ins, cin = refs[:n_in], refs[n_in:n_in + nci]
        outs = refs[n_in + nci:n_in + nci + n_out]
        cout = refs[n_in + nci + n_out:n_in + nci + n_out + nco]
        rest = refs[n_in + nci + n_out + nco:]
        scratch, sems = rest[:len(rest) - ncs], rest[len(rest) - ncs:]
        step = 0
        for ax, g in enumerate(grid):
            step = step * g + pl.program_id(ax)
        for at, fn in ((0, first), (n_steps // 2, middle)):
            if fn is not None:
                pl.when(step == at)(lambda fn=fn: fn(cin, cout, sems))
        body(*ins, *outs, *scratch)
        if last is not None:
            pl.when(step == n_steps - 1)(lambda: last(cin, cout, sems))

    res = pl.pallas_call(
        hosted, name=name, grid=grid,
        in_specs=list(in_specs) + [ANY] * nci, out_specs=list(out_specs) + [ANY] * nco,
        out_shape=list(out_shape) + list(comm.out_shapes),
        input_output_aliases={n_in + a: n_out + b for a, b in comm.aliases.items()},
        scratch_shapes=list(scratch_shapes) + list(comm.sems),
        compiler_params=_params(("arbitrary",) * len(grid)),
    )(*args, *comm.arrays)
    return res[:n_out], res[n_out:]


def _mm(name, a, b, *, tm, tn, tk, ta=False, tb=False, b_cm=False, out_cm=False,
        a_fn=None, extras=(), epi=None, out_dtypes=(BF,), epi_rows=256, comm=None):
    if ta:
        kc, m = a.shape
    else:
        m, kc = a.shape
    if b_cm:
        nc, r, c = b.shape
        n, per = (r, c) if tb else (nc * c, c)
    else:
        n = b.shape[0] if tb else b.shape[1]
    tm, tn, tk = min(tm, m), min(tn, n), min(tk, kc)
    assert m % tm == 0 and n % tn == 0 and kc % tk == 0, (name, m, n, kc, tm, tn, tk)
    nk = kc // tk
    a_spec = pl.BlockSpec((tk, tm), lambda i, j, k: (k, i)) if ta else pl.BlockSpec((tm, tk), lambda i, j, k: (i, k))
    if b_cm and not tb:
        assert per % tn == 0
        npj = per // tn
        b_spec = pl.BlockSpec((None, tk, tn), lambda i, j, k: (j // npj, k, j % npj))
    elif b_cm:
        assert per % tk == 0
        npk = per // tk
        b_spec = pl.BlockSpec((None, tn, tk), lambda i, j, k: (k // npk, j, k % npk))
    elif tb:
        b_spec = pl.BlockSpec((tn, tk), lambda i, j, k: (j, k))
    else:
        b_spec = pl.BlockSpec((tk, tn), lambda i, j, k: (k, j))
    if out_cm:
        assert (n // N_CHIPS) % tn == 0
        npo = (n // N_CHIPS) // tn
        o_spec = pl.BlockSpec((None, tm, tn), lambda i, j, k: (j // npo, i, j % npo))
        o_shape = (N_CHIPS, m, n // N_CHIPS)
    else:
        o_spec = pl.BlockSpec((tm, tn), lambda i, j, k: (i, j))
        o_shape = (m, n)
    ne, no = len(extras), len(out_dtypes)
    dims = (((0 if ta else 1,), (1 if tb else 0,)), ((), ()))
    use_acc = nk > 1 or epi is not None
    er = min(epi_rows, tm)

    def body(*refs):
        a_ref, b_ref = refs[0], refs[1]
        ex = refs[2:2 + ne]
        outs = refs[2 + ne:2 + ne + no]
        at = a_ref[...]
        if a_fn is not None:
            at = a_fn(at)
        d = lax.dot_general(at, b_ref[...], dims, preferred_element_type=F32)
        if not use_acc:
            outs[0][...] = d.astype(out_dtypes[0])
            return
        acc = refs[-1]
        k = pl.program_id(2)

        @pl.when(k == 0)
        def _():
            acc[...] = d

        if nk > 1:
            @pl.when(k > 0)
            def _():
                acc[...] += d

        @pl.when(k == nk - 1)
        def _():
            for r0 in range(0, tm, er):
                rows = slice(r0, r0 + er)
                if epi is None:
                    vals = (acc[rows, :],)
                else:
                    vals = epi(acc[rows, :], *[e[rows, :] for e in ex])
                for o, v, dt in zip(outs, vals, out_dtypes):
                    o[rows, :] = v.astype(dt)

    res, cres = _call(
        body, name=name, grid=(m // tm, n // tn, nk),
        in_specs=[a_spec, b_spec] + [pl.BlockSpec(bs, im) for _, bs, im in extras],
        out_specs=[o_spec] * no,
        out_shape=[jax.ShapeDtypeStruct(o_shape, dt) for dt in out_dtypes],
        scratch_shapes=[pltpu.VMEM((tm, tn), F32)] if use_acc else [],
        semantics=("parallel", "parallel", "arbitrary"),
        args=[a, b] + [e for e, _, _ in extras], comm=comm)
    return res if comm is None else (res, cres)


def _tile_extra(arr, tm, tn, col_block0=0):
    return (arr, (tm, tn), lambda i, j, k: (i, j + col_block0))


def _rms_fwd(name, x, g, ts=None, comm=None):
    s, d = x.shape
    ts = ts or ROW_TILE

    def body(x_ref, g_ref, h_ref):
        xv = x_ref[...]
        r = lax.rsqrt(jnp.mean(xv * xv, axis=-1, keepdims=True) + EPS)
        h_ref[...] = (xv * r * g_ref[...]).astype(BF)

    (h,), cres = _call(
        body, name=name, grid=(s // ts,),
        in_specs=[pl.BlockSpec((ts, d), lambda i: (i, 0)), pl.BlockSpec((1, d), lambda i: (0, 0))],
        out_specs=[pl.BlockSpec((ts, d), lambda i: (i, 0))],
        out_shape=[jax.ShapeDtypeStruct((s, d), BF)],
        scratch_shapes=[], semantics=("parallel",), args=[x, g], comm=comm)
    return h if comm is None else (h, cres)


def _rms_bwd(name, dh, x, g, dres, ts=None):
    s, d = x.shape
    ts = ts or ROW_TILE

    def body(dh_ref, x_ref, g_ref, dres_ref, dx_ref, dxb_ref, dg_ref):
        xv = x_ref[...]
        dhv = dh_ref[...].astype(F32)
        r = lax.rsqrt(jnp.mean(xv * xv, axis=-1, keepdims=True) + EPS)
        nrm = xv * r
        dn = dhv * g_ref[...]
        dx = dres_ref[...] + r * (dn - nrm * jnp.mean(dn * nrm, axis=-1, keepdims=True))
        dx_ref[...] = dx
        dxb_ref[...] = dx.astype(BF)
        part = jnp.sum(dhv * nrm, axis=0, keepdims=True)

        @pl.when(pl.program_id(0) == 0)
        def _():
            dg_ref[...] = part

        @pl.when(pl.program_id(0) > 0)
        def _():
            dg_ref[...] += part

    row = pl.BlockSpec((ts, d), lambda i: (i, 0))
    vec = pl.BlockSpec((1, d), lambda i: (0, 0))
    return pl.pallas_call(
        body, name=name, grid=(s // ts,),
        in_specs=[row, row, vec, row],
        out_specs=[row, row, vec],
        out_shape=[jax.ShapeDtypeStruct((s, d), F32), jax.ShapeDtypeStruct((s, d), BF), jax.ShapeDtypeStruct((1, d), F32)],
        compiler_params=_params(("arbitrary",)),
    )(dh, x, g, dres)


def _loss_bwd(x3, tgt, gfin, e, gate, ts=None):
    s, d = x3.shape
    ts = ts or ROW_TILE

    def body(x_ref, t_ref, g_ref, e_ref, gate_ref, dx_ref, de_ref, dgp_ref, sq_ref, dg_ref):
        xv = x_ref[...]
        gv = g_ref[...]
        r = lax.rsqrt(jnp.mean(xv * xv, axis=-1, keepdims=True) + EPS)
        nrm = xv * r
        err = nrm * gv - t_ref[...]
        dy = err * (1.0 / d)
        dn = dy * gv
        dx = r * (dn - nrm * jnp.mean(dn * nrm, axis=-1, keepdims=True))
        dx_ref[...] = dx
        ev = e_ref[...].astype(F32)
        gt = gate_ref[...].astype(F32)
        de_ref[...] = (dx * gt).astype(BF)
        dgp_ref[...] = (dx * ev * gt * (1.0 - gt)).astype(BF)
        sq = jnp.full((8, LANES), jnp.sum(err * err), F32)
        part = jnp.sum(dy * nrm, axis=0, keepdims=True)

        @pl.when(pl.program_id(0) == 0)
        def _():
            sq_ref[...] = sq
            dg_ref[...] = part

        @pl.when(pl.program_id(0) > 0)
        def _():
            sq_ref[...] += sq
            dg_ref[...] += part

    row = pl.BlockSpec((ts, d), lambda i: (i, 0))
    vec = pl.BlockSpec((1, d), lambda i: (0, 0))
    return pl.pallas_call(
        body, name="loss_bwd", grid=(s // ts,),
        in_specs=[row, row, vec, row, row],
        out_specs=[row, row, row, pl.BlockSpec((8, LANES), lambda i: (0, 0)), vec],
        out_shape=[jax.ShapeDtypeStruct((s, d), F32), jax.ShapeDtypeStruct((s, d), BF), jax.ShapeDtypeStruct((s, d), BF),
                   jax.ShapeDtypeStruct((8, LANES), F32), jax.ShapeDtypeStruct((1, d), F32)],
        compiler_params=_params(("arbitrary",)),
    )(x3, tgt, gfin, e, gate)


def _halo_specs(ts, s, width, col_block):
    per = ts // HALO
    last = s // HALO - 1
    return [
        pl.BlockSpec((HALO, width), lambda i: (jnp.maximum(i * per - 1, 0), col_block)),
        pl.BlockSpec((ts, width), lambda i: (i, col_block)),
        pl.BlockSpec((HALO, width), lambda i: (jnp.minimum((i + 1) * per, last), col_block)),
    ]


def _glu_ext(zp, zc, zn, ext, cw, ts, i, n_tiles):
    def glu(zr):
        zv = zr[...].astype(F32)
        return zv[:, :cw] * _sigmoid(zv[:, cw:])

    ext[0:HALO, :] = jnp.where(i > 0, glu(zp), 0.0)
    ext[HALO:HALO + ts, :] = glu(zc)
    ext[HALO + ts:, :] = jnp.where(i < n_tiles - 1, glu(zn), 0.0)


SUBLANES = 8


def _shift_scratch(ts):
    return pltpu.VMEM((SUBLANES, ts + 2 * HALO - SUBLANES, LANES), F32)


def _shifted_copies(ext, sh, cols, ts):
    n = ts + 2 * HALO - SUBLANES
    for r in range(SUBLANES):
        sh[r] = ext[r:r + n, cols]


def _tap_rows(sh, off, ts):
    q, r = divmod(off, SUBLANES)
    return sh[r, q * SUBLANES:q * SUBLANES + ts, :]


def _ln_stats(uc):
    mu = jnp.mean(uc, axis=-1, keepdims=True)
    xc = uc - mu
    rstd = lax.rsqrt(jnp.mean(xc * xc, axis=-1, keepdims=True) + EPS)
    return xc * rstd, rstd


def _conv_fwd(z, wdw, ln_g, ln_b, cw, ts=None):
    s = z.shape[0]
    ts = ts or ROW_TILE
    n_tiles = s // ts
    pad = CONV_KERNEL // 2

    def body(zp, zc, zn, w_ref, g_ref, b_ref, uc_ref, act_ref, ext, sh):
        i = pl.program_id(0)
        _glu_ext(zp, zc, zn, ext, cw, ts, i, n_tiles)

        def col_block(cb, carry):
            cols = pl.ds(pl.multiple_of(cb * LANES, LANES), LANES)
            _shifted_copies(ext, sh, cols, ts)
            acc = jnp.zeros((ts, LANES), F32)
            for j in range(CONV_KERNEL):
                acc = acc + _tap_rows(sh, HALO - pad + j, ts) * w_ref[j:j + 1, cols]
            uc_ref[:, cols] = acc
            return carry

        lax.fori_loop(0, cw // LANES, col_block, 0)
        xhat, _ = _ln_stats(uc_ref[...])
        ln = xhat * g_ref[...] + b_ref[...]
        act_ref[...] = (ln * _sigmoid(ln)).astype(BF)

    vec = pl.BlockSpec((1, cw), lambda i: (0, 0))
    row = pl.BlockSpec((ts, cw), lambda i: (i, 0))
    return pl.pallas_call(
        body, name="conv_fwd", grid=(n_tiles,),
        in_specs=_halo_specs(ts, s, 2 * cw, 0) + [pl.BlockSpec((32, cw), lambda i: (0, 0)), vec, vec],
        out_specs=[row, row],
        out_shape=[jax.ShapeDtypeStruct((s, cw), F32), jax.ShapeDtypeStruct((s, cw), BF)],
        scratch_shapes=[pltpu.VMEM((ts + 2 * HALO, cw), F32), _shift_scratch(ts)],
        compiler_params=_params(("parallel",)),
    )(z, z, z, wdw, ln_g, ln_b)


def _conv_bwd(ds, uc, z, wdw, ln_g, ln_b, cw, ts=None, comm=None):
    s = z.shape[0]
    ts = ts or ROW_TILE
    n_tiles = s // ts
    pad = CONV_KERNEL // 2

    def body(zp, zc, zn, dsp, dsc, dsn, ucp, ucc, ucn, w_ref, g_ref, b_ref,
             dz_ref, dw_ref, dg_ref, db_ref, ext, dext, sh, dsh):
        i = pl.program_id(0)
        gv, bv = g_ref[...], b_ref[...]

        def ln_bwd(ds_r, uc_r):
            xhat, rstd = _ln_stats(uc_r[...])
            ln = xhat * gv + bv
            sg = _sigmoid(ln)
            dln = ds_r[...].astype(F32) * (sg * (1.0 + ln * (1.0 - sg)))
            dxh = dln * gv
            duc = rstd * (dxh - jnp.mean(dxh, axis=-1, keepdims=True) - xhat * jnp.mean(dxh * xhat, axis=-1, keepdims=True))
            return duc, dln, xhat

        duc_p, _, _ = ln_bwd(dsp, ucp)
        duc_c, dln_c, xhat_c = ln_bwd(dsc, ucc)
        duc_n, _, _ = ln_bwd(dsn, ucn)
        dext[0:HALO, :] = jnp.where(i > 0, duc_p, 0.0)
        dext[HALO:HALO + ts, :] = duc_c
        dext[HALO + ts:, :] = jnp.where(i < n_tiles - 1, duc_n, 0.0)
        _glu_ext(zp, zc, zn, ext, cw, ts, i, n_tiles)

        dg_part = jnp.sum(dln_c * xhat_c, axis=0, keepdims=True)
        db_part = jnp.sum(dln_c, axis=0, keepdims=True)

        @pl.when(i == 0)
        def _():
            dw_ref[...] = jnp.zeros_like(dw_ref)
            dg_ref[...] = dg_part
            db_ref[...] = db_part

        @pl.when(i > 0)
        def _():
            dg_ref[...] += dg_part
            db_ref[...] += db_part

        def col_block(cb, carry):
            c0 = pl.multiple_of(cb * LANES, LANES)
            cols, gate_cols = pl.ds(c0, LANES), pl.ds(cw + c0, LANES)
            _shifted_copies(dext, dsh, cols, ts)
            _shifted_copies(ext, sh, cols, ts)
            du = jnp.zeros((ts, LANES), F32)
            for j in range(CONV_KERNEL):
                du = du + _tap_rows(dsh, HALO + pad - j, ts) * w_ref[j:j + 1, cols]
            ca, sb = zc[:, cols].astype(F32), _sigmoid(zc[:, gate_cols].astype(F32))
            dz_ref[:, cols] = (du * sb).astype(BF)
            dz_ref[:, gate_cols] = (du * ca * sb * (1.0 - sb)).astype(BF)
            duc_blk = _tap_rows(dsh, HALO, ts)
            for j in range(CONV_KERNEL):
                dw_ref[j:j + 1, cols] += jnp.sum(_tap_rows(sh, HALO - pad + j, ts) * duc_blk, axis=0, keepdims=True)
            return carry

        lax.fori_loop(0, cw // LANES, col_block, 0)

    vec = pl.BlockSpec((1, cw), lambda i: (0, 0))
    wsp = pl.BlockSpec((32, cw), lambda i: (0, 0))
    res, cres = _call(
        body, name="conv_bwd", grid=(n_tiles,),
        in_specs=_halo_specs(ts, s, 2 * cw, 0) + _halo_specs(ts, s, cw, 0) + _halo_specs(ts, s, cw, 0) + [wsp, vec, vec],
        out_specs=[pl.BlockSpec((ts, 2 * cw), lambda i: (i, 0)), wsp, vec, vec],
        out_shape=[jax.ShapeDtypeStruct((s, 2 * cw), BF), jax.ShapeDtypeStruct((32, cw), F32),
                   jax.ShapeDtypeStruct((1, cw), F32), jax.ShapeDtypeStruct((1, cw), F32)],
        scratch_shapes=[pltpu.VMEM((ts + 2 * HALO, cw), F32), pltpu.VMEM((ts + 2 * HALO, cw), F32),
                        _shift_scratch(ts), _shift_scratch(ts)],
        semantics=("arbitrary",), args=[z, z, z, ds, ds, ds, uc, uc, uc, wdw, ln_g, ln_b], comm=comm)
    return (*res, cres)


def _rope_tables(s):
    axis_dim = HEAD_DIM // 2
    t = jnp.arange(s, dtype=jnp.int32)
    row = (t // GRID_W).astype(F32)[:, None]
    col = (t % GRID_W).astype(F32)[:, None]
    inv_freq = ROPE_THETA ** (-jnp.arange(0, axis_dim, 2, dtype=F32) / axis_dim)[None, :]
    ar, ac = row * inv_freq, col * inv_freq
    cos = jnp.concatenate([jnp.cos(ar), jnp.cos(ar), jnp.cos(ac), jnp.cos(ac)], axis=-1)
    sin = jnp.concatenate([-jnp.sin(ar), jnp.sin(ar), -jnp.sin(ac), jnp.sin(ac)], axis=-1)
    return cos, sin


def _swap_quarters(x):
    q = HEAD_DIM // 4
    lane = lax.broadcasted_iota(jnp.int32, x.shape, 1)
    return jnp.where((lane % (2 * q)) < q, pltpu.roll(x, HEAD_DIM - q, 1), pltpu.roll(x, q, 1))


def _qk_fwd(z, cos, sin, qg, kg, d, ts=None):
    s = z.shape[0]
    ts = ts or ROW_TILE
    kvw = d // GROUP
    scale = Q_SCALE

    def body(q_ref, k_ref, c_ref, s_ref, qg_ref, kg_ref, qo_ref, ko_ref):
        cv, sv = c_ref[...], s_ref[...]

        def head(x_ref, g_ref, o_ref, h, mul):
            xv = x_ref[:, h * HEAD_DIM:(h + 1) * HEAD_DIM].astype(F32)
            r = lax.rsqrt(jnp.mean(xv * xv, axis=-1, keepdims=True) + EPS)
            nrm = xv * r * g_ref[...]
            out = nrm * cv + _swap_quarters(nrm) * sv
            o_ref[:, h * HEAD_DIM:(h + 1) * HEAD_DIM] = (out * mul).astype(BF)

        for h in range(d // HEAD_DIM):
            head(q_ref, qg_ref, qo_ref, h, scale)
        for h in range(kvw // HEAD_DIM):
            head(k_ref, kg_ref, ko_ref, h, 1.0)

    cw2 = d
    tab = pl.BlockSpec((ts, HEAD_DIM), lambda i: (i, 0))
    vec = pl.BlockSpec((1, HEAD_DIM), lambda i: (0, 0))
    return pl.pallas_call(
        body, name="qk_fwd", grid=(s // ts,),
        in_specs=[pl.BlockSpec((ts, d), lambda i: (i, cw2 // d)),
                  pl.BlockSpec((ts, kvw), lambda i: (i, (cw2 + d) // kvw)), tab, tab, vec, vec],
        out_specs=[pl.BlockSpec((ts, d), lambda i: (i, 0)), pl.BlockSpec((ts, kvw), lambda i: (i, 0))],
        out_shape=[jax.ShapeDtypeStruct((s, d), BF), jax.ShapeDtypeStruct((s, kvw), BF)],
        compiler_params=_params(("parallel",)),
    )(z, z, cos, sin, qg, kg)


def _qk_bwd(dqt, dkt, z, cos, sin, qg, kg, d, ts=None):
    s = z.shape[0]
    ts = ts or ROW_TILE
    kvw = d // GROUP
    scale = HEAD_DIM ** -0.5

    def body(dq_ref, dk_ref, q_ref, k_ref, c_ref, s_ref, qg_ref, kg_ref, dqo_ref, dko_ref, dqg_ref, dkg_ref):
        cv, sv = c_ref[...], s_ref[...]

        def head(dy_ref, x_ref, g_ref, o_ref, h, mul):
            dout = dy_ref[:, h * HEAD_DIM:(h + 1) * HEAD_DIM].astype(F32) * mul
            dn = dout * cv + _swap_quarters(dout * sv)
            xv = x_ref[:, h * HEAD_DIM:(h + 1) * HEAD_DIM].astype(F32)
            r = lax.rsqrt(jnp.mean(xv * xv, axis=-1, keepdims=True) + EPS)
            nh = xv * r
            dnh = dn * g_ref[...]
            o_ref[:, h * HEAD_DIM:(h + 1) * HEAD_DIM] = (r * (dnh - nh * jnp.mean(dnh * nh, axis=-1, keepdims=True))).astype(BF)
            return jnp.sum(dn * nh, axis=0, keepdims=True)

        dqg = jnp.zeros((1, HEAD_DIM), F32)
        for h in range(d // HEAD_DIM):
            dqg = dqg + head(dq_ref, q_ref, qg_ref, dqo_ref, h, scale)
        dkg = jnp.zeros((1, HEAD_DIM), F32)
        for h in range(kvw // HEAD_DIM):
            dkg = dkg + head(dk_ref, k_ref, kg_ref, dko_ref, h, LN2)

        @pl.when(pl.program_id(0) == 0)
        def _():
            dqg_ref[...] = dqg
            dkg_ref[...] = dkg

        @pl.when(pl.program_id(0) > 0)
        def _():
            dqg_ref[...] += dqg
            dkg_ref[...] += dkg

    cw2 = d
    tab = pl.BlockSpec((ts, HEAD_DIM), lambda i: (i, 0))
    vec = pl.BlockSpec((1, HEAD_DIM), lambda i: (0, 0))
    qrow = pl.BlockSpec((ts, d), lambda i: (i, 0))
    krow = pl.BlockSpec((ts, kvw), lambda i: (i, 0))
    return pl.pallas_call(
        body, name="qk_bwd", grid=(s // ts,),
        in_specs=[qrow, krow, pl.BlockSpec((ts, d), lambda i: (i, cw2 // d)),
                  pl.BlockSpec((ts, kvw), lambda i: (i, (cw2 + d) // kvw)), tab, tab, vec, vec],
        out_specs=[qrow, krow, vec, vec],
        out_shape=[jax.ShapeDtypeStruct((s, d), BF), jax.ShapeDtypeStruct((s, kvw), BF),
                   jax.ShapeDtypeStruct((1, HEAD_DIM), F32), jax.ShapeDtypeStruct((1, HEAD_DIM), F32)],
        compiler_params=_params(("arbitrary",)),
    )(dqt, dkt, z, z, cos, sin, qg, kg)


_NT = (((1,), (1,)), ((), ()))
_TN = (((0,), (0,)), ((), ()))


def _v_col_block(d):
    return (2 * d + d // GROUP) // HEAD_DIM


def _flash_fwd(qt, kt, z, d, tq=None, tk=None, comm=None):
    s = qt.shape[0]
    tq, tk = min(tq or FLASH_TQ_FWD, s), min(tk or FLASH_TK, s)
    ng, nq, nk = d // (GROUP * HEAD_DIM), s // tq, s // tk
    gw = GROUP * HEAD_DIM
    rows = GROUP * tq

    nt = tk // LANES
    assert nk % 2 == 0, (s, tk)

    def body(q_ref, k_ref, v_ref, o_ref, lse_ref, qs, v1, p_s, m_s, acc_s, sc_s):
        @pl.when(pl.program_id(1) == 0)
        def _():
            v1[:, :HEAD_DIM] = v_ref[...]
            v1[:, HEAD_DIM:] = jnp.ones((s, HEAD_DIM), BF)

        for h in range(GROUP):
            qs[h * tq:(h + 1) * tq, :] = q_ref[:, h * HEAD_DIM:(h + 1) * HEAD_DIM]
        m_s[...] = jnp.full((rows, LANES), -1e30, F32)
        acc_s[...] = jnp.zeros((rows, 2 * HEAD_DIM), F32)

        def scores(j):
            return lax.dot_general(qs[...], k_ref[pl.ds(pl.multiple_of(j * tk, tk), tk), :], _NT, preferred_element_type=F32)

        def softmax_pv(j, sc):
            kv_rows = pl.ds(pl.multiple_of(j * tk, tk), tk)
            mt = sc[:, :LANES]
            for c in range(1, nt):
                mt = jnp.maximum(mt, sc[:, c * LANES:(c + 1) * LANES])
            m_old = m_s[...]
            m_new = jnp.maximum(m_old, jnp.max(mt, axis=-1, keepdims=True))
            alpha = jnp.exp2(m_old - m_new)
            for c in range(nt):
                cs = slice(c * LANES, (c + 1) * LANES)
                p_s[:, cs] = jnp.exp2(sc[:, cs] - m_new).astype(BF)
            pv = jnp.dot(p_s[...], v1[kv_rows, :], preferred_element_type=F32)
            acc_s[:, :HEAD_DIM] = alpha * acc_s[:, :HEAD_DIM] + pv[:, :HEAD_DIM]
            acc_s[:, HEAD_DIM:] = alpha * acc_s[:, HEAD_DIM:] + pv[:, HEAD_DIM:]
            m_s[...] = m_new

        sc_s[0] = scores(0)

        def step(jj, carry):
            j = 2 * jj
            sc_s[1] = scores(j + 1)
            softmax_pv(j, sc_s[0])
            sc_s[0] = scores(jnp.minimum(j + 2, nk - 1))
            softmax_pv(j + 1, sc_s[1])
            return carry

        lax.fori_loop(0, nk // 2, step, 0)
        l = acc_s[:, HEAD_DIM:]
        o = acc_s[:, :HEAD_DIM] / l
        for h in range(GROUP):
            o_ref[:, h * HEAD_DIM:(h + 1) * HEAD_DIM] = o[h * tq:(h + 1) * tq, :].astype(BF)
        lse = m_s[...] + jnp.log2(l)
        for h in range(GROUP):
            lse_ref[h] = lse[h * tq:(h + 1) * tq, :]

    vb = _v_col_block(d)
    (o, lse), cres = _call(
        body, name="flash_fwd", grid=(ng, nq),
        in_specs=[pl.BlockSpec((tq, gw), lambda g, i: (i, g)),
                  pl.BlockSpec((s, HEAD_DIM), lambda g, i: (0, g)),
                  pl.BlockSpec((s, HEAD_DIM), lambda g, i: (0, vb + g))],
        out_specs=[pl.BlockSpec((tq, gw), lambda g, i: (i, g)),
                   pl.BlockSpec((GROUP, tq, LANES), lambda g, i: (g, i, 0))],
        out_shape=[jax.ShapeDtypeStruct((s, d), BF), jax.ShapeDtypeStruct((ng * GROUP, s, LANES), F32)],
        scratch_shapes=[pltpu.VMEM((rows, HEAD_DIM), BF), pltpu.VMEM((s, 2 * HEAD_DIM), BF), pltpu.VMEM((rows, tk), BF),
                        pltpu.VMEM((rows, LANES), F32), pltpu.VMEM((rows, 2 * HEAD_DIM), F32), pltpu.VMEM((2, rows, tk), F32)],
        semantics=("parallel", "arbitrary"), args=[qt, kt, z], comm=comm)
    return o, lse, cres


def _flash_bwd(qt, kt, z, o, do, lse, d, tq=None, tk=None, comm=None):
    s = qt.shape[0]
    tq, tk = min(tq or FLASH_TQ_BWD, s), min(tk or FLASH_TK, s)
    ng, nq, nk = d // (GROUP * HEAD_DIM), s // tq, s // tk
    gw = GROUP * HEAD_DIM
    rows = GROUP * tq

    nt = tk // LANES

    def body(q_ref, k_ref, v_ref, o_ref, do_ref, lse_ref, dq_ref, dk_ref, dv_ref, qs, dos, delta_s, dq_s, p_s, ds_s, lse_s):
        i = pl.program_id(1)
        for h in range(GROUP):
            cols = slice(h * HEAD_DIM, (h + 1) * HEAD_DIM)
            lse_s[h * tq:(h + 1) * tq, :] = lse_ref[h]
            qs[h * tq:(h + 1) * tq, :] = q_ref[:, cols]
            dov = do_ref[:, cols]
            dos[h * tq:(h + 1) * tq, :] = dov
            delta = jnp.sum(dov.astype(F32) * o_ref[:, cols].astype(F32), axis=-1, keepdims=True)
            delta_s[h * tq:(h + 1) * tq, :] = jnp.broadcast_to(delta, (tq, LANES))
        dq_s[...] = jnp.zeros((rows, HEAD_DIM), F32)

        @pl.when(i == 0)
        def _():
            dk_ref[...] = jnp.zeros_like(dk_ref)
            dv_ref[...] = jnp.zeros_like(dv_ref)

        def step(j, carry):
            kv_rows = pl.ds(pl.multiple_of(j * tk, tk), tk)
            kv, vv = k_ref[kv_rows, :], v_ref[kv_rows, :]
            sc = lax.dot_general(qs[...], kv, _NT, preferred_element_type=F32)
            dp = lax.dot_general(dos[...], vv, _NT, preferred_element_type=F32)
            lse, delta = lse_s[...], delta_s[...]
            for c in range(nt):
                cs = slice(c * LANES, (c + 1) * LANES)
                p = jnp.exp2(sc[:, cs] - lse)
                p_s[:, cs] = p.astype(BF)
                ds_s[:, cs] = (p * (dp[:, cs] - delta)).astype(BF)
            dv_ref[kv_rows, :] += lax.dot_general(p_s[...], dos[...], _TN, preferred_element_type=F32)
            dk_ref[kv_rows, :] += lax.dot_general(ds_s[...], qs[...], _TN, preferred_element_type=F32)
            dq_s[...] += jnp.dot(ds_s[...], kv, preferred_element_type=F32)
            return carry

        lax.fori_loop(0, nk, step, 0)
        for h in range(GROUP):
            dq_ref[:, h * HEAD_DIM:(h + 1) * HEAD_DIM] = dq_s[h * tq:(h + 1) * tq, :].astype(BF)

    vb = _v_col_block(d)
    qspec = pl.BlockSpec((tq, gw), lambda g, i: (i, g))
    kspec = pl.BlockSpec((s, HEAD_DIM), lambda g, i: (0, g))
    (dq, dk, dv), cres = _call(
        body, name="flash_bwd", grid=(ng, nq),
        in_specs=[qspec, kspec, pl.BlockSpec((s, HEAD_DIM), lambda g, i: (0, vb + g)), qspec, qspec,
                  pl.BlockSpec((GROUP, tq, LANES), lambda g, i: (g, i, 0))],
        out_specs=[qspec, kspec, kspec],
        out_shape=[jax.ShapeDtypeStruct((s, d), BF), jax.ShapeDtypeStruct((s, d // GROUP), F32),
                   jax.ShapeDtypeStruct((s, d // GROUP), F32)],
        scratch_shapes=[pltpu.VMEM((rows, HEAD_DIM), BF), pltpu.VMEM((rows, HEAD_DIM), BF), pltpu.VMEM((rows, LANES), F32),
                        pltpu.VMEM((rows, HEAD_DIM), F32), pltpu.VMEM((rows, tk), BF), pltpu.VMEM((rows, tk), BF),
                        pltpu.VMEM((rows, LANES), F32)],
        semantics=("parallel", "arbitrary"), args=[qt, kt, z, o, do, lse], comm=comm)
    return dq, dk, dv, cres


def _place():
    x, y, c = lax.axis_index("x"), lax.axis_index("y"), lax.axis_index("c")
    other_chips = [(1 - x, y), (x, 1 - y), (1 - x, 1 - y)]
    return x, y, c, other_chips


def _cast_place(name, w, chip_arr, tr=256):
    r, cc = w.shape
    tr = min(tr, r)

    def body(p_ref, w_ref, o_ref):
        o_ref[...] = w_ref[...].astype(BF)

    return pl.pallas_call(
        body, name=name,
        grid_spec=pltpu.PrefetchScalarGridSpec(
            num_scalar_prefetch=1, grid=(r // tr,),
            in_specs=[pl.BlockSpec((tr, cc), lambda i, p_ref: (i, 0))],
            out_specs=pl.BlockSpec((None, tr, cc), lambda i, p_ref: (p_ref[0], i, 0))),
        out_shape=jax.ShapeDtypeStruct((N_CHIPS, r, cc), BF),
        compiler_params=_params(("parallel",)),
    )(chip_arr, w)


def _gather_comm(bufs, short_host=False):
    n = len(bufs)
    pairs = [(w, j) for w in range(n) for j in range(N_CHIPS - 1)]

    def copies(dst, sems):
        send, recv, fsend, frecv = sems
        x, y, c, chips = _place()

        def part(w, chip, core_half):
            h = bufs[w].shape[1] // 2
            return dst[w].at[2 * chip[0] + chip[1], pl.ds(core_half * h, h)]

        def ici(w, j, incoming):
            slab = part(w, chips[j] if incoming else (x, y), c)
            return pltpu.make_async_remote_copy(
                src_ref=slab, dst_ref=slab, send_sem=send.at[3 * w + j], recv_sem=recv.at[3 * w + j],
                device_id=(*chips[j], c), device_id_type=MESH)

        def d2d(w, j, incoming):
            slab = part(w, chips[j], 1 - c if incoming else c)
            return pltpu.make_async_remote_copy(
                src_ref=slab, dst_ref=slab, send_sem=fsend.at[3 * w + j], recv_sem=frecv.at[3 * w + j],
                device_id=(x, y, 1 - c), device_id_type=MESH)

        return ici, d2d

    def first(_, dst, sems):
        ici, _d = copies(dst, sems)
        for w, j in pairs:
            ici(w, j, False).start()

    def middle(_, dst, sems):
        ici, d2d = copies(dst, sems)
        for w, j in pairs:
            ici(w, j, True).wait_recv()
            d2d(w, j, False).start()

    def last(_, dst, sems):
        ici, d2d = copies(dst, sems)
        for w, j in pairs:
            d2d(w, j, True).wait_recv()
        for w, j in pairs:
            ici(w, j, False).wait_send()
            d2d(w, j, False).wait_send()

    def middle_and_last(src, dst, sems):
        middle(src, dst, sems)
        last(src, dst, sems)

    phases = (first, None, middle_and_last) if short_host else (first, middle, last)
    return _Comm(arrays=list(bufs), out_shapes=[jax.ShapeDtypeStruct(b.shape, b.dtype) for b in bufs],
                 aliases={w: w for w in range(n)}, sems=[pltpu.SemaphoreType.DMA((3 * n,))] * 4, phases=phases)


def _run_comm(name, comm):
    nci, nco = len(comm.arrays), len(comm.out_shapes)

    def body(*refs):
        cin, cout, sems = refs[:nci], refs[nci:nci + nco], refs[nci + nco:]
        for fn in comm.phases:
            if fn is not None:
                fn(cin, cout, sems)

    return pl.pallas_call(
        body, name=name, in_specs=[ANY] * nci, out_specs=[ANY] * nco, out_shape=list(comm.out_shapes),
        input_output_aliases=dict(comm.aliases), scratch_shapes=list(comm.sems),
    )(*comm.arrays)


def _pair_comm(grads):
    n = len(grads)

    def copies(src, dst, sems):
        send, recv = sems
        x, y, c, _ = _place()
        out = []
        for w in range(n):
            h = grads[w].shape[1] // 2
            out.append(pltpu.make_async_remote_copy(
                src_ref=src[w].at[:, pl.ds((1 - c) * h, h), :], dst_ref=dst[w],
                send_sem=send.at[w], recv_sem=recv.at[w], device_id=(x, y, 1 - c), device_id_type=MESH))
        return out

    def first(src, dst, sems):
        for cp in copies(src, dst, sems):
            cp.start()

    def last(src, dst, sems):
        for cp in copies(src, dst, sems):
            cp.wait()

    return _Comm(arrays=list(grads),
                 out_shapes=[jax.ShapeDtypeStruct((N_CHIPS, g.shape[1] // 2, g.shape[2]), g.dtype) for g in grads],
                 aliases={}, sems=[pltpu.SemaphoreType.DMA((n,))] * 2, phases=(first, None, last))


def _pair_sum(name, own, got, c_arr, tr=256):
    nc, r, cc = own.shape
    h = r // 2
    tr = min(tr, h)
    nb = h // tr

    def body(c_ref, a_ref, b_ref, o_ref):
        o_ref[...] = (a_ref[...].astype(F32) + b_ref[...].astype(F32)).astype(BF)

    return pl.pallas_call(
        body, name=name,
        grid_spec=pltpu.PrefetchScalarGridSpec(
            num_scalar_prefetch=1, grid=(nc, nb),
            in_specs=[pl.BlockSpec((None, tr, cc), lambda s, i, c_ref: (s, c_ref[0] * nb + i, 0)),
                      pl.BlockSpec((None, tr, cc), lambda s, i, c_ref: (s, i, 0))],
            out_specs=pl.BlockSpec((None, tr, cc), lambda s, i, c_ref: (s, i, 0))),
        out_shape=jax.ShapeDtypeStruct((nc, h, cc), BF),
        compiler_params=_params(("parallel", "parallel")),
    )(c_arr, own, got)


def _chip_comm(parts):
    n = len(parts)

    def copies(src, dst, sems):
        send, recv = sems
        _, _, c, chips = _place()
        return [pltpu.make_async_remote_copy(
            src_ref=src[w].at[2 * chip[0] + chip[1]], dst_ref=dst[w].at[j],
            send_sem=send.at[3 * w + j], recv_sem=recv.at[3 * w + j], device_id=(*chip, c), device_id_type=MESH)
            for w in range(n) for j, chip in enumerate(chips)]

    def first(src, dst, sems):
        for cp in copies(src, dst, sems):
            cp.start()

    def last(src, dst, sems):
        for cp in copies(src, dst, sems):
            cp.wait()

    return _Comm(arrays=list(parts), out_shapes=[jax.ShapeDtypeStruct((N_CHIPS - 1,) + p.shape[1:], p.dtype) for p in parts],
                 aliases={}, sems=[pltpu.SemaphoreType.DMA((3 * n,))] * 2, phases=(first, None, last))


def _chip_sum(name, parts, got, chip_arr, c_arr, tr=256):
    _, h, cc = parts.shape
    tr = min(tr, h)
    nb = h // tr

    def body(chip_ref, c_ref, own_ref, got_ref, o_ref):
        acc = own_ref[...].astype(F32)
        for k in range(N_CHIPS - 1):
            acc = acc + got_ref[k].astype(F32)
        o_ref[...] = acc

    return pl.pallas_call(
        body, name=name,
        grid_spec=pltpu.PrefetchScalarGridSpec(
            num_scalar_prefetch=2, grid=(nb,),
            in_specs=[pl.BlockSpec((None, tr, cc), lambda i, chip_ref, c_ref: (chip_ref[0], i, 0)),
                      pl.BlockSpec((N_CHIPS - 1, tr, cc), lambda i, chip_ref, c_ref: (0, i, 0))],
            out_specs=pl.BlockSpec((tr, cc), lambda i, chip_ref, c_ref: (c_ref[0] * nb + i, 0))),
        out_shape=jax.ShapeDtypeStruct((2 * h, cc), F32),
        compiler_params=_params(("parallel",)),
    )(chip_arr, c_arr, parts, got)


def _pair_gather(bufs):
    n = len(bufs)

    def body(*refs):
        dst = refs[n:2 * n]
        send, recv = refs[2 * n:]
        x, y, c, _ = _place()

        def copy(w, core_half):
            h = bufs[w].shape[0] // 2
            rows = dst[w].at[pl.ds(core_half * h, h)]
            return pltpu.make_async_remote_copy(src_ref=rows, dst_ref=rows, send_sem=send.at[w], recv_sem=recv.at[w],
                                                device_id=(x, y, 1 - c), device_id_type=MESH)

        sends = [copy(w, c) for w in range(n)]
        for cp in sends:
            cp.start()
        for w in range(n):
            copy(w, 1 - c).wait_recv()
        for cp in sends:
            cp.wait_send()

    return pl.pallas_call(
        body, name="grad_pair_gather",
        in_specs=[ANY] * n, out_specs=[ANY] * n,
        out_shape=[jax.ShapeDtypeStruct(b.shape, b.dtype) for b in bufs],
        input_output_aliases={w: w for w in range(n)},
        scratch_shapes=[pltpu.SemaphoreType.DMA((n,))] * 2,
    )(*bufs)


def _all_sum_small(name, v):
    p = v.shape[0]

    def body(v_ref, o_ref, slots, send, recv):
        x, y, c, _ = _place()
        me = 4 * x + 2 * y + c
        copies = []
        for k in range(1, N_DEV):
            peer = (x ^ (k >> 2), y ^ ((k >> 1) & 1), c ^ (k & 1))
            copies.append(pltpu.make_async_remote_copy(
                src_ref=v_ref, dst_ref=slots.at[me], send_sem=send.at[k - 1], recv_sem=recv.at[k - 1],
                device_id=peer, device_id_type=MESH))
        for cp in copies:
            cp.start()
        slots[me] = v_ref[...]
        for cp in copies:
            cp.wait()
        acc = slots[0]
        for s in range(1, N_DEV):
            acc = acc + slots[s]
        o_ref[...] = acc

    vm = pl.BlockSpec(memory_space=pltpu.VMEM)
    return pl.pallas_call(
        body, name=name,
        in_specs=[vm], out_specs=vm,
        out_shape=jax.ShapeDtypeStruct(v.shape, F32),
        scratch_shapes=[pltpu.VMEM((N_DEV, p, LANES), F32), pltpu.SemaphoreType.DMA((N_DEV - 1,)),
                        pltpu.SemaphoreType.DMA((N_DEV - 1,))],
    )(v)


def _adamw(name, w, g, m, v, tr=256):
    r, c = w.shape
    tr = min(tr, r)
    assert r % tr == 0
    bc1 = 1.0 - ADAM_B1 ** ADAM_STEP
    bc2 = 1.0 - ADAM_B2 ** ADAM_STEP

    def body(w_ref, g_ref, m_ref, v_ref, d_ref, nm_ref, nv_ref):
        gv = g_ref[...]
        nm = ADAM_B1 * m_ref[...] + (1.0 - ADAM_B1) * gv
        nv = ADAM_B2 * v_ref[...] + (1.0 - ADAM_B2) * (gv * gv)
        nm_ref[...] = nm
        nv_ref[...] = nv
        d_ref[...] = -ADAM_LR * ((nm / bc1) / (jnp.sqrt(nv / bc2) + ADAM_EPS) + ADAM_WD * w_ref[...])

    blk = pl.BlockSpec((tr, c), lambda i: (i, 0))
    return pl.pallas_call(
        body, name=name, grid=(r // tr,),
        in_specs=[blk] * 4, out_specs=[blk] * 3,
        out_shape=[jax.ShapeDtypeStruct((r, c), F32)] * 3,
        compiler_params=_params(("parallel",)),
    )(w, g, m, v)


def _pack_small(parts):
    flat = jnp.concatenate([a.reshape(-1) for a in parts])
    n = flat.shape[0]
    p = -(-n // (8 * LANES)) * 8
    packed = jnp.pad(flat, (0, p * LANES - n)).reshape(p, LANES)

    def unpack(q):
        out, off = [], 0
        f = q.reshape(-1)
        for a in parts:
            out.append(f[off:off + a.size].reshape(a.shape))
            off += a.size
        return out

    return packed, unpack


def kernel(x, p, norm_mix, w_in, w_dw, conv_ln_g, conv_ln_b, w_conv_proj, q_norm, k_norm, w_attn_proj, w_out, norm_ffn, w_ff1, w_ff2, norm_ple, w_ple_gate, w_ple_proj, norm_final, loss_target, m_norm_mix, m_w_in, m_w_dw, m_conv_ln_g, m_conv_ln_b, m_w_conv_proj, m_q_norm, m_k_norm, m_w_attn_proj, m_w_out, m_norm_ffn, m_w_ff1, m_w_ff2, m_norm_ple, m_w_ple_gate, m_w_ple_proj, m_norm_final, v_norm_mix, v_w_in, v_w_dw, v_conv_ln_g, v_conv_ln_b, v_w_conv_proj, v_q_norm, v_k_norm, v_w_attn_proj, v_w_out, v_norm_ffn, v_w_ff1, v_w_ff2, v_norm_ple, v_w_ple_gate, v_w_ple_proj, v_norm_final):
    s, d = x.shape[1], x.shape[2]
    cw = d // 2
    kvw = d // GROUP
    xs, ps, tgt = x[0], p[0, 0], loss_target[0]
    cx, cy, cc = lax.axis_index("x"), lax.axis_index("y"), lax.axis_index("c")
    chip = 2 * cx + cy
    c_arr = jnp.reshape(cc, (1,)).astype(jnp.int32)
    tm, tme = min(MM_TM, s), min(MM_TM_EPI, s)

    names = ["w_in", "w_conv_proj", "w_attn_proj", "w_out", "w_ff1", "w_ff2", "w_ple_gate", "w_ple_proj"]
    big = [w_in, w_conv_proj, w_attn_proj, w_out, w_ff1, w_ff2, w_ple_gate, w_ple_proj]
    chip_arr = jnp.reshape(chip, (1,)).astype(jnp.int32)
    placed = [_cast_place("cast_" + nm, w[0], chip_arr) for nm, w in zip(names, big)]
    h0, (win,) = _rms_fwd("rms_mix", xs, norm_mix, comm=_gather_comm(placed[:1], short_host=True))
    cpc = cw // N_CHIPS
    taps = jnp.zeros((32, N_CHIPS, cpc), F32).at[:CONV_KERNEL].set(
        jnp.where(lax.broadcasted_iota(jnp.int32, (1, N_CHIPS, 1), 1) == chip, w_dw[0][:, None, :], 0.0))
    taps = jnp.where(cc == 0, taps, 0.0).reshape(32 * cw // LANES, LANES)
    wdw = _all_sum_small("gather_taps", taps).reshape(32, cw)

    cos, sin = _rope_tables(s)
    (z,) = _mm("z_proj", h0, win, b_cm=True, tm=tm, tn=win.shape[2] // 3, tk=d)
    uc, act = _conv_fwd(z, wdw, conv_ln_g, conv_ln_b, cw)
    qt, kt = _qk_fwd(z, cos, sin, q_norm, k_norm, d)
    o, lse, (wcp, wap, wout, w1, w2, wpg, wple) = _flash_fwd(qt, kt, z, d, comm=_gather_comm(placed[1:]))
    wap, wout, w2, wpg = (t.reshape(-1, t.shape[-1]) for t in (wap, wout, w2, wpg))
    (y_c,) = _mm("conv_proj", act, wcp, b_cm=True, tm=tm, tn=wcp.shape[2], tk=cw, out_dtypes=(F32,))
    tn = d // 2
    gcb = (2 * d + 2 * kvw) // tn

    def merge_epi(acc, yc, gc, ga):
        return acc, _sigmoid(gc.astype(F32)) * yc + _sigmoid(ga.astype(F32)) * acc

    y_a, merged = _mm("attn_proj", o, wap, tm=tme, tn=tn, tk=d, epi=merge_epi, out_dtypes=(BF, BF),
                      extras=[_tile_extra(y_c, tme, tn), _tile_extra(z, tme, tn, gcb), _tile_extra(z, tme, tn, gcb + 2)])
    (x1,) = _mm("out_proj", merged, wout, tm=tm, tn=tn, tk=d, epi=lambda acc, r: (r + acc,), out_dtypes=(F32,),
                extras=[_tile_extra(xs, tm, tn)])
    h1 = _rms_fwd("rms_ffn", x1, norm_ffn)
    (a,) = _mm("ff1", h1, w1, b_cm=True, tm=tm, tn=tn, tk=d)

    def relu2(t):
        return jnp.square(jnp.maximum(t, 0.0))

    (x2,) = _mm("ff2", a, w2, tm=tm, tn=tn, tk=d, a_fn=relu2, epi=lambda acc, r: (r + acc,), out_dtypes=(F32,),
                extras=[_tile_extra(x1, tm, tn)])
    h2 = _rms_fwd("rms_ple", x2, norm_ple)
    to_bf = lambda t: t.astype(BF)
    (e,) = _mm("ple_proj", ps, wple, b_cm=True, tm=tm, tn=wple.shape[2], tk=ps.shape[1], a_fn=to_bf)

    def ple_epi(acc, ev, r):
        gt = _sigmoid(acc)
        return r + gt * ev.astype(F32), gt

    x3, gate = _mm("ple_gate", h2, wpg, tm=tme, tn=tn, tk=d, epi=ple_epi, out_dtypes=(F32, BF),
                   extras=[_tile_extra(e, tme, tn), _tile_extra(x2, tme, tn)])

    dx3, de, dgp, sq, d_fin = _loss_bwd(x3, tgt, norm_final.reshape(1, d), e, gate)
    tkt = min(2048, s)
    (g_wple,) = _mm("d_wple", ps, de, ta=True, out_cm=True, tm=ps.shape[1], tn=wple.shape[2], tk=tkt, a_fn=to_bf)
    (g_wpg,) = _mm("d_wpg", h2, dgp, ta=True, tm=tm, tn=tn, tk=tkt)
    (dh2,) = _mm("d_h2", dgp, wpg, tb=True, tm=tm, tn=tn, tk=d)
    dx2, dx2b, d_ple = _rms_bwd("rms_ple_bwd", dh2, x2, norm_ple, dx3)

    (da,) = _mm("d_a", dx2b, w2, tb=True, tm=tm, tn=tn, tk=d, out_dtypes=(BF,),
                epi=lambda acc, av: (acc * (2.0 * jnp.maximum(av.astype(F32), 0.0)),), extras=[_tile_extra(a, tm, tn)])
    (g_w2,) = _mm("d_w2", a, dx2b, ta=True, tm=tm, tn=tn, tk=tkt, a_fn=relu2)
    (g_w1,) = _mm("d_w1", h1, da, ta=True, out_cm=True, tm=tm, tn=tn, tk=tkt)
    (dh1,) = _mm("d_h1", da, w1, tb=True, b_cm=True, tm=tm, tn=tn, tk=w1.shape[2])
    dx1, dx1b, d_ffn = _rms_bwd("rms_ffn_bwd", dh1, x1, norm_ffn, dx2)

    def merge_bwd(acc, gc, ga, yc, ya):
        sc, sa = _sigmoid(gc.astype(F32)), _sigmoid(ga.astype(F32))
        return acc * sc, acc * sa, acc * yc * sc * (1.0 - sc), acc * ya.astype(F32) * sa * (1.0 - sa)

    dy_c, dy_a, dg_c, dg_a = _mm(
        "d_merged", dx1b, wout, tb=True, tm=tme, tn=tn, tk=d, epi=merge_bwd, out_dtypes=(BF, BF, BF, BF),
        extras=[_tile_extra(z, tme, tn, gcb), _tile_extra(z, tme, tn, gcb + 2), _tile_extra(y_c, tme, tn),
                _tile_extra(y_a, tme, tn)])
    (g_wout,) = _mm("d_wout", merged, dx1b, ta=True, tm=tm, tn=tn, tk=tkt)
    (g_wap,) = _mm("d_wap", o, dy_a, ta=True, tm=tm, tn=tn, tk=tkt)

    def slabs(g):
        return g if g.ndim == 3 else g.reshape(N_CHIPS, g.shape[0] // N_CHIPS, g.shape[1])

    grads_a = [slabs(g) for g in (g_wap, g_wout, g_w1, g_w2, g_wpg, g_wple)]
    (do,), got_a = _mm("d_o", dy_a, wap, tb=True, tm=tm, tn=tn, tk=d, comm=_pair_comm(grads_a))
    parts_a = [_pair_sum("pair_sum_" + nm, g, r, c_arr) for nm, g, r in zip(names[2:], grads_a, got_a)]
    dqt, dkt, dv, _ = _flash_bwd(qt, kt, z, o, do, lse, d)
    dq, dk, d_qn, d_kn = _qk_bwd(dqt, dkt, z, cos, sin, q_norm, k_norm, d)
    (g_wcp,) = _mm("d_wcp", act, dy_c, ta=True, out_cm=True, tm=cw, tn=wcp.shape[2], tk=tkt)
    (dact,) = _mm("d_act", dy_c, wcp, tb=True, b_cm=True, tm=tm, tn=cw, tk=wcp.shape[2])
    p_wap, p_wout, p_w1, p_w2, p_wpg, p_wple = parts_a
    dcab, d_taps, d_lng, d_lnb, (s_wap, s_wout, s_wpg, s_wple) = _conv_bwd(
        dact, uc, z, wdw, conv_ln_g, conv_ln_b, cw, comm=_chip_comm([p_wap, p_wout, p_wpg, p_wple]))
    dz = jnp.concatenate([dcab, dq, dk, dv.astype(BF), dg_c, dg_a], axis=1)
    (g_win,), (s_w1, s_w2) = _mm("d_win", h0, dz, ta=True, out_cm=True, tm=tm, tn=win.shape[2] // 3, tk=tkt,
                                 comm=_chip_comm([p_w1, p_w2]))
    slots_a = [s_wap, s_wout, s_w1, s_w2, s_wpg, s_wple]
    grads_b = [slabs(g_win), slabs(g_wcp)]
    got_b = _run_comm("grad_pair_exchange_b", _pair_comm(grads_b))
    parts_b = [_pair_sum("pair_sum_" + nm, g, r, c_arr) for nm, g, r in zip(names[:2], grads_b, got_b)]
    (dh0,), slots_b = _mm("d_h0", dz, win, tb=True, b_cm=True, tm=tm, tn=tn, tk=win.shape[2], comm=_chip_comm(parts_b))
    dx, _, d_mix = _rms_bwd("rms_mix_bwd", dh0, xs, norm_mix, dx1)
    big_grads = _pair_gather(
        [_chip_sum("chip_sum_" + nm, cp, sl, chip_arr, c_arr)
         for nm, cp, sl in zip(names, parts_b + parts_a, list(slots_b) + list(slots_a))])

    small = [d_mix, d_taps[:CONV_KERNEL], d_lng, d_lnb, d_qn, d_kn, d_ffn, d_ple, d_fin]
    packed, unpack = _pack_small(small)
    g_mix, g_taps, g_lng, g_lnb, g_qn, g_kn, g_ffn, g_ple, g_fin = unpack(_all_sum_small("reduce_small", packed))
    g_dw = lax.dynamic_slice_in_dim(g_taps.reshape(CONV_KERNEL, N_CHIPS, cpc), chip, 1, axis=1).reshape(1, CONV_KERNEL, cpc)

    sq_local = lax.reduce_precision(sq[0, 0], 8, 23)
    loss = (0.5 / d) * lax.psum(sq_local, ("x", "y", "c"))

    grads = {
        "norm_mix": g_mix, "w_in": big_grads[0][None], "w_dw": g_dw, "conv_ln_g": g_lng, "conv_ln_b": g_lnb,
        "w_conv_proj": big_grads[1][None], "q_norm": g_qn, "k_norm": g_kn, "w_attn_proj": big_grads[2][None],
        "w_out": big_grads[3][None], "norm_ffn": g_ffn, "w_ff1": big_grads[4][None], "w_ff2": big_grads[5][None],
        "norm_ple": g_ple, "w_ple_gate": big_grads[6][None], "w_ple_proj": big_grads[7][None],
        "norm_final": g_fin.reshape(d),
    }
    weights = dict(norm_mix=norm_mix, w_in=w_in, w_dw=w_dw, conv_ln_g=conv_ln_g, conv_ln_b=conv_ln_b, w_conv_proj=w_conv_proj,
                   q_norm=q_norm, k_norm=k_norm, w_attn_proj=w_attn_proj, w_out=w_out, norm_ffn=norm_ffn, w_ff1=w_ff1,
                   w_ff2=w_ff2, norm_ple=norm_ple, w_ple_gate=w_ple_gate, w_ple_proj=w_ple_proj, norm_final=norm_final)
    m_in = dict(norm_mix=m_norm_mix, w_in=m_w_in, w_dw=m_w_dw, conv_ln_g=m_conv_ln_g, conv_ln_b=m_conv_ln_b,
                w_conv_proj=m_w_conv_proj, q_norm=m_q_norm, k_norm=m_k_norm, w_attn_proj=m_w_attn_proj, w_out=m_w_out,
                norm_ffn=m_norm_ffn, w_ff1=m_w_ff1, w_ff2=m_w_ff2, norm_ple=m_norm_ple, w_ple_gate=m_w_ple_gate,
                w_ple_proj=m_w_ple_proj, norm_final=m_norm_final)
    v_in = dict(norm_mix=v_norm_mix, w_in=v_w_in, w_dw=v_w_dw, conv_ln_g=v_conv_ln_g, conv_ln_b=v_conv_ln_b,
                w_conv_proj=v_w_conv_proj, q_norm=v_q_norm, k_norm=v_k_norm, w_attn_proj=v_w_attn_proj, w_out=v_w_out,
                norm_ffn=v_norm_ffn, w_ff1=v_w_ff1, w_ff2=v_w_ff2, norm_ple=v_norm_ple, w_ple_gate=v_w_ple_gate,
                w_ple_proj=v_w_ple_proj, norm_final=v_norm_final)
    order = list(weights)
    deltas, new_m, new_v, g_out = [], [], [], []
    for nm in order:
        w = weights[nm]
        shape = w.shape
        two_d = (-1, shape[-1])
        dl, mm_, vv_ = _adamw("adamw_" + nm, w.reshape(two_d), grads[nm].reshape(two_d), m_in[nm].reshape(two_d),
                              v_in[nm].reshape(two_d))
        g_out.append(grads[nm].reshape(shape))
        deltas.append(dl.reshape(shape))
        new_m.append(mm_.reshape(shape))
        new_v.append(vv_.reshape(shape))
    return (loss, dx[None], *g_out, *deltas, *new_m, *new_v)
```

```python
from typing import NamedTuple

import jax
import jax.numpy as jnp
from jax import lax
from jax.experimental import pallas as pl
from jax.experimental.pallas import tpu as pltpu

F32 = jnp.float32
BF = jnp.bfloat16

EPS = 1e-6
HEAD_DIM = 128
GROUP = 4
GRID_W = 64
ROPE_THETA = 10000.0
CONV_KERNEL = 31
HALO = 16
N_CHIPS = 4
N_DEV = 8
LANES = 128

ADAM_LR = 0.001
ADAM_B1 = 0.9
ADAM_B2 = 0.999
ADAM_EPS = 1e-08
ADAM_WD = 0.01
ADAM_STEP = 10

VMEM_LIMIT = 56 * 2 ** 20
LOG2E = 1.4426950408889634
LN2 = 0.6931471805599453
Q_SCALE = HEAD_DIM ** -0.5 * LOG2E
ROW_TILE = 256
FLASH_TQ_FWD = 512
FLASH_TQ_BWD = 512
FLASH_TK = 512
MM_TM = 1024
MM_TM_EPI = 512
MESH = pl.DeviceIdType.MESH
ANY = pl.BlockSpec(memory_space=pl.ANY)


def _params(sem):
    return pltpu.CompilerParams(dimension_semantics=sem, vmem_limit_bytes=VMEM_LIMIT)


def _sigmoid(x):
    return 1.0 / (1.0 + jnp.exp(-x))


class _Comm(NamedTuple):
    arrays: list
    out_shapes: list
    aliases: dict
    sems: list
    phases: tuple


def _call(body, *, name, grid, in_specs, out_specs, out_shape, scratch_shapes, semantics, args, comm=None):
    n_in, n_out = len(in_specs), len(out_specs)
    if comm is None:
        res = pl.pallas_call(body, name=name, grid=grid, in_specs=in_specs, out_specs=out_specs, out_shape=out_shape,
                             scratch_shapes=scratch_shapes, compiler_params=_params(semantics))(*args)
        return res, []
    nci, nco, ncs = len(comm.arrays), len(comm.out_shapes), len(comm.sems)
    n_steps = 1
    for g in grid:
        n_steps *= g
    first, middle, last = comm.phases

    def hosted(*refs):
        ins, cin = refs[:n_in], refs[n_in:n_in + nci]
        outs = refs[n_in + nci:n_in + nci + n_out]
        cout = refs[n_in + nci + n_out:n_in + nci + n_out + nco]
        rest = refs[n_in + nci + n_out + nco:]
        scratch, sems = rest[:len(rest) - ncs], rest[len(rest) - ncs:]
        step = 0
        for ax, g in enumerate(grid):
            step = step * g + pl.program_id(ax)
        for at, fn in ((0, first), (n_steps // 2, middle)):
            if fn is not None:
                pl.when(step == at)(lambda fn=fn: fn(cin, cout, sems))
        body(*ins, *outs, *scratch)
        if last is not None:
            pl.when(step == n_steps - 1)(lambda: last(cin, cout, sems))

    res = pl.pallas_call(
        hosted, name=name, grid=grid,
        in_specs=list(in_specs) + [ANY] * nci, out_specs=list(out_specs) + [ANY] * nco,
        out_shape=list(out_shape) + list(comm.out_shapes),
        input_output_aliases={n_in + a: n_out + b for a, b in comm.aliases.items()},
        scratch_shapes=list(scratch_shapes) + list(comm.sems),
        compiler_params=_params(("arbitrary",) * len(grid)),
    )(*args, *comm.arrays)
    return res[:n_out], res[n_out:]


def _mm(name, a, b, *, tm, tn, tk, ta=False, tb=False, b_cm=False, out_cm=False,
        a_fn=None, extras=(), epi=None, out_dtypes=(BF,), epi_rows=256, comm=None):
    if ta:
        kc, m = a.shape
    else:
        m, kc = a.shape
    if b_cm:
        nc, r, c = b.shape
        n, per = (r, c) if tb else (nc * c, c)
    else:
        n = b.shape[0] if tb else b.shape[1]
    tm, tn, tk = min(tm, m), min(tn, n), min(tk, kc)
    assert m % tm == 0 and n % tn == 0 and kc % tk == 0, (name, m, n, kc, tm, tn, tk)
    nk = kc // tk
    a_spec = pl.BlockSpec((tk, tm), lambda i, j, k: (k, i)) if ta else pl.BlockSpec((tm, tk), lambda i, j, k: (i, k))
    if b_cm and not tb:
        assert per % tn == 0
        npj = per // tn
        b_spec = pl.BlockSpec((None, tk, tn), lambda i, j, k: (j // npj, k, j % npj))
    elif b_cm:
        assert per % tk == 0
        npk = per // tk
        b_spec = pl.BlockSpec((None, tn, tk), lambda i, j, k: (k // npk, j, k % npk))
    elif tb:
        b_spec = pl.BlockSpec((tn, tk), lambda i, j, k: (j, k))
    else:
        b_spec = pl.BlockSpec((tk, tn), lambda i, j, k: (k, j))
    if out_cm:
        assert (n // N_CHIPS) % tn == 0
        npo = (n // N_CHIPS) // tn
        o_spec = pl.BlockSpec((None, tm, tn), lambda i, j, k: (j // npo, i, j % npo))
        o_shape = (N_CHIPS, m, n // N_CHIPS)
    else:
        o_spec = pl.BlockSpec((tm, tn), lambda i, j, k: (i, j))
        o_shape = (m, n)
    ne, no = len(extras), len(out_dtypes)
    dims = (((0 if ta else 1,), (1 if tb else 0,)), ((), ()))
    er = min(epi_rows, tm)
    chunked = nk == 1 and epi is not None and not ta
    use_acc = (nk > 1 or epi is not None) and not chunked

    def body(*refs):
        a_ref, b_ref = refs[0], refs[1]
        ex = refs[2:2 + ne]
        outs = refs[2 + ne:2 + ne + no]
        if chunked:
            bt = b_ref[...]
            for r0 in range(0, tm, er):
                rows = slice(r0, r0 + er)
                at = a_ref[rows, :]
                if a_fn is not None:
                    at = a_fn(at)
                d = lax.dot_general(at, bt, dims, preferred_element_type=F32)
                vals = epi(d, *[e[rows, :] for e in ex])
                for o, v, dt in zip(outs, vals, out_dtypes):
                    o[rows, :] = v.astype(dt)
            return
        at = a_ref[...]
        if a_fn is not None:
            at = a_fn(at)
        d = lax.dot_general(at, b_ref[...], dims, preferred_element_type=F32)
        if not use_acc:
            outs[0][...] = d.astype(out_dtypes[0])
            return
        acc = refs[-1]
        k = pl.program_id(2)

        @pl.when(k == 0)
        def _():
            acc[...] = d

        if nk > 1:
            @pl.when(k > 0)
            def _():
                acc[...] += d

        @pl.when(k == nk - 1)
        def _():
            for r0 in range(0, tm, er):
                rows = slice(r0, r0 + er)
                if epi is None:
                    vals = (acc[rows, :],)
                else:
                    vals = epi(acc[rows, :], *[e[rows, :] for e in ex])
                for o, v, dt in zip(outs, vals, out_dtypes):
                    o[rows, :] = v.astype(dt)

    res, cres = _call(
        body, name=name, grid=(m // tm, n // tn, nk),
        in_specs=[a_spec, b_spec] + [pl.BlockSpec(bs, im) for _, bs, im in extras],
        out_specs=[o_spec] * no,
        out_shape=[jax.ShapeDtypeStruct(o_shape, dt) for dt in out_dtypes],
        scratch_shapes=[pltpu.VMEM((tm, tn), F32)] if use_acc else [],
        semantics=("parallel", "parallel", "arbitrary"),
        args=[a, b] + [e for e, _, _ in extras], comm=comm)
    return res if comm is None else (res, cres)


def _tile_extra(arr, tm, tn, col_block0=0):
    return (arr, (tm, tn), lambda i, j, k: (i, j + col_block0))


def _rms_fwd(name, x, g, ts=None, comm=None):
    s, d = x.shape
    ts = ts or ROW_TILE

    def body(x_ref, g_ref, h_ref):
        xv = x_ref[...]
        r = lax.rsqrt(jnp.mean(xv * xv, axis=-1, keepdims=True) + EPS)
        h_ref[...] = (xv * r * g_ref[...]).astype(BF)

    (h,), cres = _call(
        body, name=name, grid=(s // ts,),
        in_specs=[pl.BlockSpec((ts, d), lambda i: (i, 0)), pl.BlockSpec((1, d), lambda i: (0, 0))],
        out_specs=[pl.BlockSpec((ts, d), lambda i: (i, 0))],
        out_shape=[jax.ShapeDtypeStruct((s, d), BF)],
        scratch_shapes=[], semantics=("parallel",), args=[x, g], comm=comm)
    return h if comm is None else (h, cres)


def _rms_bwd(name, dh, x, g, dres, ts=None):
    s, d = x.shape
    ts = ts or ROW_TILE

    def body(dh_ref, x_ref, g_ref, dres_ref, dx_ref, dxb_ref, dg_ref):
        xv = x_ref[...]
        dhv = dh_ref[...].astype(F32)
        r = lax.rsqrt(jnp.mean(xv * xv, axis=-1, keepdims=True) + EPS)
        nrm = xv * r
        dn = dhv * g_ref[...]
        dx = dres_ref[...] + r * (dn - nrm * jnp.mean(dn * nrm, axis=-1, keepdims=True))
        dx_ref[...] = dx
        dxb_ref[...] = dx.astype(BF)
        part = jnp.sum(dhv * nrm, axis=0, keepdims=True)

        @pl.when(pl.program_id(0) == 0)
        def _():
            dg_ref[...] = part

        @pl.when(pl.program_id(0) > 0)
        def _():
            dg_ref[...] += part

    row = pl.BlockSpec((ts, d), lambda i: (i, 0))
    vec = pl.BlockSpec((1, d), lambda i: (0, 0))
    return pl.pallas_call(
        body, name=name, grid=(s // ts,),
        in_specs=[row, row, vec, row],
        out_specs=[row, row, vec],
        out_shape=[jax.ShapeDtypeStruct((s, d), F32), jax.ShapeDtypeStruct((s, d), BF), jax.ShapeDtypeStruct((1, d), F32)],
        compiler_params=_params(("arbitrary",)),
    )(dh, x, g, dres)


def _loss_bwd(x3, tgt, gfin, e, gate, ts=None):
    s, d = x3.shape
    ts = ts or ROW_TILE

    def body(x_ref, t_ref, g_ref, e_ref, gate_ref, dx_ref, de_ref, dgp_ref, sq_ref, dg_ref):
        xv = x_ref[...]
        gv = g_ref[...]
        r = lax.rsqrt(jnp.mean(xv * xv, axis=-1, keepdims=True) + EPS)
        nrm = xv * r
        err = nrm * gv - t_ref[...]
        dy = err * (1.0 / d)
        dn = dy * gv
        dx = r * (dn - nrm * jnp.mean(dn * nrm, axis=-1, keepdims=True))
        dx_ref[...] = dx
        ev = e_ref[...].astype(F32)
        gt = gate_ref[...].astype(F32)
        de_ref[...] = (dx * gt).astype(BF)
        dgp_ref[...] = (dx * ev * gt * (1.0 - gt)).astype(BF)
        sq = jnp.full((8, LANES), jnp.sum(err * err), F32)
        part = jnp.sum(dy * nrm, axis=0, keepdims=True)

        @pl.when(pl.program_id(0) == 0)
        def _():
            sq_ref[...] = sq
            dg_ref[...] = part

        @pl.when(pl.program_id(0) > 0)
        def _():
            sq_ref[...] += sq
            dg_ref[...] += part

    row = pl.BlockSpec((ts, d), lambda i: (i, 0))
    vec = pl.BlockSpec((1, d), lambda i: (0, 0))
    return pl.pallas_call(
        body, name="loss_bwd", grid=(s // ts,),
        in_specs=[row, row, vec, row, row],
        out_specs=[row, row, row, pl.BlockSpec((8, LANES), lambda i: (0, 0)), vec],
        out_shape=[jax.ShapeDtypeStruct((s, d), F32), jax.ShapeDtypeStruct((s, d), BF), jax.ShapeDtypeStruct((s, d), BF),
                   jax.ShapeDtypeStruct((8, LANES), F32), jax.ShapeDtypeStruct((1, d), F32)],
        compiler_params=_params(("arbitrary",)),
    )(x3, tgt, gfin, e, gate)


def _halo_specs(ts, s, width, col_block):
    per = ts // HALO
    last = s // HALO - 1
    return [
        pl.BlockSpec((HALO, width), lambda i: (jnp.maximum(i * per - 1, 0), col_block)),
        pl.BlockSpec((ts, width), lambda i: (i, col_block)),
        pl.BlockSpec((HALO, width), lambda i: (jnp.minimum((i + 1) * per, last), col_block)),
    ]


def _glu_ext(zp, zc, zn, ext, cw, ts, i, n_tiles):
    def glu(zr):
        zv = zr[...].astype(F32)
        return zv[:, :cw] * _sigmoid(zv[:, cw:])

    ext[0:HALO, :] = jnp.where(i > 0, glu(zp), 0.0)
    ext[HALO:HALO + ts, :] = glu(zc)
    ext[HALO + ts:, :] = jnp.where(i < n_tiles - 1, glu(zn), 0.0)


SUBLANES = 8


def _shift_scratch(ts):
    return pltpu.VMEM((SUBLANES, ts + 2 * HALO - SUBLANES, LANES), F32)


def _shifted_copies(ext, sh, cols, ts):
    n = ts + 2 * HALO - SUBLANES
    for r in range(SUBLANES):
        sh[r] = ext[r:r + n, cols]


def _tap_rows(sh, off, ts):
    q, r = divmod(off, SUBLANES)
    return sh[r, q * SUBLANES:q * SUBLANES + ts, :]


def _ln_stats(uc):
    mu = jnp.mean(uc, axis=-1, keepdims=True)
    xc = uc - mu
    rstd = lax.rsqrt(jnp.mean(xc * xc, axis=-1, keepdims=True) + EPS)
    return xc * rstd, rstd


def _conv_fwd(z, wdw, ln_g, ln_b, cw, ts=None):
    s = z.shape[0]
    ts = ts or ROW_TILE
    n_tiles = s // ts
    pad = CONV_KERNEL // 2

    def body(zp, zc, zn, w_ref, g_ref, b_ref, uc_ref, act_ref, ext, sh):
        i = pl.program_id(0)
        _glu_ext(zp, zc, zn, ext, cw, ts, i, n_tiles)

        def col_block(cb, carry):
            cols = pl.ds(pl.multiple_of(cb * LANES, LANES), LANES)
            _shifted_copies(ext, sh, cols, ts)
            acc = jnp.zeros((ts, LANES), F32)
            for j in range(CONV_KERNEL):
                acc = acc + _tap_rows(sh, HALO - pad + j, ts) * w_ref[j:j + 1, cols]
            uc_ref[:, cols] = acc
            return carry

        lax.fori_loop(0, cw // LANES, col_block, 0)
        xhat, _ = _ln_stats(uc_ref[...])
        ln = xhat * g_ref[...] + b_ref[...]
        act_ref[...] = (ln * _sigmoid(ln)).astype(BF)

    vec = pl.BlockSpec((1, cw), lambda i: (0, 0))
    row = pl.BlockSpec((ts, cw), lambda i: (i, 0))
    return pl.pallas_call(
        body, name="conv_fwd", grid=(n_tiles,),
        in_specs=_halo_specs(ts, s, 2 * cw, 0) + [pl.BlockSpec((32, cw), lambda i: (0, 0)), vec, vec],
        out_specs=[row, row],
        out_shape=[jax.ShapeDtypeStruct((s, cw), F32), jax.ShapeDtypeStruct((s, cw), BF)],
        scratch_shapes=[pltpu.VMEM((ts + 2 * HALO, cw), F32), _shift_scratch(ts)],
        compiler_params=_params(("parallel",)),
    )(z, z, z, wdw, ln_g, ln_b)


def _conv_bwd(ds, uc, z, wdw, ln_g, ln_b, cw, ts=None, comm=None):
    s = z.shape[0]
    ts = ts or ROW_TILE
    n_tiles = s // ts
    pad = CONV_KERNEL // 2

    def body(zp, zc, zn, dsp, dsc, dsn, ucp, ucc, ucn, w_ref, g_ref, b_ref,
             dz_ref, dw_ref, dg_ref, db_ref, ext, dext, sh, dsh):
        i = pl.program_id(0)
        gv, bv = g_ref[...], b_ref[...]

        def ln_bwd(ds_r, uc_r):
            xhat, rstd = _ln_stats(uc_r[...])
            ln = xhat * gv + bv
            sg = _sigmoid(ln)
            dln = ds_r[...].astype(F32) * (sg * (1.0 + ln * (1.0 - sg)))
            dxh = dln * gv
            duc = rstd * (dxh - jnp.mean(dxh, axis=-1, keepdims=True) - xhat * jnp.mean(dxh * xhat, axis=-1, keepdims=True))
            return duc, dln, xhat

        duc_p, _, _ = ln_bwd(dsp, ucp)
        duc_c, dln_c, xhat_c = ln_bwd(dsc, ucc)
        duc_n, _, _ = ln_bwd(dsn, ucn)
        dext[0:HALO, :] = jnp.where(i > 0, duc_p, 0.0)
        dext[HALO:HALO + ts, :] = duc_c
        dext[HALO + ts:, :] = jnp.where(i < n_tiles - 1, duc_n, 0.0)
        _glu_ext(zp, zc, zn, ext, cw, ts, i, n_tiles)

        dg_part = jnp.sum(dln_c * xhat_c, axis=0, keepdims=True)
        db_part = jnp.sum(dln_c, axis=0, keepdims=True)

        @pl.when(i == 0)
        def _():
            dw_ref[...] = jnp.zeros_like(dw_ref)
            dg_ref[...] = dg_part
            db_ref[...] = db_part

        @pl.when(i > 0)
        def _():
            dg_ref[...] += dg_part
            db_ref[...] += db_part

        def col_block(cb, carry):
            c0 = pl.multiple_of(cb * LANES, LANES)
            cols, gate_cols = pl.ds(c0, LANES), pl.ds(cw + c0, LANES)
            _shifted_copies(dext, dsh, cols, ts)
            _shifted_copies(ext, sh, cols, ts)
            du = jnp.zeros((ts, LANES), F32)
            for j in range(CONV_KERNEL):
                du = du + _tap_rows(dsh, HALO + pad - j, ts) * w_ref[j:j + 1, cols]
            ca, sb = zc[:, cols].astype(F32), _sigmoid(zc[:, gate_cols].astype(F32))
            dz_ref[:, cols] = (du * sb).astype(BF)
            dz_ref[:, gate_cols] = (du * ca * sb * (1.0 - sb)).astype(BF)
            duc_blk = _tap_rows(dsh, HALO, ts)
            for j in range(CONV_KERNEL):
                dw_ref[j:j + 1, cols] += jnp.sum(_tap_rows(sh, HALO - pad + j, ts) * duc_blk, axis=0, keepdims=True)
            return carry

        lax.fori_loop(0, cw // LANES, col_block, 0)

    vec = pl.BlockSpec((1, cw), lambda i: (0, 0))
    wsp = pl.BlockSpec((32, cw), lambda i: (0, 0))
    res, cres = _call(
        body, name="conv_bwd", grid=(n_tiles,),
        in_specs=_halo_specs(ts, s, 2 * cw, 0) + _halo_specs(ts, s, cw, 0) + _halo_specs(ts, s, cw, 0) + [wsp, vec, vec],
        out_specs=[pl.BlockSpec((ts, 2 * cw), lambda i: (i, 0)), wsp, vec, vec],
        out_shape=[jax.ShapeDtypeStruct((s, 2 * cw), BF), jax.ShapeDtypeStruct((32, cw), F32),
                   jax.ShapeDtypeStruct((1, cw), F32), jax.ShapeDtypeStruct((1, cw), F32)],
        scratch_shapes=[pltpu.VMEM((ts + 2 * HALO, cw), F32), pltpu.VMEM((ts + 2 * HALO, cw), F32),
                        _shift_scratch(ts), _shift_scratch(ts)],
        semantics=("arbitrary",), args=[z, z, z, ds, ds, ds, uc, uc, uc, wdw, ln_g, ln_b], comm=comm)
    return (*res, cres)


def _rope_tables(s):
    axis_dim = HEAD_DIM // 2
    t = jnp.arange(s, dtype=jnp.int32)
    row = (t // GRID_W).astype(F32)[:, None]
    col = (t % GRID_W).astype(F32)[:, None]
    inv_freq = ROPE_THETA ** (-jnp.arange(0, axis_dim, 2, dtype=F32) / axis_dim)[None, :]
    ar, ac = row * inv_freq, col * inv_freq
    cos = jnp.concatenate([jnp.cos(ar), jnp.cos(ar), jnp.cos(ac), jnp.cos(ac)], axis=-1)
    sin = jnp.concatenate([-jnp.sin(ar), jnp.sin(ar), -jnp.sin(ac), jnp.sin(ac)], axis=-1)
    return cos, sin


def _swap_quarters(x):
    q = HEAD_DIM // 4
    lane = lax.broadcasted_iota(jnp.int32, x.shape, 1)
    return jnp.where((lane % (2 * q)) < q, pltpu.roll(x, HEAD_DIM - q, 1), pltpu.roll(x, q, 1))


def _qk_fwd(z, cos, sin, qg, kg, d, ts=None):
    s = z.shape[0]
    ts = ts or ROW_TILE
    kvw = d // GROUP
    scale = Q_SCALE

    def body(q_ref, k_ref, c_ref, s_ref, qg_ref, kg_ref, qo_ref, ko_ref):
        cv, sv = c_ref[...], s_ref[...]

        def head(x_ref, g_ref, o_ref, h, mul):
            xv = x_ref[:, h * HEAD_DIM:(h + 1) * HEAD_DIM].astype(F32)
            r = lax.rsqrt(jnp.mean(xv * xv, axis=-1, keepdims=True) + EPS)
            nrm = xv * r * g_ref[...]
            out = nrm * cv + _swap_quarters(nrm) * sv
            o_ref[:, h * HEAD_DIM:(h + 1) * HEAD_DIM] = (out * mul).astype(BF)

        for h in range(d // HEAD_DIM):
            head(q_ref, qg_ref, qo_ref, h, scale)
        for h in range(kvw // HEAD_DIM):
            head(k_ref, kg_ref, ko_ref, h, 1.0)

    cw2 = d
    tab = pl.BlockSpec((ts, HEAD_DIM), lambda i: (i, 0))
    vec = pl.BlockSpec((1, HEAD_DIM), lambda i: (0, 0))
    return pl.pallas_call(
        body, name="qk_fwd", grid=(s // ts,),
        in_specs=[pl.BlockSpec((ts, d), lambda i: (i, cw2 // d)),
                  pl.BlockSpec((ts, kvw), lambda i: (i, (cw2 + d) // kvw)), tab, tab, vec, vec],
        out_specs=[pl.BlockSpec((ts, d), lambda i: (i, 0)), pl.BlockSpec((ts, kvw), lambda i: (i, 0))],
        out_shape=[jax.ShapeDtypeStruct((s, d), BF), jax.ShapeDtypeStruct((s, kvw), BF)],
        compiler_params=_params(("parallel",)),
    )(z, z, cos, sin, qg, kg)


def _qk_bwd(dqt, dkt, z, cos, sin, qg, kg, d, ts=None):
    s = z.shape[0]
    ts = ts or ROW_TILE
    kvw = d // GROUP
    scale = HEAD_DIM ** -0.5

    def body(dq_ref, dk_ref, q_ref, k_ref, c_ref, s_ref, qg_ref, kg_ref, dqo_ref, dko_ref, dqg_ref, dkg_ref):
        cv, sv = c_ref[...], s_ref[...]

        def head(dy_ref, x_ref, g_ref, o_ref, h, mul):
            dout = dy_ref[:, h * HEAD_DIM:(h + 1) * HEAD_DIM].astype(F32) * mul
            dn = dout * cv + _swap_quarters(dout * sv)
            xv = x_ref[:, h * HEAD_DIM:(h + 1) * HEAD_DIM].astype(F32)
            r = lax.rsqrt(jnp.mean(xv * xv, axis=-1, keepdims=True) + EPS)
            nh = xv * r
            dnh = dn * g_ref[...]
            o_ref[:, h * HEAD_DIM:(h + 1) * HEAD_DIM] = (r * (dnh - nh * jnp.mean(dnh * nh, axis=-1, keepdims=True))).astype(BF)
            return jnp.sum(dn * nh, axis=0, keepdims=True)

        dqg = jnp.zeros((1, HEAD_DIM), F32)
        for h in range(d // HEAD_DIM):
            dqg = dqg + head(dq_ref, q_ref, qg_ref, dqo_ref, h, scale)
        dkg = jnp.zeros((1, HEAD_DIM), F32)
        for h in range(kvw // HEAD_DIM):
            dkg = dkg + head(dk_ref, k_ref, kg_ref, dko_ref, h, LN2)

        @pl.when(pl.program_id(0) == 0)
        def _():
            dqg_ref[...] = dqg
            dkg_ref[...] = dkg

        @pl.when(pl.program_id(0) > 0)
        def _():
            dqg_ref[...] += dqg
            dkg_ref[...] += dkg

    cw2 = d
    tab = pl.BlockSpec((ts, HEAD_DIM), lambda i: (i, 0))
    vec = pl.BlockSpec((1, HEAD_DIM), lambda i: (0, 0))
    qrow = pl.BlockSpec((ts, d), lambda i: (i, 0))
    krow = pl.BlockSpec((ts, kvw), lambda i: (i, 0))
    return pl.pallas_call(
        body, name="qk_bwd", grid=(s // ts,),
        in_specs=[qrow, krow, pl.BlockSpec((ts, d), lambda i: (i, cw2 // d)),
                  pl.BlockSpec((ts, kvw), lambda i: (i, (cw2 + d) // kvw)), tab, tab, vec, vec],
        out_specs=[qrow, krow, vec, vec],
        out_shape=[jax.ShapeDtypeStruct((s, d), BF), jax.ShapeDtypeStruct((s, kvw), BF),
                   jax.ShapeDtypeStruct((1, HEAD_DIM), F32), jax.ShapeDtypeStruct((1, HEAD_DIM), F32)],
        compiler_params=_params(("arbitrary",)),
    )(dqt, dkt, z, z, cos, sin, qg, kg)


_NT = (((1,), (1,)), ((), ()))
_TN = (((0,), (0,)), ((), ()))


def _v_col_block(d):
    return (2 * d + d // GROUP) // HEAD_DIM


def _flash_fwd(qt, kt, z, d, tq=None, tk=None, comm=None):
    s = qt.shape[0]
    tq, tk = min(tq or FLASH_TQ_FWD, s), min(tk or FLASH_TK, s)
    ng, nq, nk = d // (GROUP * HEAD_DIM), s // tq, s // tk
    gw = GROUP * HEAD_DIM
    rows = GROUP * tq

    nt = tk // LANES
    assert nk % 2 == 0, (s, tk)

    def body(q_ref, k_ref, v_ref, o_ref, lse_ref, qs, v1, p_s, m_s, acc_s, sc_s):
        @pl.when(pl.program_id(1) == 0)
        def _():
            v1[:, :HEAD_DIM] = v_ref[...]
            v1[:, HEAD_DIM:] = jnp.ones((s, HEAD_DIM), BF)

        for h in range(GROUP):
            qs[h * tq:(h + 1) * tq, :] = q_ref[:, h * HEAD_DIM:(h + 1) * HEAD_DIM]
        m_s[...] = jnp.full((rows, LANES), -1e30, F32)
        acc_s[...] = jnp.zeros((rows, 2 * HEAD_DIM), F32)

        def scores(j):
            return lax.dot_general(qs[...], k_ref[pl.ds(pl.multiple_of(j * tk, tk), tk), :], _NT, preferred_element_type=F32)

        def softmax_pv(j, sc):
            kv_rows = pl.ds(pl.multiple_of(j * tk, tk), tk)
            mt = sc[:, :LANES]
            for c in range(1, nt):
                mt = jnp.maximum(mt, sc[:, c * LANES:(c + 1) * LANES])
            m_old = m_s[...]
            m_new = jnp.maximum(m_old, jnp.max(mt, axis=-1, keepdims=True))
            alpha = jnp.exp2(m_old - m_new)
            for c in range(nt):
                cs = slice(c * LANES, (c + 1) * LANES)
                p_s[:, cs] = jnp.exp2(sc[:, cs] - m_new).astype(BF)
            pv = jnp.dot(p_s[...], v1[kv_rows, :], preferred_element_type=F32)
            acc_s[:, :HEAD_DIM] = alpha * acc_s[:, :HEAD_DIM] + pv[:, :HEAD_DIM]
            acc_s[:, HEAD_DIM:] = alpha * acc_s[:, HEAD_DIM:] + pv[:, HEAD_DIM:]
            m_s[...] = m_new

        sc_s[0] = scores(0)

        def step(jj, carry):
            j = 2 * jj
            sc_s[1] = scores(j + 1)
            softmax_pv(j, sc_s[0])
            sc_s[0] = scores(jnp.minimum(j + 2, nk - 1))
            softmax_pv(j + 1, sc_s[1])
            return carry

        lax.fori_loop(0, nk // 2, step, 0)
        l = acc_s[:, HEAD_DIM:]
        o = acc_s[:, :HEAD_DIM] / l
        for h in range(GROUP):
            o_ref[:, h * HEAD_DIM:(h + 1) * HEAD_DIM] = o[h * tq:(h + 1) * tq, :].astype(BF)
        lse = m_s[...] + jnp.log2(l)
        for h in range(GROUP):
            lse_ref[h] = lse[h * tq:(h + 1) * tq, :]

    vb = _v_col_block(d)
    (o, lse), cres = _call(
        body, name="flash_fwd", grid=(ng, nq),
        in_specs=[pl.BlockSpec((tq, gw), lambda g, i: (i, g)),
                  pl.BlockSpec((s, HEAD_DIM), lambda g, i: (0, g)),
                  pl.BlockSpec((s, HEAD_DIM), lambda g, i: (0, vb + g))],
        out_specs=[pl.BlockSpec((tq, gw), lambda g, i: (i, g)),
                   pl.BlockSpec((GROUP, tq, LANES), lambda g, i: (g, i, 0))],
        out_shape=[jax.ShapeDtypeStruct((s, d), BF), jax.ShapeDtypeStruct((ng * GROUP, s, LANES), F32)],
        scratch_shapes=[pltpu.VMEM((rows, HEAD_DIM), BF), pltpu.VMEM((s, 2 * HEAD_DIM), BF), pltpu.VMEM((rows, tk), BF),
                        pltpu.VMEM((rows, LANES), F32), pltpu.VMEM((rows, 2 * HEAD_DIM), F32), pltpu.VMEM((2, rows, tk), F32)],
        semantics=("parallel", "arbitrary"), args=[qt, kt, z], comm=comm)
    return o, lse, cres


def _flash_bwd(qt, kt, z, o, do, lse, d, tq=None, tk=None, comm=None):
    s = qt.shape[0]
    tq, tk = min(tq or FLASH_TQ_BWD, s), min(tk or FLASH_TK, s)
    ng, nq, nk = d // (GROUP * HEAD_DIM), s // tq, s // tk
    gw = GROUP * HEAD_DIM
    rows = GROUP * tq

    nt = tk // LANES

    def body(q_ref, k_ref, v_ref, o_ref, do_ref, lse_ref, dq_ref, dk_ref, dv_ref, qs, dos, delta_s, dq_s, p_s, ds_s, lse_s):
        i = pl.program_id(1)
        for h in range(GROUP):
            cols = slice(h * HEAD_DIM, (h + 1) * HEAD_DIM)
            lse_s[h * tq:(h + 1) * tq, :] = lse_ref[h]
            qs[h * tq:(h + 1) * tq, :] = q_ref[:, cols]
            dov = do_ref[:, cols]
            dos[h * tq:(h + 1) * tq, :] = dov
            delta = jnp.sum(dov.astype(F32) * o_ref[:, cols].astype(F32), axis=-1, keepdims=True)
            delta_s[h * tq:(h + 1) * tq, :] = jnp.broadcast_to(delta, (tq, LANES))
        dq_s[...] = jnp.zeros((rows, HEAD_DIM), F32)

        @pl.when(i == 0)
        def _():
            dk_ref[...] = jnp.zeros_like(dk_ref)
            dv_ref[...] = jnp.zeros_like(dv_ref)

        def step(j, carry):
            kv_rows = pl.ds(pl.multiple_of(j * tk, tk), tk)
            kv, vv = k_ref[kv_rows, :], v_ref[kv_rows, :]
            sc = lax.dot_general(qs[...], kv, _NT, preferred_element_type=F32)
            dp = lax.dot_general(dos[...], vv, _NT, preferred_element_type=F32)
            lse, delta = lse_s[...], delta_s[...]
            for c in range(nt):
                cs = slice(c * LANES, (c + 1) * LANES)
                p = jnp.exp2(sc[:, cs] - lse)
                p_s[:, cs] = p.astype(BF)
                ds_s[:, cs] = (p * (dp[:, cs] - delta)).astype(BF)
            dv_ref[kv_rows, :] += lax.dot_general(p_s[...], dos[...], _TN, preferred_element_type=F32)
            dk_ref[kv_rows, :] += lax.dot_general(ds_s[...], qs[...], _TN, preferred_element_type=F32)
            dq_s[...] += jnp.dot(ds_s[...], kv, preferred_element_type=F32)
            return carry

        lax.fori_loop(0, nk, step, 0)
        for h in range(GROUP):
            dq_ref[:, h * HEAD_DIM:(h + 1) * HEAD_DIM] = dq_s[h * tq:(h + 1) * tq, :].astype(BF)

    vb = _v_col_block(d)
    qspec = pl.BlockSpec((tq, gw), lambda g, i: (i, g))
    kspec = pl.BlockSpec((s, HEAD_DIM), lambda g, i: (0, g))
    (dq, dk, dv), cres = _call(
        body, name="flash_bwd", grid=(ng, nq),
        in_specs=[qspec, kspec, pl.BlockSpec((s, HEAD_DIM), lambda g, i: (0, vb + g)), qspec, qspec,
                  pl.BlockSpec((GROUP, tq, LANES), lambda g, i: (g, i, 0))],
        out_specs=[qspec, kspec, kspec],
        out_shape=[jax.ShapeDtypeStruct((s, d), BF), jax.ShapeDtypeStruct((s, d // GROUP), F32),
                   jax.ShapeDtypeStruct((s, d // GROUP), F32)],
        scratch_shapes=[pltpu.VMEM((rows, HEAD_DIM), BF), pltpu.VMEM((rows, HEAD_DIM), BF), pltpu.VMEM((rows, LANES), F32),
                        pltpu.VMEM((rows, HEAD_DIM), F32), pltpu.VMEM((rows, tk), BF), pltpu.VMEM((rows, tk), BF),
                        pltpu.VMEM((rows, LANES), F32)],
        semantics=("parallel", "arbitrary"), args=[qt, kt, z, o, do, lse], comm=comm)
    return dq, dk, dv, cres


def _place():
    x, y, c = lax.axis_index("x"), lax.axis_index("y"), lax.axis_index("c")
    other_chips = [(1 - x, y), (x, 1 - y), (1 - x, 1 - y)]
    return x, y, c, other_chips


def _cast_place(name, w, chip_arr, tr=256):
    r, cc = w.shape
    tr = min(tr, r)

    def body(p_ref, w_ref, o_ref):
        o_ref[...] = w_ref[...].astype(BF)

    return pl.pallas_call(
        body, name=name,
        grid_spec=pltpu.PrefetchScalarGridSpec(
            num_scalar_prefetch=1, grid=(r // tr,),
            in_specs=[pl.BlockSpec((tr, cc), lambda i, p_ref: (i, 0))],
            out_specs=pl.BlockSpec((None, tr, cc), lambda i, p_ref: (p_ref[0], i, 0))),
        out_shape=jax.ShapeDtypeStruct((N_CHIPS, r, cc), BF),
        compiler_params=_params(("parallel",)),
    )(chip_arr, w)


def _gather_comm(bufs, short_host=False):
    n = len(bufs)
    pairs = [(w, j) for w in range(n) for j in range(N_CHIPS - 1)]

    def copies(dst, sems):
        send, recv, fsend, frecv = sems
        x, y, c, chips = _place()

        def part(w, chip, core_half):
            h = bufs[w].shape[1] // 2
            return dst[w].at[2 * chip[0] + chip[1], pl.ds(core_half * h, h)]

        def ici(w, j, incoming):
            slab = part(w, chips[j] if incoming else (x, y), c)
            return pltpu.make_async_remote_copy(
                src_ref=slab, dst_ref=slab, send_sem=send.at[3 * w + j], recv_sem=recv.at[3 * w + j],
                device_id=(*chips[j], c), device_id_type=MESH)

        def d2d(w, j, incoming):
            slab = part(w, chips[j], 1 - c if incoming else c)
            return pltpu.make_async_remote_copy(
                src_ref=slab, dst_ref=slab, send_sem=fsend.at[3 * w + j], recv_sem=frecv.at[3 * w + j],
                device_id=(x, y, 1 - c), device_id_type=MESH)

        return ici, d2d

    def first(_, dst, sems):
        ici, _d = copies(dst, sems)
        for w, j in pairs:
            ici(w, j, False).start()

    def middle(_, dst, sems):
        ici, d2d = copies(dst, sems)
        for w, j in pairs:
            ici(w, j, True).wait_recv()
            d2d(w, j, False).start()

    def last(_, dst, sems):
        ici, d2d = copies(dst, sems)
        for w, j in pairs:
            d2d(w, j, True).wait_recv()
        for w, j in pairs:
            ici(w, j, False).wait_send()
            d2d(w, j, False).wait_send()

    def middle_and_last(src, dst, sems):
        middle(src, dst, sems)
        last(src, dst, sems)

    phases = (first, None, middle_and_last) if short_host else (first, middle, last)
    return _Comm(arrays=list(bufs), out_shapes=[jax.ShapeDtypeStruct(b.shape, b.dtype) for b in bufs],
                 aliases={w: w for w in range(n)}, sems=[pltpu.SemaphoreType.DMA((3 * n,))] * 4, phases=phases)


def _run_comm(name, comm):
    nci, nco = len(comm.arrays), len(comm.out_shapes)

    def body(*refs):
        cin, cout, sems = refs[:nci], refs[nci:nci + nco], refs[nci + nco:]
        for fn in comm.phases:
            if fn is not None:
                fn(cin, cout, sems)

    return pl.pallas_call(
        body, name=name, in_specs=[ANY] * nci, out_specs=[ANY] * nco, out_shape=list(comm.out_shapes),
        input_output_aliases=dict(comm.aliases), scratch_shapes=list(comm.sems),
    )(*comm.arrays)


def _pair_comm(grads):
    n = len(grads)

    def copies(src, dst, sems):
        send, recv = sems
        x, y, c, _ = _place()
        out = []
        for w in range(n):
            h = grads[w].shape[1] // 2
            out.append(pltpu.make_async_remote_copy(
                src_ref=src[w].at[:, pl.ds((1 - c) * h, h), :], dst_ref=dst[w],
                send_sem=send.at[w], recv_sem=recv.at[w], device_id=(x, y, 1 - c), device_id_type=MESH))
        return out

    def first(src, dst, sems):
        for cp in copies(src, dst, sems):
            cp.start()

    def last(src, dst, sems):
        for cp in copies(src, dst, sems):
            cp.wait()

    return _Comm(arrays=list(grads),
                 out_shapes=[jax.ShapeDtypeStruct((N_CHIPS, g.shape[1] // 2, g.shape[2]), g.dtype) for g in grads],
                 aliases={}, sems=[pltpu.SemaphoreType.DMA((n,))] * 2, phases=(first, None, last))


def _pair_sum(name, own, got, c_arr, tr=256):
    nc, r, cc = own.shape
    h = r // 2
    tr = min(tr, h)
    nb = h // tr

    def body(c_ref, a_ref, b_ref, o_ref):
        o_ref[...] = (a_ref[...].astype(F32) + b_ref[...].astype(F32)).astype(BF)

    return pl.pallas_call(
        body, name=name,
        grid_spec=pltpu.PrefetchScalarGridSpec(
            num_scalar_prefetch=1, grid=(nc, nb),
            in_specs=[pl.BlockSpec((None, tr, cc), lambda s, i, c_ref: (s, c_ref[0] * nb + i, 0)),
                      pl.BlockSpec((None, tr, cc), lambda s, i, c_ref: (s, i, 0))],
            out_specs=pl.BlockSpec((None, tr, cc), lambda s, i, c_ref: (s, i, 0))),
        out_shape=jax.ShapeDtypeStruct((nc, h, cc), BF),
        compiler_params=_params(("parallel", "parallel")),
    )(c_arr, own, got)


def _chip_comm(parts):
    n = len(parts)

    def copies(src, dst, sems):
        send, recv = sems
        _, _, c, chips = _place()
        return [pltpu.make_async_remote_copy(
            src_ref=src[w].at[2 * chip[0] + chip[1]], dst_ref=dst[w].at[j],
            send_sem=send.at[3 * w + j], recv_sem=recv.at[3 * w + j], device_id=(*chip, c), device_id_type=MESH)
            for w in range(n) for j, chip in enumerate(chips)]

    def first(src, dst, sems):
        for cp in copies(src, dst, sems):
            cp.start()

    def last(src, dst, sems):
        for cp in copies(src, dst, sems):
            cp.wait()

    return _Comm(arrays=list(parts), out_shapes=[jax.ShapeDtypeStruct((N_CHIPS - 1,) + p.shape[1:], p.dtype) for p in parts],
                 aliases={}, sems=[pltpu.SemaphoreType.DMA((3 * n,))] * 2, phases=(first, None, last))


def _chip_sum(name, parts, got, chip_arr, c_arr, tr=256):
    _, h, cc = parts.shape
    tr = min(tr, h)
    nb = h // tr

    def body(chip_ref, c_ref, own_ref, got_ref, o_ref):
        acc = own_ref[...].astype(F32)
        for k in range(N_CHIPS - 1):
            acc = acc + got_ref[k].astype(F32)
        o_ref[...] = acc

    return pl.pallas_call(
        body, name=name,
        grid_spec=pltpu.PrefetchScalarGridSpec(
            num_scalar_prefetch=2, grid=(nb,),
            in_specs=[pl.BlockSpec((None, tr, cc), lambda i, chip_ref, c_ref: (chip_ref[0], i, 0)),
                      pl.BlockSpec((N_CHIPS - 1, tr, cc), lambda i, chip_ref, c_ref: (0, i, 0))],
            out_specs=pl.BlockSpec((tr, cc), lambda i, chip_ref, c_ref: (c_ref[0] * nb + i, 0))),
        out_shape=jax.ShapeDtypeStruct((2 * h, cc), F32),
        compiler_params=_params(("parallel",)),
    )(chip_arr, c_arr, parts, got)


def _pair_gather(bufs):
    n = len(bufs)

    def body(*refs):
        dst = refs[n:2 * n]
        send, recv = refs[2 * n:]
        x, y, c, _ = _place()

        def copy(w, core_half):
            h = bufs[w].shape[0] // 2
            rows = dst[w].at[pl.ds(core_half * h, h)]
            return pltpu.make_async_remote_copy(src_ref=rows, dst_ref=rows, send_sem=send.at[w], recv_sem=recv.at[w],
                                                device_id=(x, y, 1 - c), device_id_type=MESH)

        sends = [copy(w, c) for w in range(n)]
        for cp in sends:
            cp.start()
        for w in range(n):
            copy(w, 1 - c).wait_recv()
        for cp in sends:
            cp.wait_send()

    return pl.pallas_call(
        body, name="grad_pair_gather",
        in_specs=[ANY] * n, out_specs=[ANY] * n,
        out_shape=[jax.ShapeDtypeStruct(b.shape, b.dtype) for b in bufs],
        input_output_aliases={w: w for w in range(n)},
        scratch_shapes=[pltpu.SemaphoreType.DMA((n,))] * 2,
    )(*bufs)


def _all_sum_small(name, v):
    p = v.shape[0]

    def body(v_ref, o_ref, slots, send, recv):
        x, y, c, _ = _place()
        me = 4 * x + 2 * y + c
        copies = []
        for k in range(1, N_DEV):
            peer = (x ^ (k >> 2), y ^ ((k >> 1) & 1), c ^ (k & 1))
            copies.append(pltpu.make_async_remote_copy(
                src_ref=v_ref, dst_ref=slots.at[me], send_sem=send.at[k - 1], recv_sem=recv.at[k - 1],
                device_id=peer, device_id_type=MESH))
        for cp in copies:
            cp.start()
        slots[me] = v_ref[...]
        for cp in copies:
            cp.wait()
        acc = slots[0]
        for s in range(1, N_DEV):
            acc = acc + slots[s]
        o_ref[...] = acc

    vm = pl.BlockSpec(memory_space=pltpu.VMEM)
    return pl.pallas_call(
        body, name=name,
        in_specs=[vm], out_specs=vm,
        out_shape=jax.ShapeDtypeStruct(v.shape, F32),
        scratch_shapes=[pltpu.VMEM((N_DEV, p, LANES), F32), pltpu.SemaphoreType.DMA((N_DEV - 1,)),
                        pltpu.SemaphoreType.DMA((N_DEV - 1,))],
    )(v)


def _adamw(name, w, g, m, v, tr=256):
    r, c = w.shape
    tr = min(tr, r)
    assert r % tr == 0
    bc1 = 1.0 - ADAM_B1 ** ADAM_STEP
    bc2 = 1.0 - ADAM_B2 ** ADAM_STEP

    def body(w_ref, g_ref, m_ref, v_ref, d_ref, nm_ref, nv_ref):
        gv = g_ref[...]
        nm = ADAM_B1 * m_ref[...] + (1.0 - ADAM_B1) * gv
        nv = ADAM_B2 * v_ref[...] + (1.0 - ADAM_B2) * (gv * gv)
        nm_ref[...] = nm
        nv_ref[...] = nv
        d_ref[...] = -ADAM_LR * ((nm / bc1) / (jnp.sqrt(nv / bc2) + ADAM_EPS) + ADAM_WD * w_ref[...])

    blk = pl.BlockSpec((tr, c), lambda i: (i, 0))
    return pl.pallas_call(
        body, name=name, grid=(r // tr,),
        in_specs=[blk] * 4, out_specs=[blk] * 3,
        out_shape=[jax.ShapeDtypeStruct((r, c), F32)] * 3,
        compiler_params=_params(("parallel",)),
    )(w, g, m, v)


def _pack_small(parts):
    flat = jnp.concatenate([a.reshape(-1) for a in parts])
    n = flat.shape[0]
    p = -(-n // (8 * LANES)) * 8
    packed = jnp.pad(flat, (0, p * LANES - n)).reshape(p, LANES)

    def unpack(q):
        out, off = [], 0
        f = q.reshape(-1)
        for a in parts:
            out.append(f[off:off + a.size].reshape(a.shape))
            off += a.size
        return out

    return packed, unpack


def kernel(x, p, norm_mix, w_in, w_dw, conv_ln_g, conv_ln_b, w_conv_proj, q_norm, k_norm, w_attn_proj, w_out, norm_ffn, w_ff1, w_ff2, norm_ple, w_ple_gate, w_ple_proj, norm_final, loss_target, m_norm_mix, m_w_in, m_w_dw, m_conv_ln_g, m_conv_ln_b, m_w_conv_proj, m_q_norm, m_k_norm, m_w_attn_proj, m_w_out, m_norm_ffn, m_w_ff1, m_w_ff2, m_norm_ple, m_w_ple_gate, m_w_ple_proj, m_norm_final, v_norm_mix, v_w_in, v_w_dw, v_conv_ln_g, v_conv_ln_b, v_w_conv_proj, v_q_norm, v_k_norm, v_w_attn_proj, v_w_out, v_norm_ffn, v_w_ff1, v_w_ff2, v_norm_ple, v_w_ple_gate, v_w_ple_proj, v_norm_final):
    s, d = x.shape[1], x.shape[2]
    cw = d // 2
    kvw = d // GROUP
    xs, ps, tgt = x[0], p[0, 0], loss_target[0]
    cx, cy, cc = lax.axis_index("x"), lax.axis_index("y"), lax.axis_index("c")
    chip = 2 * cx + cy
    c_arr = jnp.reshape(cc, (1,)).astype(jnp.int32)
    tm, tme = min(MM_TM, s), min(MM_TM_EPI, s)

    names = ["w_in", "w_conv_proj", "w_attn_proj", "w_out", "w_ff1", "w_ff2", "w_ple_gate", "w_ple_proj"]
    big = [w_in, w_conv_proj, w_attn_proj, w_out, w_ff1, w_ff2, w_ple_gate, w_ple_proj]
    chip_arr = jnp.reshape(chip, (1,)).astype(jnp.int32)
    placed = [_cast_place("cast_" + nm, w[0], chip_arr) for nm, w in zip(names, big)]
    h0, (win,) = _rms_fwd("rms_mix", xs, norm_mix, comm=_gather_comm(placed[:1], short_host=True))
    cpc = cw // N_CHIPS
    taps = jnp.zeros((32, N_CHIPS, cpc), F32).at[:CONV_KERNEL].set(
        jnp.where(lax.broadcasted_iota(jnp.int32, (1, N_CHIPS, 1), 1) == chip, w_dw[0][:, None, :], 0.0))
    taps = jnp.where(cc == 0, taps, 0.0).reshape(32 * cw // LANES, LANES)
    wdw = _all_sum_small("gather_taps", taps).reshape(32, cw)

    cos, sin = _rope_tables(s)
    (z,) = _mm("z_proj", h0, win, b_cm=True, tm=tm, tn=win.shape[2] // 3, tk=d)
    uc, act = _conv_fwd(z, wdw, conv_ln_g, conv_ln_b, cw)
    qt, kt = _qk_fwd(z, cos, sin, q_norm, k_norm, d)
    o, lse, (wcp, wap, wout, w1, w2, wpg, wple) = _flash_fwd(qt, kt, z, d, comm=_gather_comm(placed[1:]))
    wap, wout, w2, wpg = (t.reshape(-1, t.shape[-1]) for t in (wap, wout, w2, wpg))
    (y_c,) = _mm("conv_proj", act, wcp, b_cm=True, tm=tm, tn=wcp.shape[2], tk=cw, out_dtypes=(F32,))
    tn = d // 2
    gcb = (2 * d + 2 * kvw) // tn

    def merge_epi(acc, yc, gc, ga):
        return acc, _sigmoid(gc.astype(F32)) * yc + _sigmoid(ga.astype(F32)) * acc

    y_a, merged = _mm("attn_proj", o, wap, tm=tme, tn=tn, tk=d, epi=merge_epi, out_dtypes=(BF, BF),
                      extras=[_tile_extra(y_c, tme, tn), _tile_extra(z, tme, tn, gcb), _tile_extra(z, tme, tn, gcb + 2)])
    (x1,) = _mm("out_proj", merged, wout, tm=tm, tn=tn, tk=d, epi=lambda acc, r: (r + acc,), out_dtypes=(F32,),
                extras=[_tile_extra(xs, tm, tn)])
    h1 = _rms_fwd("rms_ffn", x1, norm_ffn)
    (a,) = _mm("ff1", h1, w1, b_cm=True, tm=tm, tn=tn, tk=d)

    def relu2(t):
        return jnp.square(jnp.maximum(t, 0.0))

    (x2,) = _mm("ff2", a, w2, tm=tm, tn=tn, tk=d, a_fn=relu2, epi=lambda acc, r: (r + acc,), out_dtypes=(F32,),
                extras=[_tile_extra(x1, tm, tn)])
    h2 = _rms_fwd("rms_ple", x2, norm_ple)
    to_bf = lambda t: t.astype(BF)
    (e,) = _mm("ple_proj", ps, wple, b_cm=True, tm=tm, tn=wple.shape[2], tk=ps.shape[1], a_fn=to_bf)

    def ple_epi(acc, ev, r):
        gt = _sigmoid(acc)
        return r + gt * ev.astype(F32), gt

    x3, gate = _mm("ple_gate", h2, wpg, tm=tme, tn=tn, tk=d, epi=ple_epi, out_dtypes=(F32, BF),
                   extras=[_tile_extra(e, tme, tn), _tile_extra(x2, tme, tn)])

    dx3, de, dgp, sq, d_fin = _loss_bwd(x3, tgt, norm_final.reshape(1, d), e, gate)
    tkt = min(2048, s)
    (g_wple,) = _mm("d_wple", ps, de, ta=True, out_cm=True, tm=ps.shape[1], tn=wple.shape[2], tk=tkt, a_fn=to_bf)
    (g_wpg,) = _mm("d_wpg", h2, dgp, ta=True, tm=tm, tn=tn, tk=tkt)
    (dh2,) = _mm("d_h2", dgp, wpg, tb=True, tm=tm, tn=tn, tk=d)
    dx2, dx2b, d_ple = _rms_bwd("rms_ple_bwd", dh2, x2, norm_ple, dx3)

    (da,) = _mm("d_a", dx2b, w2, tb=True, tm=tm, tn=tn, tk=d, out_dtypes=(BF,),
                epi=lambda acc, av: (acc * (2.0 * jnp.maximum(av.astype(F32), 0.0)),), extras=[_tile_extra(a, tm, tn)])
    (g_w2,) = _mm("d_w2", a, dx2b, ta=True, tm=tm, tn=tn, tk=tkt, a_fn=relu2)
    (g_w1,) = _mm("d_w1", h1, da, ta=True, out_cm=True, tm=tm, tn=tn, tk=tkt)
    (dh1,) = _mm("d_h1", da, w1, tb=True, b_cm=True, tm=tm, tn=tn, tk=w1.shape[2])
    dx1, dx1b, d_ffn = _rms_bwd("rms_ffn_bwd", dh1, x1, norm_ffn, dx2)

    def merge_bwd(acc, gc, ga, yc, ya):
        sc, sa = _sigmoid(gc.astype(F32)), _sigmoid(ga.astype(F32))
        return acc * sc, acc * sa, acc * yc * sc * (1.0 - sc), acc * ya.astype(F32) * sa * (1.0 - sa)

    dy_c, dy_a, dg_c, dg_a = _mm(
        "d_merged", dx1b, wout, tb=True, tm=tme, tn=tn, tk=d, epi=merge_bwd, out_dtypes=(BF, BF, BF, BF),
        extras=[_tile_extra(z, tme, tn, gcb), _tile_extra(z, tme, tn, gcb + 2), _tile_extra(y_c, tme, tn),
                _tile_extra(y_a, tme, tn)])
    (g_wout,) = _mm("d_wout", merged, dx1b, ta=True, tm=tm, tn=tn, tk=tkt)
    (g_wap,) = _mm("d_wap", o, dy_a, ta=True, tm=tm, tn=tn, tk=tkt)

    def slabs(g):
        return g if g.ndim == 3 else g.reshape(N_CHIPS, g.shape[0] // N_CHIPS, g.shape[1])

    grads_a = [slabs(g) for g in (g_wap, g_wout, g_w1, g_w2, g_wpg, g_wple)]
    (do,), got_a = _mm("d_o", dy_a, wap, tb=True, tm=tm, tn=tn, tk=d, comm=_pair_comm(grads_a))
    parts_a = [_pair_sum("pair_sum_" + nm, g, r, c_arr) for nm, g, r in zip(names[2:], grads_a, got_a)]
    dqt, dkt, dv, _ = _flash_bwd(qt, kt, z, o, do, lse, d)
    dq, dk, d_qn, d_kn = _qk_bwd(dqt, dkt, z, cos, sin, q_norm, k_norm, d)
    (g_wcp,) = _mm("d_wcp", act, dy_c, ta=True, out_cm=True, tm=cw, tn=wcp.shape[2], tk=tkt)
    (dact,) = _mm("d_act", dy_c, wcp, tb=True, b_cm=True, tm=tm, tn=cw, tk=wcp.shape[2])
    p_wap, p_wout, p_w1, p_w2, p_wpg, p_wple = parts_a
    dcab, d_taps, d_lng, d_lnb, (s_wap, s_wout, s_wpg, s_wple) = _conv_bwd(
        dact, uc, z, wdw, conv_ln_g, conv_ln_b, cw, comm=_chip_comm([p_wap, p_wout, p_wpg, p_wple]))
    dz = jnp.concatenate([dcab, dq, dk, dv.astype(BF), dg_c, dg_a], axis=1)
    (g_win,), (s_w1, s_w2) = _mm("d_win", h0, dz, ta=True, out_cm=True, tm=tm, tn=win.shape[2] // 3, tk=tkt,
                                 comm=_chip_comm([p_w1, p_w2]))
    slots_a = [s_wap, s_wout, s_w1, s_w2, s_wpg, s_wple]
    grads_b = [slabs(g_win), slabs(g_wcp)]
    got_b = _run_comm("grad_pair_exchange_b", _pair_comm(grads_b))
    parts_b = [_pair_sum("pair_sum_" + nm, g, r, c_arr) for nm, g, r in zip(names[:2], grads_b, got_b)]
    (dh0,), slots_b = _mm("d_h0", dz, win, tb=True, b_cm=True, tm=tm, tn=tn, tk=win.shape[2], comm=_chip_comm(parts_b))
    dx, _, d_mix = _rms_bwd("rms_mix_bwd", dh0, xs, norm_mix, dx1)
    big_grads = _pair_gather(
        [_chip_sum("chip_sum_" + nm, cp, sl, chip_arr, c_arr)
         for nm, cp, sl in zip(names, parts_b + parts_a, list(slots_b) + list(slots_a))])

    small = [d_mix, d_taps[:CONV_KERNEL], d_lng, d_lnb, d_qn, d_kn, d_ffn, d_ple, d_fin]
    packed, unpack = _pack_small(small)
    g_mix, g_taps, g_lng, g_lnb, g_qn, g_kn, g_ffn, g_ple, g_fin = unpack(_all_sum_small("reduce_small", packed))
    g_dw = lax.dynamic_slice_in_dim(g_taps.reshape(CONV_KERNEL, N_CHIPS, cpc), chip, 1, axis=1).reshape(1, CONV_KERNEL, cpc)

    sq_local = lax.reduce_precision(sq[0, 0], 8, 23)
    loss = (0.5 / d) * lax.psum(sq_local, ("x", "y", "c"))

    grads = {
        "norm_mix": g_mix, "w_in": big_grads[0][None], "w_dw": g_dw, "conv_ln_g": g_lng, "conv_ln_b": g_lnb,
        "w_conv_proj": big_grads[1][None], "q_norm": g_qn, "k_norm": g_kn, "w_attn_proj": big_grads[2][None],
        "w_out": big_grads[3][None], "norm_ffn": g_ffn, "w_ff1": big_grads[4][None], "w_ff2": big_grads[5][None],
        "norm_ple": g_ple, "w_ple_gate": big_grads[6][None], "w_ple_proj": big_grads[7][None],
        "norm_final": g_fin.reshape(d),
    }
    weights = dict(norm_mix=norm_mix, w_in=w_in, w_dw=w_dw, conv_ln_g=conv_ln_g, conv_ln_b=conv_ln_b, w_conv_proj=w_conv_proj,
                   q_norm=q_norm, k_norm=k_norm, w_attn_proj=w_attn_proj, w_out=w_out, norm_ffn=norm_ffn, w_ff1=w_ff1,
                   w_ff2=w_ff2, norm_ple=norm_ple, w_ple_gate=w_ple_gate, w_ple_proj=w_ple_proj, norm_final=norm_final)
    m_in = dict(norm_mix=m_norm_mix, w_in=m_w_in, w_dw=m_w_dw, conv_ln_g=m_conv_ln_g, conv_ln_b=m_conv_ln_b,
                w_conv_proj=m_w_conv_proj, q_norm=m_q_norm, k_norm=m_k_norm, w_attn_proj=m_w_attn_proj, w_out=m_w_out,
                norm_ffn=m_norm_ffn, w_ff1=m_w_ff1, w_ff2=m_w_ff2, norm_ple=m_norm_ple, w_ple_gate=m_w_ple_gate,
                w_ple_proj=m_w_ple_proj, norm_final=m_norm_final)
    v_in = dict(norm_mix=v_norm_mix, w_in=v_w_in, w_dw=v_w_dw, conv_ln_g=v_conv_ln_g, conv_ln_b=v_conv_ln_b,
                w_conv_proj=v_w_conv_proj, q_norm=v_q_norm, k_norm=v_k_norm, w_attn_proj=v_w_attn_proj, w_out=v_w_out,
                norm_ffn=v_norm_ffn, w_ff1=v_w_ff1, w_ff2=v_w_ff2, norm_ple=v_norm_ple, w_ple_gate=v_w_ple_gate,
                w_ple_proj=v_w_ple_proj, norm_final=v_norm_final)
    order = list(weights)
    deltas, new_m, new_v, g_out = [], [], [], []
    for nm in order:
        w = weights[nm]
        shape = w.shape
        two_d = (-1, shape[-1])
        dl, mm_, vv_ = _adamw("adamw_" + nm, w.reshape(two_d), grads[nm].reshape(two_d), m_in[nm].reshape(two_d),
                              v_in[nm].reshape(two_d))
        g_out.append(grads[nm].reshape(shape))
        deltas.append(dl.reshape(shape))
        new_m.append(mm_.reshape(shape))
        new_v.append(vv_.reshape(shape))
    return (loss, dx[None], *g_out, *deltas, *new_m, *new_v)
```

```python
from typing import NamedTuple

import jax
import jax.numpy as jnp
from jax import lax
from jax.experimental import pallas as pl
from jax.experimental.pallas import tpu as pltpu

F32 = jnp.float32
BF = jnp.bfloat16

EPS = 1e-6
HEAD_DIM = 128
GROUP = 4
GRID_W = 64
ROPE_THETA = 10000.0
CONV_KERNEL = 31
HALO = 16
N_CHIPS = 4
N_DEV = 8
LANES = 128

ADAM_LR = 0.001
ADAM_B1 = 0.9
ADAM_B2 = 0.999
ADAM_EPS = 1e-08
ADAM_WD = 0.01
ADAM_STEP = 10

VMEM_LIMIT = 56 * 2 ** 20
LOG2E = 1.4426950408889634
LN2 = 0.6931471805599453
Q_SCALE = HEAD_DIM ** -0.5 * LOG2E
ROW_TILE = 256
FLASH_TQ_FWD = 512
FLASH_TQ_BWD = 512
FLASH_TK = 512
MM_TM = 1024
MM_TM_EPI = 512
MESH = pl.DeviceIdType.MESH
ANY = pl.BlockSpec(memory_space=pl.ANY)


def _params(sem):
    return pltpu.CompilerParams(dimension_semantics=sem, vmem_limit_bytes=VMEM_LIMIT)


def _sigmoid(x):
    return 1.0 / (1.0 + jnp.exp(-x))


class _Comm(NamedTuple):
    arrays: list
    out_shapes: list
    aliases: dict
    sems: list
    phases: tuple


def _call(body, *, name, grid, in_specs, out_specs, out_shape, scratch_shapes, semantics, args, comm=None):
    n_in, n_out = len(in_specs), len(out_specs)
    if comm is None:
        res = pl.pallas_call(body, name=name, grid=grid, in_specs=in_specs, out_specs=out_specs, out_shape=out_shape,
                             scratch_shapes=scratch_shapes, compiler_params=_params(semantics))(*args)
        return res, []
    nci, nco, ncs = len(comm.arrays), len(comm.out_shapes), len(comm.sems)
    n_steps = 1
    for g in grid:
        n_steps *= g
    first, middle, last = comm.phases

    def hosted(*refs):
        ins, cin = refs[:n_in], refs[n_in:n_in + nci]
        outs = refs[n_in + nci:n_in + nci + n_out]
        cout = refs[n_in + nci + n_out:n_in + nci + n_out + nco]
        rest = refs[n_in + nci + n_out + nco:]
        scratch, sems = rest[:len(rest) - ncs], rest[len(rest) - ncs:]
        step = 0
        for ax, g in enumerate(grid):
            step = step * g + pl.program_id(ax)
        for at, fn in ((0, first), (n_steps // 2, middle)):
            if fn is not None:
                pl.when(step == at)(lambda fn=fn: fn(cin, cout, sems))
        body(*ins, *outs, *scratch)
        if last is not None:
            pl.when(step == n_steps - 1)(lambda: last(cin, cout, sems))

    res = pl.pallas_call(
        hosted, name=name, grid=grid,
        in_specs=list(in_specs) + [ANY] * nci, out_specs=list(out_specs) + [ANY] * nco,
        out_shape=list(out_shape) + list(comm.out_shapes),
        input_output_aliases={n_in + a: n_out + b for a, b in comm.aliases.items()},
        scratch_shapes=list(scratch_shapes) + list(comm.sems),
        compiler_params=_params(("arbitrary",) * len(grid)),
    )(*args, *comm.arrays)
    return res[:n_out], res[n_out:]


def _mm(name, a, b, *, tm, tn, tk, ta=False, tb=False, b_cm=False, out_cm=False,
        a_fn=None, extras=(), epi=None, out_dtypes=(BF,), epi_rows=256, comm=None, b_resident=False):
    if ta:
        kc, m = a.shape
    else:
        m, kc = a.shape
    if b_cm:
        nc, r, c = b.shape
        n, per = (r, c) if tb else (nc * c, c)
    else:
        n = b.shape[0] if tb else b.shape[1]
    tm, tn, tk = min(tm, m), min(tn, n), min(tk, kc)
    assert m % tm == 0 and n % tn == 0 and kc % tk == 0, (name, m, n, kc, tm, tn, tk)
    nk = kc // tk
    a_spec = pl.BlockSpec((tk, tm), lambda i, j, k: (k, i)) if ta else pl.BlockSpec((tm, tk), lambda i, j, k: (i, k))
    if b_cm and not tb:
        assert per % tn == 0
        npj = per // tn
        b_spec = pl.BlockSpec((None, tk, tn), lambda i, j, k: (j // npj, k, j % npj))
    elif b_cm:
        assert per % tk == 0
        npk = per // tk
        b_spec = pl.BlockSpec((None, tn, tk), lambda i, j, k: (k // npk, j, k % npk))
    elif b_resident:
        assert nk == 1
        b_spec = pl.BlockSpec(b.shape, lambda i, j, k: (0, 0))
    elif tb:
        b_spec = pl.BlockSpec((tn, tk), lambda i, j, k: (j, k))
    else:
        b_spec = pl.BlockSpec((tk, tn), lambda i, j, k: (k, j))
    if out_cm:
        assert (n // N_CHIPS) % tn == 0
        npo = (n // N_CHIPS) // tn
        o_spec = pl.BlockSpec((None, tm, tn), lambda i, j, k: (j // npo, i, j % npo))
        o_shape = (N_CHIPS, m, n // N_CHIPS)
    else:
        o_spec = pl.BlockSpec((tm, tn), lambda i, j, k: (i, j))
        o_shape = (m, n)
    ne, no = len(extras), len(out_dtypes)
    dims = (((0 if ta else 1,), (1 if tb else 0,)), ((), ()))
    er = min(epi_rows, tm)
    chunked = nk == 1 and epi is not None and not ta
    use_acc = (nk > 1 or epi is not None) and not chunked
    assert chunked or not b_resident

    def body(*refs):
        a_ref, b_ref = refs[0], refs[1]
        ex = refs[2:2 + ne]
        outs = refs[2 + ne:2 + ne + no]
        if chunked:
            if b_resident:
                cols = pl.ds(pl.multiple_of(pl.program_id(1) * tn, tn), tn)
                bt = b_ref[cols, :] if tb else b_ref[:, cols]
            else:
                bt = b_ref[...]
            for r0 in range(0, tm, er):
                rows = slice(r0, r0 + er)
                at = a_ref[rows, :]
                if a_fn is not None:
                    at = a_fn(at)
                d = lax.dot_general(at, bt, dims, preferred_element_type=F32)
                vals = epi(d, *[e[rows, :] for e in ex])
                for o, v, dt in zip(outs, vals, out_dtypes):
                    o[rows, :] = v.astype(dt)
            return
        at = a_ref[...]
        if a_fn is not None:
            at = a_fn(at)
        d = lax.dot_general(at, b_ref[...], dims, preferred_element_type=F32)
        if not use_acc:
            outs[0][...] = d.astype(out_dtypes[0])
            return
        acc = refs[-1]
        k = pl.program_id(2)

        @pl.when(k == 0)
        def _():
            acc[...] = d

        if nk > 1:
            @pl.when(k > 0)
            def _():
                acc[...] += d

        @pl.when(k == nk - 1)
        def _():
            for r0 in range(0, tm, er):
                rows = slice(r0, r0 + er)
                if epi is None:
                    vals = (acc[rows, :],)
                else:
                    vals = epi(acc[rows, :], *[e[rows, :] for e in ex])
                for o, v, dt in zip(outs, vals, out_dtypes):
                    o[rows, :] = v.astype(dt)

    res, cres = _call(
        body, name=name, grid=(m // tm, n // tn, nk),
        in_specs=[a_spec, b_spec] + [pl.BlockSpec(bs, im) for _, bs, im in extras],
        out_specs=[o_spec] * no,
        out_shape=[jax.ShapeDtypeStruct(o_shape, dt) for dt in out_dtypes],
        scratch_shapes=[pltpu.VMEM((tm, tn), F32)] if use_acc else [],
        semantics=("parallel", "parallel", "arbitrary"),
        args=[a, b] + [e for e, _, _ in extras], comm=comm)
    return res if comm is None else (res, cres)


def _tile_extra(arr, tm, tn, col_block0=0):
    return (arr, (tm, tn), lambda i, j, k: (i, j + col_block0))


def _rms_fwd(name, x, g, ts=None, comm=None):
    s, d = x.shape
    ts = ts or ROW_TILE

    def body(x_ref, g_ref, h_ref):
        xv = x_ref[...]
        r = lax.rsqrt(jnp.mean(xv * xv, axis=-1, keepdims=True) + EPS)
        h_ref[...] = (xv * r * g_ref[...]).astype(BF)

    (h,), cres = _call(
        body, name=name, grid=(s // ts,),
        in_specs=[pl.BlockSpec((ts, d), lambda i: (i, 0)), pl.BlockSpec((1, d), lambda i: (0, 0))],
        out_specs=[pl.BlockSpec((ts, d), lambda i: (i, 0))],
        out_shape=[jax.ShapeDtypeStruct((s, d), BF)],
        scratch_shapes=[], semantics=("parallel",), args=[x, g], comm=comm)
    return h if comm is None else (h, cres)


def _rms_bwd(name, dh, x, g, dres, ts=None):
    s, d = x.shape
    ts = ts or ROW_TILE

    def body(dh_ref, x_ref, g_ref, dres_ref, dx_ref, dxb_ref, dg_ref):
        xv = x_ref[...]
        dhv = dh_ref[...].astype(F32)
        r = lax.rsqrt(jnp.mean(xv * xv, axis=-1, keepdims=True) + EPS)
        nrm = xv * r
        dn = dhv * g_ref[...]
        dx = dres_ref[...] + r * (dn - nrm * jnp.mean(dn * nrm, axis=-1, keepdims=True))
        dx_ref[...] = dx
        dxb_ref[...] = dx.astype(BF)
        part = jnp.sum(dhv * nrm, axis=0, keepdims=True)

        @pl.when(pl.program_id(0) == 0)
        def _():
            dg_ref[...] = part

        @pl.when(pl.program_id(0) > 0)
        def _():
            dg_ref[...] += part

    row = pl.BlockSpec((ts, d), lambda i: (i, 0))
    vec = pl.BlockSpec((1, d), lambda i: (0, 0))
    return pl.pallas_call(
        body, name=name, grid=(s // ts,),
        in_specs=[row, row, vec, row],
        out_specs=[row, row, vec],
        out_shape=[jax.ShapeDtypeStruct((s, d), F32), jax.ShapeDtypeStruct((s, d), BF), jax.ShapeDtypeStruct((1, d), F32)],
        compiler_params=_params(("arbitrary",)),
    )(dh, x, g, dres)


def _loss_bwd(x3, tgt, gfin, e, gate, ts=None):
    s, d = x3.shape
    ts = ts or ROW_TILE

    def body(x_ref, t_ref, g_ref, e_ref, gate_ref, dx_ref, de_ref, dgp_ref, sq_ref, dg_ref):
        xv = x_ref[...]
        gv = g_ref[...]
        r = lax.rsqrt(jnp.mean(xv * xv, axis=-1, keepdims=True) + EPS)
        nrm = xv * r
        err = nrm * gv - t_ref[...]
        dy = err * (1.0 / d)
        dn = dy * gv
        dx = r * (dn - nrm * jnp.mean(dn * nrm, axis=-1, keepdims=True))
        dx_ref[...] = dx
        ev = e_ref[...].astype(F32)
        gt = gate_ref[...].astype(F32)
        de_ref[...] = (dx * gt).astype(BF)
        dgp_ref[...] = (dx * ev * gt * (1.0 - gt)).astype(BF)
        sq = jnp.full((8, LANES), jnp.sum(err * err), F32)
        part = jnp.sum(dy * nrm, axis=0, keepdims=True)

        @pl.when(pl.program_id(0) == 0)
        def _():
            sq_ref[...] = sq
            dg_ref[...] = part

        @pl.when(pl.program_id(0) > 0)
        def _():
            sq_ref[...] += sq
            dg_ref[...] += part

    row = pl.BlockSpec((ts, d), lambda i: (i, 0))
    vec = pl.BlockSpec((1, d), lambda i: (0, 0))
    return pl.pallas_call(
        body, name="loss_bwd", grid=(s // ts,),
        in_specs=[row, row, vec, row, row],
        out_specs=[row, row, row, pl.BlockSpec((8, LANES), lambda i: (0, 0)), vec],
        out_shape=[jax.ShapeDtypeStruct((s, d), F32), jax.ShapeDtypeStruct((s, d), BF), jax.ShapeDtypeStruct((s, d), BF),
                   jax.ShapeDtypeStruct((8, LANES), F32), jax.ShapeDtypeStruct((1, d), F32)],
        compiler_params=_params(("arbitrary",)),
    )(x3, tgt, gfin, e, gate)


def _halo_specs(ts, s, width, col_block):
    per = ts // HALO
    last = s // HALO - 1
    return [
        pl.BlockSpec((HALO, width), lambda i: (jnp.maximum(i * per - 1, 0), col_block)),
        pl.BlockSpec((ts, width), lambda i: (i, col_block)),
        pl.BlockSpec((HALO, width), lambda i: (jnp.minimum((i + 1) * per, last), col_block)),
    ]


def _glu_ext(zp, zc, zn, ext, cw, ts, i, n_tiles):
    def glu(zr):
        zv = zr[...].astype(F32)
        return zv[:, :cw] * _sigmoid(zv[:, cw:])

    ext[0:HALO, :] = jnp.where(i > 0, glu(zp), 0.0)
    ext[HALO:HALO + ts, :] = glu(zc)
    ext[HALO + ts:, :] = jnp.where(i < n_tiles - 1, glu(zn), 0.0)


SUBLANES = 8


def _shift_scratch(ts):
    return pltpu.VMEM((SUBLANES, ts + 2 * HALO - SUBLANES, LANES), F32)


def _shifted_copies(ext, sh, cols, ts):
    n = ts + 2 * HALO - SUBLANES
    for r in range(SUBLANES):
        sh[r] = ext[r:r + n, cols]


def _tap_rows(sh, off, ts):
    q, r = divmod(off, SUBLANES)
    return sh[r, q * SUBLANES:q * SUBLANES + ts, :]


def _ln_stats(uc):
    mu = jnp.mean(uc, axis=-1, keepdims=True)
    xc = uc - mu
    rstd = lax.rsqrt(jnp.mean(xc * xc, axis=-1, keepdims=True) + EPS)
    return xc * rstd, rstd


def _conv_fwd(z, wdw, ln_g, ln_b, cw, ts=None):
    s = z.shape[0]
    ts = ts or ROW_TILE
    n_tiles = s // ts
    pad = CONV_KERNEL // 2

    def body(zp, zc, zn, w_ref, g_ref, b_ref, uc_ref, act_ref, ext, sh):
        i = pl.program_id(0)
        _glu_ext(zp, zc, zn, ext, cw, ts, i, n_tiles)

        def col_block(cb, carry):
            cols = pl.ds(pl.multiple_of(cb * LANES, LANES), LANES)
            _shifted_copies(ext, sh, cols, ts)
            acc = jnp.zeros((ts, LANES), F32)
            for j in range(CONV_KERNEL):
                acc = acc + _tap_rows(sh, HALO - pad + j, ts) * w_ref[j:j + 1, cols]
            uc_ref[:, cols] = acc
            return carry

        lax.fori_loop(0, cw // LANES, col_block, 0)
        xhat, _ = _ln_stats(uc_ref[...])
        ln = xhat * g_ref[...] + b_ref[...]
        act_ref[...] = (ln * _sigmoid(ln)).astype(BF)

    vec = pl.BlockSpec((1, cw), lambda i: (0, 0))
    row = pl.BlockSpec((ts, cw), lambda i: (i, 0))
    return pl.pallas_call(
        body, name="conv_fwd", grid=(n_tiles,),
        in_specs=_halo_specs(ts, s, 2 * cw, 0) + [pl.BlockSpec((32, cw), lambda i: (0, 0)), vec, vec],
        out_specs=[row, row],
        out_shape=[jax.ShapeDtypeStruct((s, cw), F32), jax.ShapeDtypeStruct((s, cw), BF)],
        scratch_shapes=[pltpu.VMEM((ts + 2 * HALO, cw), F32), _shift_scratch(ts)],
        compiler_params=_params(("parallel",)),
    )(z, z, z, wdw, ln_g, ln_b)


def _conv_bwd(ds, uc, z, wdw, ln_g, ln_b, cw, ts=None, comm=None):
    s = z.shape[0]
    ts = ts or ROW_TILE
    n_tiles = s // ts
    pad = CONV_KERNEL // 2

    def body(zp, zc, zn, dsp, dsc, dsn, ucp, ucc, ucn, w_ref, g_ref, b_ref,
             dz_ref, dw_ref, dg_ref, db_ref, ext, dext, sh, dsh):
        i = pl.program_id(0)
        gv, bv = g_ref[...], b_ref[...]

        def ln_bwd(ds_r, uc_r):
            xhat, rstd = _ln_stats(uc_r[...])
            ln = xhat * gv + bv
            sg = _sigmoid(ln)
            dln = ds_r[...].astype(F32) * (sg * (1.0 + ln * (1.0 - sg)))
            dxh = dln * gv
            duc = rstd * (dxh - jnp.mean(dxh, axis=-1, keepdims=True) - xhat * jnp.mean(dxh * xhat, axis=-1, keepdims=True))
            return duc, dln, xhat

        duc_p, _, _ = ln_bwd(dsp, ucp)
        duc_c, dln_c, xhat_c = ln_bwd(dsc, ucc)
        duc_n, _, _ = ln_bwd(dsn, ucn)
        dext[0:HALO, :] = jnp.where(i > 0, duc_p, 0.0)
        dext[HALO:HALO + ts, :] = duc_c
        dext[HALO + ts:, :] = jnp.where(i < n_tiles - 1, duc_n, 0.0)
        _glu_ext(zp, zc, zn, ext, cw, ts, i, n_tiles)

        dg_part = jnp.sum(dln_c * xhat_c, axis=0, keepdims=True)
        db_part = jnp.sum(dln_c, axis=0, keepdims=True)

        @pl.when(i == 0)
        def _():
            dw_ref[...] = jnp.zeros_like(dw_ref)
            dg_ref[...] = dg_part
            db_ref[...] = db_part

        @pl.when(i > 0)
        def _():
            dg_ref[...] += dg_part
            db_ref[...] += db_part

        def col_block(cb, carry):
            c0 = pl.multiple_of(cb * LANES, LANES)
            cols, gate_cols = pl.ds(c0, LANES), pl.ds(cw + c0, LANES)
            _shifted_copies(dext, dsh, cols, ts)
            _shifted_copies(ext, sh, cols, ts)
            du = jnp.zeros((ts, LANES), F32)
            for j in range(CONV_KERNEL):
                du = du + _tap_rows(dsh, HALO + pad - j, ts) * w_ref[j:j + 1, cols]
            ca, sb = zc[:, cols].astype(F32), _sigmoid(zc[:, gate_cols].astype(F32))
            dz_ref[:, cols] = (du * sb).astype(BF)
            dz_ref[:, gate_cols] = (du * ca * sb * (1.0 - sb)).astype(BF)
            duc_blk = _tap_rows(dsh, HALO, ts)
            for j in range(CONV_KERNEL):
                dw_ref[j:j + 1, cols] += jnp.sum(_tap_rows(sh, HALO - pad + j, ts) * duc_blk, axis=0, keepdims=True)
            return carry

        lax.fori_loop(0, cw // LANES, col_block, 0)

    vec = pl.BlockSpec((1, cw), lambda i: (0, 0))
    wsp = pl.BlockSpec((32, cw), lambda i: (0, 0))
    res, cres = _call(
        body, name="conv_bwd", grid=(n_tiles,),
        in_specs=_halo_specs(ts, s, 2 * cw, 0) + _halo_specs(ts, s, cw, 0) + _halo_specs(ts, s, cw, 0) + [wsp, vec, vec],
        out_specs=[pl.BlockSpec((ts, 2 * cw), lambda i: (i, 0)), wsp, vec, vec],
        out_shape=[jax.ShapeDtypeStruct((s, 2 * cw), BF), jax.ShapeDtypeStruct((32, cw), F32),
                   jax.ShapeDtypeStruct((1, cw), F32), jax.ShapeDtypeStruct((1, cw), F32)],
        scratch_shapes=[pltpu.VMEM((ts + 2 * HALO, cw), F32), pltpu.VMEM((ts + 2 * HALO, cw), F32),
                        _shift_scratch(ts), _shift_scratch(ts)],
        semantics=("arbitrary",), args=[z, z, z, ds, ds, ds, uc, uc, uc, wdw, ln_g, ln_b], comm=comm)
    return (*res, cres)


def _rope_tables(s):
    axis_dim = HEAD_DIM // 2
    t = jnp.arange(s, dtype=jnp.int32)
    row = (t // GRID_W).astype(F32)[:, None]
    col = (t % GRID_W).astype(F32)[:, None]
    inv_freq = ROPE_THETA ** (-jnp.arange(0, axis_dim, 2, dtype=F32) / axis_dim)[None, :]
    ar, ac = row * inv_freq, col * inv_freq
    cos = jnp.concatenate([jnp.cos(ar), jnp.cos(ar), jnp.cos(ac), jnp.cos(ac)], axis=-1)
    sin = jnp.concatenate([-jnp.sin(ar), jnp.sin(ar), -jnp.sin(ac), jnp.sin(ac)], axis=-1)
    return cos, sin


def _swap_quarters(x):
    q = HEAD_DIM // 4
    lane = lax.broadcasted_iota(jnp.int32, x.shape, 1)
    return jnp.where((lane % (2 * q)) < q, pltpu.roll(x, HEAD_DIM - q, 1), pltpu.roll(x, q, 1))


def _qk_fwd(z, cos, sin, qg, kg, d, ts=None):
    s = z.shape[0]
    ts = ts or ROW_TILE
    kvw = d // GROUP
    scale = Q_SCALE

    def body(q_ref, k_ref, c_ref, s_ref, qg_ref, kg_ref, qo_ref, ko_ref):
        cv, sv = c_ref[...], s_ref[...]

        def head(x_ref, g_ref, o_ref, h, mul):
            xv = x_ref[:, h * HEAD_DIM:(h + 1) * HEAD_DIM].astype(F32)
            r = lax.rsqrt(jnp.mean(xv * xv, axis=-1, keepdims=True) + EPS)
            nrm = xv * r * g_ref[...]
            out = nrm * cv + _swap_quarters(nrm) * sv
            o_ref[:, h * HEAD_DIM:(h + 1) * HEAD_DIM] = (out * mul).astype(BF)

        for h in range(d // HEAD_DIM):
            head(q_ref, qg_ref, qo_ref, h, scale)
        for h in range(kvw // HEAD_DIM):
            head(k_ref, kg_ref, ko_ref, h, 1.0)

    cw2 = d
    tab = pl.BlockSpec((ts, HEAD_DIM), lambda i: (i, 0))
    vec = pl.BlockSpec((1, HEAD_DIM), lambda i: (0, 0))
    return pl.pallas_call(
        body, name="qk_fwd", grid=(s // ts,),
        in_specs=[pl.BlockSpec((ts, d), lambda i: (i, cw2 // d)),
                  pl.BlockSpec((ts, kvw), lambda i: (i, (cw2 + d) // kvw)), tab, tab, vec, vec],
        out_specs=[pl.BlockSpec((ts, d), lambda i: (i, 0)), pl.BlockSpec((ts, kvw), lambda i: (i, 0))],
        out_shape=[jax.ShapeDtypeStruct((s, d), BF), jax.ShapeDtypeStruct((s, kvw), BF)],
        compiler_params=_params(("parallel",)),
    )(z, z, cos, sin, qg, kg)


def _qk_bwd(dqt, dkt, z, cos, sin, qg, kg, d, ts=None):
    s = z.shape[0]
    ts = ts or ROW_TILE
    kvw = d // GROUP
    scale = HEAD_DIM ** -0.5

    def body(dq_ref, dk_ref, q_ref, k_ref, c_ref, s_ref, qg_ref, kg_ref, dqo_ref, dko_ref, dqg_ref, dkg_ref):
        cv, sv = c_ref[...], s_ref[...]

        def head(dy_ref, x_ref, g_ref, o_ref, h, mul):
            dout = dy_ref[:, h * HEAD_DIM:(h + 1) * HEAD_DIM].astype(F32) * mul
            dn = dout * cv + _swap_quarters(dout * sv)
            xv = x_ref[:, h * HEAD_DIM:(h + 1) * HEAD_DIM].astype(F32)
            r = lax.rsqrt(jnp.mean(xv * xv, axis=-1, keepdims=True) + EPS)
            nh = xv * r
            dnh = dn * g_ref[...]
            o_ref[:, h * HEAD_DIM:(h + 1) * HEAD_DIM] = (r * (dnh - nh * jnp.mean(dnh * nh, axis=-1, keepdims=True))).astype(BF)
            return jnp.sum(dn * nh, axis=0, keepdims=True)

        dqg = jnp.zeros((1, HEAD_DIM), F32)
        for h in range(d // HEAD_DIM):
            dqg = dqg + head(dq_ref, q_ref, qg_ref, dqo_ref, h, scale)
        dkg = jnp.zeros((1, HEAD_DIM), F32)
        for h in range(kvw // HEAD_DIM):
            dkg = dkg + head(dk_ref, k_ref, kg_ref, dko_ref, h, LN2)

        @pl.when(pl.program_id(0) == 0)
        def _():
            dqg_ref[...] = dqg
            dkg_ref[...] = dkg

        @pl.when(pl.program_id(0) > 0)
        def _():
            dqg_ref[...] += dqg
            dkg_ref[...] += dkg

    cw2 = d
    tab = pl.BlockSpec((ts, HEAD_DIM), lambda i: (i, 0))
    vec = pl.BlockSpec((1, HEAD_DIM), lambda i: (0, 0))
    qrow = pl.BlockSpec((ts, d), lambda i: (i, 0))
    krow = pl.BlockSpec((ts, kvw), lambda i: (i, 0))
    return pl.pallas_call(
        body, name="qk_bwd", grid=(s // ts,),
        in_specs=[qrow, krow, pl.BlockSpec((ts, d), lambda i: (i, cw2 // d)),
                  pl.BlockSpec((ts, kvw), lambda i: (i, (cw2 + d) // kvw)), tab, tab, vec, vec],
        out_specs=[qrow, krow, vec, vec],
        out_shape=[jax.ShapeDtypeStruct((s, d), BF), jax.ShapeDtypeStruct((s, kvw), BF),
                   jax.ShapeDtypeStruct((1, HEAD_DIM), F32), jax.ShapeDtypeStruct((1, HEAD_DIM), F32)],
        compiler_params=_params(("arbitrary",)),
    )(dqt, dkt, z, z, cos, sin, qg, kg)


_NT = (((1,), (1,)), ((), ()))
_TN = (((0,), (0,)), ((), ()))


def _v_col_block(d):
    return (2 * d + d // GROUP) // HEAD_DIM


def _flash_fwd(qt, kt, z, d, tq=None, tk=None, comm=None):
    s = qt.shape[0]
    tq, tk = min(tq or FLASH_TQ_FWD, s), min(tk or FLASH_TK, s)
    ng, nq, nk = d // (GROUP * HEAD_DIM), s // tq, s // tk
    gw = GROUP * HEAD_DIM
    rows = GROUP * tq

    nt = tk // LANES
    assert nk % 2 == 0, (s, tk)

    def body(q_ref, k_ref, v_ref, o_ref, lse_ref, qs, v1, p_s, m_s, acc_s, sc_s):
        @pl.when(pl.program_id(1) == 0)
        def _():
            v1[:, :HEAD_DIM] = v_ref[...]
            v1[:, HEAD_DIM:] = jnp.ones((s, HEAD_DIM), BF)

        for h in range(GROUP):
            qs[h * tq:(h + 1) * tq, :] = q_ref[:, h * HEAD_DIM:(h + 1) * HEAD_DIM]
        m_s[...] = jnp.full((rows, LANES), -1e30, F32)
        acc_s[...] = jnp.zeros((rows, 2 * HEAD_DIM), F32)

        def scores(j):
            return lax.dot_general(qs[...], k_ref[pl.ds(pl.multiple_of(j * tk, tk), tk), :], _NT, preferred_element_type=F32)

        def softmax_pv(j, sc):
            kv_rows = pl.ds(pl.multiple_of(j * tk, tk), tk)
            mt = sc[:, :LANES]
            for c in range(1, nt):
                mt = jnp.maximum(mt, sc[:, c * LANES:(c + 1) * LANES])
            m_old = m_s[...]
            m_new = jnp.maximum(m_old, jnp.max(mt, axis=-1, keepdims=True))
            alpha = jnp.exp2(m_old - m_new)
            for c in range(nt):
                cs = slice(c * LANES, (c + 1) * LANES)
                p_s[:, cs] = jnp.exp2(sc[:, cs] - m_new).astype(BF)
            pv = jnp.dot(p_s[...], v1[kv_rows, :], preferred_element_type=F32)
            acc_s[:, :HEAD_DIM] = alpha * acc_s[:, :HEAD_DIM] + pv[:, :HEAD_DIM]
            acc_s[:, HEAD_DIM:] = alpha * acc_s[:, HEAD_DIM:] + pv[:, HEAD_DIM:]
            m_s[...] = m_new

        sc_s[0] = scores(0)

        def step(jj, carry):
            j = 2 * jj
            sc_s[1] = scores(j + 1)
            softmax_pv(j, sc_s[0])
            sc_s[0] = scores(jnp.minimum(j + 2, nk - 1))
            softmax_pv(j + 1, sc_s[1])
            return carry

        lax.fori_loop(0, nk // 2, step, 0)
        l = acc_s[:, HEAD_DIM:]
        o = acc_s[:, :HEAD_DIM] / l
        for h in range(GROUP):
            o_ref[:, h * HEAD_DIM:(h + 1) * HEAD_DIM] = o[h * tq:(h + 1) * tq, :].astype(BF)
        lse = m_s[...] + jnp.log2(l)
        for h in range(GROUP):
            lse_ref[h] = lse[h * tq:(h + 1) * tq, :]

    vb = _v_col_block(d)
    (o, lse), cres = _call(
        body, name="flash_fwd", grid=(ng, nq),
        in_specs=[pl.BlockSpec((tq, gw), lambda g, i: (i, g)),
                  pl.BlockSpec((s, HEAD_DIM), lambda g, i: (0, g)),
                  pl.BlockSpec((s, HEAD_DIM), lambda g, i: (0, vb + g))],
        out_specs=[pl.BlockSpec((tq, gw), lambda g, i: (i, g)),
                   pl.BlockSpec((GROUP, tq, LANES), lambda g, i: (g, i, 0))],
        out_shape=[jax.ShapeDtypeStruct((s, d), BF), jax.ShapeDtypeStruct((ng * GROUP, s, LANES), F32)],
        scratch_shapes=[pltpu.VMEM((rows, HEAD_DIM), BF), pltpu.VMEM((s, 2 * HEAD_DIM), BF), pltpu.VMEM((rows, tk), BF),
                        pltpu.VMEM((rows, LANES), F32), pltpu.VMEM((rows, 2 * HEAD_DIM), F32), pltpu.VMEM((2, rows, tk), F32)],
        semantics=("parallel", "arbitrary"), args=[qt, kt, z], comm=comm)
    return o, lse, cres


def _flash_bwd(qt, kt, z, o, do, lse, d, tq=None, tk=None, comm=None):
    s = qt.shape[0]
    tq, tk = min(tq or FLASH_TQ_BWD, s), min(tk or FLASH_TK, s)
    ng, nq, nk = d // (GROUP * HEAD_DIM), s // tq, s // tk
    gw = GROUP * HEAD_DIM
    rows = GROUP * tq

    nt = tk // LANES

    def body(q_ref, k_ref, v_ref, o_ref, do_ref, lse_ref, dq_ref, dk_ref, dv_ref, qs, dos, delta_s, dq_s, p_s, ds_s, lse_s):
        i = pl.program_id(1)
        for h in range(GROUP):
            cols = slice(h * HEAD_DIM, (h + 1) * HEAD_DIM)
            lse_s[h * tq:(h + 1) * tq, :] = lse_ref[h]
            qs[h * tq:(h + 1) * tq, :] = q_ref[:, cols]
            dov = do_ref[:, cols]
            dos[h * tq:(h + 1) * tq, :] = dov
            delta = jnp.sum(dov.astype(F32) * o_ref[:, cols].astype(F32), axis=-1, keepdims=True)
            delta_s[h * tq:(h + 1) * tq, :] = jnp.broadcast_to(delta, (tq, LANES))
        dq_s[...] = jnp.zeros((rows, HEAD_DIM), F32)

        @pl.when(i == 0)
        def _():
            dk_ref[...] = jnp.zeros_like(dk_ref)
            dv_ref[...] = jnp.zeros_like(dv_ref)

        def step(j, carry):
            kv_rows = pl.ds(pl.multiple_of(j * tk, tk), tk)
            kv, vv = k_ref[kv_rows, :], v_ref[kv_rows, :]
            sc = lax.dot_general(qs[...], kv, _NT, preferred_element_type=F32)
            dp = lax.dot_general(dos[...], vv, _NT, preferred_element_type=F32)
            lse, delta = lse_s[...], delta_s[...]
            for c in range(nt):
                cs = slice(c * LANES, (c + 1) * LANES)
                p = jnp.exp2(sc[:, cs] - lse)
                p_s[:, cs] = p.astype(BF)
                ds_s[:, cs] = (p * (dp[:, cs] - delta)).astype(BF)
            dv_ref[kv_rows, :] += lax.dot_general(p_s[...], dos[...], _TN, preferred_element_type=F32)
            dk_ref[kv_rows, :] += lax.dot_general(ds_s[...], qs[...], _TN, preferred_element_type=F32)
            dq_s[...] += jnp.dot(ds_s[...], kv, preferred_element_type=F32)
            return carry

        lax.fori_loop(0, nk, step, 0)
        for h in range(GROUP):
            dq_ref[:, h * HEAD_DIM:(h + 1) * HEAD_DIM] = dq_s[h * tq:(h + 1) * tq, :].astype(BF)

    vb = _v_col_block(d)
    qspec = pl.BlockSpec((tq, gw), lambda g, i: (i, g))
    kspec = pl.BlockSpec((s, HEAD_DIM), lambda g, i: (0, g))
    (dq, dk, dv), cres = _call(
        body, name="flash_bwd", grid=(ng, nq),
        in_specs=[qspec, kspec, pl.BlockSpec((s, HEAD_DIM), lambda g, i: (0, vb + g)), qspec, qspec,
                  pl.BlockSpec((GROUP, tq, LANES), lambda g, i: (g, i, 0))],
        out_specs=[qspec, kspec, kspec],
        out_shape=[jax.ShapeDtypeStruct((s, d), BF), jax.ShapeDtypeStruct((s, d // GROUP), F32),
                   jax.ShapeDtypeStruct((s, d // GROUP), F32)],
        scratch_shapes=[pltpu.VMEM((rows, HEAD_DIM), BF), pltpu.VMEM((rows, HEAD_DIM), BF), pltpu.VMEM((rows, LANES), F32),
                        pltpu.VMEM((rows, HEAD_DIM), F32), pltpu.VMEM((rows, tk), BF), pltpu.VMEM((rows, tk), BF),
                        pltpu.VMEM((rows, LANES), F32)],
        semantics=("parallel", "arbitrary"), args=[qt, kt, z, o, do, lse], comm=comm)
    return dq, dk, dv, cres


def _place():
    x, y, c = lax.axis_index("x"), lax.axis_index("y"), lax.axis_index("c")
    other_chips = [(1 - x, y), (x, 1 - y), (1 - x, 1 - y)]
    return x, y, c, other_chips


def _cast_place(name, w, chip_arr, tr=256):
    r, cc = w.shape
    tr = min(tr, r)

    def body(p_ref, w_ref, o_ref):
        o_ref[...] = w_ref[...].astype(BF)

    return pl.pallas_call(
        body, name=name,
        grid_spec=pltpu.PrefetchScalarGridSpec(
            num_scalar_prefetch=1, grid=(r // tr,),
            in_specs=[pl.BlockSpec((tr, cc), lambda i, p_ref: (i, 0))],
            out_specs=pl.BlockSpec((None, tr, cc), lambda i, p_ref: (p_ref[0], i, 0))),
        out_shape=jax.ShapeDtypeStruct((N_CHIPS, r, cc), BF),
        compiler_params=_params(("parallel",)),
    )(chip_arr, w)


def _gather_comm(bufs, short_host=False):
    n = len(bufs)
    pairs = [(w, j) for w in range(n) for j in range(N_CHIPS - 1)]

    def copies(dst, sems):
        send, recv, fsend, frecv = sems
        x, y, c, chips = _place()

        def part(w, chip, core_half):
            h = bufs[w].shape[1] // 2
            return dst[w].at[2 * chip[0] + chip[1], pl.ds(core_half * h, h)]

        def ici(w, j, incoming):
            slab = part(w, chips[j] if incoming else (x, y), c)
            return pltpu.make_async_remote_copy(
                src_ref=slab, dst_ref=slab, send_sem=send.at[3 * w + j], recv_sem=recv.at[3 * w + j],
                device_id=(*chips[j], c), device_id_type=MESH)

        def d2d(w, j, incoming):
            slab = part(w, chips[j], 1 - c if incoming else c)
            return pltpu.make_async_remote_copy(
                src_ref=slab, dst_ref=slab, send_sem=fsend.at[3 * w + j], recv_sem=frecv.at[3 * w + j],
                device_id=(x, y, 1 - c), device_id_type=MESH)

        return ici, d2d

    def first(_, dst, sems):
        ici, _d = copies(dst, sems)
        for w, j in pairs:
            ici(w, j, False).start()

    def middle(_, dst, sems):
        ici, d2d = copies(dst, sems)
        for w, j in pairs:
            ici(w, j, True).wait_recv()
            d2d(w, j, False).start()

    def last(_, dst, sems):
        ici, d2d = copies(dst, sems)
        for w, j in pairs:
            d2d(w, j, True).wait_recv()
        for w, j in pairs:
            ici(w, j, False).wait_send()
            d2d(w, j, False).wait_send()

    def middle_and_last(src, dst, sems):
        middle(src, dst, sems)
        last(src, dst, sems)

    phases = (first, None, middle_and_last) if short_host else (first, middle, last)
    return _Comm(arrays=list(bufs), out_shapes=[jax.ShapeDtypeStruct(b.shape, b.dtype) for b in bufs],
                 aliases={w: w for w in range(n)}, sems=[pltpu.SemaphoreType.DMA((3 * n,))] * 4, phases=phases)


def _run_comm(name, comm):
    nci, nco = len(comm.arrays), len(comm.out_shapes)

    def body(*refs):
        cin, cout, sems = refs[:nci], refs[nci:nci + nco], refs[nci + nco:]
        for fn in comm.phases:
            if fn is not None:
                fn(cin, cout, sems)

    return pl.pallas_call(
        body, name=name, in_specs=[ANY] * nci, out_specs=[ANY] * nco, out_shape=list(comm.out_shapes),
        input_output_aliases=dict(comm.aliases), scratch_shapes=list(comm.sems),
    )(*comm.arrays)


def _pair_comm(grads):
    n = len(grads)

    def copies(src, dst, sems):
        send, recv = sems
        x, y, c, _ = _place()
        out = []
        for w in range(n):
            h = grads[w].shape[1] // 2
            out.append(pltpu.make_async_remote_copy(
                src_ref=src[w].at[:, pl.ds((1 - c) * h, h), :], dst_ref=dst[w],
                send_sem=send.at[w], recv_sem=recv.at[w], device_id=(x, y, 1 - c), device_id_type=MESH))
        return out

    def first(src, dst, sems):
        for cp in copies(src, dst, sems):
            cp.start()

    def last(src, dst, sems):
        for cp in copies(src, dst, sems):
            cp.wait()

    return _Comm(arrays=list(grads),
                 out_shapes=[jax.ShapeDtypeStruct((N_CHIPS, g.shape[1] // 2, g.shape[2]), g.dtype) for g in grads],
                 aliases={}, sems=[pltpu.SemaphoreType.DMA((n,))] * 2, phases=(first, None, last))


def _pair_sum(name, own, got, c_arr, tr=256):
    nc, r, cc = own.shape
    h = r // 2
    tr = min(tr, h)
    nb = h // tr

    def body(c_ref, a_ref, b_ref, o_ref):
        o_ref[...] = (a_ref[...].astype(F32) + b_ref[...].astype(F32)).astype(BF)

    return pl.pallas_call(
        body, name=name,
        grid_spec=pltpu.PrefetchScalarGridSpec(
            num_scalar_prefetch=1, grid=(nc, nb),
            in_specs=[pl.BlockSpec((None, tr, cc), lambda s, i, c_ref: (s, c_ref[0] * nb + i, 0)),
                      pl.BlockSpec((None, tr, cc), lambda s, i, c_ref: (s, i, 0))],
            out_specs=pl.BlockSpec((None, tr, cc), lambda s, i, c_ref: (s, i, 0))),
        out_shape=jax.ShapeDtypeStruct((nc, h, cc), BF),
        compiler_params=_params(("parallel", "parallel")),
    )(c_arr, own, got)


def _chip_comm(parts):
    n = len(parts)

    def copies(src, dst, sems):
        send, recv = sems
        _, _, c, chips = _place()
        return [pltpu.make_async_remote_copy(
            src_ref=src[w].at[2 * chip[0] + chip[1]], dst_ref=dst[w].at[j],
            send_sem=send.at[3 * w + j], recv_sem=recv.at[3 * w + j], device_id=(*chip, c), device_id_type=MESH)
            for w in range(n) for j, chip in enumerate(chips)]

    def first(src, dst, sems):
        for cp in copies(src, dst, sems):
            cp.start()

    def last(src, dst, sems):
        for cp in copies(src, dst, sems):
            cp.wait()

    return _Comm(arrays=list(parts), out_shapes=[jax.ShapeDtypeStruct((N_CHIPS - 1,) + p.shape[1:], p.dtype) for p in parts],
                 aliases={}, sems=[pltpu.SemaphoreType.DMA((3 * n,))] * 2, phases=(first, None, last))


def _chip_sum(name, parts, got, chip_arr, c_arr, tr=256):
    _, h, cc = parts.shape
    tr = min(tr, h)
    nb = h // tr

    def body(chip_ref, c_ref, own_ref, got_ref, o_ref):
        acc = own_ref[...].astype(F32)
        for k in range(N_CHIPS - 1):
            acc = acc + got_ref[k].astype(F32)
        o_ref[...] = acc

    return pl.pallas_call(
        body, name=name,
        grid_spec=pltpu.PrefetchScalarGridSpec(
            num_scalar_prefetch=2, grid=(nb,),
            in_specs=[pl.BlockSpec((None, tr, cc), lambda i, chip_ref, c_ref: (chip_ref[0], i, 0)),
                      pl.BlockSpec((N_CHIPS - 1, tr, cc), lambda i, chip_ref, c_ref: (0, i, 0))],
            out_specs=pl.BlockSpec((tr, cc), lambda i, chip_ref, c_ref: (c_ref[0] * nb + i, 0))),
        out_shape=jax.ShapeDtypeStruct((2 * h, cc), F32),
        compiler_params=_params(("parallel",)),
    )(chip_arr, c_arr, parts, got)


def _pair_gather(bufs):
    n = len(bufs)

    def body(*refs):
        dst = refs[n:2 * n]
        send, recv = refs[2 * n:]
        x, y, c, _ = _place()

        def copy(w, core_half):
            h = bufs[w].shape[0] // 2
            rows = dst[w].at[pl.ds(core_half * h, h)]
            return pltpu.make_async_remote_copy(src_ref=rows, dst_ref=rows, send_sem=send.at[w], recv_sem=recv.at[w],
                                                device_id=(x, y, 1 - c), device_id_type=MESH)

        sends = [copy(w, c) for w in range(n)]
        for cp in sends:
            cp.start()
        for w in range(n):
            copy(w, 1 - c).wait_recv()
        for cp in sends:
            cp.wait_send()

    return pl.pallas_call(
        body, name="grad_pair_gather",
        in_specs=[ANY] * n, out_specs=[ANY] * n,
        out_shape=[jax.ShapeDtypeStruct(b.shape, b.dtype) for b in bufs],
        input_output_aliases={w: w for w in range(n)},
        scratch_shapes=[pltpu.SemaphoreType.DMA((n,))] * 2,
    )(*bufs)


def _all_sum_small(name, v):
    p = v.shape[0]

    def body(v_ref, o_ref, slots, send, recv):
        x, y, c, _ = _place()
        me = 4 * x + 2 * y + c
        copies = []
        for k in range(1, N_DEV):
            peer = (x ^ (k >> 2), y ^ ((k >> 1) & 1), c ^ (k & 1))
            copies.append(pltpu.make_async_remote_copy(
                src_ref=v_ref, dst_ref=slots.at[me], send_sem=send.at[k - 1], recv_sem=recv.at[k - 1],
                device_id=peer, device_id_type=MESH))
        for cp in copies:
            cp.start()
        slots[me] = v_ref[...]
        for cp in copies:
            cp.wait()
        acc = slots[0]
        for s in range(1, N_DEV):
            acc = acc + slots[s]
        o_ref[...] = acc

    vm = pl.BlockSpec(memory_space=pltpu.VMEM)
    return pl.pallas_call(
        body, name=name,
        in_specs=[vm], out_specs=vm,
        out_shape=jax.ShapeDtypeStruct(v.shape, F32),
        scratch_shapes=[pltpu.VMEM((N_DEV, p, LANES), F32), pltpu.SemaphoreType.DMA((N_DEV - 1,)),
                        pltpu.SemaphoreType.DMA((N_DEV - 1,))],
    )(v)


def _adamw(name, w, g, m, v, tr=256):
    r, c = w.shape
    tr = min(tr, r)
    assert r % tr == 0
    bc1 = 1.0 - ADAM_B1 ** ADAM_STEP
    bc2 = 1.0 - ADAM_B2 ** ADAM_STEP

    def body(w_ref, g_ref, m_ref, v_ref, d_ref, nm_ref, nv_ref):
        gv = g_ref[...]
        nm = ADAM_B1 * m_ref[...] + (1.0 - ADAM_B1) * gv
        nv = ADAM_B2 * v_ref[...] + (1.0 - ADAM_B2) * (gv * gv)
        nm_ref[...] = nm
        nv_ref[...] = nv
        d_ref[...] = -ADAM_LR * ((nm / bc1) / (jnp.sqrt(nv / bc2) + ADAM_EPS) + ADAM_WD * w_ref[...])

    blk = pl.BlockSpec((tr, c), lambda i: (i, 0))
    return pl.pallas_call(
        body, name=name, grid=(r // tr,),
        in_specs=[blk] * 4, out_specs=[blk] * 3,
        out_shape=[jax.ShapeDtypeStruct((r, c), F32)] * 3,
        compiler_params=_params(("parallel",)),
    )(w, g, m, v)


def _pack_small(parts):
    flat = jnp.concatenate([a.reshape(-1) for a in parts])
    n = flat.shape[0]
    p = -(-n // (8 * LANES)) * 8
    packed = jnp.pad(flat, (0, p * LANES - n)).reshape(p, LANES)

    def unpack(q):
        out, off = [], 0
        f = q.reshape(-1)
        for a in parts:
            out.append(f[off:off + a.size].reshape(a.shape))
            off += a.size
        return out

    return packed, unpack


def kernel(x, p, norm_mix, w_in, w_dw, conv_ln_g, conv_ln_b, w_conv_proj, q_norm, k_norm, w_attn_proj, w_out, norm_ffn, w_ff1, w_ff2, norm_ple, w_ple_gate, w_ple_proj, norm_final, loss_target, m_norm_mix, m_w_in, m_w_dw, m_conv_ln_g, m_conv_ln_b, m_w_conv_proj, m_q_norm, m_k_norm, m_w_attn_proj, m_w_out, m_norm_ffn, m_w_ff1, m_w_ff2, m_norm_ple, m_w_ple_gate, m_w_ple_proj, m_norm_final, v_norm_mix, v_w_in, v_w_dw, v_conv_ln_g, v_conv_ln_b, v_w_conv_proj, v_q_norm, v_k_norm, v_w_attn_proj, v_w_out, v_norm_ffn, v_w_ff1, v_w_ff2, v_norm_ple, v_w_ple_gate, v_w_ple_proj, v_norm_final):
    s, d = x.shape[1], x.shape[2]
    cw = d // 2
    kvw = d // GROUP
    xs, ps, tgt = x[0], p[0, 0], loss_target[0]
    cx, cy, cc = lax.axis_index("x"), lax.axis_index("y"), lax.axis_index("c")
    chip = 2 * cx + cy
    c_arr = jnp.reshape(cc, (1,)).astype(jnp.int32)
    tm, tme = min(MM_TM, s), min(MM_TM_EPI, s)

    names = ["w_in", "w_conv_proj", "w_attn_proj", "w_out", "w_ff1", "w_ff2", "w_ple_gate", "w_ple_proj"]
    big = [w_in, w_conv_proj, w_attn_proj, w_out, w_ff1, w_ff2, w_ple_gate, w_ple_proj]
    chip_arr = jnp.reshape(chip, (1,)).astype(jnp.int32)
    placed = [_cast_place("cast_" + nm, w[0], chip_arr) for nm, w in zip(names, big)]
    h0, (win,) = _rms_fwd("rms_mix", xs, norm_mix, comm=_gather_comm(placed[:1], short_host=True))
    cpc = cw // N_CHIPS
    taps = jnp.zeros((32, N_CHIPS, cpc), F32).at[:CONV_KERNEL].set(
        jnp.where(lax.broadcasted_iota(jnp.int32, (1, N_CHIPS, 1), 1) == chip, w_dw[0][:, None, :], 0.0))
    taps = jnp.where(cc == 0, taps, 0.0).reshape(32 * cw // LANES, LANES)
    wdw = _all_sum_small("gather_taps", taps).reshape(32, cw)

    cos, sin = _rope_tables(s)
    (z,) = _mm("z_proj", h0, win, b_cm=True, tm=tm, tn=win.shape[2] // 3, tk=d)
    uc, act = _conv_fwd(z, wdw, conv_ln_g, conv_ln_b, cw)
    qt, kt = _qk_fwd(z, cos, sin, q_norm, k_norm, d)
    o, lse, (wcp, wap, wout, w1, w2, wpg, wple) = _flash_fwd(qt, kt, z, d, comm=_gather_comm(placed[1:]))
    wap, wout, w2, wpg = (t.reshape(-1, t.shape[-1]) for t in (wap, wout, w2, wpg))
    (y_c,) = _mm("conv_proj", act, wcp, b_cm=True, tm=tm, tn=wcp.shape[2], tk=cw, out_dtypes=(F32,))
    tn = d // 2
    gcb = (2 * d + 2 * kvw) // tn

    def merge_epi(acc, yc, gc, ga):
        return acc, _sigmoid(gc.astype(F32)) * yc + _sigmoid(ga.astype(F32)) * acc

    y_a, merged = _mm("attn_proj", o, wap, tm=tme, tn=tn, tk=d, epi=merge_epi, out_dtypes=(BF, BF), b_resident=True,
                      extras=[_tile_extra(y_c, tme, tn), _tile_extra(z, tme, tn, gcb), _tile_extra(z, tme, tn, gcb + 2)])
    (x1,) = _mm("out_proj", merged, wout, tm=tm, tn=tn, tk=d, epi=lambda acc, r: (r + acc,), out_dtypes=(F32,),
                extras=[_tile_extra(xs, tm, tn)])
    h1 = _rms_fwd("rms_ffn", x1, norm_ffn)
    (a,) = _mm("ff1", h1, w1, b_cm=True, tm=tm, tn=tn, tk=d)

    def relu2(t):
        return jnp.square(jnp.maximum(t, 0.0))

    (x2,) = _mm("ff2", a, w2, tm=tm, tn=tn, tk=d, a_fn=relu2, epi=lambda acc, r: (r + acc,), out_dtypes=(F32,),
                extras=[_tile_extra(x1, tm, tn)])
    h2 = _rms_fwd("rms_ple", x2, norm_ple)
    to_bf = lambda t: t.astype(BF)
    (e,) = _mm("ple_proj", ps, wple, b_cm=True, tm=tm, tn=wple.shape[2], tk=ps.shape[1], a_fn=to_bf)

    def ple_epi(acc, ev, r):
        gt = _sigmoid(acc)
        return r + gt * ev.astype(F32), gt

    x3, gate = _mm("ple_gate", h2, wpg, tm=tme, tn=tn, tk=d, epi=ple_epi, out_dtypes=(F32, BF), b_resident=True,
                   extras=[_tile_extra(e, tme, tn), _tile_extra(x2, tme, tn)])

    dx3, de, dgp, sq, d_fin = _loss_bwd(x3, tgt, norm_final.reshape(1, d), e, gate)
    tkt = min(2048, s)
    (g_wple,) = _mm("d_wple", ps, de, ta=True, out_cm=True, tm=ps.shape[1], tn=wple.shape[2], tk=tkt, a_fn=to_bf)
    (g_wpg,) = _mm("d_wpg", h2, dgp, ta=True, tm=tm, tn=tn, tk=tkt)
    (dh2,) = _mm("d_h2", dgp, wpg, tb=True, tm=tm, tn=tn, tk=d)
    dx2, dx2b, d_ple = _rms_bwd("rms_ple_bwd", dh2, x2, norm_ple, dx3)

    (da,) = _mm("d_a", dx2b, w2, tb=True, tm=tm, tn=tn, tk=d, out_dtypes=(BF,),
                epi=lambda acc, av: (acc * (2.0 * jnp.maximum(av.astype(F32), 0.0)),), extras=[_tile_extra(a, tm, tn)])
    (g_w2,) = _mm("d_w2", a, dx2b, ta=True, tm=tm, tn=tn, tk=tkt, a_fn=relu2)
    (g_w1,) = _mm("d_w1", h1, da, ta=True, out_cm=True, tm=tm, tn=tn, tk=tkt)
    (dh1,) = _mm("d_h1", da, w1, tb=True, b_cm=True, tm=tm, tn=tn, tk=w1.shape[2])
    dx1, dx1b, d_ffn = _rms_bwd("rms_ffn_bwd", dh1, x1, norm_ffn, dx2)

    def merge_bwd(acc, gc, ga, yc, ya):
        sc, sa = _sigmoid(gc.astype(F32)), _sigmoid(ga.astype(F32))
        return acc * sc, acc * sa, acc * yc * sc * (1.0 - sc), acc * ya.astype(F32) * sa * (1.0 - sa)

    dy_c, dy_a, dg_c, dg_a = _mm(
        "d_merged", dx1b, wout, tb=True, tm=tme, tn=tn, tk=d, epi=merge_bwd, out_dtypes=(BF, BF, BF, BF), b_resident=True,
        extras=[_tile_extra(z, tme, tn, gcb), _tile_extra(z, tme, tn, gcb + 2), _tile_extra(y_c, tme, tn),
                _tile_extra(y_a, tme, tn)])
    (g_wout,) = _mm("d_wout", merged, dx1b, ta=True, tm=tm, tn=tn, tk=tkt)
    (g_wap,) = _mm("d_wap", o, dy_a, ta=True, tm=tm, tn=tn, tk=tkt)

    def slabs(g):
        return g if g.ndim == 3 else g.reshape(N_CHIPS, g.shape[0] // N_CHIPS, g.shape[1])

    grads_a = [slabs(g) for g in (g_wap, g_wout, g_w1, g_w2, g_wpg, g_wple)]
    (do,), got_a = _mm("d_o", dy_a, wap, tb=True, tm=tm, tn=tn, tk=d, comm=_pair_comm(grads_a))
    parts_a = [_pair_sum("pair_sum_" + nm, g, r, c_arr) for nm, g, r in zip(names[2:], grads_a, got_a)]
    dqt, dkt, dv, _ = _flash_bwd(qt, kt, z, o, do, lse, d)
    dq, dk, d_qn, d_kn = _qk_bwd(dqt, dkt, z, cos, sin, q_norm, k_norm, d)
    (g_wcp,) = _mm("d_wcp", act, dy_c, ta=True, out_cm=True, tm=cw, tn=wcp.shape[2], tk=tkt)
    (dact,) = _mm("d_act", dy_c, wcp, tb=True, b_cm=True, tm=tm, tn=cw, tk=wcp.shape[2])
    p_wap, p_wout, p_w1, p_w2, p_wpg, p_wple = parts_a
    dcab, d_taps, d_lng, d_lnb, (s_wap, s_wout, s_wpg, s_wple) = _conv_bwd(
        dact, uc, z, wdw, conv_ln_g, conv_ln_b, cw, comm=_chip_comm([p_wap, p_wout, p_wpg, p_wple]))
    dz = jnp.concatenate([dcab, dq, dk, dv.astype(BF), dg_c, dg_a], axis=1)
    (g_win,), (s_w1, s_w2) = _mm("d_win", h0, dz, ta=True, out_cm=True, tm=tm, tn=win.shape[2] // 3, tk=tkt,
                                 comm=_chip_comm([p_w1, p_w2]))
    slots_a = [s_wap, s_wout, s_w1, s_w2, s_wpg, s_wple]
    grads_b = [slabs(g_win), slabs(g_wcp)]
    got_b = _run_comm("grad_pair_exchange_b", _pair_comm(grads_b))
    parts_b = [_pair_sum("pair_sum_" + nm, g, r, c_arr) for nm, g, r in zip(names[:2], grads_b, got_b)]
    (dh0,), slots_b = _mm("d_h0", dz, win, tb=True, b_cm=True, tm=tm, tn=tn, tk=win.shape[2], comm=_chip_comm(parts_b))
    dx, _, d_mix = _rms_bwd("rms_mix_bwd", dh0, xs, norm_mix, dx1)
    big_grads = _pair_gather(
        [_chip_sum("chip_sum_" + nm, cp, sl, chip_arr, c_arr)
         for nm, cp, sl in zip(names, parts_b + parts_a, list(slots_b) + list(slots_a))])

    small = [d_mix, d_taps[:CONV_KERNEL], d_lng, d_lnb, d_qn, d_kn, d_ffn, d_ple, d_fin]
    packed, unpack = _pack_small(small)
    g_mix, g_taps, g_lng, g_lnb, g_qn, g_kn, g_ffn, g_ple, g_fin = unpack(_all_sum_small("reduce_small", packed))
    g_dw = lax.dynamic_slice_in_dim(g_taps.reshape(CONV_KERNEL, N_CHIPS, cpc), chip, 1, axis=1).reshape(1, CONV_KERNEL, cpc)

    sq_local = lax.reduce_precision(sq[0, 0], 8, 23)
    loss = (0.5 / d) * lax.psum(sq_local, ("x", "y", "c"))

    grads = {
        "norm_mix": g_mix, "w_in": big_grads[0][None], "w_dw": g_dw, "conv_ln_g": g_lng, "conv_ln_b": g_lnb,
        "w_conv_proj": big_grads[1][None], "q_norm": g_qn, "k_norm": g_kn, "w_attn_proj": big_grads[2][None],
        "w_out": big_grads[3][None], "norm_ffn": g_ffn, "w_ff1": big_grads[4][None], "w_ff2": big_grads[5][None],
        "norm_ple": g_ple, "w_ple_gate": big_grads[6][None], "w_ple_proj": big_grads[7][None],
        "norm_final": g_fin.reshape(d),
    }
    weights = dict(norm_mix=norm_mix, w_in=w_in, w_dw=w_dw, conv_ln_g=conv_ln_g, conv_ln_b=conv_ln_b, w_conv_proj=w_conv_proj,
                   q_norm=q_norm, k_norm=k_norm, w_attn_proj=w_attn_proj, w_out=w_out, norm_ffn=norm_ffn, w_ff1=w_ff1,
                   w_ff2=w_ff2, norm_ple=norm_ple, w_ple_gate=w_ple_gate, w_ple_proj=w_ple_proj, norm_final=norm_final)
    m_in = dict(norm_mix=m_norm_mix, w_in=m_w_in, w_dw=m_w_dw, conv_ln_g=m_conv_ln_g, conv_ln_b=m_conv_ln_b,
                w_conv_proj=m_w_conv_proj, q_norm=m_q_norm, k_norm=m_k_norm, w_attn_proj=m_w_attn_proj, w_out=m_w_out,
                norm_ffn=m_norm_ffn, w_ff1=m_w_ff1, w_ff2=m_w_ff2, norm_ple=m_norm_ple, w_ple_gate=m_w_ple_gate,
                w_ple_proj=m_w_ple_proj, norm_final=m_norm_final)
    v_in = dict(norm_mix=v_norm_mix, w_in=v_w_in, w_dw=v_w_dw, conv_ln_g=v_conv_ln_g, conv_ln_b=v_conv_ln_b,
                w_conv_proj=v_w_conv_proj, q_norm=v_q_norm, k_norm=v_k_norm, w_attn_proj=v_w_attn_proj, w_out=v_w_out,
                norm_ffn=v_norm_ffn, w_ff1=v_w_ff1, w_ff2=v_w_ff2, norm_ple=v_norm_ple, w_ple_gate=v_w_ple_gate,
                w_ple_proj=v_w_ple_proj, norm_final=v_norm_final)
    order = list(weights)
    deltas, new_m, new_v, g_out = [], [], [], []
    for nm in order:
        w = weights[nm]
        shape = w.shape
        two_d = (-1, shape[-1])
        dl, mm_, vv_ = _adamw("adamw_" + nm, w.reshape(two_d), grads[nm].reshape(two_d), m_in[nm].reshape(two_d),
                              v_in[nm].reshape(two_d))
        g_out.append(grads[nm].reshape(shape))
        deltas.append(dl.reshape(shape))
        new_m.append(mm_.reshape(shape))
        new_v.append(vv_.reshape(shape))
    return (loss, dx[None], *g_out, *deltas, *new_m, *new_v)
```

```python
from typing import NamedTuple

import jax
import jax.numpy as jnp
from jax import lax
from jax.experimental import pallas as pl
from jax.experimental.pallas import tpu as pltpu

F32 = jnp.float32
BF = jnp.bfloat16

EPS = 1e-6
HEAD_DIM = 128
GROUP = 4
GRID_W = 64
ROPE_THETA = 10000.0
CONV_KERNEL = 31
HALO = 16
N_CHIPS = 4
N_DEV = 8
LANES = 128

ADAM_LR = 0.001
ADAM_B1 = 0.9
ADAM_B2 = 0.999
ADAM_EPS = 1e-08
ADAM_WD = 0.01
ADAM_STEP = 10

VMEM_LIMIT = 56 * 2 ** 20
LOG2E = 1.4426950408889634
LN2 = 0.6931471805599453
Q_SCALE = HEAD_DIM ** -0.5 * LOG2E
ROW_TILE = 256
FLASH_TQ_FWD = 512
FLASH_TQ_BWD = 512
FLASH_TK = 512
MM_TM = 1024
MM_TM_EPI = 512
MM_TM_DZ = 256
MESH = pl.DeviceIdType.MESH
ANY = pl.BlockSpec(memory_space=pl.ANY)


def _params(sem):
    return pltpu.CompilerParams(dimension_semantics=sem, vmem_limit_bytes=VMEM_LIMIT)


def _sigmoid(x):
    return 1.0 / (1.0 + jnp.exp(-x))


class _Comm(NamedTuple):
    arrays: list
    out_shapes: list
    aliases: dict
    sems: list
    phases: tuple


def _call(body, *, name, grid, in_specs, out_specs, out_shape, scratch_shapes, semantics, args, comm=None, aliases=None):
    n_in, n_out = len(in_specs), len(out_specs)
    aliases = dict(aliases or {})
    if comm is None:
        res = pl.pallas_call(body, name=name, grid=grid, in_specs=in_specs, out_specs=out_specs, out_shape=out_shape,
                             scratch_shapes=scratch_shapes, input_output_aliases=aliases,
                             compiler_params=_params(semantics))(*args)
        return res, []
    nci, nco, ncs = len(comm.arrays), len(comm.out_shapes), len(comm.sems)
    n_steps = 1
    for g in grid:
        n_steps *= g
    first, middle, last = comm.phases

    def hosted(*refs):
        ins, cin = refs[:n_in], refs[n_in:n_in + nci]
        outs = refs[n_in + nci:n_in + nci + n_out]
        cout = refs[n_in + nci + n_out:n_in + nci + n_out + nco]
        rest = refs[n_in + nci + n_out + nco:]
        scratch, sems = rest[:len(rest) - ncs], rest[len(rest) - ncs:]
        step = 0
        for ax, g in enumerate(grid):
            step = step * g + pl.program_id(ax)
        for at, fn in ((0, first), (n_steps // 2, middle)):
            if fn is not None:
                pl.when(step == at)(lambda fn=fn: fn(cin, cout, sems))
        body(*ins, *outs, *scratch)
        if last is not None:
            pl.when(step == n_steps - 1)(lambda: last(cin, cout, sems))

    res = pl.pallas_call(
        hosted, name=name, grid=grid,
        in_specs=list(in_specs) + [ANY] * nci, out_specs=list(out_specs) + [ANY] * nco,
        out_shape=list(out_shape) + list(comm.out_shapes),
        input_output_aliases={**aliases, **{n_in + a: n_out + b for a, b in comm.aliases.items()}},
        scratch_shapes=list(scratch_shapes) + list(comm.sems),
        compiler_params=_params(("arbitrary",) * len(grid)),
    )(*args, *comm.arrays)
    return res[:n_out], res[n_out:]


def _mm(name, a, b, *, tm, tn, tk, ta=False, tb=False, b_cm=False, out_cm=False,
        a_fn=None, extras=(), epi=None, out_dtypes=(BF,), epi_rows=256, comm=None, b_resident=False,
        out_overrides=None):
    if ta:
        kc, m = a.shape
    else:
        m, kc = a.shape
    if b_cm:
        nc, r, c = b.shape
        n, per = (r, c) if tb else (nc * c, c)
    else:
        n = b.shape[0] if tb else b.shape[1]
    tm, tn, tk = min(tm, m), min(tn, n), min(tk, kc)
    assert m % tm == 0 and n % tn == 0 and kc % tk == 0, (name, m, n, kc, tm, tn, tk)
    nk = kc // tk
    a_spec = pl.BlockSpec((tk, tm), lambda i, j, k: (k, i)) if ta else pl.BlockSpec((tm, tk), lambda i, j, k: (i, k))
    if b_cm and not tb:
        assert per % tn == 0
        npj = per // tn
        b_spec = pl.BlockSpec((None, tk, tn), lambda i, j, k: (j // npj, k, j % npj))
    elif b_cm:
        assert per % tk == 0
        npk = per // tk
        b_spec = pl.BlockSpec((None, tn, tk), lambda i, j, k: (k // npk, j, k % npk))
    elif b_resident:
        assert nk == 1
        b_spec = pl.BlockSpec(b.shape, lambda i, j, k: (0, 0))
    elif tb:
        b_spec = pl.BlockSpec((tn, tk), lambda i, j, k: (j, k))
    else:
        b_spec = pl.BlockSpec((tk, tn), lambda i, j, k: (k, j))
    if out_cm:
        assert (n // N_CHIPS) % tn == 0
        npo = (n // N_CHIPS) // tn
        o_spec = pl.BlockSpec((None, tm, tn), lambda i, j, k: (j // npo, i, j % npo))
        o_shape = (N_CHIPS, m, n // N_CHIPS)
    else:
        o_spec = pl.BlockSpec((tm, tn), lambda i, j, k: (i, j))
        o_shape = (m, n)
    ne, no = len(extras), len(out_dtypes)
    dims = (((0 if ta else 1,), (1 if tb else 0,)), ((), ()))
    er = min(epi_rows, tm)
    chunked = nk == 1 and epi is not None and not ta
    use_acc = (nk > 1 or epi is not None) and not chunked
    assert chunked or not b_resident

    def body(*refs):
        a_ref, b_ref = refs[0], refs[1]
        ex = refs[2:2 + ne]
        outs = refs[2 + ne:2 + ne + no]
        if chunked:
            if b_resident:
                cols = pl.ds(pl.multiple_of(pl.program_id(1) * tn, tn), tn)
                bt = b_ref[cols, :] if tb else b_ref[:, cols]
            else:
                bt = b_ref[...]
            for r0 in range(0, tm, er):
                rows = slice(r0, r0 + er)
                at = a_ref[rows, :]
                if a_fn is not None:
                    at = a_fn(at)
                d = lax.dot_general(at, bt, dims, preferred_element_type=F32)
                vals = epi(d, *[e[rows, :] for e in ex])
                for o, v, dt in zip(outs, vals, out_dtypes):
                    o[rows, :] = v.astype(dt)
            return
        at = a_ref[...]
        if a_fn is not None:
            at = a_fn(at)
        d = lax.dot_general(at, b_ref[...], dims, preferred_element_type=F32)
        if not use_acc:
            outs[0][...] = d.astype(out_dtypes[0])
            return
        acc = refs[-1]
        k = pl.program_id(2)

        @pl.when(k == 0)
        def _():
            acc[...] = d

        if nk > 1:
            @pl.when(k > 0)
            def _():
                acc[...] += d

        @pl.when(k == nk - 1)
        def _():
            for r0 in range(0, tm, er):
                rows = slice(r0, r0 + er)
                if epi is None:
                    vals = (acc[rows, :],)
                else:
                    vals = epi(acc[rows, :], *[e[rows, :] for e in ex])
                for o, v, dt in zip(outs, vals, out_dtypes):
                    o[rows, :] = v.astype(dt)

    out_specs = [o_spec] * no
    out_shapes = [jax.ShapeDtypeStruct(o_shape, dt) for dt in out_dtypes]
    for idx, (shape, spec) in (out_overrides or {}).items():
        out_specs[idx], out_shapes[idx] = spec, jax.ShapeDtypeStruct(shape, out_dtypes[idx])
    res, cres = _call(
        body, name=name, grid=(m // tm, n // tn, nk),
        in_specs=[a_spec, b_spec] + [pl.BlockSpec(bs, im) for _, bs, im in extras],
        out_specs=out_specs,
        out_shape=out_shapes,
        scratch_shapes=[pltpu.VMEM((tm, tn), F32)] if use_acc else [],
        semantics=("parallel", "parallel", "arbitrary"),
        args=[a, b] + [e for e, _, _ in extras], comm=comm)
    return res if comm is None else (res, cres)


def _tile_extra(arr, tm, tn, col_block0=0):
    return (arr, (tm, tn), lambda i, j, k: (i, j + col_block0))


def _rms_fwd(name, x, g, ts=None, comm=None):
    s, d = x.shape
    ts = ts or ROW_TILE

    def body(x_ref, g_ref, h_ref):
        xv = x_ref[...]
        r = lax.rsqrt(jnp.mean(xv * xv, axis=-1, keepdims=True) + EPS)
        h_ref[...] = (xv * r * g_ref[...]).astype(BF)

    (h,), cres = _call(
        body, name=name, grid=(s // ts,),
        in_specs=[pl.BlockSpec((ts, d), lambda i: (i, 0)), pl.BlockSpec((1, d), lambda i: (0, 0))],
        out_specs=[pl.BlockSpec((ts, d), lambda i: (i, 0))],
        out_shape=[jax.ShapeDtypeStruct((s, d), BF)],
        scratch_shapes=[], semantics=("parallel",), args=[x, g], comm=comm)
    return h if comm is None else (h, cres)


def _rms_bwd(name, dh, x, g, dres, ts=None):
    s, d = x.shape
    ts = ts or ROW_TILE

    def body(dh_ref, x_ref, g_ref, dres_ref, dx_ref, dxb_ref, dg_ref):
        xv = x_ref[...]
        dhv = dh_ref[...].astype(F32)
        r = lax.rsqrt(jnp.mean(xv * xv, axis=-1, keepdims=True) + EPS)
        nrm = xv * r
        dn = dhv * g_ref[...]
        dx = dres_ref[...] + r * (dn - nrm * jnp.mean(dn * nrm, axis=-1, keepdims=True))
        dx_ref[...] = dx
        dxb_ref[...] = dx.astype(BF)
        part = jnp.sum(dhv * nrm, axis=0, keepdims=True)

        @pl.when(pl.program_id(0) == 0)
        def _():
            dg_ref[...] = part

        @pl.when(pl.program_id(0) > 0)
        def _():
            dg_ref[...] += part

    row = pl.BlockSpec((ts, d), lambda i: (i, 0))
    vec = pl.BlockSpec((1, d), lambda i: (0, 0))
    return pl.pallas_call(
        body, name=name, grid=(s // ts,),
        in_specs=[row, row, vec, row],
        out_specs=[row, row, vec],
        out_shape=[jax.ShapeDtypeStruct((s, d), F32), jax.ShapeDtypeStruct((s, d), BF), jax.ShapeDtypeStruct((1, d), F32)],
        compiler_params=_params(("arbitrary",)),
    )(dh, x, g, dres)


def _loss_bwd(x3, tgt, gfin, e, gate, ts=None):
    s, d = x3.shape
    ts = ts or ROW_TILE

    def body(x_ref, t_ref, g_ref, e_ref, gate_ref, dx_ref, de_ref, dgp_ref, sq_ref, dg_ref):
        xv = x_ref[...]
        gv = g_ref[...]
        r = lax.rsqrt(jnp.mean(xv * xv, axis=-1, keepdims=True) + EPS)
        nrm = xv * r
        err = nrm * gv - t_ref[...]
        dy = err * (1.0 / d)
        dn = dy * gv
        dx = r * (dn - nrm * jnp.mean(dn * nrm, axis=-1, keepdims=True))
        dx_ref[...] = dx
        ev = e_ref[...].astype(F32)
        gt = gate_ref[...].astype(F32)
        de_ref[...] = (dx * gt).astype(BF)
        dgp_ref[...] = (dx * ev * gt * (1.0 - gt)).astype(BF)
        sq = jnp.full((8, LANES), jnp.sum(err * err), F32)
        part = jnp.sum(dy * nrm, axis=0, keepdims=True)

        @pl.when(pl.program_id(0) == 0)
        def _():
            sq_ref[...] = sq
            dg_ref[...] = part

        @pl.when(pl.program_id(0) > 0)
        def _():
            sq_ref[...] += sq
            dg_ref[...] += part

    row = pl.BlockSpec((ts, d), lambda i: (i, 0))
    vec = pl.BlockSpec((1, d), lambda i: (0, 0))
    return pl.pallas_call(
        body, name="loss_bwd", grid=(s // ts,),
        in_specs=[row, row, vec, row, row],
        out_specs=[row, row, row, pl.BlockSpec((8, LANES), lambda i: (0, 0)), vec],
        out_shape=[jax.ShapeDtypeStruct((s, d), F32), jax.ShapeDtypeStruct((s, d), BF), jax.ShapeDtypeStruct((s, d), BF),
                   jax.ShapeDtypeStruct((8, LANES), F32), jax.ShapeDtypeStruct((1, d), F32)],
        compiler_params=_params(("arbitrary",)),
    )(x3, tgt, gfin, e, gate)


def _halo_specs(ts, s, width, col_block):
    per = ts // HALO
    last = s // HALO - 1
    return [
        pl.BlockSpec((HALO, width), lambda i: (jnp.maximum(i * per - 1, 0), col_block)),
        pl.BlockSpec((ts, width), lambda i: (i, col_block)),
        pl.BlockSpec((HALO, width), lambda i: (jnp.minimum((i + 1) * per, last), col_block)),
    ]


def _glu_ext(zp, zc, zn, ext, cw, ts, i, n_tiles):
    def glu(zr):
        zv = zr[...].astype(F32)
        return zv[:, :cw] * _sigmoid(zv[:, cw:])

    ext[0:HALO, :] = jnp.where(i > 0, glu(zp), 0.0)
    ext[HALO:HALO + ts, :] = glu(zc)
    ext[HALO + ts:, :] = jnp.where(i < n_tiles - 1, glu(zn), 0.0)


SUBLANES = 8


def _shift_scratch(ts):
    return pltpu.VMEM((SUBLANES, ts + 2 * HALO - SUBLANES, LANES), F32)


def _shifted_copies(ext, sh, cols, ts):
    n = ts + 2 * HALO - SUBLANES
    for r in range(SUBLANES):
        sh[r] = ext[r:r + n, cols]


def _tap_rows(sh, off, ts):
    q, r = divmod(off, SUBLANES)
    return sh[r, q * SUBLANES:q * SUBLANES + ts, :]


def _ln_stats(uc):
    mu = jnp.mean(uc, axis=-1, keepdims=True)
    xc = uc - mu
    rstd = lax.rsqrt(jnp.mean(xc * xc, axis=-1, keepdims=True) + EPS)
    return xc * rstd, rstd


def _conv_fwd(z, wdw, ln_g, ln_b, cw, ts=None):
    s = z.shape[0]
    ts = ts or ROW_TILE
    n_tiles = s // ts
    pad = CONV_KERNEL // 2

    def body(zp, zc, zn, w_ref, g_ref, b_ref, uc_ref, act_ref, ext, sh):
        i = pl.program_id(0)
        _glu_ext(zp, zc, zn, ext, cw, ts, i, n_tiles)

        def col_block(cb, carry):
            cols = pl.ds(pl.multiple_of(cb * LANES, LANES), LANES)
            _shifted_copies(ext, sh, cols, ts)
            acc = jnp.zeros((ts, LANES), F32)
            for j in range(CONV_KERNEL):
                acc = acc + _tap_rows(sh, HALO - pad + j, ts) * w_ref[j:j + 1, cols]
            uc_ref[:, cols] = acc
            return carry

        lax.fori_loop(0, cw // LANES, col_block, 0)
        xhat, _ = _ln_stats(uc_ref[...])
        ln = xhat * g_ref[...] + b_ref[...]
        act_ref[...] = (ln * _sigmoid(ln)).astype(BF)

    vec = pl.BlockSpec((1, cw), lambda i: (0, 0))
    row = pl.BlockSpec((ts, cw), lambda i: (i, 0))
    return pl.pallas_call(
        body, name="conv_fwd", grid=(n_tiles,),
        in_specs=_halo_specs(ts, s, 2 * cw, 0) + [pl.BlockSpec((32, cw), lambda i: (0, 0)), vec, vec],
        out_specs=[row, row],
        out_shape=[jax.ShapeDtypeStruct((s, cw), F32), jax.ShapeDtypeStruct((s, cw), BF)],
        scratch_shapes=[pltpu.VMEM((ts + 2 * HALO, cw), F32), _shift_scratch(ts)],
        compiler_params=_params(("parallel",)),
    )(z, z, z, wdw, ln_g, ln_b)


def _conv_bwd(ds, uc, z, wdw, ln_g, ln_b, cw, dz, ts=None, comm=None):
    s = z.shape[0]
    ts = ts or ROW_TILE
    n_tiles = s // ts
    pad = CONV_KERNEL // 2

    def body(zp, zc, zn, dsp, dsc, dsn, ucp, ucc, ucn, w_ref, g_ref, b_ref, _dz_in,
             dz_ref, dw_ref, dg_ref, db_ref, ext, dext, sh, dsh):
        i = pl.program_id(0)
        gv, bv = g_ref[...], b_ref[...]

        def ln_bwd(ds_r, uc_r):
            xhat, rstd = _ln_stats(uc_r[...])
            ln = xhat * gv + bv
            sg = _sigmoid(ln)
            dln = ds_r[...].astype(F32) * (sg * (1.0 + ln * (1.0 - sg)))
            dxh = dln * gv
            duc = rstd * (dxh - jnp.mean(dxh, axis=-1, keepdims=True) - xhat * jnp.mean(dxh * xhat, axis=-1, keepdims=True))
            return duc, dln, xhat

        duc_p, _, _ = ln_bwd(dsp, ucp)
        duc_c, dln_c, xhat_c = ln_bwd(dsc, ucc)
        duc_n, _, _ = ln_bwd(dsn, ucn)
        dext[0:HALO, :] = jnp.where(i > 0, duc_p, 0.0)
        dext[HALO:HALO + ts, :] = duc_c
        dext[HALO + ts:, :] = jnp.where(i < n_tiles - 1, duc_n, 0.0)
        _glu_ext(zp, zc, zn, ext, cw, ts, i, n_tiles)

        dg_part = jnp.sum(dln_c * xhat_c, axis=0, keepdims=True)
        db_part = jnp.sum(dln_c, axis=0, keepdims=True)

        @pl.when(i == 0)
        def _():
            dw_ref[...] = jnp.zeros_like(dw_ref)
            dg_ref[...] = dg_part
            db_ref[...] = db_part

        @pl.when(i > 0)
        def _():
            dg_ref[...] += dg_part
            db_ref[...] += db_part

        def col_block(cb, carry):
            c0 = pl.multiple_of(cb * LANES, LANES)
            cols, gate_cols = pl.ds(c0, LANES), pl.ds(cw + c0, LANES)
            _shifted_copies(dext, dsh, cols, ts)
            _shifted_copies(ext, sh, cols, ts)
            du = jnp.zeros((ts, LANES), F32)
            for j in range(CONV_KERNEL):
                du = du + _tap_rows(dsh, HALO + pad - j, ts) * w_ref[j:j + 1, cols]
            ca, sb = zc[:, cols].astype(F32), _sigmoid(zc[:, gate_cols].astype(F32))
            dz_ref[:, cols] = (du * sb).astype(BF)
            dz_ref[:, gate_cols] = (du * ca * sb * (1.0 - sb)).astype(BF)
            duc_blk = _tap_rows(dsh, HALO, ts)
            for j in range(CONV_KERNEL):
                dw_ref[j:j + 1, cols] += jnp.sum(_tap_rows(sh, HALO - pad + j, ts) * duc_blk, axis=0, keepdims=True)
            return carry

        lax.fori_loop(0, cw // LANES, col_block, 0)

    vec = pl.BlockSpec((1, cw), lambda i: (0, 0))
    wsp = pl.BlockSpec((32, cw), lambda i: (0, 0))
    res, cres = _call(
        body, name="conv_bwd", grid=(n_tiles,),
        in_specs=(_halo_specs(ts, s, 2 * cw, 0) + _halo_specs(ts, s, cw, 0) + _halo_specs(ts, s, cw, 0)
                  + [wsp, vec, vec, ANY]),
        out_specs=[pl.BlockSpec((ts, 2 * cw), lambda i: (i, 0)), wsp, vec, vec],
        out_shape=[jax.ShapeDtypeStruct(dz.shape, BF), jax.ShapeDtypeStruct((32, cw), F32),
                   jax.ShapeDtypeStruct((1, cw), F32), jax.ShapeDtypeStruct((1, cw), F32)],
        scratch_shapes=[pltpu.VMEM((ts + 2 * HALO, cw), F32), pltpu.VMEM((ts + 2 * HALO, cw), F32),
                        _shift_scratch(ts), _shift_scratch(ts)],
        semantics=("arbitrary",), args=[z, z, z, ds, ds, ds, uc, uc, uc, wdw, ln_g, ln_b, dz], comm=comm,
        aliases={12: 0})
    return (*res, cres)


def _rope_tables(s):
    axis_dim = HEAD_DIM // 2
    t = jnp.arange(s, dtype=jnp.int32)
    row = (t // GRID_W).astype(F32)[:, None]
    col = (t % GRID_W).astype(F32)[:, None]
    inv_freq = ROPE_THETA ** (-jnp.arange(0, axis_dim, 2, dtype=F32) / axis_dim)[None, :]
    ar, ac = row * inv_freq, col * inv_freq
    cos = jnp.concatenate([jnp.cos(ar), jnp.cos(ar), jnp.cos(ac), jnp.cos(ac)], axis=-1)
    sin = jnp.concatenate([-jnp.sin(ar), jnp.sin(ar), -jnp.sin(ac), jnp.sin(ac)], axis=-1)
    return cos, sin


def _swap_quarters(x):
    q = HEAD_DIM // 4
    lane = lax.broadcasted_iota(jnp.int32, x.shape, 1)
    return jnp.where((lane % (2 * q)) < q, pltpu.roll(x, HEAD_DIM - q, 1), pltpu.roll(x, q, 1))


def _qk_fwd(z, cos, sin, qg, kg, d, ts=None):
    s = z.shape[0]
    ts = ts or ROW_TILE
    kvw = d // GROUP
    scale = Q_SCALE

    def body(q_ref, k_ref, c_ref, s_ref, qg_ref, kg_ref, qo_ref, ko_ref):
        cv, sv = c_ref[...], s_ref[...]

        def head(x_ref, g_ref, o_ref, h, mul):
            xv = x_ref[:, h * HEAD_DIM:(h + 1) * HEAD_DIM].astype(F32)
            r = lax.rsqrt(jnp.mean(xv * xv, axis=-1, keepdims=True) + EPS)
            nrm = xv * r * g_ref[...]
            out = nrm * cv + _swap_quarters(nrm) * sv
            o_ref[:, h * HEAD_DIM:(h + 1) * HEAD_DIM] = (out * mul).astype(BF)

        for h in range(d // HEAD_DIM):
            head(q_ref, qg_ref, qo_ref, h, scale)
        for h in range(kvw // HEAD_DIM):
            head(k_ref, kg_ref, ko_ref, h, 1.0)

    cw2 = d
    tab = pl.BlockSpec((ts, HEAD_DIM), lambda i: (i, 0))
    vec = pl.BlockSpec((1, HEAD_DIM), lambda i: (0, 0))
    return pl.pallas_call(
        body, name="qk_fwd", grid=(s // ts,),
        in_specs=[pl.BlockSpec((ts, d), lambda i: (i, cw2 // d)),
                  pl.BlockSpec((ts, kvw), lambda i: (i, (cw2 + d) // kvw)), tab, tab, vec, vec],
        out_specs=[pl.BlockSpec((ts, d), lambda i: (i, 0)), pl.BlockSpec((ts, kvw), lambda i: (i, 0))],
        out_shape=[jax.ShapeDtypeStruct((s, d), BF), jax.ShapeDtypeStruct((s, kvw), BF)],
        compiler_params=_params(("parallel",)),
    )(z, z, cos, sin, qg, kg)


def _qk_bwd(dqt, dkt, z, cos, sin, qg, kg, d, dz, ts=None):
    s = z.shape[0]
    ts = ts or ROW_TILE
    kvw = d // GROUP
    scale = HEAD_DIM ** -0.5

    def body(dq_ref, dk_ref, q_ref, k_ref, c_ref, s_ref, qg_ref, kg_ref, _dz_in, dzo_ref, dqg_ref, dkg_ref):
        cv, sv = c_ref[...], s_ref[...]

        def head(dy_ref, x_ref, g_ref, col0, h, mul):
            dout = dy_ref[:, h * HEAD_DIM:(h + 1) * HEAD_DIM].astype(F32) * mul
            dn = dout * cv + _swap_quarters(dout * sv)
            xv = x_ref[:, h * HEAD_DIM:(h + 1) * HEAD_DIM].astype(F32)
            r = lax.rsqrt(jnp.mean(xv * xv, axis=-1, keepdims=True) + EPS)
            nh = xv * r
            dnh = dn * g_ref[...]
            c0 = col0 + h * HEAD_DIM
            dzo_ref[:, c0:c0 + HEAD_DIM] = (r * (dnh - nh * jnp.mean(dnh * nh, axis=-1, keepdims=True))).astype(BF)
            return jnp.sum(dn * nh, axis=0, keepdims=True)

        dqg = jnp.zeros((1, HEAD_DIM), F32)
        for h in range(d // HEAD_DIM):
            dqg = dqg + head(dq_ref, q_ref, qg_ref, 0, h, scale)
        dkg = jnp.zeros((1, HEAD_DIM), F32)
        for h in range(kvw // HEAD_DIM):
            dkg = dkg + head(dk_ref, k_ref, kg_ref, d, h, LN2)

        @pl.when(pl.program_id(0) == 0)
        def _():
            dqg_ref[...] = dqg
            dkg_ref[...] = dkg

        @pl.when(pl.program_id(0) > 0)
        def _():
            dqg_ref[...] += dqg
            dkg_ref[...] += dkg

    cw2 = d
    tab = pl.BlockSpec((ts, HEAD_DIM), lambda i: (i, 0))
    vec = pl.BlockSpec((1, HEAD_DIM), lambda i: (0, 0))
    qrow = pl.BlockSpec((ts, d), lambda i: (i, 0))
    krow = pl.BlockSpec((ts, kvw), lambda i: (i, 0))
    window = pl.BlockSpec((pl.Element(ts), pl.Element(d + kvw)), lambda i: (i * ts, cw2))
    return pl.pallas_call(
        body, name="qk_bwd", grid=(s // ts,),
        in_specs=[qrow, krow, pl.BlockSpec((ts, d), lambda i: (i, cw2 // d)),
                  pl.BlockSpec((ts, kvw), lambda i: (i, (cw2 + d) // kvw)), tab, tab, vec, vec, ANY],
        out_specs=[window, vec, vec],
        out_shape=[jax.ShapeDtypeStruct(dz.shape, BF),
                   jax.ShapeDtypeStruct((1, HEAD_DIM), F32), jax.ShapeDtypeStruct((1, HEAD_DIM), F32)],
        input_output_aliases={8: 0},
        compiler_params=_params(("arbitrary",)),
    )(dqt, dkt, z, z, cos, sin, qg, kg, dz)


_NT = (((1,), (1,)), ((), ()))
_TN = (((0,), (0,)), ((), ()))


def _v_col_block(d):
    return (2 * d + d // GROUP) // HEAD_DIM


def _flash_fwd(qt, kt, z, d, tq=None, tk=None, comm=None):
    s = qt.shape[0]
    tq, tk = min(tq or FLASH_TQ_FWD, s), min(tk or FLASH_TK, s)
    ng, nq, nk = d // (GROUP * HEAD_DIM), s // tq, s // tk
    gw = GROUP * HEAD_DIM
    rows = GROUP * tq

    nt = tk // LANES
    assert nk % 2 == 0, (s, tk)

    def body(q_ref, k_ref, v_ref, o_ref, lse_ref, qs, v1, p_s, m_s, acc_s, sc_s):
        @pl.when(pl.program_id(1) == 0)
        def _():
            v1[:, :HEAD_DIM] = v_ref[...]
            v1[:, HEAD_DIM:] = jnp.ones((s, HEAD_DIM), BF)

        for h in range(GROUP):
            qs[h * tq:(h + 1) * tq, :] = q_ref[:, h * HEAD_DIM:(h + 1) * HEAD_DIM]
        m_s[...] = jnp.full((rows, LANES), -1e30, F32)
        acc_s[...] = jnp.zeros((rows, 2 * HEAD_DIM), F32)

        def scores(j):
            return lax.dot_general(qs[...], k_ref[pl.ds(pl.multiple_of(j * tk, tk), tk), :], _NT, preferred_element_type=F32)

        def softmax_pv(j, sc):
            kv_rows = pl.ds(pl.multiple_of(j * tk, tk), tk)
            mt = sc[:, :LANES]
            for c in range(1, nt):
                mt = jnp.maximum(mt, sc[:, c * LANES:(c + 1) * LANES])
            m_old = m_s[...]
            m_new = jnp.maximum(m_old, jnp.max(mt, axis=-1, keepdims=True))
            alpha = jnp.exp2(m_old - m_new)
            for c in range(nt):
                cs = slice(c * LANES, (c + 1) * LANES)
                p_s[:, cs] = jnp.exp2(sc[:, cs] - m_new).astype(BF)
            pv = jnp.dot(p_s[...], v1[kv_rows, :], preferred_element_type=F32)
            acc_s[:, :HEAD_DIM] = alpha * acc_s[:, :HEAD_DIM] + pv[:, :HEAD_DIM]
            acc_s[:, HEAD_DIM:] = alpha * acc_s[:, HEAD_DIM:] + pv[:, HEAD_DIM:]
            m_s[...] = m_new

        sc_s[0] = scores(0)

        def step(jj, carry):
            j = 2 * jj
            sc_s[1] = scores(j + 1)
            softmax_pv(j, sc_s[0])
            sc_s[0] = scores(jnp.minimum(j + 2, nk - 1))
            softmax_pv(j + 1, sc_s[1])
            return carry

        lax.fori_loop(0, nk // 2, step, 0)
        l = acc_s[:, HEAD_DIM:]
        o = acc_s[:, :HEAD_DIM] / l
        for h in range(GROUP):
            o_ref[:, h * HEAD_DIM:(h + 1) * HEAD_DIM] = o[h * tq:(h + 1) * tq, :].astype(BF)
        lse = m_s[...] + jnp.log2(l)
        for h in range(GROUP):
            lse_ref[h] = lse[h * tq:(h + 1) * tq, :]

    vb = _v_col_block(d)
    (o, lse), cres = _call(
        body, name="flash_fwd", grid=(ng, nq),
        in_specs=[pl.BlockSpec((tq, gw), lambda g, i: (i, g)),
                  pl.BlockSpec((s, HEAD_DIM), lambda g, i: (0, g)),
                  pl.BlockSpec((s, HEAD_DIM), lambda g, i: (0, vb + g))],
        out_specs=[pl.BlockSpec((tq, gw), lambda g, i: (i, g)),
                   pl.BlockSpec((GROUP, tq, LANES), lambda g, i: (g, i, 0))],
        out_shape=[jax.ShapeDtypeStruct((s, d), BF), jax.ShapeDtypeStruct((ng * GROUP, s, LANES), F32)],
        scratch_shapes=[pltpu.VMEM((rows, HEAD_DIM), BF), pltpu.VMEM((s, 2 * HEAD_DIM), BF), pltpu.VMEM((rows, tk), BF),
                        pltpu.VMEM((rows, LANES), F32), pltpu.VMEM((rows, 2 * HEAD_DIM), F32), pltpu.VMEM((2, rows, tk), F32)],
        semantics=("parallel", "arbitrary"), args=[qt, kt, z], comm=comm)
    return o, lse, cres


def _flash_bwd(qt, kt, z, o, do, lse, d, dz, tq=None, tk=None, comm=None):
    s = qt.shape[0]
    tq, tk = min(tq or FLASH_TQ_BWD, s), min(tk or FLASH_TK, s)
    ng, nq, nk = d // (GROUP * HEAD_DIM), s // tq, s // tk
    gw = GROUP * HEAD_DIM
    rows = GROUP * tq

    nt = tk // LANES

    def body(q_ref, k_ref, v_ref, o_ref, do_ref, lse_ref, _dz_in, dq_ref, dk_ref, dzv_ref,
             qs, dos, delta_s, dq_s, p_s, ds_s, lse_s, dv_ref):
        i = pl.program_id(1)
        for h in range(GROUP):
            cols = slice(h * HEAD_DIM, (h + 1) * HEAD_DIM)
            lse_s[h * tq:(h + 1) * tq, :] = lse_ref[h]
            qs[h * tq:(h + 1) * tq, :] = q_ref[:, cols]
            dov = do_ref[:, cols]
            dos[h * tq:(h + 1) * tq, :] = dov
            delta = jnp.sum(dov.astype(F32) * o_ref[:, cols].astype(F32), axis=-1, keepdims=True)
            delta_s[h * tq:(h + 1) * tq, :] = jnp.broadcast_to(delta, (tq, LANES))
        dq_s[...] = jnp.zeros((rows, HEAD_DIM), F32)

        @pl.when(i == 0)
        def _():
            dk_ref[...] = jnp.zeros_like(dk_ref)
            dv_ref[...] = jnp.zeros_like(dv_ref)

        def step(j, carry):
            kv_rows = pl.ds(pl.multiple_of(j * tk, tk), tk)
            kv, vv = k_ref[kv_rows, :], v_ref[kv_rows, :]
            sc = lax.dot_general(qs[...], kv, _NT, preferred_element_type=F32)
            dp = lax.dot_general(dos[...], vv, _NT, preferred_element_type=F32)
            lse, delta = lse_s[...], delta_s[...]
            for c in range(nt):
                cs = slice(c * LANES, (c + 1) * LANES)
                p = jnp.exp2(sc[:, cs] - lse)
                p_s[:, cs] = p.astype(BF)
                ds_s[:, cs] = (p * (dp[:, cs] - delta)).astype(BF)
            dv_ref[kv_rows, :] += lax.dot_general(p_s[...], dos[...], _TN, preferred_element_type=F32)
            dk_ref[kv_rows, :] += lax.dot_general(ds_s[...], qs[...], _TN, preferred_element_type=F32)
            dq_s[...] += jnp.dot(ds_s[...], kv, preferred_element_type=F32)
            return carry

        lax.fori_loop(0, nk, step, 0)
        for h in range(GROUP):
            dq_ref[:, h * HEAD_DIM:(h + 1) * HEAD_DIM] = dq_s[h * tq:(h + 1) * tq, :].astype(BF)

        @pl.when(i == nq - 1)
        def _():
            dzv_ref[...] = dv_ref[...].astype(BF)

    vb = _v_col_block(d)
    qspec = pl.BlockSpec((tq, gw), lambda g, i: (i, g))
    kspec = pl.BlockSpec((s, HEAD_DIM), lambda g, i: (0, g))
    vspec = pl.BlockSpec((s, HEAD_DIM), lambda g, i: (0, vb + g))
    (dq, dk, dz), cres = _call(
        body, name="flash_bwd", grid=(ng, nq),
        in_specs=[qspec, kspec, vspec, qspec, qspec, pl.BlockSpec((GROUP, tq, LANES), lambda g, i: (g, i, 0)), ANY],
        out_specs=[qspec, kspec, vspec],
        out_shape=[jax.ShapeDtypeStruct((s, d), BF), jax.ShapeDtypeStruct((s, d // GROUP), F32),
                   jax.ShapeDtypeStruct(dz.shape, BF)],
        scratch_shapes=[pltpu.VMEM((rows, HEAD_DIM), BF), pltpu.VMEM((rows, HEAD_DIM), BF), pltpu.VMEM((rows, LANES), F32),
                        pltpu.VMEM((rows, HEAD_DIM), F32), pltpu.VMEM((rows, tk), BF), pltpu.VMEM((rows, tk), BF),
                        pltpu.VMEM((rows, LANES), F32), pltpu.VMEM((s, HEAD_DIM), F32)],
        semantics=("parallel", "arbitrary"), args=[qt, kt, z, o, do, lse, dz], comm=comm, aliases={6: 2})
    return dq, dk, dz, cres


def _place():
    x, y, c = lax.axis_index("x"), lax.axis_index("y"), lax.axis_index("c")
    other_chips = [(1 - x, y), (x, 1 - y), (1 - x, 1 - y)]
    return x, y, c, other_chips


def _cast_place(name, w, chip_arr, tr=256):
    r, cc = w.shape
    tr = min(tr, r)

    def body(p_ref, w_ref, o_ref):
        o_ref[...] = w_ref[...].astype(BF)

    return pl.pallas_call(
        body, name=name,
        grid_spec=pltpu.PrefetchScalarGridSpec(
            num_scalar_prefetch=1, grid=(r // tr,),
            in_specs=[pl.BlockSpec((tr, cc), lambda i, p_ref: (i, 0))],
            out_specs=pl.BlockSpec((None, tr, cc), lambda i, p_ref: (p_ref[0], i, 0))),
        out_shape=jax.ShapeDtypeStruct((N_CHIPS, r, cc), BF),
        compiler_params=_params(("parallel",)),
    )(chip_arr, w)


def _gather_comm(bufs, short_host=False):
    n = len(bufs)
    pairs = [(w, j) for w in range(n) for j in range(N_CHIPS - 1)]

    def copies(dst, sems):
        send, recv, fsend, frecv = sems
        x, y, c, chips = _place()

        def part(w, chip, core_half):
            h = bufs[w].shape[1] // 2
            return dst[w].at[2 * chip[0] + chip[1], pl.ds(core_half * h, h)]

        def ici(w, j, incoming):
            slab = part(w, chips[j] if incoming else (x, y), c)
            return pltpu.make_async_remote_copy(
                src_ref=slab, dst_ref=slab, send_sem=send.at[3 * w + j], recv_sem=recv.at[3 * w + j],
                device_id=(*chips[j], c), device_id_type=MESH)

        def d2d(w, j, incoming):
            slab = part(w, chips[j], 1 - c if incoming else c)
            return pltpu.make_async_remote_copy(
                src_ref=slab, dst_ref=slab, send_sem=fsend.at[3 * w + j], recv_sem=frecv.at[3 * w + j],
                device_id=(x, y, 1 - c), device_id_type=MESH)

        return ici, d2d

    def first(_, dst, sems):
        ici, _d = copies(dst, sems)
        for w, j in pairs:
            ici(w, j, False).start()

    def middle(_, dst, sems):
        ici, d2d = copies(dst, sems)
        for w, j in pairs:
            ici(w, j, True).wait_recv()
            d2d(w, j, False).start()

    def last(_, dst, sems):
        ici, d2d = copies(dst, sems)
        for w, j in pairs:
            d2d(w, j, True).wait_recv()
        for w, j in pairs:
            ici(w, j, False).wait_send()
            d2d(w, j, False).wait_send()

    def middle_and_last(src, dst, sems):
        middle(src, dst, sems)
        last(src, dst, sems)

    phases = (first, None, middle_and_last) if short_host else (first, middle, last)
    return _Comm(arrays=list(bufs), out_shapes=[jax.ShapeDtypeStruct(b.shape, b.dtype) for b in bufs],
                 aliases={w: w for w in range(n)}, sems=[pltpu.SemaphoreType.DMA((3 * n,))] * 4, phases=phases)


def _run_comm(name, comm):
    nci, nco = len(comm.arrays), len(comm.out_shapes)

    def body(*refs):
        cin, cout, sems = refs[:nci], refs[nci:nci + nco], refs[nci + nco:]
        for fn in comm.phases:
            if fn is not None:
                fn(cin, cout, sems)

    return pl.pallas_call(
        body, name=name, in_specs=[ANY] * nci, out_specs=[ANY] * nco, out_shape=list(comm.out_shapes),
        input_output_aliases=dict(comm.aliases), scratch_shapes=list(comm.sems),
    )(*comm.arrays)


def _pair_comm(grads):
    n = len(grads)

    def copies(src, dst, sems):
        send, recv = sems
        x, y, c, _ = _place()
        out = []
        for w in range(n):
            h = grads[w].shape[1] // 2
            out.append(pltpu.make_async_remote_copy(
                src_ref=src[w].at[:, pl.ds((1 - c) * h, h), :], dst_ref=dst[w],
                send_sem=send.at[w], recv_sem=recv.at[w], device_id=(x, y, 1 - c), device_id_type=MESH))
        return out

    def first(src, dst, sems):
        for cp in copies(src, dst, sems):
            cp.start()

    def last(src, dst, sems):
        for cp in copies(src, dst, sems):
            cp.wait()

    return _Comm(arrays=list(grads),
                 out_shapes=[jax.ShapeDtypeStruct((N_CHIPS, g.shape[1] // 2, g.shape[2]), g.dtype) for g in grads],
                 aliases={}, sems=[pltpu.SemaphoreType.DMA((n,))] * 2, phases=(first, None, last))


def _pair_sum(name, own, got, c_arr, tr=256):
    nc, r, cc = own.shape
    h = r // 2
    tr = min(tr, h)
    nb = h // tr

    def body(c_ref, a_ref, b_ref, o_ref):
        o_ref[...] = (a_ref[...].astype(F32) + b_ref[...].astype(F32)).astype(BF)

    return pl.pallas_call(
        body, name=name,
        grid_spec=pltpu.PrefetchScalarGridSpec(
            num_scalar_prefetch=1, grid=(nc, nb),
            in_specs=[pl.BlockSpec((None, tr, cc), lambda s, i, c_ref: (s, c_ref[0] * nb + i, 0)),
                      pl.BlockSpec((None, tr, cc), lambda s, i, c_ref: (s, i, 0))],
            out_specs=pl.BlockSpec((None, tr, cc), lambda s, i, c_ref: (s, i, 0))),
        out_shape=jax.ShapeDtypeStruct((nc, h, cc), BF),
        compiler_params=_params(("parallel", "parallel")),
    )(c_arr, own, got)


def _chip_comm(parts):
    n = len(parts)

    def copies(src, dst, sems):
        send, recv = sems
        _, _, c, chips = _place()
        return [pltpu.make_async_remote_copy(
            src_ref=src[w].at[2 * chip[0] + chip[1]], dst_ref=dst[w].at[j],
            send_sem=send.at[3 * w + j], recv_sem=recv.at[3 * w + j], device_id=(*chip, c), device_id_type=MESH)
            for w in range(n) for j, chip in enumerate(chips)]

    def first(src, dst, sems):
        for cp in copies(src, dst, sems):
            cp.start()

    def last(src, dst, sems):
        for cp in copies(src, dst, sems):
            cp.wait()

    return _Comm(arrays=list(parts), out_shapes=[jax.ShapeDtypeStruct((N_CHIPS - 1,) + p.shape[1:], p.dtype) for p in parts],
                 aliases={}, sems=[pltpu.SemaphoreType.DMA((3 * n,))] * 2, phases=(first, None, last))


def _chip_sum(name, parts, got, chip_arr, c_arr, tr=256):
    _, h, cc = parts.shape
    tr = min(tr, h)
    nb = h // tr

    def body(chip_ref, c_ref, own_ref, got_ref, o_ref):
        acc = own_ref[...].astype(F32)
        for k in range(N_CHIPS - 1):
            acc = acc + got_ref[k].astype(F32)
        o_ref[...] = acc

    return pl.pallas_call(
        body, name=name,
        grid_spec=pltpu.PrefetchScalarGridSpec(
            num_scalar_prefetch=2, grid=(nb,),
            in_specs=[pl.BlockSpec((None, tr, cc), lambda i, chip_ref, c_ref: (chip_ref[0], i, 0)),
                      pl.BlockSpec((N_CHIPS - 1, tr, cc), lambda i, chip_ref, c_ref: (0, i, 0))],
            out_specs=pl.BlockSpec((tr, cc), lambda i, chip_ref, c_ref: (c_ref[0] * nb + i, 0))),
        out_shape=jax.ShapeDtypeStruct((2 * h, cc), F32),
        compiler_params=_params(("parallel",)),
    )(chip_arr, c_arr, parts, got)


def _pair_gather(bufs):
    n = len(bufs)

    def body(*refs):
        dst = refs[n:2 * n]
        send, recv = refs[2 * n:]
        x, y, c, _ = _place()

        def copy(w, core_half):
            h = bufs[w].shape[0] // 2
            rows = dst[w].at[pl.ds(core_half * h, h)]
            return pltpu.make_async_remote_copy(src_ref=rows, dst_ref=rows, send_sem=send.at[w], recv_sem=recv.at[w],
                                                device_id=(x, y, 1 - c), device_id_type=MESH)

        sends = [copy(w, c) for w in range(n)]
        for cp in sends:
            cp.start()
        for w in range(n):
            copy(w, 1 - c).wait_recv()
        for cp in sends:
            cp.wait_send()

    return pl.pallas_call(
        body, name="grad_pair_gather",
        in_specs=[ANY] * n, out_specs=[ANY] * n,
        out_shape=[jax.ShapeDtypeStruct(b.shape, b.dtype) for b in bufs],
        input_output_aliases={w: w for w in range(n)},
        scratch_shapes=[pltpu.SemaphoreType.DMA((n,))] * 2,
    )(*bufs)


def _all_sum_small(name, v):
    p = v.shape[0]

    def body(v_ref, o_ref, slots, send, recv):
        x, y, c, _ = _place()
        me = 4 * x + 2 * y + c
        copies = []
        for k in range(1, N_DEV):
            peer = (x ^ (k >> 2), y ^ ((k >> 1) & 1), c ^ (k & 1))
            copies.append(pltpu.make_async_remote_copy(
                src_ref=v_ref, dst_ref=slots.at[me], send_sem=send.at[k - 1], recv_sem=recv.at[k - 1],
                device_id=peer, device_id_type=MESH))
        for cp in copies:
            cp.start()
        slots[me] = v_ref[...]
        for cp in copies:
            cp.wait()
        acc = slots[0]
        for s in range(1, N_DEV):
            acc = acc + slots[s]
        o_ref[...] = acc

    vm = pl.BlockSpec(memory_space=pltpu.VMEM)
    return pl.pallas_call(
        body, name=name,
        in_specs=[vm], out_specs=vm,
        out_shape=jax.ShapeDtypeStruct(v.shape, F32),
        scratch_shapes=[pltpu.VMEM((N_DEV, p, LANES), F32), pltpu.SemaphoreType.DMA((N_DEV - 1,)),
                        pltpu.SemaphoreType.DMA((N_DEV - 1,))],
    )(v)


def _adamw(name, w, g, m, v, tr=256):
    r, c = w.shape
    tr = min(tr, r)
    assert r % tr == 0
    bc1 = 1.0 - ADAM_B1 ** ADAM_STEP
    bc2 = 1.0 - ADAM_B2 ** ADAM_STEP

    def body(w_ref, g_ref, m_ref, v_ref, d_ref, nm_ref, nv_ref):
        gv = g_ref[...]
        nm = ADAM_B1 * m_ref[...] + (1.0 - ADAM_B1) * gv
        nv = ADAM_B2 * v_ref[...] + (1.0 - ADAM_B2) * (gv * gv)
        nm_ref[...] = nm
        nv_ref[...] = nv
        d_ref[...] = -ADAM_LR * ((nm / bc1) / (jnp.sqrt(nv / bc2) + ADAM_EPS) + ADAM_WD * w_ref[...])

    blk = pl.BlockSpec((tr, c), lambda i: (i, 0))
    return pl.pallas_call(
        body, name=name, grid=(r // tr,),
        in_specs=[blk] * 4, out_specs=[blk] * 3,
        out_shape=[jax.ShapeDtypeStruct((r, c), F32)] * 3,
        compiler_params=_params(("parallel",)),
    )(w, g, m, v)


def _pack_small(parts):
    flat = jnp.concatenate([a.reshape(-1) for a in parts])
    n = flat.shape[0]
    p = -(-n // (8 * LANES)) * 8
    packed = jnp.pad(flat, (0, p * LANES - n)).reshape(p, LANES)

    def unpack(q):
        out, off = [], 0
        f = q.reshape(-1)
        for a in parts:
            out.append(f[off:off + a.size].reshape(a.shape))
            off += a.size
        return out

    return packed, unpack


def kernel(x, p, norm_mix, w_in, w_dw, conv_ln_g, conv_ln_b, w_conv_proj, q_norm, k_norm, w_attn_proj, w_out, norm_ffn, w_ff1, w_ff2, norm_ple, w_ple_gate, w_ple_proj, norm_final, loss_target, m_norm_mix, m_w_in, m_w_dw, m_conv_ln_g, m_conv_ln_b, m_w_conv_proj, m_q_norm, m_k_norm, m_w_attn_proj, m_w_out, m_norm_ffn, m_w_ff1, m_w_ff2, m_norm_ple, m_w_ple_gate, m_w_ple_proj, m_norm_final, v_norm_mix, v_w_in, v_w_dw, v_conv_ln_g, v_conv_ln_b, v_w_conv_proj, v_q_norm, v_k_norm, v_w_attn_proj, v_w_out, v_norm_ffn, v_w_ff1, v_w_ff2, v_norm_ple, v_w_ple_gate, v_w_ple_proj, v_norm_final):
    s, d = x.shape[1], x.shape[2]
    cw = d // 2
    kvw = d // GROUP
    xs, ps, tgt = x[0], p[0, 0], loss_target[0]
    cx, cy, cc = lax.axis_index("x"), lax.axis_index("y"), lax.axis_index("c")
    chip = 2 * cx + cy
    c_arr = jnp.reshape(cc, (1,)).astype(jnp.int32)
    tm, tme = min(MM_TM, s), min(MM_TM_EPI, s)

    names = ["w_in", "w_conv_proj", "w_attn_proj", "w_out", "w_ff1", "w_ff2", "w_ple_gate", "w_ple_proj"]
    big = [w_in, w_conv_proj, w_attn_proj, w_out, w_ff1, w_ff2, w_ple_gate, w_ple_proj]
    chip_arr = jnp.reshape(chip, (1,)).astype(jnp.int32)
    placed = [_cast_place("cast_" + nm, w[0], chip_arr) for nm, w in zip(names, big)]
    h0, (win,) = _rms_fwd("rms_mix", xs, norm_mix, comm=_gather_comm(placed[:1], short_host=True))
    cpc = cw // N_CHIPS
    taps = jnp.zeros((32, N_CHIPS, cpc), F32).at[:CONV_KERNEL].set(
        jnp.where(lax.broadcasted_iota(jnp.int32, (1, N_CHIPS, 1), 1) == chip, w_dw[0][:, None, :], 0.0))
    taps = jnp.where(cc == 0, taps, 0.0).reshape(32 * cw // LANES, LANES)
    wdw = _all_sum_small("gather_taps", taps).reshape(32, cw)

    cos, sin = _rope_tables(s)
    (z,) = _mm("z_proj", h0, win, b_cm=True, tm=tm, tn=win.shape[2] // 3, tk=d)
    uc, act = _conv_fwd(z, wdw, conv_ln_g, conv_ln_b, cw)
    qt, kt = _qk_fwd(z, cos, sin, q_norm, k_norm, d)
    o, lse, (wcp, wap, wout, w1, w2, wpg, wple) = _flash_fwd(qt, kt, z, d, comm=_gather_comm(placed[1:]))
    wap, wout, w2, wpg = (t.reshape(-1, t.shape[-1]) for t in (wap, wout, w2, wpg))
    (y_c,) = _mm("conv_proj", act, wcp, b_cm=True, tm=tm, tn=wcp.shape[2], tk=cw, out_dtypes=(F32,))
    tn = d // 2
    gcb = (2 * d + 2 * kvw) // tn

    def merge_epi(acc, yc, gc, ga):
        return acc, _sigmoid(gc.astype(F32)) * yc + _sigmoid(ga.astype(F32)) * acc

    y_a, merged = _mm("attn_proj", o, wap, tm=tme, tn=tn, tk=d, epi=merge_epi, out_dtypes=(BF, BF), b_resident=True,
                      extras=[_tile_extra(y_c, tme, tn), _tile_extra(z, tme, tn, gcb), _tile_extra(z, tme, tn, gcb + 2)])
    (x1,) = _mm("out_proj", merged, wout, tm=tm, tn=tn, tk=d, epi=lambda acc, r: (r + acc,), out_dtypes=(F32,),
                extras=[_tile_extra(xs, tm, tn)])
    h1 = _rms_fwd("rms_ffn", x1, norm_ffn)
    (a,) = _mm("ff1", h1, w1, b_cm=True, tm=tm, tn=tn, tk=d)

    def relu2(t):
        return jnp.square(jnp.maximum(t, 0.0))

    (x2,) = _mm("ff2", a, w2, tm=tm, tn=tn, tk=d, a_fn=relu2, epi=lambda acc, r: (r + acc,), out_dtypes=(F32,),
                extras=[_tile_extra(x1, tm, tn)])
    h2 = _rms_fwd("rms_ple", x2, norm_ple)
    to_bf = lambda t: t.astype(BF)
    (e,) = _mm("ple_proj", ps, wple, b_cm=True, tm=tm, tn=wple.shape[2], tk=ps.shape[1], a_fn=to_bf)

    def ple_epi(acc, ev, r):
        gt = _sigmoid(acc)
        return r + gt * ev.astype(F32), gt

    x3, gate = _mm("ple_gate", h2, wpg, tm=tme, tn=tn, tk=d, epi=ple_epi, out_dtypes=(F32, BF), b_resident=True,
                   extras=[_tile_extra(e, tme, tn), _tile_extra(x2, tme, tn)])

    dx3, de, dgp, sq, d_fin = _loss_bwd(x3, tgt, norm_final.reshape(1, d), e, gate)
    tkt = min(2048, s)
    (g_wple,) = _mm("d_wple", ps, de, ta=True, out_cm=True, tm=ps.shape[1], tn=wple.shape[2], tk=tkt, a_fn=to_bf)
    (g_wpg,) = _mm("d_wpg", h2, dgp, ta=True, tm=tm, tn=tn, tk=tkt)
    (dh2,) = _mm("d_h2", dgp, wpg, tb=True, tm=tm, tn=tn, tk=d)
    dx2, dx2b, d_ple = _rms_bwd("rms_ple_bwd", dh2, x2, norm_ple, dx3)

    (da,) = _mm("d_a", dx2b, w2, tb=True, tm=tm, tn=tn, tk=d, out_dtypes=(BF,),
                epi=lambda acc, av: (acc * (2.0 * jnp.maximum(av.astype(F32), 0.0)),), extras=[_tile_extra(a, tm, tn)])
    (g_w2,) = _mm("d_w2", a, dx2b, ta=True, tm=tm, tn=tn, tk=tkt, a_fn=relu2)
    (g_w1,) = _mm("d_w1", h1, da, ta=True, out_cm=True, tm=tm, tn=tn, tk=tkt)
    (dh1,) = _mm("d_h1", da, w1, tb=True, b_cm=True, tm=tm, tn=tn, tk=w1.shape[2])
    dx1, dx1b, d_ffn = _rms_bwd("rms_ffn_bwd", dh1, x1, norm_ffn, dx2)

    def merge_bwd(acc, gc, ga, yc, ya):
        sc, sa = _sigmoid(gc.astype(F32)), _sigmoid(ga.astype(F32))
        return acc * sc, acc * sa, jnp.concatenate(
            [acc * yc * sc * (1.0 - sc), acc * ya.astype(F32) * sa * (1.0 - sa)], axis=1)

    tmd = min(MM_TM_DZ, s)
    gate0 = 2 * d + 2 * kvw
    z_cols = z.shape[1]

    def gate_window(width, col0):
        return (pl.Element(tmd), pl.Element(width)), lambda i, j, k: (i * tmd, col0)

    dy_c, dy_a, dz = _mm(
        "d_merged", dx1b, wout, tb=True, tm=tmd, tn=d, tk=d, epi=merge_bwd, out_dtypes=(BF, BF, BF), b_resident=True,
        extras=[(z, *gate_window(d, gate0)), (z, *gate_window(d, gate0 + d)), _tile_extra(y_c, tmd, d),
                _tile_extra(y_a, tmd, d)],
        out_overrides={2: ((s, z_cols), pl.BlockSpec(*gate_window(2 * d, gate0)))})
    (g_wout,) = _mm("d_wout", merged, dx1b, ta=True, tm=tm, tn=tn, tk=tkt)
    (g_wap,) = _mm("d_wap", o, dy_a, ta=True, tm=tm, tn=tn, tk=tkt)

    def slabs(g):
        return g if g.ndim == 3 else g.reshape(N_CHIPS, g.shape[0] // N_CHIPS, g.shape[1])

    grads_a = [slabs(g) for g in (g_wap, g_wout, g_w1, g_w2, g_wpg, g_wple)]
    (do,), got_a = _mm("d_o", dy_a, wap, tb=True, tm=tm, tn=tn, tk=d, comm=_pair_comm(grads_a))
    parts_a = [_pair_sum("pair_sum_" + nm, g, r, c_arr) for nm, g, r in zip(names[2:], grads_a, got_a)]
    dqt, dkt, dz, _ = _flash_bwd(qt, kt, z, o, do, lse, d, dz)
    dz, d_qn, d_kn = _qk_bwd(dqt, dkt, z, cos, sin, q_norm, k_norm, d, dz)
    (g_wcp,) = _mm("d_wcp", act, dy_c, ta=True, out_cm=True, tm=cw, tn=wcp.shape[2], tk=tkt)
    (dact,) = _mm("d_act", dy_c, wcp, tb=True, b_cm=True, tm=tm, tn=cw, tk=wcp.shape[2])
    p_wap, p_wout, p_w1, p_w2, p_wpg, p_wple = parts_a
    dz, d_taps, d_lng, d_lnb, (s_wap, s_wout, s_wpg, s_wple) = _conv_bwd(
        dact, uc, z, wdw, conv_ln_g, conv_ln_b, cw, dz, comm=_chip_comm([p_wap, p_wout, p_wpg, p_wple]))
    (g_win,), (s_w1, s_w2) = _mm("d_win", h0, dz, ta=True, out_cm=True, tm=tm, tn=win.shape[2] // 3, tk=tkt,
                                 comm=_chip_comm([p_w1, p_w2]))
    slots_a = [s_wap, s_wout, s_w1, s_w2, s_wpg, s_wple]
    grads_b = [slabs(g_win), slabs(g_wcp)]
    got_b = _run_comm("grad_pair_exchange_b", _pair_comm(grads_b))
    parts_b = [_pair_sum("pair_sum_" + nm, g, r, c_arr) for nm, g, r in zip(names[:2], grads_b, got_b)]
    (dh0,), slots_b = _mm("d_h0", dz, win, tb=True, b_cm=True, tm=tm, tn=tn, tk=win.shape[2], comm=_chip_comm(parts_b))
    dx, _, d_mix = _rms_bwd("rms_mix_bwd", dh0, xs, norm_mix, dx1)
    big_grads = _pair_gather(
        [_chip_sum("chip_sum_" + nm, cp, sl, chip_arr, c_arr)
         for nm, cp, sl in zip(names, parts_b + parts_a, list(slots_b) + list(slots_a))])

    small = [d_mix, d_taps[:CONV_KERNEL], d_lng, d_lnb, d_qn, d_kn, d_ffn, d_ple, d_fin]
    packed, unpack = _pack_small(small)
    g_mix, g_taps, g_lng, g_lnb, g_qn, g_kn, g_ffn, g_ple, g_fin = unpack(_all_sum_small("reduce_small", packed))
    g_dw = lax.dynamic_slice_in_dim(g_taps.reshape(CONV_KERNEL, N_CHIPS, cpc), chip, 1, axis=1).reshape(1, CONV_KERNEL, cpc)

    sq_local = lax.reduce_precision(sq[0, 0], 8, 23)
    loss = (0.5 / d) * lax.psum(sq_local, ("x", "y", "c"))

    grads = {
        "norm_mix": g_mix, "w_in": big_grads[0][None], "w_dw": g_dw, "conv_ln_g": g_lng, "conv_ln_b": g_lnb,
        "w_conv_proj": big_grads[1][None], "q_norm": g_qn, "k_norm": g_kn, "w_attn_proj": big_grads[2][None],
        "w_out": big_grads[3][None], "norm_ffn": g_ffn, "w_ff1": big_grads[4][None], "w_ff2": big_grads[5][None],
        "norm_ple": g_ple, "w_ple_gate": big_grads[6][None], "w_ple_proj": big_grads[7][None],
        "norm_final": g_fin.reshape(d),
    }
    weights = dict(norm_mix=norm_mix, w_in=w_in, w_dw=w_dw, conv_ln_g=conv_ln_g, conv_ln_b=conv_ln_b, w_conv_proj=w_conv_proj,
                   q_norm=q_norm, k_norm=k_norm, w_attn_proj=w_attn_proj, w_out=w_out, norm_ffn=norm_ffn, w_ff1=w_ff1,
                   w_ff2=w_ff2, norm_ple=norm_ple, w_ple_gate=w_ple_gate, w_ple_proj=w_ple_proj, norm_final=norm_final)
    m_in = dict(norm_mix=m_norm_mix, w_in=m_w_in, w_dw=m_w_dw, conv_ln_g=m_conv_ln_g, conv_ln_b=m_conv_ln_b,
                w_conv_proj=m_w_conv_proj, q_norm=m_q_norm, k_norm=m_k_norm, w_attn_proj=m_w_attn_proj, w_out=m_w_out,
                norm_ffn=m_norm_ffn, w_ff1=m_w_ff1, w_ff2=m_w_ff2, norm_ple=m_norm_ple, w_ple_gate=m_w_ple_gate,
                w_ple_proj=m_w_ple_proj, norm_final=m_norm_final)
    v_in = dict(norm_mix=v_norm_mix, w_in=v_w_in, w_dw=v_w_dw, conv_ln_g=v_conv_ln_g, conv_ln_b=v_conv_ln_b,
                w_conv_proj=v_w_conv_proj, q_norm=v_q_norm, k_norm=v_k_norm, w_attn_proj=v_w_attn_proj, w_out=v_w_out,
                norm_ffn=v_norm_ffn, w_ff1=v_w_ff1, w_ff2=v_w_ff2, norm_ple=v_norm_ple, w_ple_gate=v_w_ple_gate,
                w_ple_proj=v_w_ple_proj, norm_final=v_norm_final)
    order = list(weights)
    deltas, new_m, new_v, g_out = [], [], [], []
    for nm in order:
        w = weights[nm]
        shape = w.shape
        two_d = (-1, shape[-1])
        dl, mm_, vv_ = _adamw("adamw_" + nm, w.reshape(two_d), grads[nm].reshape(two_d), m_in[nm].reshape(two_d),
                              v_in[nm].reshape(two_d))
        g_out.append(grads[nm].reshape(shape))
        deltas.append(dl.reshape(shape))
        new_m.append(mm_.reshape(shape))
        new_v.append(vv_.reshape(shape))
    return (loss, dx[None], *g_out, *deltas, *new_m, *new_v)
```

```python
from typing import NamedTuple

import jax
import jax.numpy as jnp
from jax import lax
from jax.experimental import pallas as pl
from jax.experimental.pallas import tpu as pltpu

F32 = jnp.float32
BF = jnp.bfloat16

EPS = 1e-6
HEAD_DIM = 128
GROUP = 4
GRID_W = 64
ROPE_THETA = 10000.0
CONV_KERNEL = 31
HALO = 16
N_CHIPS = 4
N_DEV = 8
LANES = 128

ADAM_LR = 0.001
ADAM_B1 = 0.9
ADAM_B2 = 0.999
ADAM_EPS = 1e-08
ADAM_WD = 0.01
ADAM_STEP = 10

VMEM_LIMIT = 56 * 2 ** 20
LOG2E = 1.4426950408889634
LN2 = 0.6931471805599453
Q_SCALE = HEAD_DIM ** -0.5 * LOG2E
ROW_TILE = 256
FLASH_TQ_FWD = 512
FLASH_TQ_BWD = 512
FLASH_TK = 512
MM_TM = 1024
MM_TM_EPI = 512
MM_TM_DZ = 256
MESH = pl.DeviceIdType.MESH
ANY = pl.BlockSpec(memory_space=pl.ANY)


def _params(sem):
    return pltpu.CompilerParams(dimension_semantics=sem, vmem_limit_bytes=VMEM_LIMIT)


def _sigmoid(x):
    return 1.0 / (1.0 + jnp.exp(-x))


class _Comm(NamedTuple):
    arrays: list
    out_shapes: list
    aliases: dict
    sems: list
    phases: tuple


def _call(body, *, name, grid, in_specs, out_specs, out_shape, scratch_shapes, semantics, args, comm=None, aliases=None):
    n_in, n_out = len(in_specs), len(out_specs)
    aliases = dict(aliases or {})
    if comm is None:
        res = pl.pallas_call(body, name=name, grid=grid, in_specs=in_specs, out_specs=out_specs, out_shape=out_shape,
                             scratch_shapes=scratch_shapes, input_output_aliases=aliases,
                             compiler_params=_params(semantics))(*args)
        return res, []
    nci, nco, ncs = len(comm.arrays), len(comm.out_shapes), len(comm.sems)
    n_steps = 1
    for g in grid:
        n_steps *= g
    first, middle, last = comm.phases

    def hosted(*refs):
        ins, cin = refs[:n_in], refs[n_in:n_in + nci]
        outs = refs[n_in + nci:n_in + nci + n_out]
        cout = refs[n_in + nci + n_out:n_in + nci + n_out + nco]
        rest = refs[n_in + nci + n_out + nco:]
        scratch, sems = rest[:len(rest) - ncs], rest[len(rest) - ncs:]
        step = 0
        for ax, g in enumerate(grid):
            step = step * g + pl.program_id(ax)
        for at, fn in ((0, first), (n_steps // 2, middle)):
            if fn is not None:
                pl.when(step == at)(lambda fn=fn: fn(cin, cout, sems))
        body(*ins, *outs, *scratch)
        if last is not None:
            pl.when(step == n_steps - 1)(lambda: last(cin, cout, sems))

    res = pl.pallas_call(
        hosted, name=name, grid=grid,
        in_specs=list(in_specs) + [ANY] * nci, out_specs=list(out_specs) + [ANY] * nco,
        out_shape=list(out_shape) + list(comm.out_shapes),
        input_output_aliases={**aliases, **{n_in + a: n_out + b for a, b in comm.aliases.items()}},
        scratch_shapes=list(scratch_shapes) + list(comm.sems),
        compiler_params=_params(("arbitrary",) * len(grid)),
    )(*args, *comm.arrays)
    return res[:n_out], res[n_out:]


def _mm(name, a, b, *, tm, tn, tk, ta=False, tb=False, b_cm=False, out_cm=False,
        a_fn=None, extras=(), epi=None, out_dtypes=(BF,), epi_rows=256, comm=None, b_resident=False,
        out_overrides=None):
    if ta:
        kc, m = a.shape
    else:
        m, kc = a.shape
    if b_cm:
        nc, r, c = b.shape
        n, per = (r, c) if tb else (nc * c, c)
    else:
        n = b.shape[0] if tb else b.shape[1]
    tm, tn, tk = min(tm, m), min(tn, n), min(tk, kc)
    assert m % tm == 0 and n % tn == 0 and kc % tk == 0, (name, m, n, kc, tm, tn, tk)
    nk = kc // tk
    a_spec = pl.BlockSpec((tk, tm), lambda i, j, k: (k, i)) if ta else pl.BlockSpec((tm, tk), lambda i, j, k: (i, k))
    if b_cm and not tb:
        assert per % tn == 0
        npj = per // tn
        b_spec = pl.BlockSpec((None, tk, tn), lambda i, j, k: (j // npj, k, j % npj))
    elif b_cm:
        assert per % tk == 0
        npk = per // tk
        b_spec = pl.BlockSpec((None, tn, tk), lambda i, j, k: (k // npk, j, k % npk))
    elif b_resident:
        assert nk == 1
        b_spec = pl.BlockSpec(b.shape, lambda i, j, k: (0, 0))
    elif tb:
        b_spec = pl.BlockSpec((tn, tk), lambda i, j, k: (j, k))
    else:
        b_spec = pl.BlockSpec((tk, tn), lambda i, j, k: (k, j))
    if out_cm:
        assert (n // N_CHIPS) % tn == 0
        npo = (n // N_CHIPS) // tn
        o_spec = pl.BlockSpec((None, tm, tn), lambda i, j, k: (j // npo, i, j % npo))
        o_shape = (N_CHIPS, m, n // N_CHIPS)
    else:
        o_spec = pl.BlockSpec((tm, tn), lambda i, j, k: (i, j))
        o_shape = (m, n)
    ne, no = len(extras), len(out_dtypes)
    dims = (((0 if ta else 1,), (1 if tb else 0,)), ((), ()))
    er = min(epi_rows, tm)
    chunked = nk == 1 and epi is not None and not ta
    use_acc = (nk > 1 or epi is not None) and not chunked
    assert chunked or not b_resident

    def body(*refs):
        a_ref, b_ref = refs[0], refs[1]
        ex = refs[2:2 + ne]
        outs = refs[2 + ne:2 + ne + no]
        if chunked:
            if b_resident:
                cols = pl.ds(pl.multiple_of(pl.program_id(1) * tn, tn), tn)
                bt = b_ref[cols, :] if tb else b_ref[:, cols]
            else:
                bt = b_ref[...]
            for r0 in range(0, tm, er):
                rows = slice(r0, r0 + er)
                at = a_ref[rows, :]
                if a_fn is not None:
                    at = a_fn(at)
                d = lax.dot_general(at, bt, dims, preferred_element_type=F32)
                vals = epi(d, *[e[rows, :] for e in ex])
                for o, v, dt in zip(outs, vals, out_dtypes):
                    o[rows, :] = v.astype(dt)
            return
        at = a_ref[...]
        if a_fn is not None:
            at = a_fn(at)
        d = lax.dot_general(at, b_ref[...], dims, preferred_element_type=F32)
        if not use_acc:
            outs[0][...] = d.astype(out_dtypes[0])
            return
        acc = refs[-1]
        k = pl.program_id(2)

        @pl.when(k == 0)
        def _():
            acc[...] = d

        if nk > 1:
            @pl.when(k > 0)
            def _():
                acc[...] += d

        @pl.when(k == nk - 1)
        def _():
            for r0 in range(0, tm, er):
                rows = slice(r0, r0 + er)
                if epi is None:
                    vals = (acc[rows, :],)
                else:
                    vals = epi(acc[rows, :], *[e[rows, :] for e in ex])
                for o, v, dt in zip(outs, vals, out_dtypes):
                    o[rows, :] = v.astype(dt)

    out_specs = [o_spec] * no
    out_shapes = [jax.ShapeDtypeStruct(o_shape, dt) for dt in out_dtypes]
    for idx, (shape, spec) in (out_overrides or {}).items():
        out_specs[idx], out_shapes[idx] = spec, jax.ShapeDtypeStruct(shape, out_dtypes[idx])
    res, cres = _call(
        body, name=name, grid=(m // tm, n // tn, nk),
        in_specs=[a_spec, b_spec] + [pl.BlockSpec(bs, im) for _, bs, im in extras],
        out_specs=out_specs,
        out_shape=out_shapes,
        scratch_shapes=[pltpu.VMEM((tm, tn), F32)] if use_acc else [],
        semantics=("parallel", "parallel", "arbitrary"),
        args=[a, b] + [e for e, _, _ in extras], comm=comm)
    return res if comm is None else (res, cres)


def _tile_extra(arr, tm, tn, col_block0=0):
    return (arr, (tm, tn), lambda i, j, k: (i, j + col_block0))


def _rms_fwd(name, x, g, ts=None, comm=None):
    s, d = x.shape
    ts = ts or ROW_TILE

    def body(x_ref, g_ref, h_ref):
        xv = x_ref[...]
        r = lax.rsqrt(jnp.mean(xv * xv, axis=-1, keepdims=True) + EPS)
        h_ref[...] = (xv * r * g_ref[...]).astype(BF)

    (h,), cres = _call(
        body, name=name, grid=(s // ts,),
        in_specs=[pl.BlockSpec((ts, d), lambda i: (i, 0)), pl.BlockSpec((1, d), lambda i: (0, 0))],
        out_specs=[pl.BlockSpec((ts, d), lambda i: (i, 0))],
        out_shape=[jax.ShapeDtypeStruct((s, d), BF)],
        scratch_shapes=[], semantics=("parallel",), args=[x, g], comm=comm)
    return h if comm is None else (h, cres)


def _rms_bwd(name, dh, x, g, dres, ts=None):
    s, d = x.shape
    ts = ts or ROW_TILE

    def body(dh_ref, x_ref, g_ref, dres_ref, dx_ref, dxb_ref, dg_ref):
        xv = x_ref[...]
        dhv = dh_ref[...].astype(F32)
        r = lax.rsqrt(jnp.mean(xv * xv, axis=-1, keepdims=True) + EPS)
        nrm = xv * r
        dn = dhv * g_ref[...]
        dx = dres_ref[...] + r * (dn - nrm * jnp.mean(dn * nrm, axis=-1, keepdims=True))
        dx_ref[...] = dx
        dxb_ref[...] = dx.astype(BF)
        part = jnp.sum(dhv * nrm, axis=0, keepdims=True)

        @pl.when(pl.program_id(0) == 0)
        def _():
            dg_ref[...] = part

        @pl.when(pl.program_id(0) > 0)
        def _():
            dg_ref[...] += part

    row = pl.BlockSpec((ts, d), lambda i: (i, 0))
    vec = pl.BlockSpec((1, d), lambda i: (0, 0))
    return pl.pallas_call(
        body, name=name, grid=(s // ts,),
        in_specs=[row, row, vec, row],
        out_specs=[row, row, vec],
        out_shape=[jax.ShapeDtypeStruct((s, d), F32), jax.ShapeDtypeStruct((s, d), BF), jax.ShapeDtypeStruct((1, d), F32)],
        compiler_params=_params(("arbitrary",)),
    )(dh, x, g, dres)


def _loss_bwd(x3, tgt, gfin, e, gate, ts=None):
    s, d = x3.shape
    ts = ts or ROW_TILE

    def body(x_ref, t_ref, g_ref, e_ref, gate_ref, dx_ref, de_ref, dgp_ref, sq_ref, dg_ref):
        xv = x_ref[...]
        gv = g_ref[...]
        r = lax.rsqrt(jnp.mean(xv * xv, axis=-1, keepdims=True) + EPS)
        nrm = xv * r
        err = nrm * gv - t_ref[...]
        dy = err * (1.0 / d)
        dn = dy * gv
        dx = r * (dn - nrm * jnp.mean(dn * nrm, axis=-1, keepdims=True))
        dx_ref[...] = dx
        ev = e_ref[...].astype(F32)
        gt = gate_ref[...].astype(F32)
        de_ref[...] = (dx * gt).astype(BF)
        dgp_ref[...] = (dx * ev * gt * (1.0 - gt)).astype(BF)
        sq = jnp.full((8, LANES), jnp.sum(err * err), F32)
        part = jnp.sum(dy * nrm, axis=0, keepdims=True)

        @pl.when(pl.program_id(0) == 0)
        def _():
            sq_ref[...] = sq
            dg_ref[...] = part

        @pl.when(pl.program_id(0) > 0)
        def _():
            sq_ref[...] += sq
            dg_ref[...] += part

    row = pl.BlockSpec((ts, d), lambda i: (i, 0))
    vec = pl.BlockSpec((1, d), lambda i: (0, 0))
    return pl.pallas_call(
        body, name="loss_bwd", grid=(s // ts,),
        in_specs=[row, row, vec, row, row],
        out_specs=[row, row, row, pl.BlockSpec((8, LANES), lambda i: (0, 0)), vec],
        out_shape=[jax.ShapeDtypeStruct((s, d), F32), jax.ShapeDtypeStruct((s, d), BF), jax.ShapeDtypeStruct((s, d), BF),
                   jax.ShapeDtypeStruct((8, LANES), F32), jax.ShapeDtypeStruct((1, d), F32)],
        compiler_params=_params(("arbitrary",)),
    )(x3, tgt, gfin, e, gate)


def _halo_specs(ts, s, width, col_block):
    per = ts // HALO
    last = s // HALO - 1
    return [
        pl.BlockSpec((HALO, width), lambda i: (jnp.maximum(i * per - 1, 0), col_block)),
        pl.BlockSpec((ts, width), lambda i: (i, col_block)),
        pl.BlockSpec((HALO, width), lambda i: (jnp.minimum((i + 1) * per, last), col_block)),
    ]


def _glu_ext(zp, zc, zn, ext, cw, ts, i, n_tiles):
    def glu(zr):
        zv = zr[...].astype(F32)
        return zv[:, :cw] * _sigmoid(zv[:, cw:])

    ext[0:HALO, :] = jnp.where(i > 0, glu(zp), 0.0)
    ext[HALO:HALO + ts, :] = glu(zc)
    ext[HALO + ts:, :] = jnp.where(i < n_tiles - 1, glu(zn), 0.0)


SUBLANES = 8


def _shift_scratch(ts):
    return pltpu.VMEM((SUBLANES, ts + 2 * HALO - SUBLANES, LANES), F32)


def _shifted_copies(ext, sh, cols, ts):
    n = ts + 2 * HALO - SUBLANES
    for r in range(SUBLANES):
        sh[r] = ext[r:r + n, cols]


def _tap_rows(sh, off, ts):
    q, r = divmod(off, SUBLANES)
    return sh[r, q * SUBLANES:q * SUBLANES + ts, :]


def _ln_stats(uc):
    mu = jnp.mean(uc, axis=-1, keepdims=True)
    xc = uc - mu
    rstd = lax.rsqrt(jnp.mean(xc * xc, axis=-1, keepdims=True) + EPS)
    return xc * rstd, rstd


def _conv_fwd(z, wdw, ln_g, ln_b, cw, ts=None):
    s = z.shape[0]
    ts = ts or ROW_TILE
    n_tiles = s // ts
    pad = CONV_KERNEL // 2

    def body(zp, zc, zn, w_ref, g_ref, b_ref, uc_ref, act_ref, ext, sh):
        i = pl.program_id(0)
        _glu_ext(zp, zc, zn, ext, cw, ts, i, n_tiles)

        def col_block(cb, carry):
            cols = pl.ds(pl.multiple_of(cb * LANES, LANES), LANES)
            _shifted_copies(ext, sh, cols, ts)
            acc = jnp.zeros((ts, LANES), F32)
            for j in range(CONV_KERNEL):
                acc = acc + _tap_rows(sh, HALO - pad + j, ts) * w_ref[j:j + 1, cols]
            uc_ref[:, cols] = acc
            return carry

        lax.fori_loop(0, cw // LANES, col_block, 0)
        xhat, _ = _ln_stats(uc_ref[...])
        ln = xhat * g_ref[...] + b_ref[...]
        act_ref[...] = (ln * _sigmoid(ln)).astype(BF)

    vec = pl.BlockSpec((1, cw), lambda i: (0, 0))
    row = pl.BlockSpec((ts, cw), lambda i: (i, 0))
    return pl.pallas_call(
        body, name="conv_fwd", grid=(n_tiles,),
        in_specs=_halo_specs(ts, s, 2 * cw, 0) + [pl.BlockSpec((32, cw), lambda i: (0, 0)), vec, vec],
        out_specs=[row, row],
        out_shape=[jax.ShapeDtypeStruct((s, cw), F32), jax.ShapeDtypeStruct((s, cw), BF)],
        scratch_shapes=[pltpu.VMEM((ts + 2 * HALO, cw), F32), _shift_scratch(ts)],
        compiler_params=_params(("parallel",)),
    )(z, z, z, wdw, ln_g, ln_b)


def _conv_bwd(ds, uc, z, wdw, ln_g, ln_b, cw, dz, ts=None, comm=None):
    s = z.shape[0]
    ts = ts or ROW_TILE
    n_tiles = s // ts
    pad = CONV_KERNEL // 2

    def body(zp, zc, zn, dsp, dsc, dsn, ucp, ucc, ucn, w_ref, g_ref, b_ref, _dz_in,
             dz_ref, dw_ref, dg_ref, db_ref, ext, dext, sh, dsh):
        i = pl.program_id(0)
        gv, bv = g_ref[...], b_ref[...]

        def ln_bwd(ds_r, uc_r):
            xhat, rstd = _ln_stats(uc_r[...])
            ln = xhat * gv + bv
            sg = _sigmoid(ln)
            dln = ds_r[...].astype(F32) * (sg * (1.0 + ln * (1.0 - sg)))
            dxh = dln * gv
            duc = rstd * (dxh - jnp.mean(dxh, axis=-1, keepdims=True) - xhat * jnp.mean(dxh * xhat, axis=-1, keepdims=True))
            return duc, dln, xhat

        duc_p, _, _ = ln_bwd(dsp, ucp)
        duc_c, dln_c, xhat_c = ln_bwd(dsc, ucc)
        duc_n, _, _ = ln_bwd(dsn, ucn)
        dext[0:HALO, :] = jnp.where(i > 0, duc_p, 0.0)
        dext[HALO:HALO + ts, :] = duc_c
        dext[HALO + ts:, :] = jnp.where(i < n_tiles - 1, duc_n, 0.0)
        _glu_ext(zp, zc, zn, ext, cw, ts, i, n_tiles)

        dg_part = jnp.sum(dln_c * xhat_c, axis=0, keepdims=True)
        db_part = jnp.sum(dln_c, axis=0, keepdims=True)

        @pl.when(i == 0)
        def _():
            dw_ref[...] = jnp.zeros_like(dw_ref)
            dg_ref[...] = dg_part
            db_ref[...] = db_part

        @pl.when(i > 0)
        def _():
            dg_ref[...] += dg_part
            db_ref[...] += db_part

        def col_block(cb, carry):
            c0 = pl.multiple_of(cb * LANES, LANES)
            cols, gate_cols = pl.ds(c0, LANES), pl.ds(cw + c0, LANES)
            _shifted_copies(dext, dsh, cols, ts)
            _shifted_copies(ext, sh, cols, ts)
            du = jnp.zeros((ts, LANES), F32)
            for j in range(CONV_KERNEL):
                du = du + _tap_rows(dsh, HALO + pad - j, ts) * w_ref[j:j + 1, cols]
            ca, sb = zc[:, cols].astype(F32), _sigmoid(zc[:, gate_cols].astype(F32))
            dz_ref[:, cols] = (du * sb).astype(BF)
            dz_ref[:, gate_cols] = (du * ca * sb * (1.0 - sb)).astype(BF)
            duc_blk = _tap_rows(dsh, HALO, ts)
            for j in range(CONV_KERNEL):
                dw_ref[j:j + 1, cols] += jnp.sum(_tap_rows(sh, HALO - pad + j, ts) * duc_blk, axis=0, keepdims=True)
            return carry

        lax.fori_loop(0, cw // LANES, col_block, 0)

    vec = pl.BlockSpec((1, cw), lambda i: (0, 0))
    wsp = pl.BlockSpec((32, cw), lambda i: (0, 0))
    res, cres = _call(
        body, name="conv_bwd", grid=(n_tiles,),
        in_specs=(_halo_specs(ts, s, 2 * cw, 0) + _halo_specs(ts, s, cw, 0) + _halo_specs(ts, s, cw, 0)
                  + [wsp, vec, vec, ANY]),
        out_specs=[pl.BlockSpec((ts, 2 * cw), lambda i: (i, 0)), wsp, vec, vec],
        out_shape=[jax.ShapeDtypeStruct(dz.shape, BF), jax.ShapeDtypeStruct((32, cw), F32),
                   jax.ShapeDtypeStruct((1, cw), F32), jax.ShapeDtypeStruct((1, cw), F32)],
        scratch_shapes=[pltpu.VMEM((ts + 2 * HALO, cw), F32), pltpu.VMEM((ts + 2 * HALO, cw), F32),
                        _shift_scratch(ts), _shift_scratch(ts)],
        semantics=("arbitrary",), args=[z, z, z, ds, ds, ds, uc, uc, uc, wdw, ln_g, ln_b, dz], comm=comm,
        aliases={12: 0})
    return (*res, cres)


def _rope_tables(s):
    axis_dim = HEAD_DIM // 2
    n_rows = s // GRID_W
    inv_freq = ROPE_THETA ** (-jnp.arange(0, axis_dim, 2, dtype=F32) / axis_dim)[None, :]
    ar = jnp.arange(n_rows, dtype=jnp.int32).astype(F32)[:, None] * inv_freq
    ac = jnp.arange(GRID_W, dtype=jnp.int32).astype(F32)[:, None] * inv_freq

    def table(fr, fc):
        by_row = jnp.broadcast_to(fr[:, None, :], (n_rows, GRID_W, axis_dim))
        by_col = jnp.broadcast_to(fc[None, :, :], (n_rows, GRID_W, axis_dim))
        return jnp.concatenate([by_row, by_col], axis=-1).reshape(s, HEAD_DIM)

    cos = table(jnp.concatenate([jnp.cos(ar), jnp.cos(ar)], axis=-1), jnp.concatenate([jnp.cos(ac), jnp.cos(ac)], axis=-1))
    sin = table(jnp.concatenate([-jnp.sin(ar), jnp.sin(ar)], axis=-1), jnp.concatenate([-jnp.sin(ac), jnp.sin(ac)], axis=-1))
    return cos, sin


def _swap_quarters(x):
    q = HEAD_DIM // 4
    lane = lax.broadcasted_iota(jnp.int32, x.shape, 1)
    return jnp.where((lane % (2 * q)) < q, pltpu.roll(x, HEAD_DIM - q, 1), pltpu.roll(x, q, 1))


def _qk_fwd(z, cos, sin, qg, kg, d, ts=None):
    s = z.shape[0]
    ts = ts or ROW_TILE
    kvw = d // GROUP
    scale = Q_SCALE

    def body(q_ref, k_ref, c_ref, s_ref, qg_ref, kg_ref, qo_ref, ko_ref):
        cv, sv = c_ref[...], s_ref[...]

        def head(x_ref, g_ref, o_ref, h, mul):
            xv = x_ref[:, h * HEAD_DIM:(h + 1) * HEAD_DIM].astype(F32)
            r = lax.rsqrt(jnp.mean(xv * xv, axis=-1, keepdims=True) + EPS)
            nrm = xv * r * g_ref[...]
            out = nrm * cv + _swap_quarters(nrm) * sv
            o_ref[:, h * HEAD_DIM:(h + 1) * HEAD_DIM] = (out * mul).astype(BF)

        for h in range(d // HEAD_DIM):
            head(q_ref, qg_ref, qo_ref, h, scale)
        for h in range(kvw // HEAD_DIM):
            head(k_ref, kg_ref, ko_ref, h, 1.0)

    cw2 = d
    tab = pl.BlockSpec((ts, HEAD_DIM), lambda i: (i, 0))
    vec = pl.BlockSpec((1, HEAD_DIM), lambda i: (0, 0))
    return pl.pallas_call(
        body, name="qk_fwd", grid=(s // ts,),
        in_specs=[pl.BlockSpec((ts, d), lambda i: (i, cw2 // d)),
                  pl.BlockSpec((ts, kvw), lambda i: (i, (cw2 + d) // kvw)), tab, tab, vec, vec],
        out_specs=[pl.BlockSpec((ts, d), lambda i: (i, 0)), pl.BlockSpec((ts, kvw), lambda i: (i, 0))],
        out_shape=[jax.ShapeDtypeStruct((s, d), BF), jax.ShapeDtypeStruct((s, kvw), BF)],
        compiler_params=_params(("parallel",)),
    )(z, z, cos, sin, qg, kg)


def _qk_bwd(dqt, dkt, z, cos, sin, qg, kg, d, dz, ts=None):
    s = z.shape[0]
    ts = ts or ROW_TILE
    kvw = d // GROUP
    scale = HEAD_DIM ** -0.5

    def body(dq_ref, dk_ref, q_ref, k_ref, c_ref, s_ref, qg_ref, kg_ref, _dz_in, dzo_ref, dqg_ref, dkg_ref):
        cv, sv = c_ref[...], s_ref[...]

        def head(dy_ref, x_ref, g_ref, col0, h, mul):
            dout = dy_ref[:, h * HEAD_DIM:(h + 1) * HEAD_DIM].astype(F32) * mul
            dn = dout * cv + _swap_quarters(dout * sv)
            xv = x_ref[:, h * HEAD_DIM:(h + 1) * HEAD_DIM].astype(F32)
            r = lax.rsqrt(jnp.mean(xv * xv, axis=-1, keepdims=True) + EPS)
            nh = xv * r
            dnh = dn * g_ref[...]
            c0 = col0 + h * HEAD_DIM
            dzo_ref[:, c0:c0 + HEAD_DIM] = (r * (dnh - nh * jnp.mean(dnh * nh, axis=-1, keepdims=True))).astype(BF)
            return jnp.sum(dn * nh, axis=0, keepdims=True)

        dqg = jnp.zeros((1, HEAD_DIM), F32)
        for h in range(d // HEAD_DIM):
            dqg = dqg + head(dq_ref, q_ref, qg_ref, 0, h, scale)
        dkg = jnp.zeros((1, HEAD_DIM), F32)
        for h in range(kvw // HEAD_DIM):
            dkg = dkg + head(dk_ref, k_ref, kg_ref, d, h, LN2)

        @pl.when(pl.program_id(0) == 0)
        def _():
            dqg_ref[...] = dqg
            dkg_ref[...] = dkg

        @pl.when(pl.program_id(0) > 0)
        def _():
            dqg_ref[...] += dqg
            dkg_ref[...] += dkg

    cw2 = d
    tab = pl.BlockSpec((ts, HEAD_DIM), lambda i: (i, 0))
    vec = pl.BlockSpec((1, HEAD_DIM), lambda i: (0, 0))
    qrow = pl.BlockSpec((ts, d), lambda i: (i, 0))
    krow = pl.BlockSpec((ts, kvw), lambda i: (i, 0))
    window = pl.BlockSpec((pl.Element(ts), pl.Element(d + kvw)), lambda i: (i * ts, cw2))
    return pl.pallas_call(
        body, name="qk_bwd", grid=(s // ts,),
        in_specs=[qrow, krow, pl.BlockSpec((ts, d), lambda i: (i, cw2 // d)),
                  pl.BlockSpec((ts, kvw), lambda i: (i, (cw2 + d) // kvw)), tab, tab, vec, vec, ANY],
        out_specs=[window, vec, vec],
        out_shape=[jax.ShapeDtypeStruct(dz.shape, BF),
                   jax.ShapeDtypeStruct((1, HEAD_DIM), F32), jax.ShapeDtypeStruct((1, HEAD_DIM), F32)],
        input_output_aliases={8: 0},
        compiler_params=_params(("arbitrary",)),
    )(dqt, dkt, z, z, cos, sin, qg, kg, dz)


_NT = (((1,), (1,)), ((), ()))
_TN = (((0,), (0,)), ((), ()))


def _v_col_block(d):
    return (2 * d + d // GROUP) // HEAD_DIM


def _flash_fwd(qt, kt, z, d, tq=None, tk=None, comm=None):
    s = qt.shape[0]
    tq, tk = min(tq or FLASH_TQ_FWD, s), min(tk or FLASH_TK, s)
    ng, nq, nk = d // (GROUP * HEAD_DIM), s // tq, s // tk
    gw = GROUP * HEAD_DIM
    rows = GROUP * tq

    nt = tk // LANES
    assert nk % 2 == 0, (s, tk)

    def body(q_ref, k_ref, v_ref, o_ref, lse_ref, qs, v1, p_s, m_s, acc_s, sc_s):
        @pl.when(pl.program_id(1) == 0)
        def _():
            v1[:, :HEAD_DIM] = v_ref[...]
            v1[:, HEAD_DIM:] = jnp.ones((s, HEAD_DIM), BF)

        for h in range(GROUP):
            qs[h * tq:(h + 1) * tq, :] = q_ref[:, h * HEAD_DIM:(h + 1) * HEAD_DIM]
        m_s[...] = jnp.full((rows, LANES), -1e30, F32)
        acc_s[...] = jnp.zeros((rows, 2 * HEAD_DIM), F32)

        def scores(j):
            return lax.dot_general(qs[...], k_ref[pl.ds(pl.multiple_of(j * tk, tk), tk), :], _NT, preferred_element_type=F32)

        def softmax_pv(j, sc):
            kv_rows = pl.ds(pl.multiple_of(j * tk, tk), tk)
            mt = sc[:, :LANES]
            for c in range(1, nt):
                mt = jnp.maximum(mt, sc[:, c * LANES:(c + 1) * LANES])
            m_old = m_s[...]
            m_new = jnp.maximum(m_old, jnp.max(mt, axis=-1, keepdims=True))
            alpha = jnp.exp2(m_old - m_new)
            for c in range(nt):
                cs = slice(c * LANES, (c + 1) * LANES)
                p_s[:, cs] = jnp.exp2(sc[:, cs] - m_new).astype(BF)
            pv = jnp.dot(p_s[...], v1[kv_rows, :], preferred_element_type=F32)
            acc_s[:, :HEAD_DIM] = alpha * acc_s[:, :HEAD_DIM] + pv[:, :HEAD_DIM]
            acc_s[:, HEAD_DIM:] = alpha * acc_s[:, HEAD_DIM:] + pv[:, HEAD_DIM:]
            m_s[...] = m_new

        sc_s[0] = scores(0)

        def step(jj, carry):
            j = 2 * jj
            sc_s[1] = scores(j + 1)
            softmax_pv(j, sc_s[0])
            sc_s[0] = scores(jnp.minimum(j + 2, nk - 1))
            softmax_pv(j + 1, sc_s[1])
            return carry

        lax.fori_loop(0, nk // 2, step, 0)
        l = acc_s[:, HEAD_DIM:]
        o = acc_s[:, :HEAD_DIM] / l
        for h in range(GROUP):
            o_ref[:, h * HEAD_DIM:(h + 1) * HEAD_DIM] = o[h * tq:(h + 1) * tq, :].astype(BF)
        lse = m_s[...] + jnp.log2(l)
        for h in range(GROUP):
            lse_ref[h] = lse[h * tq:(h + 1) * tq, :]

    vb = _v_col_block(d)
    (o, lse), cres = _call(
        body, name="flash_fwd", grid=(ng, nq),
        in_specs=[pl.BlockSpec((tq, gw), lambda g, i: (i, g)),
                  pl.BlockSpec((s, HEAD_DIM), lambda g, i: (0, g)),
                  pl.BlockSpec((s, HEAD_DIM), lambda g, i: (0, vb + g))],
        out_specs=[pl.BlockSpec((tq, gw), lambda g, i: (i, g)),
                   pl.BlockSpec((GROUP, tq, LANES), lambda g, i: (g, i, 0))],
        out_shape=[jax.ShapeDtypeStruct((s, d), BF), jax.ShapeDtypeStruct((ng * GROUP, s, LANES), F32)],
        scratch_shapes=[pltpu.VMEM((rows, HEAD_DIM), BF), pltpu.VMEM((s, 2 * HEAD_DIM), BF), pltpu.VMEM((rows, tk), BF),
                        pltpu.VMEM((rows, LANES), F32), pltpu.VMEM((rows, 2 * HEAD_DIM), F32), pltpu.VMEM((2, rows, tk), F32)],
        semantics=("parallel", "arbitrary"), args=[qt, kt, z], comm=comm)
    return o, lse, cres


def _flash_bwd(qt, kt, z, o, do, lse, d, dz, tq=None, tk=None, comm=None):
    s = qt.shape[0]
    tq, tk = min(tq or FLASH_TQ_BWD, s), min(tk or FLASH_TK, s)
    ng, nq, nk = d // (GROUP * HEAD_DIM), s // tq, s // tk
    gw = GROUP * HEAD_DIM
    rows = GROUP * tq

    nt = tk // LANES

    def body(q_ref, k_ref, v_ref, o_ref, do_ref, lse_ref, _dz_in, dq_ref, dk_ref, dzv_ref,
             qs, dos, delta_s, dq_s, p_s, ds_s, lse_s, dv_ref):
        i = pl.program_id(1)
        for h in range(GROUP):
            cols = slice(h * HEAD_DIM, (h + 1) * HEAD_DIM)
            lse_s[h * tq:(h + 1) * tq, :] = lse_ref[h]
            qs[h * tq:(h + 1) * tq, :] = q_ref[:, cols]
            dov = do_ref[:, cols]
            dos[h * tq:(h + 1) * tq, :] = dov
            delta = jnp.sum(dov.astype(F32) * o_ref[:, cols].astype(F32), axis=-1, keepdims=True)
            delta_s[h * tq:(h + 1) * tq, :] = jnp.broadcast_to(delta, (tq, LANES))
        dq_s[...] = jnp.zeros((rows, HEAD_DIM), F32)

        @pl.when(i == 0)
        def _():
            dk_ref[...] = jnp.zeros_like(dk_ref)
            dv_ref[...] = jnp.zeros_like(dv_ref)

        def step(j, carry):
            kv_rows = pl.ds(pl.multiple_of(j * tk, tk), tk)
            kv, vv = k_ref[kv_rows, :], v_ref[kv_rows, :]
            sc = lax.dot_general(qs[...], kv, _NT, preferred_element_type=F32)
            dp = lax.dot_general(dos[...], vv, _NT, preferred_element_type=F32)
            lse, delta = lse_s[...], delta_s[...]
            for c in range(nt):
                cs = slice(c * LANES, (c + 1) * LANES)
                p = jnp.exp2(sc[:, cs] - lse)
                p_s[:, cs] = p.astype(BF)
                ds_s[:, cs] = (p * (dp[:, cs] - delta)).astype(BF)
            dv_ref[kv_rows, :] += lax.dot_general(p_s[...], dos[...], _TN, preferred_element_type=F32)
            dk_ref[kv_rows, :] += lax.dot_general(ds_s[...], qs[...], _TN, preferred_element_type=F32)
            dq_s[...] += jnp.dot(ds_s[...], kv, preferred_element_type=F32)
            return carry

        lax.fori_loop(0, nk, step, 0)
        for h in range(GROUP):
            dq_ref[:, h * HEAD_DIM:(h + 1) * HEAD_DIM] = dq_s[h * tq:(h + 1) * tq, :].astype(BF)

        @pl.when(i == nq - 1)
        def _():
            dzv_ref[...] = dv_ref[...].astype(BF)

    vb = _v_col_block(d)
    qspec = pl.BlockSpec((tq, gw), lambda g, i: (i, g))
    kspec = pl.BlockSpec((s, HEAD_DIM), lambda g, i: (0, g))
    vspec = pl.BlockSpec((s, HEAD_DIM), lambda g, i: (0, vb + g))
    (dq, dk, dz), cres = _call(
        body, name="flash_bwd", grid=(ng, nq),
        in_specs=[qspec, kspec, vspec, qspec, qspec, pl.BlockSpec((GROUP, tq, LANES), lambda g, i: (g, i, 0)), ANY],
        out_specs=[qspec, kspec, vspec],
        out_shape=[jax.ShapeDtypeStruct((s, d), BF), jax.ShapeDtypeStruct((s, d // GROUP), F32),
                   jax.ShapeDtypeStruct(dz.shape, BF)],
        scratch_shapes=[pltpu.VMEM((rows, HEAD_DIM), BF), pltpu.VMEM((rows, HEAD_DIM), BF), pltpu.VMEM((rows, LANES), F32),
                        pltpu.VMEM((rows, HEAD_DIM), F32), pltpu.VMEM((rows, tk), BF), pltpu.VMEM((rows, tk), BF),
                        pltpu.VMEM((rows, LANES), F32), pltpu.VMEM((s, HEAD_DIM), F32)],
        semantics=("parallel", "arbitrary"), args=[qt, kt, z, o, do, lse, dz], comm=comm, aliases={6: 2})
    return dq, dk, dz, cres


def _place():
    x, y, c = lax.axis_index("x"), lax.axis_index("y"), lax.axis_index("c")
    other_chips = [(1 - x, y), (x, 1 - y), (1 - x, 1 - y)]
    return x, y, c, other_chips


def _cast_place(name, w, chip_arr, tr=256):
    r, cc = w.shape
    tr = min(tr, r)

    def body(p_ref, w_ref, o_ref):
        o_ref[...] = w_ref[...].astype(BF)

    return pl.pallas_call(
        body, name=name,
        grid_spec=pltpu.PrefetchScalarGridSpec(
            num_scalar_prefetch=1, grid=(r // tr,),
            in_specs=[pl.BlockSpec((tr, cc), lambda i, p_ref: (i, 0))],
            out_specs=pl.BlockSpec((None, tr, cc), lambda i, p_ref: (p_ref[0], i, 0))),
        out_shape=jax.ShapeDtypeStruct((N_CHIPS, r, cc), BF),
        compiler_params=_params(("parallel",)),
    )(chip_arr, w)


def _gather_comm(bufs, short_host=False):
    n = len(bufs)
    pairs = [(w, j) for w in range(n) for j in range(N_CHIPS - 1)]

    def copies(dst, sems):
        send, recv, fsend, frecv = sems
        x, y, c, chips = _place()

        def part(w, chip, core_half):
            h = bufs[w].shape[1] // 2
            return dst[w].at[2 * chip[0] + chip[1], pl.ds(core_half * h, h)]

        def ici(w, j, incoming):
            slab = part(w, chips[j] if incoming else (x, y), c)
            return pltpu.make_async_remote_copy(
                src_ref=slab, dst_ref=slab, send_sem=send.at[3 * w + j], recv_sem=recv.at[3 * w + j],
                device_id=(*chips[j], c), device_id_type=MESH)

        def d2d(w, j, incoming):
            slab = part(w, chips[j], 1 - c if incoming else c)
            return pltpu.make_async_remote_copy(
                src_ref=slab, dst_ref=slab, send_sem=fsend.at[3 * w + j], recv_sem=frecv.at[3 * w + j],
                device_id=(x, y, 1 - c), device_id_type=MESH)

        return ici, d2d

    def first(_, dst, sems):
        ici, _d = copies(dst, sems)
        for w, j in pairs:
            ici(w, j, False).start()

    def middle(_, dst, sems):
        ici, d2d = copies(dst, sems)
        for w, j in pairs:
            ici(w, j, True).wait_recv()
            d2d(w, j, False).start()

    def last(_, dst, sems):
        ici, d2d = copies(dst, sems)
        for w, j in pairs:
            d2d(w, j, True).wait_recv()
        for w, j in pairs:
            ici(w, j, False).wait_send()
            d2d(w, j, False).wait_send()

    def middle_and_last(src, dst, sems):
        middle(src, dst, sems)
        last(src, dst, sems)

    phases = (first, None, middle_and_last) if short_host else (first, middle, last)
    return _Comm(arrays=list(bufs), out_shapes=[jax.ShapeDtypeStruct(b.shape, b.dtype) for b in bufs],
                 aliases={w: w for w in range(n)}, sems=[pltpu.SemaphoreType.DMA((3 * n,))] * 4, phases=phases)


def _run_comm(name, comm):
    nci, nco = len(comm.arrays), len(comm.out_shapes)

    def body(*refs):
        cin, cout, sems = refs[:nci], refs[nci:nci + nco], refs[nci + nco:]
        for fn in comm.phases:
            if fn is not None:
                fn(cin, cout, sems)

    return pl.pallas_call(
        body, name=name, in_specs=[ANY] * nci, out_specs=[ANY] * nco, out_shape=list(comm.out_shapes),
        input_output_aliases=dict(comm.aliases), scratch_shapes=list(comm.sems),
    )(*comm.arrays)


def _pair_comm(grads):
    n = len(grads)

    def copies(src, dst, sems):
        send, recv = sems
        x, y, c, _ = _place()
        out = []
        for w in range(n):
            h = grads[w].shape[1] // 2
            out.append(pltpu.make_async_remote_copy(
                src_ref=src[w].at[:, pl.ds((1 - c) * h, h), :], dst_ref=dst[w],
                send_sem=send.at[w], recv_sem=recv.at[w], device_id=(x, y, 1 - c), device_id_type=MESH))
        return out

    def first(src, dst, sems):
        for cp in copies(src, dst, sems):
            cp.start()

    def last(src, dst, sems):
        for cp in copies(src, dst, sems):
            cp.wait()

    return _Comm(arrays=list(grads),
                 out_shapes=[jax.ShapeDtypeStruct((N_CHIPS, g.shape[1] // 2, g.shape[2]), g.dtype) for g in grads],
                 aliases={}, sems=[pltpu.SemaphoreType.DMA((n,))] * 2, phases=(first, None, last))


def _pair_sum(name, own, got, c_arr, tr=256):
    nc, r, cc = own.shape
    h = r // 2
    tr = min(tr, h)
    nb = h // tr

    def body(c_ref, a_ref, b_ref, o_ref):
        o_ref[...] = (a_ref[...].astype(F32) + b_ref[...].astype(F32)).astype(BF)

    return pl.pallas_call(
        body, name=name,
        grid_spec=pltpu.PrefetchScalarGridSpec(
            num_scalar_prefetch=1, grid=(nc, nb),
            in_specs=[pl.BlockSpec((None, tr, cc), lambda s, i, c_ref: (s, c_ref[0] * nb + i, 0)),
                      pl.BlockSpec((None, tr, cc), lambda s, i, c_ref: (s, i, 0))],
            out_specs=pl.BlockSpec((None, tr, cc), lambda s, i, c_ref: (s, i, 0))),
        out_shape=jax.ShapeDtypeStruct((nc, h, cc), BF),
        compiler_params=_params(("parallel", "parallel")),
    )(c_arr, own, got)


def _chip_comm(parts):
    n = len(parts)

    def copies(src, dst, sems):
        send, recv = sems
        _, _, c, chips = _place()
        return [pltpu.make_async_remote_copy(
            src_ref=src[w].at[2 * chip[0] + chip[1]], dst_ref=dst[w].at[j],
            send_sem=send.at[3 * w + j], recv_sem=recv.at[3 * w + j], device_id=(*chip, c), device_id_type=MESH)
            for w in range(n) for j, chip in enumerate(chips)]

    def first(src, dst, sems):
        for cp in copies(src, dst, sems):
            cp.start()

    def last(src, dst, sems):
        for cp in copies(src, dst, sems):
            cp.wait()

    return _Comm(arrays=list(parts), out_shapes=[jax.ShapeDtypeStruct((N_CHIPS - 1,) + p.shape[1:], p.dtype) for p in parts],
                 aliases={}, sems=[pltpu.SemaphoreType.DMA((3 * n,))] * 2, phases=(first, None, last))


def _chip_sum(name, parts, got, chip_arr, c_arr, tr=256):
    _, h, cc = parts.shape
    tr = min(tr, h)
    nb = h // tr

    def body(chip_ref, c_ref, own_ref, got_ref, o_ref):
        acc = own_ref[...].astype(F32)
        for k in range(N_CHIPS - 1):
            acc = acc + got_ref[k].astype(F32)
        o_ref[...] = acc

    return pl.pallas_call(
        body, name=name,
        grid_spec=pltpu.PrefetchScalarGridSpec(
            num_scalar_prefetch=2, grid=(nb,),
            in_specs=[pl.BlockSpec((None, tr, cc), lambda i, chip_ref, c_ref: (chip_ref[0], i, 0)),
                      pl.BlockSpec((N_CHIPS - 1, tr, cc), lambda i, chip_ref, c_ref: (0, i, 0))],
            out_specs=pl.BlockSpec((tr, cc), lambda i, chip_ref, c_ref: (c_ref[0] * nb + i, 0))),
        out_shape=jax.ShapeDtypeStruct((2 * h, cc), F32),
        compiler_params=_params(("parallel",)),
    )(chip_arr, c_arr, parts, got)


def _pair_gather_comm(bufs):
    n = len(bufs)

    def copy(dst, sems, w, core_half):
        send, recv = sems
        x, y, c, _ = _place()
        h = bufs[w].shape[0] // 2
        rows = dst[w].at[pl.ds((1 - c if core_half == "theirs" else c) * h, h)]
        return pltpu.make_async_remote_copy(src_ref=rows, dst_ref=rows, send_sem=send.at[w], recv_sem=recv.at[w],
                                            device_id=(x, y, 1 - c), device_id_type=MESH)

    def first(_, dst, sems):
        for w in range(n):
            copy(dst, sems, w, "mine").start()

    def last(_, dst, sems):
        for w in range(n):
            copy(dst, sems, w, "theirs").wait_recv()
        for w in range(n):
            copy(dst, sems, w, "mine").wait_send()

    return _Comm(arrays=list(bufs), out_shapes=[jax.ShapeDtypeStruct(b.shape, b.dtype) for b in bufs],
                 aliases={w: w for w in range(n)}, sems=[pltpu.SemaphoreType.DMA((n,))] * 2, phases=(first, None, last))


def _merge_comms(a, b):
    nai, nao, nas = len(a.arrays), len(a.out_shapes), len(a.sems)

    def both(fa, fb):
        if fa is None and fb is None:
            return None

        def phase(cin, cout, sems):
            if fa is not None:
                fa(cin[:nai], cout[:nao], sems[:nas])
            if fb is not None:
                fb(cin[nai:], cout[nao:], sems[nas:])
        return phase

    return _Comm(arrays=a.arrays + b.arrays, out_shapes=a.out_shapes + b.out_shapes,
                 aliases={**a.aliases, **{nai + k: nao + v for k, v in b.aliases.items()}},
                 sems=a.sems + b.sems, phases=tuple(both(fa, fb) for fa, fb in zip(a.phases, b.phases)))


def _all_sum_small(name, v):
    p = v.shape[0]

    def body(v_ref, o_ref, slots, send, recv):
        x, y, c, _ = _place()
        me = 4 * x + 2 * y + c
        copies = []
        for k in range(1, N_DEV):
            peer = (x ^ (k >> 2), y ^ ((k >> 1) & 1), c ^ (k & 1))
            copies.append(pltpu.make_async_remote_copy(
                src_ref=v_ref, dst_ref=slots.at[me], send_sem=send.at[k - 1], recv_sem=recv.at[k - 1],
                device_id=peer, device_id_type=MESH))
        for cp in copies:
            cp.start()
        slots[me] = v_ref[...]
        for cp in copies:
            cp.wait()
        acc = slots[0]
        for s in range(1, N_DEV):
            acc = acc + slots[s]
        o_ref[...] = acc

    vm = pl.BlockSpec(memory_space=pltpu.VMEM)
    return pl.pallas_call(
        body, name=name,
        in_specs=[vm], out_specs=vm,
        out_shape=jax.ShapeDtypeStruct(v.shape, F32),
        scratch_shapes=[pltpu.VMEM((N_DEV, p, LANES), F32), pltpu.SemaphoreType.DMA((N_DEV - 1,)),
                        pltpu.SemaphoreType.DMA((N_DEV - 1,))],
    )(v)


def _adamw(name, w, g, m, v, tr=256):
    r, c = w.shape
    tr = min(tr, r)
    assert r % tr == 0
    bc1 = 1.0 - ADAM_B1 ** ADAM_STEP
    bc2 = 1.0 - ADAM_B2 ** ADAM_STEP

    def body(w_ref, g_ref, m_ref, v_ref, d_ref, nm_ref, nv_ref):
        gv = g_ref[...]
        nm = ADAM_B1 * m_ref[...] + (1.0 - ADAM_B1) * gv
        nv = ADAM_B2 * v_ref[...] + (1.0 - ADAM_B2) * (gv * gv)
        nm_ref[...] = nm
        nv_ref[...] = nv
        d_ref[...] = -ADAM_LR * ((nm / bc1) / (jnp.sqrt(nv / bc2) + ADAM_EPS) + ADAM_WD * w_ref[...])

    blk = pl.BlockSpec((tr, c), lambda i: (i, 0))
    return pl.pallas_call(
        body, name=name, grid=(r // tr,),
        in_specs=[blk] * 4, out_specs=[blk] * 3,
        out_shape=[jax.ShapeDtypeStruct((r, c), F32)] * 3,
        compiler_params=_params(("parallel",)),
    )(w, g, m, v)


def _pack_small(parts):
    flat = jnp.concatenate([a.reshape(-1) for a in parts])
    n = flat.shape[0]
    p = -(-n // (8 * LANES)) * 8
    packed = jnp.pad(flat, (0, p * LANES - n)).reshape(p, LANES)

    def unpack(q):
        out, off = [], 0
        f = q.reshape(-1)
        for a in parts:
            out.append(f[off:off + a.size].reshape(a.shape))
            off += a.size
        return out

    return packed, unpack


def kernel(x, p, norm_mix, w_in, w_dw, conv_ln_g, conv_ln_b, w_conv_proj, q_norm, k_norm, w_attn_proj, w_out, norm_ffn, w_ff1, w_ff2, norm_ple, w_ple_gate, w_ple_proj, norm_final, loss_target, m_norm_mix, m_w_in, m_w_dw, m_conv_ln_g, m_conv_ln_b, m_w_conv_proj, m_q_norm, m_k_norm, m_w_attn_proj, m_w_out, m_norm_ffn, m_w_ff1, m_w_ff2, m_norm_ple, m_w_ple_gate, m_w_ple_proj, m_norm_final, v_norm_mix, v_w_in, v_w_dw, v_conv_ln_g, v_conv_ln_b, v_w_conv_proj, v_q_norm, v_k_norm, v_w_attn_proj, v_w_out, v_norm_ffn, v_w_ff1, v_w_ff2, v_norm_ple, v_w_ple_gate, v_w_ple_proj, v_norm_final):
    s, d = x.shape[1], x.shape[2]
    cw = d // 2
    kvw = d // GROUP
    xs, ps, tgt = x[0], p[0, 0], loss_target[0]
    cx, cy, cc = lax.axis_index("x"), lax.axis_index("y"), lax.axis_index("c")
    chip = 2 * cx + cy
    c_arr = jnp.reshape(cc, (1,)).astype(jnp.int32)
    tm, tme = min(MM_TM, s), min(MM_TM_EPI, s)

    names = ["w_in", "w_conv_proj", "w_attn_proj", "w_out", "w_ff1", "w_ff2", "w_ple_gate", "w_ple_proj"]
    big = [w_in, w_conv_proj, w_attn_proj, w_out, w_ff1, w_ff2, w_ple_gate, w_ple_proj]
    chip_arr = jnp.reshape(chip, (1,)).astype(jnp.int32)
    placed = [_cast_place("cast_" + nm, w[0], chip_arr) for nm, w in zip(names, big)]
    cpc = cw // N_CHIPS
    taps_rows = 32
    my_taps = jnp.pad(w_dw[0], ((0, taps_rows - CONV_KERNEL), (0, 0)))[None]
    taps_buf = lax.dynamic_update_slice(jnp.zeros((N_CHIPS, taps_rows, cpc), F32), my_taps, (chip, 0, 0))
    h0, (win, taps_all) = _rms_fwd("rms_mix", xs, norm_mix, comm=_gather_comm([placed[0], taps_buf], short_host=True))
    wdw = taps_all.transpose(1, 0, 2).reshape(taps_rows, cw)

    cos, sin = _rope_tables(s)
    (z,) = _mm("z_proj", h0, win, b_cm=True, tm=tm, tn=win.shape[2] // 3, tk=d)
    uc, act = _conv_fwd(z, wdw, conv_ln_g, conv_ln_b, cw)
    qt, kt = _qk_fwd(z, cos, sin, q_norm, k_norm, d)
    o, lse, (wcp, wap, wout, w1, w2, wpg, wple) = _flash_fwd(qt, kt, z, d, comm=_gather_comm(placed[1:]))
    wap, wout, w2, wpg = (t.reshape(-1, t.shape[-1]) for t in (wap, wout, w2, wpg))
    (y_c,) = _mm("conv_proj", act, wcp, b_cm=True, tm=tm, tn=wcp.shape[2], tk=cw, out_dtypes=(F32,))
    tn = d // 2
    gcb = (2 * d + 2 * kvw) // tn

    def merge_epi(acc, yc, gc, ga):
        return acc, _sigmoid(gc.astype(F32)) * yc + _sigmoid(ga.astype(F32)) * acc

    y_a, merged = _mm("attn_proj", o, wap, tm=tme, tn=tn, tk=d, epi=merge_epi, out_dtypes=(BF, BF), b_resident=True,
                      extras=[_tile_extra(y_c, tme, tn), _tile_extra(z, tme, tn, gcb), _tile_extra(z, tme, tn, gcb + 2)])
    (x1,) = _mm("out_proj", merged, wout, tm=tm, tn=tn, tk=d, epi=lambda acc, r: (r + acc,), out_dtypes=(F32,),
                extras=[_tile_extra(xs, tm, tn)])
    h1 = _rms_fwd("rms_ffn", x1, norm_ffn)
    (a,) = _mm("ff1", h1, w1, b_cm=True, tm=tm, tn=tn, tk=d)

    def relu2(t):
        return jnp.square(jnp.maximum(t, 0.0))

    (x2,) = _mm("ff2", a, w2, tm=tm, tn=tn, tk=d, a_fn=relu2, epi=lambda acc, r: (r + acc,), out_dtypes=(F32,),
                extras=[_tile_extra(x1, tm, tn)])
    h2 = _rms_fwd("rms_ple", x2, norm_ple)
    to_bf = lambda t: t.astype(BF)
    (e,) = _mm("ple_proj", ps, wple, b_cm=True, tm=tm, tn=wple.shape[2], tk=ps.shape[1], a_fn=to_bf)

    def ple_epi(acc, ev, r):
        gt = _sigmoid(acc)
        return r + gt * ev.astype(F32), gt

    x3, gate = _mm("ple_gate", h2, wpg, tm=tme, tn=tn, tk=d, epi=ple_epi, out_dtypes=(F32, BF), b_resident=True,
                   extras=[_tile_extra(e, tme, tn), _tile_extra(x2, tme, tn)])

    dx3, de, dgp, sq, d_fin = _loss_bwd(x3, tgt, norm_final.reshape(1, d), e, gate)
    tkt = min(2048, s)
    (g_wple,) = _mm("d_wple", ps, de, ta=True, out_cm=True, tm=ps.shape[1], tn=wple.shape[2], tk=tkt, a_fn=to_bf)
    (g_wpg,) = _mm("d_wpg", h2, dgp, ta=True, tm=tm, tn=tn, tk=tkt)
    (dh2,) = _mm("d_h2", dgp, wpg, tb=True, tm=tm, tn=tn, tk=d)
    dx2, dx2b, d_ple = _rms_bwd("rms_ple_bwd", dh2, x2, norm_ple, dx3)

    (da,) = _mm("d_a", dx2b, w2, tb=True, tm=tm, tn=tn, tk=d, out_dtypes=(BF,),
                epi=lambda acc, av: (acc * (2.0 * jnp.maximum(av.astype(F32), 0.0)),), extras=[_tile_extra(a, tm, tn)])
    (g_w2,) = _mm("d_w2", a, dx2b, ta=True, tm=tm, tn=tn, tk=tkt, a_fn=relu2)
    (g_w1,) = _mm("d_w1", h1, da, ta=True, out_cm=True, tm=tm, tn=tn, tk=tkt)
    (dh1,) = _mm("d_h1", da, w1, tb=True, b_cm=True, tm=tm, tn=tn, tk=w1.shape[2])
    dx1, dx1b, d_ffn = _rms_bwd("rms_ffn_bwd", dh1, x1, norm_ffn, dx2)

    def merge_bwd(acc, gc, ga, yc, ya):
        sc, sa = _sigmoid(gc.astype(F32)), _sigmoid(ga.astype(F32))
        return acc * sc, acc * sa, jnp.concatenate(
            [acc * yc * sc * (1.0 - sc), acc * ya.astype(F32) * sa * (1.0 - sa)], axis=1)

    tmd = min(MM_TM_DZ, s)
    gate0 = 2 * d + 2 * kvw
    z_cols = z.shape[1]

    def gate_window(width, col0):
        return (pl.Element(tmd), pl.Element(width)), lambda i, j, k: (i * tmd, col0)

    dy_c, dy_a, dz = _mm(
        "d_merged", dx1b, wout, tb=True, tm=tmd, tn=d, tk=d, epi=merge_bwd, out_dtypes=(BF, BF, BF), b_resident=True,
        extras=[(z, *gate_window(d, gate0)), (z, *gate_window(d, gate0 + d)), _tile_extra(y_c, tmd, d),
                _tile_extra(y_a, tmd, d)],
        out_overrides={2: ((s, z_cols), pl.BlockSpec(*gate_window(2 * d, gate0)))})
    (g_wout,) = _mm("d_wout", merged, dx1b, ta=True, tm=tm, tn=tn, tk=tkt)
    (g_wap,) = _mm("d_wap", o, dy_a, ta=True, tm=tm, tn=tn, tk=tkt)

    def slabs(g):
        return g if g.ndim == 3 else g.reshape(N_CHIPS, g.shape[0] // N_CHIPS, g.shape[1])

    grads_a = [slabs(g) for g in (g_wap, g_wout, g_w1, g_w2, g_wpg, g_wple)]
    (do,), got_a = _mm("d_o", dy_a, wap, tb=True, tm=tm, tn=tn, tk=d, comm=_pair_comm(grads_a))
    parts_a = [_pair_sum("pair_sum_" + nm, g, r, c_arr) for nm, g, r in zip(names[2:], grads_a, got_a)]
    dqt, dkt, dz, _ = _flash_bwd(qt, kt, z, o, do, lse, d, dz)
    dz, d_qn, d_kn = _qk_bwd(dqt, dkt, z, cos, sin, q_norm, k_norm, d, dz)
    (g_wcp,) = _mm("d_wcp", act, dy_c, ta=True, out_cm=True, tm=cw, tn=wcp.shape[2], tk=tkt)
    (dact,) = _mm("d_act", dy_c, wcp, tb=True, b_cm=True, tm=tm, tn=cw, tk=wcp.shape[2])
    p_wap, p_wout, p_w1, p_w2, p_wpg, p_wple = parts_a
    dz, d_taps, d_lng, d_lnb, (s_wap, s_wout, s_wpg, s_wple) = _conv_bwd(
        dact, uc, z, wdw, conv_ln_g, conv_ln_b, cw, dz, comm=_chip_comm([p_wap, p_wout, p_wpg, p_wple]))
    (g_win,), (s_w1, s_w2) = _mm("d_win", h0, dz, ta=True, out_cm=True, tm=tm, tn=win.shape[2] // 3, tk=tkt,
                                 comm=_chip_comm([p_w1, p_w2]))
    slots_a = [s_wap, s_wout, s_w1, s_w2, s_wpg, s_wple]
    grads_b = [slabs(g_win), slabs(g_wcp)]
    got_b = _run_comm("grad_pair_exchange_b", _pair_comm(grads_b))
    parts_b = [_pair_sum("pair_sum_" + nm, g, r, c_arr) for nm, g, r in zip(names[:2], grads_b, got_b)]
    halves_a = [_chip_sum("chip_sum_" + nm, cp, sl, chip_arr, c_arr) for nm, cp, sl in zip(names[2:], parts_a, slots_a)]
    (dh0,), hosted = _mm("d_h0", dz, win, tb=True, b_cm=True, tm=tm, tn=tn, tk=win.shape[2],
                         comm=_merge_comms(_chip_comm(parts_b), _pair_gather_comm(halves_a)))
    slots_b, grads_a_done = hosted[:2], hosted[2:]
    dx, _, d_mix = _rms_bwd("rms_mix_bwd", dh0, xs, norm_mix, dx1)
    halves_b = [_chip_sum("chip_sum_" + nm, cp, sl, chip_arr, c_arr) for nm, cp, sl in zip(names[:2], parts_b, slots_b)]
    big_grads = list(_run_comm("grad_pair_gather_b", _pair_gather_comm(halves_b))) + list(grads_a_done)

    small = [d_mix, d_taps[:CONV_KERNEL], d_lng, d_lnb, d_qn, d_kn, d_ffn, d_ple, d_fin]
    packed, unpack = _pack_small(small)
    g_mix, g_taps, g_lng, g_lnb, g_qn, g_kn, g_ffn, g_ple, g_fin = unpack(_all_sum_small("reduce_small", packed))
    g_dw = lax.dynamic_slice_in_dim(g_taps.reshape(CONV_KERNEL, N_CHIPS, cpc), chip, 1, axis=1).reshape(1, CONV_KERNEL, cpc)

    sq_local = lax.reduce_precision(sq[0, 0], 8, 23)
    loss = (0.5 / d) * lax.psum(sq_local, ("x", "y", "c"))

    grads = {
        "norm_mix": g_mix, "w_in": big_grads[0][None], "w_dw": g_dw, "conv_ln_g": g_lng, "conv_ln_b": g_lnb,
        "w_conv_proj": big_grads[1][None], "q_norm": g_qn, "k_norm": g_kn, "w_attn_proj": big_grads[2][None],
        "w_out": big_grads[3][None], "norm_ffn": g_ffn, "w_ff1": big_grads[4][None], "w_ff2": big_grads[5][None],
        "norm_ple": g_ple, "w_ple_gate": big_grads[6][None], "w_ple_proj": big_grads[7][None],
        "norm_final": g_fin.reshape(d),
    }
    weights = dict(norm_mix=norm_mix, w_in=w_in, w_dw=w_dw, conv_ln_g=conv_ln_g, conv_ln_b=conv_ln_b, w_conv_proj=w_conv_proj,
                   q_norm=q_norm, k_norm=k_norm, w_attn_proj=w_attn_proj, w_out=w_out, norm_ffn=norm_ffn, w_ff1=w_ff1,
                   w_ff2=w_ff2, norm_ple=norm_ple, w_ple_gate=w_ple_gate, w_ple_proj=w_ple_proj, norm_final=norm_final)
    m_in = dict(norm_mix=m_norm_mix, w_in=m_w_in, w_dw=m_w_dw, conv_ln_g=m_conv_ln_g, conv_ln_b=m_conv_ln_b,
                w_conv_proj=m_w_conv_proj, q_norm=m_q_norm, k_norm=m_k_norm, w_attn_proj=m_w_attn_proj, w_out=m_w_out,
                norm_ffn=m_norm_ffn, w_ff1=m_w_ff1, w_ff2=m_w_ff2, norm_ple=m_norm_ple, w_ple_gate=m_w_ple_gate,
                w_ple_proj=m_w_ple_proj, norm_final=m_norm_final)
    v_in = dict(norm_mix=v_norm_mix, w_in=v_w_in, w_dw=v_w_dw, conv_ln_g=v_conv_ln_g, conv_ln_b=v_conv_ln_b,
                w_conv_proj=v_w_conv_proj, q_norm=v_q_norm, k_norm=v_k_norm, w_attn_proj=v_w_attn_proj, w_out=v_w_out,
                norm_ffn=v_norm_ffn, w_ff1=v_w_ff1, w_ff2=v_w_ff2, norm_ple=v_norm_ple, w_ple_gate=v_w_ple_gate,
                w_ple_proj=v_w_ple_proj, norm_final=v_norm_final)
    order = list(weights)
    deltas, new_m, new_v, g_out = [], [], [], []
    for nm in order:
        w = weights[nm]
        shape = w.shape
        two_d = (-1, shape[-1])
        dl, mm_, vv_ = _adamw("adamw_" + nm, w.reshape(two_d), grads[nm].reshape(two_d), m_in[nm].reshape(two_d),
                              v_in[nm].reshape(two_d))
        g_out.append(grads[nm].reshape(shape))
        deltas.append(dl.reshape(shape))
        new_m.append(mm_.reshape(shape))
        new_v.append(vv_.reshape(shape))
    return (loss, dx[None], *g_out, *deltas, *new_m, *new_v)
```

```python
from typing import NamedTuple

import jax
import jax.numpy as jnp
from jax import lax
from jax.experimental import pallas as pl
from jax.experimental.pallas import tpu as pltpu

F32 = jnp.float32
BF = jnp.bfloat16

EPS = 1e-6
HEAD_DIM = 128
GROUP = 4
GRID_W = 64
ROPE_THETA = 10000.0
CONV_KERNEL = 31
HALO = 16
N_CHIPS = 4
N_DEV = 8
LANES = 128

ADAM_LR = 0.001
ADAM_B1 = 0.9
ADAM_B2 = 0.999
ADAM_EPS = 1e-08
ADAM_WD = 0.01
ADAM_STEP = 10

VMEM_LIMIT = 56 * 2 ** 20
LOG2E = 1.4426950408889634
LN2 = 0.6931471805599453
Q_SCALE = HEAD_DIM ** -0.5 * LOG2E
ROW_TILE = 256
FLASH_TQ_FWD = 512
FLASH_TQ_BWD = 512
FLASH_TK = 512
MM_TM = 1024
MM_TM_EPI = 512
MM_TM_DZ = 256
MESH = pl.DeviceIdType.MESH
ANY = pl.BlockSpec(memory_space=pl.ANY)


def _params(sem):
    return pltpu.CompilerParams(dimension_semantics=sem, vmem_limit_bytes=VMEM_LIMIT)


def _sigmoid(x):
    return 1.0 / (1.0 + jnp.exp(-x))


class _Comm(NamedTuple):
    arrays: list
    out_shapes: list
    aliases: dict
    sems: list
    phases: tuple


def _call(body, *, name, grid, in_specs, out_specs, out_shape, scratch_shapes, semantics, args, comm=None, aliases=None):
    n_in, n_out = len(in_specs), len(out_specs)
    aliases = dict(aliases or {})
    if comm is None:
        res = pl.pallas_call(body, name=name, grid=grid, in_specs=in_specs, out_specs=out_specs, out_shape=out_shape,
                             scratch_shapes=scratch_shapes, input_output_aliases=aliases,
                             compiler_params=_params(semantics))(*args)
        return res, []
    nci, nco, ncs = len(comm.arrays), len(comm.out_shapes), len(comm.sems)
    n_steps = 1
    for g in grid:
        n_steps *= g
    first, middle, last = comm.phases

    def hosted(*refs):
        ins, cin = refs[:n_in], refs[n_in:n_in + nci]
        outs = refs[n_in + nci:n_in + nci + n_out]
        cout = refs[n_in + nci + n_out:n_in + nci + n_out + nco]
        rest = refs[n_in + nci + n_out + nco:]
        scratch, sems = rest[:len(rest) - ncs], rest[len(rest) - ncs:]
        step = 0
        for ax, g in enumerate(grid):
            step = step * g + pl.program_id(ax)
        for at, fn in ((0, first), (n_steps // 2, middle)):
            if fn is not None:
                pl.when(step == at)(lambda fn=fn: fn(cin, cout, sems))
        body(*ins, *outs, *scratch)
        if last is not None:
            pl.when(step == n_steps - 1)(lambda: last(cin, cout, sems))

    res = pl.pallas_call(
        hosted, name=name, grid=grid,
        in_specs=list(in_specs) + [ANY] * nci, out_specs=list(out_specs) + [ANY] * nco,
        out_shape=list(out_shape) + list(comm.out_shapes),
        input_output_aliases={**aliases, **{n_in + a: n_out + b for a, b in comm.aliases.items()}},
        scratch_shapes=list(scratch_shapes) + list(comm.sems),
        compiler_params=_params(("arbitrary",) * len(grid)),
    )(*args, *comm.arrays)
    return res[:n_out], res[n_out:]


def _mm(name, a, b, *, tm, tn, tk, ta=False, tb=False, b_cm=False, out_cm=False,
        a_fn=None, extras=(), epi=None, out_dtypes=(BF,), epi_rows=256, comm=None, b_resident=False,
        out_overrides=None):
    if ta:
        kc, m = a.shape
    else:
        m, kc = a.shape
    if b_cm:
        nc, r, c = b.shape
        n, per = (r, c) if tb else (nc * c, c)
    else:
        n = b.shape[0] if tb else b.shape[1]
    tm, tn, tk = min(tm, m), min(tn, n), min(tk, kc)
    assert m % tm == 0 and n % tn == 0 and kc % tk == 0, (name, m, n, kc, tm, tn, tk)
    nk = kc // tk
    a_spec = pl.BlockSpec((tk, tm), lambda i, j, k: (k, i)) if ta else pl.BlockSpec((tm, tk), lambda i, j, k: (i, k))
    if b_cm and not tb:
        assert per % tn == 0
        npj = per // tn
        b_spec = pl.BlockSpec((None, tk, tn), lambda i, j, k: (j // npj, k, j % npj))
    elif b_cm:
        assert per % tk == 0
        npk = per // tk
        b_spec = pl.BlockSpec((None, tn, tk), lambda i, j, k: (k // npk, j, k % npk))
    elif b_resident:
        assert nk == 1
        b_spec = pl.BlockSpec(b.shape, lambda i, j, k: (0, 0))
    elif tb:
        b_spec = pl.BlockSpec((tn, tk), lambda i, j, k: (j, k))
    else:
        b_spec = pl.BlockSpec((tk, tn), lambda i, j, k: (k, j))
    if out_cm:
        assert (n // N_CHIPS) % tn == 0
        npo = (n // N_CHIPS) // tn
        o_spec = pl.BlockSpec((None, tm, tn), lambda i, j, k: (j // npo, i, j % npo))
        o_shape = (N_CHIPS, m, n // N_CHIPS)
    else:
        o_spec = pl.BlockSpec((tm, tn), lambda i, j, k: (i, j))
        o_shape = (m, n)
    ne, no = len(extras), len(out_dtypes)
    whole = [len(e) > 3 and e[3] for e in extras]
    extras = [e[:3] for e in extras]

    def epi_args(ex, rows):
        return [e[...] if w else e[rows, :] for e, w in zip(ex, whole)]

    dims = (((0 if ta else 1,), (1 if tb else 0,)), ((), ()))
    er = min(epi_rows, tm)
    chunked = nk == 1 and epi is not None and not ta
    use_acc = (nk > 1 or epi is not None) and not chunked
    assert chunked or not b_resident

    def body(*refs):
        a_ref, b_ref = refs[0], refs[1]
        ex = refs[2:2 + ne]
        outs = refs[2 + ne:2 + ne + no]
        if chunked:
            if b_resident:
                cols = pl.ds(pl.multiple_of(pl.program_id(1) * tn, tn), tn)
                bt = b_ref[cols, :] if tb else b_ref[:, cols]
            else:
                bt = b_ref[...]
            for r0 in range(0, tm, er):
                rows = slice(r0, r0 + er)
                at = a_ref[rows, :]
                if a_fn is not None:
                    at = a_fn(at)
                d = lax.dot_general(at, bt, dims, preferred_element_type=F32)
                vals = epi(d, *epi_args(ex, rows))
                for o, v, dt in zip(outs, vals, out_dtypes):
                    o[rows, :] = v.astype(dt)
            return
        at = a_ref[...]
        if a_fn is not None:
            at = a_fn(at)
        d = lax.dot_general(at, b_ref[...], dims, preferred_element_type=F32)
        if not use_acc:
            outs[0][...] = d.astype(out_dtypes[0])
            return
        acc = refs[-1]
        k = pl.program_id(2)

        @pl.when(k == 0)
        def _():
            acc[...] = d

        if nk > 1:
            @pl.when(k > 0)
            def _():
                acc[...] += d

        @pl.when(k == nk - 1)
        def _():
            for r0 in range(0, tm, er):
                rows = slice(r0, r0 + er)
                if epi is None:
                    vals = (acc[rows, :],)
                else:
                    vals = epi(acc[rows, :], *epi_args(ex, rows))
                for o, v, dt in zip(outs, vals, out_dtypes):
                    o[rows, :] = v.astype(dt)

    out_specs = [o_spec] * no
    out_shapes = [jax.ShapeDtypeStruct(o_shape, dt) for dt in out_dtypes]
    for idx, (shape, spec) in (out_overrides or {}).items():
        out_specs[idx], out_shapes[idx] = spec, jax.ShapeDtypeStruct(shape, out_dtypes[idx])
    res, cres = _call(
        body, name=name, grid=(m // tm, n // tn, nk),
        in_specs=[a_spec, b_spec] + [pl.BlockSpec(bs, im) for _, bs, im in extras],
        out_specs=out_specs,
        out_shape=out_shapes,
        scratch_shapes=[pltpu.VMEM((tm, tn), F32)] if use_acc else [],
        semantics=("parallel", "parallel", "arbitrary"),
        args=[a, b] + [e for e, _, _ in extras], comm=comm)
    return res if comm is None else (res, cres)


def _tile_extra(arr, tm, tn, col_block0=0):
    return (arr, (tm, tn), lambda i, j, k: (i, j + col_block0))


def _rms_fwd(name, x, g, ts=None, comm=None):
    s, d = x.shape
    ts = ts or ROW_TILE

    def body(x_ref, g_ref, h_ref):
        xv = x_ref[...]
        r = lax.rsqrt(jnp.mean(xv * xv, axis=-1, keepdims=True) + EPS)
        h_ref[...] = (xv * r * g_ref[...]).astype(BF)

    (h,), cres = _call(
        body, name=name, grid=(s // ts,),
        in_specs=[pl.BlockSpec((ts, d), lambda i: (i, 0)), pl.BlockSpec((1, d), lambda i: (0, 0))],
        out_specs=[pl.BlockSpec((ts, d), lambda i: (i, 0))],
        out_shape=[jax.ShapeDtypeStruct((s, d), BF)],
        scratch_shapes=[], semantics=("parallel",), args=[x, g], comm=comm)
    return h if comm is None else (h, cres)


def _rms_bwd(name, dh, x, g, dres, ts=None):
    s, d = x.shape
    ts = ts or ROW_TILE

    def body(dh_ref, x_ref, g_ref, dres_ref, dx_ref, dxb_ref, dg_ref):
        xv = x_ref[...]
        dhv = dh_ref[...].astype(F32)
        r = lax.rsqrt(jnp.mean(xv * xv, axis=-1, keepdims=True) + EPS)
        nrm = xv * r
        dn = dhv * g_ref[...]
        dx = dres_ref[...] + r * (dn - nrm * jnp.mean(dn * nrm, axis=-1, keepdims=True))
        dx_ref[...] = dx
        dxb_ref[...] = dx.astype(BF)
        part = jnp.sum(dhv * nrm, axis=0, keepdims=True)

        @pl.when(pl.program_id(0) == 0)
        def _():
            dg_ref[...] = part

        @pl.when(pl.program_id(0) > 0)
        def _():
            dg_ref[...] += part

    row = pl.BlockSpec((ts, d), lambda i: (i, 0))
    vec = pl.BlockSpec((1, d), lambda i: (0, 0))
    return pl.pallas_call(
        body, name=name, grid=(s // ts,),
        in_specs=[row, row, vec, row],
        out_specs=[row, row, vec],
        out_shape=[jax.ShapeDtypeStruct((s, d), F32), jax.ShapeDtypeStruct((s, d), BF), jax.ShapeDtypeStruct((1, d), F32)],
        compiler_params=_params(("arbitrary",)),
    )(dh, x, g, dres)


def _loss_bwd(x3, tgt, gfin, e, gate, ts=None):
    s, d = x3.shape
    ts = ts or ROW_TILE

    def body(x_ref, t_ref, g_ref, e_ref, gate_ref, dx_ref, de_ref, dgp_ref, sq_ref, dg_ref):
        xv = x_ref[...]
        gv = g_ref[...]
        r = lax.rsqrt(jnp.mean(xv * xv, axis=-1, keepdims=True) + EPS)
        nrm = xv * r
        err = nrm * gv - t_ref[...]
        dy = err * (1.0 / d)
        dn = dy * gv
        dx = r * (dn - nrm * jnp.mean(dn * nrm, axis=-1, keepdims=True))
        dx_ref[...] = dx
        ev = e_ref[...].astype(F32)
        gt = gate_ref[...].astype(F32)
        de_ref[...] = (dx * gt).astype(BF)
        dgp_ref[...] = (dx * ev * gt * (1.0 - gt)).astype(BF)
        sq = jnp.full((8, LANES), jnp.sum(err * err), F32)
        part = jnp.sum(dy * nrm, axis=0, keepdims=True)

        @pl.when(pl.program_id(0) == 0)
        def _():
            sq_ref[...] = sq
            dg_ref[...] = part

        @pl.when(pl.program_id(0) > 0)
        def _():
            sq_ref[...] += sq
            dg_ref[...] += part

    row = pl.BlockSpec((ts, d), lambda i: (i, 0))
    vec = pl.BlockSpec((1, d), lambda i: (0, 0))
    return pl.pallas_call(
        body, name="loss_bwd", grid=(s // ts,),
        in_specs=[row, row, vec, row, row],
        out_specs=[row, row, row, pl.BlockSpec((8, LANES), lambda i: (0, 0)), vec],
        out_shape=[jax.ShapeDtypeStruct((s, d), F32), jax.ShapeDtypeStruct((s, d), BF), jax.ShapeDtypeStruct((s, d), BF),
                   jax.ShapeDtypeStruct((8, LANES), F32), jax.ShapeDtypeStruct((1, d), F32)],
        compiler_params=_params(("arbitrary",)),
    )(x3, tgt, gfin, e, gate)


def _halo_specs(ts, s, width, col_block):
    per = ts // HALO
    last = s // HALO - 1
    return [
        pl.BlockSpec((HALO, width), lambda i: (jnp.maximum(i * per - 1, 0), col_block)),
        pl.BlockSpec((ts, width), lambda i: (i, col_block)),
        pl.BlockSpec((HALO, width), lambda i: (jnp.minimum((i + 1) * per, last), col_block)),
    ]


def _glu_ext(zp, zc, zn, ext, cw, ts, i, n_tiles):
    def glu(zr):
        zv = zr[...].astype(F32)
        return zv[:, :cw] * _sigmoid(zv[:, cw:])

    ext[0:HALO, :] = jnp.where(i > 0, glu(zp), 0.0)
    ext[HALO:HALO + ts, :] = glu(zc)
    ext[HALO + ts:, :] = jnp.where(i < n_tiles - 1, glu(zn), 0.0)


SUBLANES = 8


def _shift_scratch(ts):
    return pltpu.VMEM((SUBLANES, ts + 2 * HALO - SUBLANES, LANES), F32)


def _shifted_copies(ext, sh, cols, ts):
    n = ts + 2 * HALO - SUBLANES
    for r in range(SUBLANES):
        sh[r] = ext[r:r + n, cols]


def _tap_rows(sh, off, ts):
    q, r = divmod(off, SUBLANES)
    return sh[r, q * SUBLANES:q * SUBLANES + ts, :]


def _ln_stats(uc):
    mu = jnp.mean(uc, axis=-1, keepdims=True)
    xc = uc - mu
    rstd = lax.rsqrt(jnp.mean(xc * xc, axis=-1, keepdims=True) + EPS)
    return xc * rstd, rstd


def _conv_fwd(z, wdw, ln_g, ln_b, cw, ts=None):
    s = z.shape[0]
    ts = ts or ROW_TILE
    n_tiles = s // ts
    pad = CONV_KERNEL // 2

    def body(zp, zc, zn, w_ref, g_ref, b_ref, uc_ref, act_ref, ext, sh):
        i = pl.program_id(0)
        _glu_ext(zp, zc, zn, ext, cw, ts, i, n_tiles)

        def col_block(cb, carry):
            cols = pl.ds(pl.multiple_of(cb * LANES, LANES), LANES)
            _shifted_copies(ext, sh, cols, ts)
            acc = jnp.zeros((ts, LANES), F32)
            for j in range(CONV_KERNEL):
                acc = acc + _tap_rows(sh, HALO - pad + j, ts) * w_ref[j:j + 1, cols]
            uc_ref[:, cols] = acc
            return carry

        lax.fori_loop(0, cw // LANES, col_block, 0)
        xhat, _ = _ln_stats(uc_ref[...])
        ln = xhat * g_ref[...] + b_ref[...]
        act_ref[...] = (ln * _sigmoid(ln)).astype(BF)

    vec = pl.BlockSpec((1, cw), lambda i: (0, 0))
    row = pl.BlockSpec((ts, cw), lambda i: (i, 0))
    return pl.pallas_call(
        body, name="conv_fwd", grid=(n_tiles,),
        in_specs=_halo_specs(ts, s, 2 * cw, 0) + [pl.BlockSpec((32, cw), lambda i: (0, 0)), vec, vec],
        out_specs=[row, row],
        out_shape=[jax.ShapeDtypeStruct((s, cw), F32), jax.ShapeDtypeStruct((s, cw), BF)],
        scratch_shapes=[pltpu.VMEM((ts + 2 * HALO, cw), F32), _shift_scratch(ts)],
        compiler_params=_params(("parallel",)),
    )(z, z, z, wdw, ln_g, ln_b)


def _conv_bwd(ds, uc, z, wdw, ln_g, ln_b, cw, dz, ts=None, comm=None):
    s = z.shape[0]
    ts = ts or ROW_TILE
    n_tiles = s // ts
    pad = CONV_KERNEL // 2

    def body(zp, zc, zn, dsp, dsc, dsn, ucp, ucc, ucn, w_ref, g_ref, b_ref, _dz_in,
             dz_ref, dw_ref, dg_ref, db_ref, ext, dext, sh, dsh):
        i = pl.program_id(0)
        gv, bv = g_ref[...], b_ref[...]

        def ln_bwd(ds_r, uc_r):
            xhat, rstd = _ln_stats(uc_r[...])
            ln = xhat * gv + bv
            sg = _sigmoid(ln)
            dln = ds_r[...].astype(F32) * (sg * (1.0 + ln * (1.0 - sg)))
            dxh = dln * gv
            duc = rstd * (dxh - jnp.mean(dxh, axis=-1, keepdims=True) - xhat * jnp.mean(dxh * xhat, axis=-1, keepdims=True))
            return duc, dln, xhat

        duc_p, _, _ = ln_bwd(dsp, ucp)
        duc_c, dln_c, xhat_c = ln_bwd(dsc, ucc)
        duc_n, _, _ = ln_bwd(dsn, ucn)
        dext[0:HALO, :] = jnp.where(i > 0, duc_p, 0.0)
        dext[HALO:HALO + ts, :] = duc_c
        dext[HALO + ts:, :] = jnp.where(i < n_tiles - 1, duc_n, 0.0)
        _glu_ext(zp, zc, zn, ext, cw, ts, i, n_tiles)

        dg_part = jnp.sum(dln_c * xhat_c, axis=0, keepdims=True)
        db_part = jnp.sum(dln_c, axis=0, keepdims=True)

        @pl.when(i == 0)
        def _():
            dw_ref[...] = jnp.zeros_like(dw_ref)
            dg_ref[...] = dg_part
            db_ref[...] = db_part

        @pl.when(i > 0)
        def _():
            dg_ref[...] += dg_part
            db_ref[...] += db_part

        def col_block(cb, carry):
            c0 = pl.multiple_of(cb * LANES, LANES)
            cols, gate_cols = pl.ds(c0, LANES), pl.ds(cw + c0, LANES)
            _shifted_copies(dext, dsh, cols, ts)
            _shifted_copies(ext, sh, cols, ts)
            du = jnp.zeros((ts, LANES), F32)
            for j in range(CONV_KERNEL):
                du = du + _tap_rows(dsh, HALO + pad - j, ts) * w_ref[j:j + 1, cols]
            ca, sb = zc[:, cols].astype(F32), _sigmoid(zc[:, gate_cols].astype(F32))
            dz_ref[:, cols] = (du * sb).astype(BF)
            dz_ref[:, gate_cols] = (du * ca * sb * (1.0 - sb)).astype(BF)
            duc_blk = _tap_rows(dsh, HALO, ts)
            for j in range(CONV_KERNEL):
                dw_ref[j:j + 1, cols] += jnp.sum(_tap_rows(sh, HALO - pad + j, ts) * duc_blk, axis=0, keepdims=True)
            return carry

        lax.fori_loop(0, cw // LANES, col_block, 0)

    vec = pl.BlockSpec((1, cw), lambda i: (0, 0))
    wsp = pl.BlockSpec((32, cw), lambda i: (0, 0))
    res, cres = _call(
        body, name="conv_bwd", grid=(n_tiles,),
        in_specs=(_halo_specs(ts, s, 2 * cw, 0) + _halo_specs(ts, s, cw, 0) + _halo_specs(ts, s, cw, 0)
                  + [wsp, vec, vec, ANY]),
        out_specs=[pl.BlockSpec((ts, 2 * cw), lambda i: (i, 0)), wsp, vec, vec],
        out_shape=[jax.ShapeDtypeStruct(dz.shape, BF), jax.ShapeDtypeStruct((32, cw), F32),
                   jax.ShapeDtypeStruct((1, cw), F32), jax.ShapeDtypeStruct((1, cw), F32)],
        scratch_shapes=[pltpu.VMEM((ts + 2 * HALO, cw), F32), pltpu.VMEM((ts + 2 * HALO, cw), F32),
                        _shift_scratch(ts), _shift_scratch(ts)],
        semantics=("arbitrary",), args=[z, z, z, ds, ds, ds, uc, uc, uc, wdw, ln_g, ln_b, dz], comm=comm,
        aliases={12: 0})
    return (*res, cres)


def _rope_tables(s):
    axis_dim = HEAD_DIM // 2
    n_rows = s // GRID_W
    inv_freq = ROPE_THETA ** (-jnp.arange(0, axis_dim, 2, dtype=F32) / axis_dim)[None, :]
    ar = jnp.arange(n_rows, dtype=jnp.int32).astype(F32)[:, None] * inv_freq
    ac = jnp.arange(GRID_W, dtype=jnp.int32).astype(F32)[:, None] * inv_freq

    def table(fr, fc):
        by_row = jnp.broadcast_to(fr[:, None, :], (n_rows, GRID_W, axis_dim))
        by_col = jnp.broadcast_to(fc[None, :, :], (n_rows, GRID_W, axis_dim))
        return jnp.concatenate([by_row, by_col], axis=-1).reshape(s, HEAD_DIM)

    cos = table(jnp.concatenate([jnp.cos(ar), jnp.cos(ar)], axis=-1), jnp.concatenate([jnp.cos(ac), jnp.cos(ac)], axis=-1))
    sin = table(jnp.concatenate([-jnp.sin(ar), jnp.sin(ar)], axis=-1), jnp.concatenate([-jnp.sin(ac), jnp.sin(ac)], axis=-1))
    return cos, sin


def _swap_quarters(x):
    q = HEAD_DIM // 4
    lane = lax.broadcasted_iota(jnp.int32, x.shape, 1)
    return jnp.where((lane % (2 * q)) < q, pltpu.roll(x, HEAD_DIM - q, 1), pltpu.roll(x, q, 1))


def _qk_fwd(z, cos, sin, qg, kg, d, ts=None):
    s = z.shape[0]
    ts = ts or ROW_TILE
    kvw = d // GROUP
    scale = Q_SCALE

    def body(q_ref, k_ref, c_ref, s_ref, qg_ref, kg_ref, qo_ref, ko_ref):
        cv, sv = c_ref[...], s_ref[...]

        def head(x_ref, g_ref, o_ref, h, mul):
            xv = x_ref[:, h * HEAD_DIM:(h + 1) * HEAD_DIM].astype(F32)
            r = lax.rsqrt(jnp.mean(xv * xv, axis=-1, keepdims=True) + EPS)
            nrm = xv * r * g_ref[...]
            out = nrm * cv + _swap_quarters(nrm) * sv
            o_ref[:, h * HEAD_DIM:(h + 1) * HEAD_DIM] = (out * mul).astype(BF)

        for h in range(d // HEAD_DIM):
            head(q_ref, qg_ref, qo_ref, h, scale)
        for h in range(kvw // HEAD_DIM):
            head(k_ref, kg_ref, ko_ref, h, 1.0)

    cw2 = d
    tab = pl.BlockSpec((ts, HEAD_DIM), lambda i: (i, 0))
    vec = pl.BlockSpec((1, HEAD_DIM), lambda i: (0, 0))
    return pl.pallas_call(
        body, name="qk_fwd", grid=(s // ts,),
        in_specs=[pl.BlockSpec((ts, d), lambda i: (i, cw2 // d)),
                  pl.BlockSpec((ts, kvw), lambda i: (i, (cw2 + d) // kvw)), tab, tab, vec, vec],
        out_specs=[pl.BlockSpec((ts, d), lambda i: (i, 0)), pl.BlockSpec((ts, kvw), lambda i: (i, 0))],
        out_shape=[jax.ShapeDtypeStruct((s, d), BF), jax.ShapeDtypeStruct((s, kvw), BF)],
        compiler_params=_params(("parallel",)),
    )(z, z, cos, sin, qg, kg)


def _qk_bwd(dqt, dkt, z, cos, sin, qg, kg, d, dz, ts=None):
    s = z.shape[0]
    ts = ts or ROW_TILE
    kvw = d // GROUP
    scale = HEAD_DIM ** -0.5

    def body(dq_ref, dk_ref, q_ref, k_ref, c_ref, s_ref, qg_ref, kg_ref, _dz_in, dzo_ref, dqg_ref, dkg_ref):
        cv, sv = c_ref[...], s_ref[...]

        def head(dy_ref, x_ref, g_ref, col0, h, mul):
            dout = dy_ref[:, h * HEAD_DIM:(h + 1) * HEAD_DIM].astype(F32) * mul
            dn = dout * cv + _swap_quarters(dout * sv)
            xv = x_ref[:, h * HEAD_DIM:(h + 1) * HEAD_DIM].astype(F32)
            r = lax.rsqrt(jnp.mean(xv * xv, axis=-1, keepdims=True) + EPS)
            nh = xv * r
            dnh = dn * g_ref[...]
            c0 = col0 + h * HEAD_DIM
            dzo_ref[:, c0:c0 + HEAD_DIM] = (r * (dnh - nh * jnp.mean(dnh * nh, axis=-1, keepdims=True))).astype(BF)
            return jnp.sum(dn * nh, axis=0, keepdims=True)

        dqg = jnp.zeros((1, HEAD_DIM), F32)
        for h in range(d // HEAD_DIM):
            dqg = dqg + head(dq_ref, q_ref, qg_ref, 0, h, scale)
        dkg = jnp.zeros((1, HEAD_DIM), F32)
        for h in range(kvw // HEAD_DIM):
            dkg = dkg + head(dk_ref, k_ref, kg_ref, d, h, LN2)

        @pl.when(pl.program_id(0) == 0)
        def _():
            dqg_ref[...] = dqg
            dkg_ref[...] = dkg

        @pl.when(pl.program_id(0) > 0)
        def _():
            dqg_ref[...] += dqg
            dkg_ref[...] += dkg

    cw2 = d
    tab = pl.BlockSpec((ts, HEAD_DIM), lambda i: (i, 0))
    vec = pl.BlockSpec((1, HEAD_DIM), lambda i: (0, 0))
    qrow = pl.BlockSpec((ts, d), lambda i: (i, 0))
    krow = pl.BlockSpec((ts, kvw), lambda i: (i, 0))
    window = pl.BlockSpec((pl.Element(ts), pl.Element(d + kvw)), lambda i: (i * ts, cw2))
    return pl.pallas_call(
        body, name="qk_bwd", grid=(s // ts,),
        in_specs=[qrow, krow, pl.BlockSpec((ts, d), lambda i: (i, cw2 // d)),
                  pl.BlockSpec((ts, kvw), lambda i: (i, (cw2 + d) // kvw)), tab, tab, vec, vec, ANY],
        out_specs=[window, vec, vec],
        out_shape=[jax.ShapeDtypeStruct(dz.shape, BF),
                   jax.ShapeDtypeStruct((1, HEAD_DIM), F32), jax.ShapeDtypeStruct((1, HEAD_DIM), F32)],
        input_output_aliases={8: 0},
        compiler_params=_params(("arbitrary",)),
    )(dqt, dkt, z, z, cos, sin, qg, kg, dz)


_NT = (((1,), (1,)), ((), ()))
_TN = (((0,), (0,)), ((), ()))


def _v_col_block(d):
    return (2 * d + d // GROUP) // HEAD_DIM


def _flash_fwd(qt, kt, z, d, tq=None, tk=None, comm=None):
    s = qt.shape[0]
    tq, tk = min(tq or FLASH_TQ_FWD, s), min(tk or FLASH_TK, s)
    ng, nq, nk = d // (GROUP * HEAD_DIM), s // tq, s // tk
    gw = GROUP * HEAD_DIM
    rows = GROUP * tq

    nt = tk // LANES
    assert nk % 2 == 0, (s, tk)

    def body(q_ref, k_ref, v_ref, o_ref, lse_ref, qs, v1, p_s, m_s, acc_s, sc_s):
        @pl.when(pl.program_id(1) == 0)
        def _():
            v1[:, :HEAD_DIM] = v_ref[...]
            v1[:, HEAD_DIM:] = jnp.ones((s, HEAD_DIM), BF)

        for h in range(GROUP):
            qs[h * tq:(h + 1) * tq, :] = q_ref[:, h * HEAD_DIM:(h + 1) * HEAD_DIM]
        m_s[...] = jnp.full((rows, LANES), -1e30, F32)
        acc_s[...] = jnp.zeros((rows, 2 * HEAD_DIM), F32)

        def scores(j):
            return lax.dot_general(qs[...], k_ref[pl.ds(pl.multiple_of(j * tk, tk), tk), :], _NT, preferred_element_type=F32)

        def softmax_pv(j, sc):
            kv_rows = pl.ds(pl.multiple_of(j * tk, tk), tk)
            mt = sc[:, :LANES]
            for c in range(1, nt):
                mt = jnp.maximum(mt, sc[:, c * LANES:(c + 1) * LANES])
            m_old = m_s[...]
            m_new = jnp.maximum(m_old, jnp.max(mt, axis=-1, keepdims=True))
            alpha = jnp.exp2(m_old - m_new)
            for c in range(nt):
                cs = slice(c * LANES, (c + 1) * LANES)
                p_s[:, cs] = jnp.exp2(sc[:, cs] - m_new).astype(BF)
            pv = jnp.dot(p_s[...], v1[kv_rows, :], preferred_element_type=F32)
            acc_s[:, :HEAD_DIM] = alpha * acc_s[:, :HEAD_DIM] + pv[:, :HEAD_DIM]
            acc_s[:, HEAD_DIM:] = alpha * acc_s[:, HEAD_DIM:] + pv[:, HEAD_DIM:]
            m_s[...] = m_new

        sc_s[0] = scores(0)

        def step(jj, carry):
            j = 2 * jj
            sc_s[1] = scores(j + 1)
            softmax_pv(j, sc_s[0])
            sc_s[0] = scores(jnp.minimum(j + 2, nk - 1))
            softmax_pv(j + 1, sc_s[1])
            return carry

        lax.fori_loop(0, nk // 2, step, 0)
        l = acc_s[:, HEAD_DIM:]
        o = acc_s[:, :HEAD_DIM] / l
        for h in range(GROUP):
            o_ref[:, h * HEAD_DIM:(h + 1) * HEAD_DIM] = o[h * tq:(h + 1) * tq, :].astype(BF)
        lse = m_s[...] + jnp.log2(l)
        for h in range(GROUP):
            lse_ref[h] = lse[h * tq:(h + 1) * tq, :]

    vb = _v_col_block(d)
    (o, lse), cres = _call(
        body, name="flash_fwd", grid=(ng, nq),
        in_specs=[pl.BlockSpec((tq, gw), lambda g, i: (i, g)),
                  pl.BlockSpec((s, HEAD_DIM), lambda g, i: (0, g)),
                  pl.BlockSpec((s, HEAD_DIM), lambda g, i: (0, vb + g))],
        out_specs=[pl.BlockSpec((tq, gw), lambda g, i: (i, g)),
                   pl.BlockSpec((GROUP, tq, LANES), lambda g, i: (g, i, 0))],
        out_shape=[jax.ShapeDtypeStruct((s, d), BF), jax.ShapeDtypeStruct((ng * GROUP, s, LANES), F32)],
        scratch_shapes=[pltpu.VMEM((rows, HEAD_DIM), BF), pltpu.VMEM((s, 2 * HEAD_DIM), BF), pltpu.VMEM((rows, tk), BF),
                        pltpu.VMEM((rows, LANES), F32), pltpu.VMEM((rows, 2 * HEAD_DIM), F32), pltpu.VMEM((2, rows, tk), F32)],
        semantics=("parallel", "arbitrary"), args=[qt, kt, z], comm=comm)
    return o, lse, cres


def _flash_bwd(qt, kt, z, o, do, lse, d, dz, tq=None, tk=None, comm=None):
    s = qt.shape[0]
    tq, tk = min(tq or FLASH_TQ_BWD, s), min(tk or FLASH_TK, s)
    ng, nq, nk = d // (GROUP * HEAD_DIM), s // tq, s // tk
    gw = GROUP * HEAD_DIM
    rows = GROUP * tq

    nt = tk // LANES

    def body(q_ref, k_ref, v_ref, o_ref, do_ref, lse_ref, _dz_in, dq_ref, dk_ref, dzv_ref,
             qs, dos, delta_s, dq_s, p_s, ds_s, lse_s, dv_ref):
        i = pl.program_id(1)
        for h in range(GROUP):
            cols = slice(h * HEAD_DIM, (h + 1) * HEAD_DIM)
            lse_s[h * tq:(h + 1) * tq, :] = lse_ref[h]
            qs[h * tq:(h + 1) * tq, :] = q_ref[:, cols]
            dov = do_ref[:, cols]
            dos[h * tq:(h + 1) * tq, :] = dov
            delta = jnp.sum(dov.astype(F32) * o_ref[:, cols].astype(F32), axis=-1, keepdims=True)
            delta_s[h * tq:(h + 1) * tq, :] = jnp.broadcast_to(delta, (tq, LANES))
        dq_s[...] = jnp.zeros((rows, HEAD_DIM), F32)

        @pl.when(i == 0)
        def _():
            dk_ref[...] = jnp.zeros_like(dk_ref)
            dv_ref[...] = jnp.zeros_like(dv_ref)

        def step(j, carry):
            kv_rows = pl.ds(pl.multiple_of(j * tk, tk), tk)
            kv, vv = k_ref[kv_rows, :], v_ref[kv_rows, :]
            sc = lax.dot_general(qs[...], kv, _NT, preferred_element_type=F32)
            dp = lax.dot_general(dos[...], vv, _NT, preferred_element_type=F32)
            lse, delta = lse_s[...], delta_s[...]
            for c in range(nt):
                cs = slice(c * LANES, (c + 1) * LANES)
                p = jnp.exp2(sc[:, cs] - lse)
                p_s[:, cs] = p.astype(BF)
                ds_s[:, cs] = (p * (dp[:, cs] - delta)).astype(BF)
            dv_ref[kv_rows, :] += lax.dot_general(p_s[...], dos[...], _TN, preferred_element_type=F32)
            dk_ref[kv_rows, :] += lax.dot_general(ds_s[...], qs[...], _TN, preferred_element_type=F32)
            dq_s[...] += jnp.dot(ds_s[...], kv, preferred_element_type=F32)
            return carry

        lax.fori_loop(0, nk, step, 0)
        for h in range(GROUP):
            dq_ref[:, h * HEAD_DIM:(h + 1) * HEAD_DIM] = dq_s[h * tq:(h + 1) * tq, :].astype(BF)

        @pl.when(i == nq - 1)
        def _():
            dzv_ref[...] = dv_ref[...].astype(BF)

    vb = _v_col_block(d)
    qspec = pl.BlockSpec((tq, gw), lambda g, i: (i, g))
    kspec = pl.BlockSpec((s, HEAD_DIM), lambda g, i: (0, g))
    vspec = pl.BlockSpec((s, HEAD_DIM), lambda g, i: (0, vb + g))
    (dq, dk, dz), cres = _call(
        body, name="flash_bwd", grid=(ng, nq),
        in_specs=[qspec, kspec, vspec, qspec, qspec, pl.BlockSpec((GROUP, tq, LANES), lambda g, i: (g, i, 0)), ANY],
        out_specs=[qspec, kspec, vspec],
        out_shape=[jax.ShapeDtypeStruct((s, d), BF), jax.ShapeDtypeStruct((s, d // GROUP), F32),
                   jax.ShapeDtypeStruct(dz.shape, BF)],
        scratch_shapes=[pltpu.VMEM((rows, HEAD_DIM), BF), pltpu.VMEM((rows, HEAD_DIM), BF), pltpu.VMEM((rows, LANES), F32),
                        pltpu.VMEM((rows, HEAD_DIM), F32), pltpu.VMEM((rows, tk), BF), pltpu.VMEM((rows, tk), BF),
                        pltpu.VMEM((rows, LANES), F32), pltpu.VMEM((s, HEAD_DIM), F32)],
        semantics=("parallel", "arbitrary"), args=[qt, kt, z, o, do, lse, dz], comm=comm, aliases={6: 2})
    return dq, dk, dz, cres


def _place():
    x, y, c = lax.axis_index("x"), lax.axis_index("y"), lax.axis_index("c")
    other_chips = [(1 - x, y), (x, 1 - y), (1 - x, 1 - y)]
    return x, y, c, other_chips


def _cast_place(name, w, chip_arr, tr=256):
    r, cc = w.shape
    tr = min(tr, r)

    def body(p_ref, w_ref, o_ref):
        o_ref[...] = w_ref[...].astype(BF)

    return pl.pallas_call(
        body, name=name,
        grid_spec=pltpu.PrefetchScalarGridSpec(
            num_scalar_prefetch=1, grid=(r // tr,),
            in_specs=[pl.BlockSpec((tr, cc), lambda i, p_ref: (i, 0))],
            out_specs=pl.BlockSpec((None, tr, cc), lambda i, p_ref: (p_ref[0], i, 0))),
        out_shape=jax.ShapeDtypeStruct((N_CHIPS, r, cc), BF),
        compiler_params=_params(("parallel",)),
    )(chip_arr, w)


def _gather_comm(bufs, short_host=False):
    n = len(bufs)
    pairs = [(w, j) for w in range(n) for j in range(N_CHIPS - 1)]

    def copies(dst, sems):
        send, recv, fsend, frecv = sems
        x, y, c, chips = _place()

        def part(w, chip, core_half):
            h = bufs[w].shape[1] // 2
            return dst[w].at[2 * chip[0] + chip[1], pl.ds(core_half * h, h)]

        def ici(w, j, incoming):
            slab = part(w, chips[j] if incoming else (x, y), c)
            return pltpu.make_async_remote_copy(
                src_ref=slab, dst_ref=slab, send_sem=send.at[3 * w + j], recv_sem=recv.at[3 * w + j],
                device_id=(*chips[j], c), device_id_type=MESH)

        def d2d(w, j, incoming):
            slab = part(w, chips[j], 1 - c if incoming else c)
            return pltpu.make_async_remote_copy(
                src_ref=slab, dst_ref=slab, send_sem=fsend.at[3 * w + j], recv_sem=frecv.at[3 * w + j],
                device_id=(x, y, 1 - c), device_id_type=MESH)

        return ici, d2d

    def first(_, dst, sems):
        ici, _d = copies(dst, sems)
        for w, j in pairs:
            ici(w, j, False).start()

    def middle(_, dst, sems):
        ici, d2d = copies(dst, sems)
        for w, j in pairs:
            ici(w, j, True).wait_recv()
            d2d(w, j, False).start()

    def last(_, dst, sems):
        ici, d2d = copies(dst, sems)
        for w, j in pairs:
            d2d(w, j, True).wait_recv()
        for w, j in pairs:
            ici(w, j, False).wait_send()
            d2d(w, j, False).wait_send()

    def middle_and_last(src, dst, sems):
        middle(src, dst, sems)
        last(src, dst, sems)

    phases = (first, None, middle_and_last) if short_host else (first, middle, last)
    return _Comm(arrays=list(bufs), out_shapes=[jax.ShapeDtypeStruct(b.shape, b.dtype) for b in bufs],
                 aliases={w: w for w in range(n)}, sems=[pltpu.SemaphoreType.DMA((3 * n,))] * 4, phases=phases)


def _run_comm(name, comm):
    nci, nco = len(comm.arrays), len(comm.out_shapes)

    def body(*refs):
        cin, cout, sems = refs[:nci], refs[nci:nci + nco], refs[nci + nco:]
        for fn in comm.phases:
            if fn is not None:
                fn(cin, cout, sems)

    return pl.pallas_call(
        body, name=name, in_specs=[ANY] * nci, out_specs=[ANY] * nco, out_shape=list(comm.out_shapes),
        input_output_aliases=dict(comm.aliases), scratch_shapes=list(comm.sems),
    )(*comm.arrays)


def _pair_comm(grads):
    n = len(grads)

    def copies(src, dst, sems):
        send, recv = sems
        x, y, c, _ = _place()
        out = []
        for w in range(n):
            h = grads[w].shape[1] // 2
            out.append(pltpu.make_async_remote_copy(
                src_ref=src[w].at[:, pl.ds((1 - c) * h, h), :], dst_ref=dst[w],
                send_sem=send.at[w], recv_sem=recv.at[w], device_id=(x, y, 1 - c), device_id_type=MESH))
        return out

    def first(src, dst, sems):
        for cp in copies(src, dst, sems):
            cp.start()

    def last(src, dst, sems):
        for cp in copies(src, dst, sems):
            cp.wait()

    return _Comm(arrays=list(grads),
                 out_shapes=[jax.ShapeDtypeStruct((N_CHIPS, g.shape[1] // 2, g.shape[2]), g.dtype) for g in grads],
                 aliases={}, sems=[pltpu.SemaphoreType.DMA((n,))] * 2, phases=(first, None, last))


def _pair_sum(name, own, got, c_arr, tr=256):
    nc, r, cc = own.shape
    h = r // 2
    tr = min(tr, h)
    nb = h // tr

    def body(c_ref, a_ref, b_ref, o_ref):
        o_ref[...] = (a_ref[...].astype(F32) + b_ref[...].astype(F32)).astype(BF)

    return pl.pallas_call(
        body, name=name,
        grid_spec=pltpu.PrefetchScalarGridSpec(
            num_scalar_prefetch=1, grid=(nc, nb),
            in_specs=[pl.BlockSpec((None, tr, cc), lambda s, i, c_ref: (s, c_ref[0] * nb + i, 0)),
                      pl.BlockSpec((None, tr, cc), lambda s, i, c_ref: (s, i, 0))],
            out_specs=pl.BlockSpec((None, tr, cc), lambda s, i, c_ref: (s, i, 0))),
        out_shape=jax.ShapeDtypeStruct((nc, h, cc), BF),
        compiler_params=_params(("parallel", "parallel")),
    )(c_arr, own, got)


def _chip_comm(parts):
    n = len(parts)

    def copies(src, dst, sems):
        send, recv = sems
        _, _, c, chips = _place()
        return [pltpu.make_async_remote_copy(
            src_ref=src[w].at[2 * chip[0] + chip[1]], dst_ref=dst[w].at[j],
            send_sem=send.at[3 * w + j], recv_sem=recv.at[3 * w + j], device_id=(*chip, c), device_id_type=MESH)
            for w in range(n) for j, chip in enumerate(chips)]

    def first(src, dst, sems):
        for cp in copies(src, dst, sems):
            cp.start()

    def last(src, dst, sems):
        for cp in copies(src, dst, sems):
            cp.wait()

    return _Comm(arrays=list(parts), out_shapes=[jax.ShapeDtypeStruct((N_CHIPS - 1,) + p.shape[1:], p.dtype) for p in parts],
                 aliases={}, sems=[pltpu.SemaphoreType.DMA((3 * n,))] * 2, phases=(first, None, last))


def _chip_sum(name, parts, got, chip_arr, c_arr, tr=256):
    _, h, cc = parts.shape
    tr = min(tr, h)
    nb = h // tr

    def body(chip_ref, c_ref, own_ref, got_ref, o_ref):
        acc = own_ref[...].astype(F32)
        for k in range(N_CHIPS - 1):
            acc = acc + got_ref[k].astype(F32)
        o_ref[...] = acc

    return pl.pallas_call(
        body, name=name,
        grid_spec=pltpu.PrefetchScalarGridSpec(
            num_scalar_prefetch=2, grid=(nb,),
            in_specs=[pl.BlockSpec((None, tr, cc), lambda i, chip_ref, c_ref: (chip_ref[0], i, 0)),
                      pl.BlockSpec((N_CHIPS - 1, tr, cc), lambda i, chip_ref, c_ref: (0, i, 0))],
            out_specs=pl.BlockSpec((tr, cc), lambda i, chip_ref, c_ref: (c_ref[0] * nb + i, 0))),
        out_shape=jax.ShapeDtypeStruct((2 * h, cc), F32),
        compiler_params=_params(("parallel",)),
    )(chip_arr, c_arr, parts, got)


def _pair_gather_comm(bufs):
    n = len(bufs)

    def copy(dst, sems, w, core_half):
        send, recv = sems
        x, y, c, _ = _place()
        h = bufs[w].shape[0] // 2
        rows = dst[w].at[pl.ds((1 - c if core_half == "theirs" else c) * h, h)]
        return pltpu.make_async_remote_copy(src_ref=rows, dst_ref=rows, send_sem=send.at[w], recv_sem=recv.at[w],
                                            device_id=(x, y, 1 - c), device_id_type=MESH)

    def first(_, dst, sems):
        for w in range(n):
            copy(dst, sems, w, "mine").start()

    def last(_, dst, sems):
        for w in range(n):
            copy(dst, sems, w, "theirs").wait_recv()
        for w in range(n):
            copy(dst, sems, w, "mine").wait_send()

    return _Comm(arrays=list(bufs), out_shapes=[jax.ShapeDtypeStruct(b.shape, b.dtype) for b in bufs],
                 aliases={w: w for w in range(n)}, sems=[pltpu.SemaphoreType.DMA((n,))] * 2, phases=(first, None, last))


def _merge_comms(a, b):
    nai, nao, nas = len(a.arrays), len(a.out_shapes), len(a.sems)

    def both(fa, fb):
        if fa is None and fb is None:
            return None

        def phase(cin, cout, sems):
            if fa is not None:
                fa(cin[:nai], cout[:nao], sems[:nas])
            if fb is not None:
                fb(cin[nai:], cout[nao:], sems[nas:])
        return phase

    return _Comm(arrays=a.arrays + b.arrays, out_shapes=a.out_shapes + b.out_shapes,
                 aliases={**a.aliases, **{nai + k: nao + v for k, v in b.aliases.items()}},
                 sems=a.sems + b.sems, phases=tuple(both(fa, fb) for fa, fb in zip(a.phases, b.phases)))


def _all_sum_small(name, v):
    p = v.shape[0]

    def body(v_ref, o_ref, slots, send, recv):
        x, y, c, _ = _place()
        me = 4 * x + 2 * y + c
        copies = []
        for k in range(1, N_DEV):
            peer = (x ^ (k >> 2), y ^ ((k >> 1) & 1), c ^ (k & 1))
            copies.append(pltpu.make_async_remote_copy(
                src_ref=v_ref, dst_ref=slots.at[me], send_sem=send.at[k - 1], recv_sem=recv.at[k - 1],
                device_id=peer, device_id_type=MESH))
        for cp in copies:
            cp.start()
        slots[me] = v_ref[...]
        for cp in copies:
            cp.wait()
        acc = slots[0]
        for s in range(1, N_DEV):
            acc = acc + slots[s]
        o_ref[...] = acc

    vm = pl.BlockSpec(memory_space=pltpu.VMEM)
    return pl.pallas_call(
        body, name=name,
        in_specs=[vm], out_specs=vm,
        out_shape=jax.ShapeDtypeStruct(v.shape, F32),
        scratch_shapes=[pltpu.VMEM((N_DEV, p, LANES), F32), pltpu.SemaphoreType.DMA((N_DEV - 1,)),
                        pltpu.SemaphoreType.DMA((N_DEV - 1,))],
    )(v)


def _adamw(name, w, g, m, v, tr=256):
    r, c = w.shape
    tr = min(tr, r)
    assert r % tr == 0
    bc1 = 1.0 - ADAM_B1 ** ADAM_STEP
    bc2 = 1.0 - ADAM_B2 ** ADAM_STEP

    def body(w_ref, g_ref, m_ref, v_ref, d_ref, nm_ref, nv_ref):
        gv = g_ref[...]
        nm = ADAM_B1 * m_ref[...] + (1.0 - ADAM_B1) * gv
        nv = ADAM_B2 * v_ref[...] + (1.0 - ADAM_B2) * (gv * gv)
        nm_ref[...] = nm
        nv_ref[...] = nv
        d_ref[...] = -ADAM_LR * ((nm / bc1) / (jnp.sqrt(nv / bc2) + ADAM_EPS) + ADAM_WD * w_ref[...])

    blk = pl.BlockSpec((tr, c), lambda i: (i, 0))
    return pl.pallas_call(
        body, name=name, grid=(r // tr,),
        in_specs=[blk] * 4, out_specs=[blk] * 3,
        out_shape=[jax.ShapeDtypeStruct((r, c), F32)] * 3,
        compiler_params=_params(("parallel",)),
    )(w, g, m, v)


def _pack_small(parts):
    flat = jnp.concatenate([a.reshape(-1) for a in parts])
    n = flat.shape[0]
    p = -(-n // (8 * LANES)) * 8
    packed = jnp.pad(flat, (0, p * LANES - n)).reshape(p, LANES)

    def unpack(q):
        out, off = [], 0
        f = q.reshape(-1)
        for a in parts:
            out.append(f[off:off + a.size].reshape(a.shape))
            off += a.size
        return out

    return packed, unpack


def kernel(x, p, norm_mix, w_in, w_dw, conv_ln_g, conv_ln_b, w_conv_proj, q_norm, k_norm, w_attn_proj, w_out, norm_ffn, w_ff1, w_ff2, norm_ple, w_ple_gate, w_ple_proj, norm_final, loss_target, m_norm_mix, m_w_in, m_w_dw, m_conv_ln_g, m_conv_ln_b, m_w_conv_proj, m_q_norm, m_k_norm, m_w_attn_proj, m_w_out, m_norm_ffn, m_w_ff1, m_w_ff2, m_norm_ple, m_w_ple_gate, m_w_ple_proj, m_norm_final, v_norm_mix, v_w_in, v_w_dw, v_conv_ln_g, v_conv_ln_b, v_w_conv_proj, v_q_norm, v_k_norm, v_w_attn_proj, v_w_out, v_norm_ffn, v_w_ff1, v_w_ff2, v_norm_ple, v_w_ple_gate, v_w_ple_proj, v_norm_final):
    s, d = x.shape[1], x.shape[2]
    cw = d // 2
    kvw = d // GROUP
    xs, ps, tgt = x[0], p[0, 0], loss_target[0]
    cx, cy, cc = lax.axis_index("x"), lax.axis_index("y"), lax.axis_index("c")
    chip = 2 * cx + cy
    c_arr = jnp.reshape(cc, (1,)).astype(jnp.int32)
    tm, tme = min(MM_TM, s), min(MM_TM_EPI, s)

    names = ["w_in", "w_conv_proj", "w_attn_proj", "w_out", "w_ff1", "w_ff2", "w_ple_gate", "w_ple_proj"]
    big = [w_in, w_conv_proj, w_attn_proj, w_out, w_ff1, w_ff2, w_ple_gate, w_ple_proj]
    chip_arr = jnp.reshape(chip, (1,)).astype(jnp.int32)
    placed = [_cast_place("cast_" + nm, w[0], chip_arr) for nm, w in zip(names, big)]
    cpc = cw // N_CHIPS
    taps_rows = 32
    my_taps = jnp.pad(w_dw[0], ((0, taps_rows - CONV_KERNEL), (0, 0)))[None]
    taps_buf = lax.dynamic_update_slice(jnp.zeros((N_CHIPS, taps_rows, cpc), F32), my_taps, (chip, 0, 0))
    h0, (win, taps_all) = _rms_fwd("rms_mix", xs, norm_mix, comm=_gather_comm([placed[0], taps_buf], short_host=True))
    wdw = taps_all.transpose(1, 0, 2).reshape(taps_rows, cw)

    cos, sin = _rope_tables(s)
    (z,) = _mm("z_proj", h0, win, b_cm=True, tm=tm, tn=win.shape[2] // 3, tk=d)
    uc, act = _conv_fwd(z, wdw, conv_ln_g, conv_ln_b, cw)
    qt, kt = _qk_fwd(z, cos, sin, q_norm, k_norm, d)
    o, lse, (wcp, wap, wout, w1, w2, wpg, wple) = _flash_fwd(qt, kt, z, d, comm=_gather_comm(placed[1:]))
    wap, wout, w2, wpg = (t.reshape(-1, t.shape[-1]) for t in (wap, wout, w2, wpg))
    (y_c,) = _mm("conv_proj", act, wcp, b_cm=True, tm=tm, tn=wcp.shape[2], tk=cw, out_dtypes=(F32,))
    tn = d // 2
    gcb = (2 * d + 2 * kvw) // tn

    def merge_epi(acc, yc, gc, ga):
        return acc, _sigmoid(gc.astype(F32)) * yc + _sigmoid(ga.astype(F32)) * acc

    y_a, merged = _mm("attn_proj", o, wap, tm=tme, tn=tn, tk=d, epi=merge_epi, out_dtypes=(BF, BF), b_resident=True,
                      extras=[_tile_extra(y_c, tme, tn), _tile_extra(z, tme, tn, gcb), _tile_extra(z, tme, tn, gcb + 2)])
    def residual_norm(acc, r, g):
        xn = r + acc
        return xn, xn * lax.rsqrt(jnp.mean(xn * xn, axis=-1, keepdims=True) + EPS) * g

    gain = lambda g: (g, (1, d), lambda i, j, k: (0, 0), True)
    x1, h1 = _mm("out_proj", merged, wout, tm=tme, tn=d, tk=d, epi=residual_norm, out_dtypes=(F32, BF), b_resident=True,
                 extras=[_tile_extra(xs, tme, d), gain(norm_ffn)])
    (a,) = _mm("ff1", h1, w1, b_cm=True, tm=tm, tn=tn, tk=d)

    def relu2(t):
        return jnp.square(jnp.maximum(t, 0.0))

    x2, h2 = _mm("ff2", a, w2, tm=tme, tn=d, tk=d, a_fn=relu2, epi=residual_norm, out_dtypes=(F32, BF),
                 extras=[_tile_extra(x1, tme, d), gain(norm_ple)])
    to_bf = lambda t: t.astype(BF)
    (e,) = _mm("ple_proj", ps, wple, b_cm=True, tm=tm, tn=wple.shape[2], tk=ps.shape[1], a_fn=to_bf)

    def ple_epi(acc, ev, r):
        gt = _sigmoid(acc)
        return r + gt * ev.astype(F32), gt

    x3, gate = _mm("ple_gate", h2, wpg, tm=tme, tn=tn, tk=d, epi=ple_epi, out_dtypes=(F32, BF), b_resident=True,
                   extras=[_tile_extra(e, tme, tn), _tile_extra(x2, tme, tn)])

    dx3, de, dgp, sq, d_fin = _loss_bwd(x3, tgt, norm_final.reshape(1, d), e, gate)
    tkt = min(2048, s)
    (g_wple,) = _mm("d_wple", ps, de, ta=True, out_cm=True, tm=ps.shape[1], tn=wple.shape[2], tk=tkt, a_fn=to_bf)
    (g_wpg,) = _mm("d_wpg", h2, dgp, ta=True, tm=tm, tn=tn, tk=tkt)
    (dh2,) = _mm("d_h2", dgp, wpg, tb=True, tm=tm, tn=tn, tk=d)
    dx2, dx2b, d_ple = _rms_bwd("rms_ple_bwd", dh2, x2, norm_ple, dx3)

    (da,) = _mm("d_a", dx2b, w2, tb=True, tm=tm, tn=tn, tk=d, out_dtypes=(BF,),
                epi=lambda acc, av: (acc * (2.0 * jnp.maximum(av.astype(F32), 0.0)),), extras=[_tile_extra(a, tm, tn)])
    (g_w2,) = _mm("d_w2", a, dx2b, ta=True, tm=tm, tn=tn, tk=tkt, a_fn=relu2)
    (g_w1,) = _mm("d_w1", h1, da, ta=True, out_cm=True, tm=tm, tn=tn, tk=tkt)
    (dh1,) = _mm("d_h1", da, w1, tb=True, b_cm=True, tm=tm, tn=tn, tk=w1.shape[2])
    dx1, dx1b, d_ffn = _rms_bwd("rms_ffn_bwd", dh1, x1, norm_ffn, dx2)

    def merge_bwd(acc, gc, ga, yc, ya):
        sc, sa = _sigmoid(gc.astype(F32)), _sigmoid(ga.astype(F32))
        return acc * sc, acc * sa, jnp.concatenate(
            [acc * yc * sc * (1.0 - sc), acc * ya.astype(F32) * sa * (1.0 - sa)], axis=1)

    tmd = min(MM_TM_DZ, s)
    gate0 = 2 * d + 2 * kvw
    z_cols = z.shape[1]

    def gate_window(width, col0):
        return (pl.Element(tmd), pl.Element(width)), lambda i, j, k: (i * tmd, col0)

    dy_c, dy_a, dz = _mm(
        "d_merged", dx1b, wout, tb=True, tm=tmd, tn=d, tk=d, epi=merge_bwd, out_dtypes=(BF, BF, BF), b_resident=True,
        extras=[(z, *gate_window(d, gate0)), (z, *gate_window(d, gate0 + d)), _tile_extra(y_c, tmd, d),
                _tile_extra(y_a, tmd, d)],
        out_overrides={2: ((s, z_cols), pl.BlockSpec(*gate_window(2 * d, gate0)))})
    (g_wout,) = _mm("d_wout", merged, dx1b, ta=True, tm=tm, tn=tn, tk=tkt)
    (g_wap,) = _mm("d_wap", o, dy_a, ta=True, tm=tm, tn=tn, tk=tkt)

    def slabs(g):
        return g if g.ndim == 3 else g.reshape(N_CHIPS, g.shape[0] // N_CHIPS, g.shape[1])

    grads_a = [slabs(g) for g in (g_wap, g_wout, g_w1, g_w2, g_wpg, g_wple)]
    (do,), got_a = _mm("d_o", dy_a, wap, tb=True, tm=tm, tn=tn, tk=d, comm=_pair_comm(grads_a))
    parts_a = [_pair_sum("pair_sum_" + nm, g, r, c_arr) for nm, g, r in zip(names[2:], grads_a, got_a)]
    dqt, dkt, dz, _ = _flash_bwd(qt, kt, z, o, do, lse, d, dz)
    dz, d_qn, d_kn = _qk_bwd(dqt, dkt, z, cos, sin, q_norm, k_norm, d, dz)
    (g_wcp,) = _mm("d_wcp", act, dy_c, ta=True, out_cm=True, tm=cw, tn=wcp.shape[2], tk=tkt)
    (dact,) = _mm("d_act", dy_c, wcp, tb=True, b_cm=True, tm=tm, tn=cw, tk=wcp.shape[2])
    p_wap, p_wout, p_w1, p_w2, p_wpg, p_wple = parts_a
    dz, d_taps, d_lng, d_lnb, (s_wap, s_wout, s_wpg, s_wple, s_w1) = _conv_bwd(
        dact, uc, z, wdw, conv_ln_g, conv_ln_b, cw, dz, comm=_chip_comm([p_wap, p_wout, p_wpg, p_wple, p_w1]))
    (g_win,), (s_w2,) = _mm("d_win", h0, dz, ta=True, out_cm=True, tm=tm, tn=win.shape[2] // 3, tk=tkt,
                            comm=_chip_comm([p_w2]))
    slots_a = [s_wap, s_wout, s_w1, s_w2, s_wpg, s_wple]
    grads_b = [slabs(g_win), slabs(g_wcp)]
    got_b = _run_comm("grad_pair_exchange_b", _pair_comm(grads_b))
    parts_b = [_pair_sum("pair_sum_" + nm, g, r, c_arr) for nm, g, r in zip(names[:2], grads_b, got_b)]
    halves_a = [_chip_sum("chip_sum_" + nm, cp, sl, chip_arr, c_arr) for nm, cp, sl in zip(names[2:], parts_a, slots_a)]
    (dh0,), hosted = _mm("d_h0", dz, win, tb=True, b_cm=True, tm=tm, tn=tn, tk=win.shape[2],
                         comm=_merge_comms(_chip_comm(parts_b), _pair_gather_comm(halves_a)))
    slots_b, grads_a_done = hosted[:2], hosted[2:]
    dx, _, d_mix = _rms_bwd("rms_mix_bwd", dh0, xs, norm_mix, dx1)
    halves_b = [_chip_sum("chip_sum_" + nm, cp, sl, chip_arr, c_arr) for nm, cp, sl in zip(names[:2], parts_b, slots_b)]
    big_grads = list(_run_comm("grad_pair_gather_b", _pair_gather_comm(halves_b))) + list(grads_a_done)

    small = [d_mix, d_taps[:CONV_KERNEL], d_lng, d_lnb, d_qn, d_kn, d_ffn, d_ple, d_fin]
    packed, unpack = _pack_small(small)
    g_mix, g_taps, g_lng, g_lnb, g_qn, g_kn, g_ffn, g_ple, g_fin = unpack(_all_sum_small("reduce_small", packed))
    g_dw = lax.dynamic_slice_in_dim(g_taps.reshape(CONV_KERNEL, N_CHIPS, cpc), chip, 1, axis=1).reshape(1, CONV_KERNEL, cpc)

    sq_local = lax.reduce_precision(sq[0, 0], 8, 23)
    loss = (0.5 / d) * lax.psum(sq_local, ("x", "y", "c"))

    grads = {
        "norm_mix": g_mix, "w_in": big_grads[0][None], "w_dw": g_dw, "conv_ln_g": g_lng, "conv_ln_b": g_lnb,
        "w_conv_proj": big_grads[1][None], "q_norm": g_qn, "k_norm": g_kn, "w_attn_proj": big_grads[2][None],
        "w_out": big_grads[3][None], "norm_ffn": g_ffn, "w_ff1": big_grads[4][None], "w_ff2": big_grads[5][None],
        "norm_ple": g_ple, "w_ple_gate": big_grads[6][None], "w_ple_proj": big_grads[7][None],
        "norm_final": g_fin.reshape(d),
    }
    weights = dict(norm_mix=norm_mix, w_in=w_in, w_dw=w_dw, conv_ln_g=conv_ln_g, conv_ln_b=conv_ln_b, w_conv_proj=w_conv_proj,
                   q_norm=q_norm, k_norm=k_norm, w_attn_proj=w_attn_proj, w_out=w_out, norm_ffn=norm_ffn, w_ff1=w_ff1,
                   w_ff2=w_ff2, norm_ple=norm_ple, w_ple_gate=w_ple_gate, w_ple_proj=w_ple_proj, norm_final=norm_final)
    m_in = dict(norm_mix=m_norm_mix, w_in=m_w_in, w_dw=m_w_dw, conv_ln_g=m_conv_ln_g, conv_ln_b=m_conv_ln_b,
                w_conv_proj=m_w_conv_proj, q_norm=m_q_norm, k_norm=m_k_norm, w_attn_proj=m_w_attn_proj, w_out=m_w_out,
                norm_ffn=m_norm_ffn, w_ff1=m_w_ff1, w_ff2=m_w_ff2, norm_ple=m_norm_ple, w_ple_gate=m_w_ple_gate,
                w_ple_proj=m_w_ple_proj, norm_final=m_norm_final)
    v_in = dict(norm_mix=v_norm_mix, w_in=v_w_in, w_dw=v_w_dw, conv_ln_g=v_conv_ln_g, conv_ln_b=v_conv_ln_b,
                w_conv_proj=v_w_conv_proj, q_norm=v_q_norm, k_norm=v_k_norm, w_attn_proj=v_w_attn_proj, w_out=v_w_out,
                norm_ffn=v_norm_ffn, w_ff1=v_w_ff1, w_ff2=v_w_ff2, norm_ple=v_norm_ple, w_ple_gate=v_w_ple_gate,
                w_ple_proj=v_w_ple_proj, norm_final=v_norm_final)
    order = list(weights)
    deltas, new_m, new_v, g_out = [], [], [], []
    for nm in order:
        w = weights[nm]
        shape = w.shape
        two_d = (-1, shape[-1])
        dl, mm_, vv_ = _adamw("adamw_" + nm, w.reshape(two_d), grads[nm].reshape(two_d), m_in[nm].reshape(two_d),
                              v_in[nm].reshape(two_d))
        g_out.append(grads[nm].reshape(shape))
        deltas.append(dl.reshape(shape))
        new_m.append(mm_.reshape(shape))
        new_v.append(vv_.reshape(shape))
    return (loss, dx[None], *g_out, *deltas, *new_m, *new_v)
```

```python
from typing import NamedTuple

import jax
import jax.numpy as jnp
from jax import lax
from jax.experimental import pallas as pl
from jax.experimental.pallas import tpu as pltpu

F32 = jnp.float32
BF = jnp.bfloat16

EPS = 1e-6
HEAD_DIM = 128
GROUP = 4
GRID_W = 64
ROPE_THETA = 10000.0
CONV_KERNEL = 31
HALO = 16
N_CHIPS = 4
N_DEV = 8
LANES = 128

ADAM_LR = 0.001
ADAM_B1 = 0.9
ADAM_B2 = 0.999
ADAM_EPS = 1e-08
ADAM_WD = 0.01
ADAM_STEP = 10

VMEM_LIMIT = 56 * 2 ** 20
LOG2E = 1.4426950408889634
LN2 = 0.6931471805599453
Q_SCALE = HEAD_DIM ** -0.5 * LOG2E
ROW_TILE = 256
FLASH_TQ_FWD = 512
FLASH_TQ_BWD = 512
FLASH_TK = 512
MM_TM = 1024
MM_TM_EPI = 512
MM_TM_DZ = 256
MESH = pl.DeviceIdType.MESH
ANY = pl.BlockSpec(memory_space=pl.ANY)


def _params(sem):
    return pltpu.CompilerParams(dimension_semantics=sem, vmem_limit_bytes=VMEM_LIMIT)


def _sigmoid(x):
    return 1.0 / (1.0 + jnp.exp(-x))


class _Comm(NamedTuple):
    arrays: list
    out_shapes: list
    aliases: dict
    sems: list
    phases: tuple


def _call(body, *, name, grid, in_specs, out_specs, out_shape, scratch_shapes, semantics, args, comm=None, aliases=None):
    n_in, n_out = len(in_specs), len(out_specs)
    aliases = dict(aliases or {})
    if comm is None:
        res = pl.pallas_call(body, name=name, grid=grid, in_specs=in_specs, out_specs=out_specs, out_shape=out_shape,
                             scratch_shapes=scratch_shapes, input_output_aliases=aliases,
                             compiler_params=_params(semantics))(*args)
        return res, []
    nci, nco, ncs = len(comm.arrays), len(comm.out_shapes), len(comm.sems)
    n_steps = 1
    for g in grid:
        n_steps *= g
    first, middle, last = comm.phases

    def hosted(*refs):
        ins, cin = refs[:n_in], refs[n_in:n_in + nci]
        outs = refs[n_in + nci:n_in + nci + n_out]
        cout = refs[n_in + nci + n_out:n_in + nci + n_out + nco]
        rest = refs[n_in + nci + n_out + nco:]
        scratch, sems = rest[:len(rest) - ncs], rest[len(rest) - ncs:]
        step = 0
        for ax, g in enumerate(grid):
            step = step * g + pl.program_id(ax)
        for at, fn in ((0, first), (n_steps // 2, middle)):
            if fn is not None:
                pl.when(step == at)(lambda fn=fn: fn(cin, cout, sems))
        body(*ins, *outs, *scratch)
        if last is not None:
            pl.when(step == n_steps - 1)(lambda: last(cin, cout, sems))

    res = pl.pallas_call(
        hosted, name=name, grid=grid,
        in_specs=list(in_specs) + [ANY] * nci, out_specs=list(out_specs) + [ANY] * nco,
        out_shape=list(out_shape) + list(comm.out_shapes),
        input_output_aliases={**aliases, **{n_in + a: n_out + b for a, b in comm.aliases.items()}},
        scratch_shapes=list(scratch_shapes) + list(comm.sems),
        compiler_params=_params(("arbitrary",) * len(grid)),
    )(*args, *comm.arrays)
    return res[:n_out], res[n_out:]


def _mm(name, a, b, *, tm, tn, tk, ta=False, tb=False, b_cm=False, out_cm=False,
        a_fn=None, extras=(), epi=None, out_dtypes=(BF,), epi_rows=256, comm=None, b_resident=False,
        out_overrides=None):
    if ta:
        kc, m = a.shape
    else:
        m, kc = a.shape
    if b_cm:
        nc, r, c = b.shape
        n, per = (r, c) if tb else (nc * c, c)
    else:
        n = b.shape[0] if tb else b.shape[1]
    tm, tn, tk = min(tm, m), min(tn, n), min(tk, kc)
    assert m % tm == 0 and n % tn == 0 and kc % tk == 0, (name, m, n, kc, tm, tn, tk)
    nk = kc // tk
    a_spec = pl.BlockSpec((tk, tm), lambda i, j, k: (k, i)) if ta else pl.BlockSpec((tm, tk), lambda i, j, k: (i, k))
    if b_cm and not tb:
        assert per % tn == 0
        npj = per // tn
        b_spec = pl.BlockSpec((None, tk, tn), lambda i, j, k: (j // npj, k, j % npj))
    elif b_cm:
        assert per % tk == 0
        npk = per // tk
        b_spec = pl.BlockSpec((None, tn, tk), lambda i, j, k: (k // npk, j, k % npk))
    elif b_resident:
        assert nk == 1
        b_spec = pl.BlockSpec(b.shape, lambda i, j, k: (0, 0))
    elif tb:
        b_spec = pl.BlockSpec((tn, tk), lambda i, j, k: (j, k))
    else:
        b_spec = pl.BlockSpec((tk, tn), lambda i, j, k: (k, j))
    if out_cm:
        assert (n // N_CHIPS) % tn == 0
        npo = (n // N_CHIPS) // tn
        o_spec = pl.BlockSpec((None, tm, tn), lambda i, j, k: (j // npo, i, j % npo))
        o_shape = (N_CHIPS, m, n // N_CHIPS)
    else:
        o_spec = pl.BlockSpec((tm, tn), lambda i, j, k: (i, j))
        o_shape = (m, n)
    ne, no = len(extras), len(out_dtypes)
    whole = [len(e) > 3 and e[3] for e in extras]
    extras = [e[:3] for e in extras]

    def epi_args(ex, rows):
        return [e[...] if w else e[rows, :] for e, w in zip(ex, whole)]

    dims = (((0 if ta else 1,), (1 if tb else 0,)), ((), ()))
    er = min(epi_rows, tm)
    chunked = nk == 1 and epi is not None and not ta
    use_acc = (nk > 1 or epi is not None) and not chunked
    assert chunked or not b_resident

    def body(*refs):
        a_ref, b_ref = refs[0], refs[1]
        ex = refs[2:2 + ne]
        outs = refs[2 + ne:2 + ne + no]
        if chunked:
            if b_resident:
                cols = pl.ds(pl.multiple_of(pl.program_id(1) * tn, tn), tn)
                bt = b_ref[cols, :] if tb else b_ref[:, cols]
            else:
                bt = b_ref[...]
            for r0 in range(0, tm, er):
                rows = slice(r0, r0 + er)
                at = a_ref[rows, :]
                if a_fn is not None:
                    at = a_fn(at)
                d = lax.dot_general(at, bt, dims, preferred_element_type=F32)
                vals = epi(d, *epi_args(ex, rows))
                for o, v, dt in zip(outs, vals, out_dtypes):
                    o[rows, :] = v.astype(dt)
            return
        at = a_ref[...]
        if a_fn is not None:
            at = a_fn(at)
        d = lax.dot_general(at, b_ref[...], dims, preferred_element_type=F32)
        if not use_acc:
            outs[0][...] = d.astype(out_dtypes[0])
            return
        acc = refs[-1]
        k = pl.program_id(2)

        @pl.when(k == 0)
        def _():
            acc[...] = d

        if nk > 1:
            @pl.when(k > 0)
            def _():
                acc[...] += d

        @pl.when(k == nk - 1)
        def _():
            for r0 in range(0, tm, er):
                rows = slice(r0, r0 + er)
                if epi is None:
                    vals = (acc[rows, :],)
                else:
                    vals = epi(acc[rows, :], *epi_args(ex, rows))
                for o, v, dt in zip(outs, vals, out_dtypes):
                    o[rows, :] = v.astype(dt)

    out_specs = [o_spec] * no
    out_shapes = [jax.ShapeDtypeStruct(o_shape, dt) for dt in out_dtypes]
    for idx, (shape, spec) in (out_overrides or {}).items():
        out_specs[idx], out_shapes[idx] = spec, jax.ShapeDtypeStruct(shape, out_dtypes[idx])
    res, cres = _call(
        body, name=name, grid=(m // tm, n // tn, nk),
        in_specs=[a_spec, b_spec] + [pl.BlockSpec(bs, im) for _, bs, im in extras],
        out_specs=out_specs,
        out_shape=out_shapes,
        scratch_shapes=[pltpu.VMEM((tm, tn), F32)] if use_acc else [],
        semantics=("parallel", "parallel", "arbitrary"),
        args=[a, b] + [e for e, _, _ in extras], comm=comm)
    return res if comm is None else (res, cres)


def _tile_extra(arr, tm, tn, col_block0=0):
    return (arr, (tm, tn), lambda i, j, k: (i, j + col_block0))


def _rms_fwd(name, x, g, ts=None, comm=None):
    s, d = x.shape
    ts = ts or ROW_TILE

    def body(x_ref, g_ref, h_ref):
        xv = x_ref[...]
        r = lax.rsqrt(jnp.mean(xv * xv, axis=-1, keepdims=True) + EPS)
        h_ref[...] = (xv * r * g_ref[...]).astype(BF)

    (h,), cres = _call(
        body, name=name, grid=(s // ts,),
        in_specs=[pl.BlockSpec((ts, d), lambda i: (i, 0)), pl.BlockSpec((1, d), lambda i: (0, 0))],
        out_specs=[pl.BlockSpec((ts, d), lambda i: (i, 0))],
        out_shape=[jax.ShapeDtypeStruct((s, d), BF)],
        scratch_shapes=[], semantics=("parallel",), args=[x, g], comm=comm)
    return h if comm is None else (h, cres)


def _rms_bwd(name, dh, x, g, dres, ts=None):
    s, d = x.shape
    ts = ts or ROW_TILE

    def body(dh_ref, x_ref, g_ref, dres_ref, dx_ref, dxb_ref, dg_ref):
        xv = x_ref[...]
        dhv = dh_ref[...].astype(F32)
        r = lax.rsqrt(jnp.mean(xv * xv, axis=-1, keepdims=True) + EPS)
        nrm = xv * r
        dn = dhv * g_ref[...]
        dx = dres_ref[...] + r * (dn - nrm * jnp.mean(dn * nrm, axis=-1, keepdims=True))
        dx_ref[...] = dx
        dxb_ref[...] = dx.astype(BF)
        part = jnp.sum(dhv * nrm, axis=0, keepdims=True)

        @pl.when(pl.program_id(0) == 0)
        def _():
            dg_ref[...] = part

        @pl.when(pl.program_id(0) > 0)
        def _():
            dg_ref[...] += part

    row = pl.BlockSpec((ts, d), lambda i: (i, 0))
    vec = pl.BlockSpec((1, d), lambda i: (0, 0))
    return pl.pallas_call(
        body, name=name, grid=(s // ts,),
        in_specs=[row, row, vec, row],
        out_specs=[row, row, vec],
        out_shape=[jax.ShapeDtypeStruct((s, d), F32), jax.ShapeDtypeStruct((s, d), BF), jax.ShapeDtypeStruct((1, d), F32)],
        compiler_params=_params(("arbitrary",)),
    )(dh, x, g, dres)


def _loss_bwd(x3, tgt, gfin, e, gate, ts=None):
    s, d = x3.shape
    ts = ts or ROW_TILE

    def body(x_ref, t_ref, g_ref, e_ref, gate_ref, dx_ref, de_ref, dgp_ref, sq_ref, dg_ref):
        xv = x_ref[...]
        gv = g_ref[...]
        r = lax.rsqrt(jnp.mean(xv * xv, axis=-1, keepdims=True) + EPS)
        nrm = xv * r
        err = nrm * gv - t_ref[...]
        dy = err * (1.0 / d)
        dn = dy * gv
        dx = r * (dn - nrm * jnp.mean(dn * nrm, axis=-1, keepdims=True))
        dx_ref[...] = dx
        ev = e_ref[...].astype(F32)
        gt = gate_ref[...].astype(F32)
        de_ref[...] = (dx * gt).astype(BF)
        dgp_ref[...] = (dx * ev * gt * (1.0 - gt)).astype(BF)
        sq = jnp.full((8, LANES), jnp.sum(err * err), F32)
        part = jnp.sum(dy * nrm, axis=0, keepdims=True)

        @pl.when(pl.program_id(0) == 0)
        def _():
            sq_ref[...] = sq
            dg_ref[...] = part

        @pl.when(pl.program_id(0) > 0)
        def _():
            sq_ref[...] += sq
            dg_ref[...] += part

    row = pl.BlockSpec((ts, d), lambda i: (i, 0))
    vec = pl.BlockSpec((1, d), lambda i: (0, 0))
    return pl.pallas_call(
        body, name="loss_bwd", grid=(s // ts,),
        in_specs=[row, row, vec, row, row],
        out_specs=[row, row, row, pl.BlockSpec((8, LANES), lambda i: (0, 0)), vec],
        out_shape=[jax.ShapeDtypeStruct((s, d), F32), jax.ShapeDtypeStruct((s, d), BF), jax.ShapeDtypeStruct((s, d), BF),
                   jax.ShapeDtypeStruct((8, LANES), F32), jax.ShapeDtypeStruct((1, d), F32)],
        compiler_params=_params(("arbitrary",)),
    )(x3, tgt, gfin, e, gate)


def _halo_specs(ts, s, width, col_block):
    per = ts // HALO
    last = s // HALO - 1
    return [
        pl.BlockSpec((HALO, width), lambda i: (jnp.maximum(i * per - 1, 0), col_block)),
        pl.BlockSpec((ts, width), lambda i: (i, col_block)),
        pl.BlockSpec((HALO, width), lambda i: (jnp.minimum((i + 1) * per, last), col_block)),
    ]


def _glu_ext(zp, zc, zn, ext, cw, ts, i, n_tiles):
    def glu(zr):
        zv = zr[...].astype(F32)
        return zv[:, :cw] * _sigmoid(zv[:, cw:])

    ext[0:HALO, :] = jnp.where(i > 0, glu(zp), 0.0)
    ext[HALO:HALO + ts, :] = glu(zc)
    ext[HALO + ts:, :] = jnp.where(i < n_tiles - 1, glu(zn), 0.0)


SUBLANES = 8


def _shift_scratch(ts):
    return pltpu.VMEM((SUBLANES, ts + 2 * HALO - SUBLANES, LANES), F32)


def _shifted_copies(ext, sh, cols, ts):
    n = ts + 2 * HALO - SUBLANES
    for r in range(SUBLANES):
        sh[r] = ext[r:r + n, cols]


def _tap_rows(sh, off, ts):
    q, r = divmod(off, SUBLANES)
    return sh[r, q * SUBLANES:q * SUBLANES + ts, :]


def _ln_stats(uc):
    mu = jnp.mean(uc, axis=-1, keepdims=True)
    xc = uc - mu
    rstd = lax.rsqrt(jnp.mean(xc * xc, axis=-1, keepdims=True) + EPS)
    return xc * rstd, rstd


def _conv_fwd(z, wdw, ln_g, ln_b, cw, ts=None):
    s = z.shape[0]
    ts = ts or ROW_TILE
    n_tiles = s // ts
    pad = CONV_KERNEL // 2

    def body(zp, zc, zn, w_ref, g_ref, b_ref, uc_ref, act_ref, ext, sh):
        i = pl.program_id(0)
        _glu_ext(zp, zc, zn, ext, cw, ts, i, n_tiles)

        def col_block(cb, carry):
            cols = pl.ds(pl.multiple_of(cb * LANES, LANES), LANES)
            _shifted_copies(ext, sh, cols, ts)
            acc = jnp.zeros((ts, LANES), F32)
            for j in range(CONV_KERNEL):
                acc = acc + _tap_rows(sh, HALO - pad + j, ts) * w_ref[j:j + 1, cols]
            uc_ref[:, cols] = acc
            return carry

        lax.fori_loop(0, cw // LANES, col_block, 0)
        xhat, _ = _ln_stats(uc_ref[...])
        ln = xhat * g_ref[...] + b_ref[...]
        act_ref[...] = (ln * _sigmoid(ln)).astype(BF)

    vec = pl.BlockSpec((1, cw), lambda i: (0, 0))
    row = pl.BlockSpec((ts, cw), lambda i: (i, 0))
    return pl.pallas_call(
        body, name="conv_fwd", grid=(n_tiles,),
        in_specs=_halo_specs(ts, s, 2 * cw, 0) + [pl.BlockSpec((32, cw), lambda i: (0, 0)), vec, vec],
        out_specs=[row, row],
        out_shape=[jax.ShapeDtypeStruct((s, cw), F32), jax.ShapeDtypeStruct((s, cw), BF)],
        scratch_shapes=[pltpu.VMEM((ts + 2 * HALO, cw), F32), _shift_scratch(ts)],
        compiler_params=_params(("parallel",)),
    )(z, z, z, wdw, ln_g, ln_b)


def _conv_bwd(ds, uc, z, wdw, ln_g, ln_b, cw, dz, ts=None, comm=None):
    s = z.shape[0]
    ts = ts or ROW_TILE
    n_tiles = s // ts
    pad = CONV_KERNEL // 2

    def body(zp, zc, zn, dsp, dsc, dsn, ucp, ucc, ucn, w_ref, g_ref, b_ref, _dz_in,
             dz_ref, dw_ref, dg_ref, db_ref, ext, dext, sh, dsh):
        i = pl.program_id(0)
        gv, bv = g_ref[...], b_ref[...]

        def ln_bwd(ds_r, uc_r):
            xhat, rstd = _ln_stats(uc_r[...])
            ln = xhat * gv + bv
            sg = _sigmoid(ln)
            dln = ds_r[...].astype(F32) * (sg * (1.0 + ln * (1.0 - sg)))
            dxh = dln * gv
            duc = rstd * (dxh - jnp.mean(dxh, axis=-1, keepdims=True) - xhat * jnp.mean(dxh * xhat, axis=-1, keepdims=True))
            return duc, dln, xhat

        duc_p, _, _ = ln_bwd(dsp, ucp)
        duc_c, dln_c, xhat_c = ln_bwd(dsc, ucc)
        duc_n, _, _ = ln_bwd(dsn, ucn)
        dext[0:HALO, :] = jnp.where(i > 0, duc_p, 0.0)
        dext[HALO:HALO + ts, :] = duc_c
        dext[HALO + ts:, :] = jnp.where(i < n_tiles - 1, duc_n, 0.0)
        _glu_ext(zp, zc, zn, ext, cw, ts, i, n_tiles)

        dg_part = jnp.sum(dln_c * xhat_c, axis=0, keepdims=True)
        db_part = jnp.sum(dln_c, axis=0, keepdims=True)

        @pl.when(i == 0)
        def _():
            dw_ref[...] = jnp.zeros_like(dw_ref)
            dg_ref[...] = dg_part
            db_ref[...] = db_part

        @pl.when(i > 0)
        def _():
            dg_ref[...] += dg_part
            db_ref[...] += db_part

        def col_block(cb, carry):
            c0 = pl.multiple_of(cb * LANES, LANES)
            cols, gate_cols = pl.ds(c0, LANES), pl.ds(cw + c0, LANES)
            _shifted_copies(dext, dsh, cols, ts)
            _shifted_copies(ext, sh, cols, ts)
            du = jnp.zeros((ts, LANES), F32)
            for j in range(CONV_KERNEL):
                du = du + _tap_rows(dsh, HALO + pad - j, ts) * w_ref[j:j + 1, cols]
            ca, sb = zc[:, cols].astype(F32), _sigmoid(zc[:, gate_cols].astype(F32))
            dz_ref[:, cols] = (du * sb).astype(BF)
            dz_ref[:, gate_cols] = (du * ca * sb * (1.0 - sb)).astype(BF)
            duc_blk = _tap_rows(dsh, HALO, ts)
            for j in range(CONV_KERNEL):
                dw_ref[j:j + 1, cols] += jnp.sum(_tap_rows(sh, HALO - pad + j, ts) * duc_blk, axis=0, keepdims=True)
            return carry

        lax.fori_loop(0, cw // LANES, col_block, 0)

    vec = pl.BlockSpec((1, cw), lambda i: (0, 0))
    wsp = pl.BlockSpec((32, cw), lambda i: (0, 0))
    res, cres = _call(
        body, name="conv_bwd", grid=(n_tiles,),
        in_specs=(_halo_specs(ts, s, 2 * cw, 0) + _halo_specs(ts, s, cw, 0) + _halo_specs(ts, s, cw, 0)
                  + [wsp, vec, vec, ANY]),
        out_specs=[pl.BlockSpec((ts, 2 * cw), lambda i: (i, 0)), wsp, vec, vec],
        out_shape=[jax.ShapeDtypeStruct(dz.shape, BF), jax.ShapeDtypeStruct((32, cw), F32),
                   jax.ShapeDtypeStruct((1, cw), F32), jax.ShapeDtypeStruct((1, cw), F32)],
        scratch_shapes=[pltpu.VMEM((ts + 2 * HALO, cw), F32), pltpu.VMEM((ts + 2 * HALO, cw), F32),
                        _shift_scratch(ts), _shift_scratch(ts)],
        semantics=("arbitrary",), args=[z, z, z, ds, ds, ds, uc, uc, uc, wdw, ln_g, ln_b, dz], comm=comm,
        aliases={12: 0})
    return (*res, cres)


def _rope_tables(s):
    axis_dim = HEAD_DIM // 2
    n_rows = s // GRID_W
    inv_freq = ROPE_THETA ** (-jnp.arange(0, axis_dim, 2, dtype=F32) / axis_dim)[None, :]
    ar = jnp.arange(n_rows, dtype=jnp.int32).astype(F32)[:, None] * inv_freq
    ac = jnp.arange(GRID_W, dtype=jnp.int32).astype(F32)[:, None] * inv_freq

    def table(fr, fc):
        by_row = jnp.broadcast_to(fr[:, None, :], (n_rows, GRID_W, axis_dim))
        by_col = jnp.broadcast_to(fc[None, :, :], (n_rows, GRID_W, axis_dim))
        return jnp.concatenate([by_row, by_col], axis=-1).reshape(s, HEAD_DIM)

    cos = table(jnp.concatenate([jnp.cos(ar), jnp.cos(ar)], axis=-1), jnp.concatenate([jnp.cos(ac), jnp.cos(ac)], axis=-1))
    sin = table(jnp.concatenate([-jnp.sin(ar), jnp.sin(ar)], axis=-1), jnp.concatenate([-jnp.sin(ac), jnp.sin(ac)], axis=-1))
    return cos, sin


def _swap_quarters(x):
    q = HEAD_DIM // 4
    lane = lax.broadcasted_iota(jnp.int32, x.shape, 1)
    return jnp.where((lane % (2 * q)) < q, pltpu.roll(x, HEAD_DIM - q, 1), pltpu.roll(x, q, 1))


def _qk_fwd(z, cos, sin, qg, kg, d, ts=None):
    s = z.shape[0]
    ts = ts or ROW_TILE
    kvw = d // GROUP
    scale = Q_SCALE

    def body(q_ref, k_ref, c_ref, s_ref, qg_ref, kg_ref, qo_ref, ko_ref):
        cv, sv = c_ref[...], s_ref[...]

        def head(x_ref, g_ref, o_ref, h, mul):
            xv = x_ref[:, h * HEAD_DIM:(h + 1) * HEAD_DIM].astype(F32)
            r = lax.rsqrt(jnp.mean(xv * xv, axis=-1, keepdims=True) + EPS)
            nrm = xv * r * g_ref[...]
            out = nrm * cv + _swap_quarters(nrm) * sv
            o_ref[:, h * HEAD_DIM:(h + 1) * HEAD_DIM] = (out * mul).astype(BF)

        for h in range(d // HEAD_DIM):
            head(q_ref, qg_ref, qo_ref, h, scale)
        for h in range(kvw // HEAD_DIM):
            head(k_ref, kg_ref, ko_ref, h, 1.0)

    cw2 = d
    tab = pl.BlockSpec((ts, HEAD_DIM), lambda i: (i, 0))
    vec = pl.BlockSpec((1, HEAD_DIM), lambda i: (0, 0))
    return pl.pallas_call(
        body, name="qk_fwd", grid=(s // ts,),
        in_specs=[pl.BlockSpec((ts, d), lambda i: (i, cw2 // d)),
                  pl.BlockSpec((ts, kvw), lambda i: (i, (cw2 + d) // kvw)), tab, tab, vec, vec],
        out_specs=[pl.BlockSpec((ts, d), lambda i: (i, 0)), pl.BlockSpec((ts, kvw), lambda i: (i, 0))],
        out_shape=[jax.ShapeDtypeStruct((s, d), BF), jax.ShapeDtypeStruct((s, kvw), BF)],
        compiler_params=_params(("parallel",)),
    )(z, z, cos, sin, qg, kg)


def _qk_bwd(dqt, dkt, z, cos, sin, qg, kg, d, dz, ts=None):
    s = z.shape[0]
    ts = ts or ROW_TILE
    kvw = d // GROUP
    scale = HEAD_DIM ** -0.5

    def body(dq_ref, dk_ref, q_ref, k_ref, c_ref, s_ref, qg_ref, kg_ref, _dz_in, dzo_ref, dqg_ref, dkg_ref):
        cv, sv = c_ref[...], s_ref[...]

        def head(dy_ref, x_ref, g_ref, col0, h, mul):
            dout = dy_ref[:, h * HEAD_DIM:(h + 1) * HEAD_DIM].astype(F32) * mul
            dn = dout * cv + _swap_quarters(dout * sv)
            xv = x_ref[:, h * HEAD_DIM:(h + 1) * HEAD_DIM].astype(F32)
            r = lax.rsqrt(jnp.mean(xv * xv, axis=-1, keepdims=True) + EPS)
            nh = xv * r
            dnh = dn * g_ref[...]
            c0 = col0 + h * HEAD_DIM
            dzo_ref[:, c0:c0 + HEAD_DIM] = (r * (dnh - nh * jnp.mean(dnh * nh, axis=-1, keepdims=True))).astype(BF)
            return jnp.sum(dn * nh, axis=0, keepdims=True)

        dqg = jnp.zeros((1, HEAD_DIM), F32)
        for h in range(d // HEAD_DIM):
            dqg = dqg + head(dq_ref, q_ref, qg_ref, 0, h, scale)
        dkg = jnp.zeros((1, HEAD_DIM), F32)
        for h in range(kvw // HEAD_DIM):
            dkg = dkg + head(dk_ref, k_ref, kg_ref, d, h, LN2)

        @pl.when(pl.program_id(0) == 0)
        def _():
            dqg_ref[...] = dqg
            dkg_ref[...] = dkg

        @pl.when(pl.program_id(0) > 0)
        def _():
            dqg_ref[...] += dqg
            dkg_ref[...] += dkg

    cw2 = d
    tab = pl.BlockSpec((ts, HEAD_DIM), lambda i: (i, 0))
    vec = pl.BlockSpec((1, HEAD_DIM), lambda i: (0, 0))
    qrow = pl.BlockSpec((ts, d), lambda i: (i, 0))
    krow = pl.BlockSpec((ts, kvw), lambda i: (i, 0))
    window = pl.BlockSpec((pl.Element(ts), pl.Element(d + kvw)), lambda i: (i * ts, cw2))
    return pl.pallas_call(
        body, name="qk_bwd", grid=(s // ts,),
        in_specs=[qrow, krow, pl.BlockSpec((ts, d), lambda i: (i, cw2 // d)),
                  pl.BlockSpec((ts, kvw), lambda i: (i, (cw2 + d) // kvw)), tab, tab, vec, vec, ANY],
        out_specs=[window, vec, vec],
        out_shape=[jax.ShapeDtypeStruct(dz.shape, BF),
                   jax.ShapeDtypeStruct((1, HEAD_DIM), F32), jax.ShapeDtypeStruct((1, HEAD_DIM), F32)],
        input_output_aliases={8: 0},
        compiler_params=_params(("arbitrary",)),
    )(dqt, dkt, z, z, cos, sin, qg, kg, dz)


_NT = (((1,), (1,)), ((), ()))
_TN = (((0,), (0,)), ((), ()))


def _v_col_block(d):
    return (2 * d + d // GROUP) // HEAD_DIM


def _flash_fwd(qt, kt, z, d, tq=None, tk=None, comm=None):
    s = qt.shape[0]
    tq, tk = min(tq or FLASH_TQ_FWD, s), min(tk or FLASH_TK, s)
    ng, nq, nk = d // (GROUP * HEAD_DIM), s // tq, s // tk
    gw = GROUP * HEAD_DIM
    rows = GROUP * tq

    nt = tk // LANES
    assert nk % 2 == 0, (s, tk)

    def body(q_ref, k_ref, v_ref, o_ref, lse_ref, qs, v1, p_s, m_s, acc_s, sc_s):
        @pl.when(pl.program_id(1) == 0)
        def _():
            v1[:, :HEAD_DIM] = v_ref[...]
            v1[:, HEAD_DIM:] = jnp.ones((s, HEAD_DIM), BF)

        for h in range(GROUP):
            qs[h * tq:(h + 1) * tq, :] = q_ref[:, h * HEAD_DIM:(h + 1) * HEAD_DIM]
        m_s[...] = jnp.full((rows, LANES), -1e30, F32)
        acc_s[...] = jnp.zeros((rows, 2 * HEAD_DIM), F32)

        def scores(j):
            return lax.dot_general(qs[...], k_ref[pl.ds(pl.multiple_of(j * tk, tk), tk), :], _NT, preferred_element_type=F32)

        def softmax_pv(j, sc):
            kv_rows = pl.ds(pl.multiple_of(j * tk, tk), tk)
            mt = sc[:, :LANES]
            for c in range(1, nt):
                mt = jnp.maximum(mt, sc[:, c * LANES:(c + 1) * LANES])
            m_old = m_s[...]
            m_new = jnp.maximum(m_old, jnp.max(mt, axis=-1, keepdims=True))
            alpha = jnp.exp2(m_old - m_new)
            for c in range(nt):
                cs = slice(c * LANES, (c + 1) * LANES)
                p_s[:, cs] = jnp.exp2(sc[:, cs] - m_new).astype(BF)
            pv = jnp.dot(p_s[...], v1[kv_rows, :], preferred_element_type=F32)
            acc_s[:, :HEAD_DIM] = alpha * acc_s[:, :HEAD_DIM] + pv[:, :HEAD_DIM]
            acc_s[:, HEAD_DIM:] = alpha * acc_s[:, HEAD_DIM:] + pv[:, HEAD_DIM:]
            m_s[...] = m_new

        sc_s[0] = scores(0)

        def step(jj, carry):
            j = 2 * jj
            sc_s[1] = scores(j + 1)
            softmax_pv(j, sc_s[0])
            sc_s[0] = scores(jnp.minimum(j + 2, nk - 1))
            softmax_pv(j + 1, sc_s[1])
            return carry

        lax.fori_loop(0, nk // 2, step, 0)
        l = acc_s[:, HEAD_DIM:]
        o = acc_s[:, :HEAD_DIM] / l
        for h in range(GROUP):
            o_ref[:, h * HEAD_DIM:(h + 1) * HEAD_DIM] = o[h * tq:(h + 1) * tq, :].astype(BF)
        lse = m_s[...] + jnp.log2(l)
        for h in range(GROUP):
            lse_ref[h] = lse[h * tq:(h + 1) * tq, :]

    vb = _v_col_block(d)
    (o, lse), cres = _call(
        body, name="flash_fwd", grid=(ng, nq),
        in_specs=[pl.BlockSpec((tq, gw), lambda g, i: (i, g)),
                  pl.BlockSpec((s, HEAD_DIM), lambda g, i: (0, g)),
                  pl.BlockSpec((s, HEAD_DIM), lambda g, i: (0, vb + g))],
        out_specs=[pl.BlockSpec((tq, gw), lambda g, i: (i, g)),
                   pl.BlockSpec((GROUP, tq, LANES), lambda g, i: (g, i, 0))],
        out_shape=[jax.ShapeDtypeStruct((s, d), BF), jax.ShapeDtypeStruct((ng * GROUP, s, LANES), F32)],
        scratch_shapes=[pltpu.VMEM((rows, HEAD_DIM), BF), pltpu.VMEM((s, 2 * HEAD_DIM), BF), pltpu.VMEM((rows, tk), BF),
                        pltpu.VMEM((rows, LANES), F32), pltpu.VMEM((rows, 2 * HEAD_DIM), F32), pltpu.VMEM((2, rows, tk), F32)],
        semantics=("parallel", "arbitrary"), args=[qt, kt, z], comm=comm)
    return o, lse, cres


def _flash_bwd(qt, kt, z, o, do, lse, d, dz, tq=None, tk=None, comm=None):
    s = qt.shape[0]
    tq, tk = min(tq or FLASH_TQ_BWD, s), min(tk or FLASH_TK, s)
    ng, nq, nk = d // (GROUP * HEAD_DIM), s // tq, s // tk
    gw = GROUP * HEAD_DIM
    rows = GROUP * tq

    nt = tk // LANES

    def body(q_ref, k_ref, v_ref, o_ref, do_ref, lse_ref, _dz_in, dq_ref, dk_ref, dzv_ref,
             qs, dos, delta_s, dq_s, p_s, ds_s, lse_s, dv_ref):
        i = pl.program_id(1)
        for h in range(GROUP):
            cols = slice(h * HEAD_DIM, (h + 1) * HEAD_DIM)
            lse_s[h * tq:(h + 1) * tq, :] = lse_ref[h]
            qs[h * tq:(h + 1) * tq, :] = q_ref[:, cols]
            dov = do_ref[:, cols]
            dos[h * tq:(h + 1) * tq, :] = dov
            delta = jnp.sum(dov.astype(F32) * o_ref[:, cols].astype(F32), axis=-1, keepdims=True)
            delta_s[h * tq:(h + 1) * tq, :] = jnp.broadcast_to(delta, (tq, LANES))
        dq_s[...] = jnp.zeros((rows, HEAD_DIM), F32)

        @pl.when(i == 0)
        def _():
            dk_ref[...] = jnp.zeros_like(dk_ref)
            dv_ref[...] = jnp.zeros_like(dv_ref)

        def step(j, carry):
            kv_rows = pl.ds(pl.multiple_of(j * tk, tk), tk)
            kv, vv = k_ref[kv_rows, :], v_ref[kv_rows, :]
            sc = lax.dot_general(qs[...], kv, _NT, preferred_element_type=F32)
            dp = lax.dot_general(dos[...], vv, _NT, preferred_element_type=F32)
            lse, delta = lse_s[...], delta_s[...]
            for c in range(nt):
                cs = slice(c * LANES, (c + 1) * LANES)
                p = jnp.exp2(sc[:, cs] - lse)
                p_s[:, cs] = p.astype(BF)
                ds_s[:, cs] = (p * (dp[:, cs] - delta)).astype(BF)
            dv_ref[kv_rows, :] += lax.dot_general(p_s[...], dos[...], _TN, preferred_element_type=F32)
            dk_ref[kv_rows, :] += lax.dot_general(ds_s[...], qs[...], _TN, preferred_element_type=F32)
            dq_s[...] += jnp.dot(ds_s[...], kv, preferred_element_type=F32)
            return carry

        lax.fori_loop(0, nk, step, 0)
        for h in range(GROUP):
            dq_ref[:, h * HEAD_DIM:(h + 1) * HEAD_DIM] = dq_s[h * tq:(h + 1) * tq, :].astype(BF)

        @pl.when(i == nq - 1)
        def _():
            dzv_ref[...] = dv_ref[...].astype(BF)

    vb = _v_col_block(d)
    qspec = pl.BlockSpec((tq, gw), lambda g, i: (i, g))
    kspec = pl.BlockSpec((s, HEAD_DIM), lambda g, i: (0, g))
    vspec = pl.BlockSpec((s, HEAD_DIM), lambda g, i: (0, vb + g))
    (dq, dk, dz), cres = _call(
        body, name="flash_bwd", grid=(ng, nq),
        in_specs=[qspec, kspec, vspec, qspec, qspec, pl.BlockSpec((GROUP, tq, LANES), lambda g, i: (g, i, 0)), ANY],
        out_specs=[qspec, kspec, vspec],
        out_shape=[jax.ShapeDtypeStruct((s, d), BF), jax.ShapeDtypeStruct((s, d // GROUP), F32),
                   jax.ShapeDtypeStruct(dz.shape, BF)],
        scratch_shapes=[pltpu.VMEM((rows, HEAD_DIM), BF), pltpu.VMEM((rows, HEAD_DIM), BF), pltpu.VMEM((rows, LANES), F32),
                        pltpu.VMEM((rows, HEAD_DIM), F32), pltpu.VMEM((rows, tk), BF), pltpu.VMEM((rows, tk), BF),
                        pltpu.VMEM((rows, LANES), F32), pltpu.VMEM((s, HEAD_DIM), F32)],
        semantics=("parallel", "arbitrary"), args=[qt, kt, z, o, do, lse, dz], comm=comm, aliases={6: 2})
    return dq, dk, dz, cres


def _place():
    x, y, c = lax.axis_index("x"), lax.axis_index("y"), lax.axis_index("c")
    other_chips = [(1 - x, y), (x, 1 - y), (1 - x, 1 - y)]
    return x, y, c, other_chips


def _cast_place(name, w, chip_arr, tr=256):
    r, cc = w.shape
    tr = min(tr, r)

    def body(p_ref, w_ref, o_ref):
        o_ref[...] = w_ref[...].astype(BF)

    return pl.pallas_call(
        body, name=name,
        grid_spec=pltpu.PrefetchScalarGridSpec(
            num_scalar_prefetch=1, grid=(r // tr,),
            in_specs=[pl.BlockSpec((tr, cc), lambda i, p_ref: (i, 0))],
            out_specs=pl.BlockSpec((None, tr, cc), lambda i, p_ref: (p_ref[0], i, 0))),
        out_shape=jax.ShapeDtypeStruct((N_CHIPS, r, cc), BF),
        compiler_params=_params(("parallel",)),
    )(chip_arr, w)


def _gather_comm(bufs, short_host=False):
    n = len(bufs)
    pairs = [(w, j) for w in range(n) for j in range(N_CHIPS - 1)]

    def copies(dst, sems):
        send, recv, fsend, frecv = sems
        x, y, c, chips = _place()

        def part(w, chip, core_half):
            h = bufs[w].shape[1] // 2
            return dst[w].at[2 * chip[0] + chip[1], pl.ds(core_half * h, h)]

        def ici(w, j, incoming):
            slab = part(w, chips[j] if incoming else (x, y), c)
            return pltpu.make_async_remote_copy(
                src_ref=slab, dst_ref=slab, send_sem=send.at[3 * w + j], recv_sem=recv.at[3 * w + j],
                device_id=(*chips[j], c), device_id_type=MESH)

        def d2d(w, j, incoming):
            slab = part(w, chips[j], 1 - c if incoming else c)
            return pltpu.make_async_remote_copy(
                src_ref=slab, dst_ref=slab, send_sem=fsend.at[3 * w + j], recv_sem=frecv.at[3 * w + j],
                device_id=(x, y, 1 - c), device_id_type=MESH)

        return ici, d2d

    def first(_, dst, sems):
        ici, _d = copies(dst, sems)
        for w, j in pairs:
            ici(w, j, False).start()

    def middle(_, dst, sems):
        ici, d2d = copies(dst, sems)
        for w, j in pairs:
            ici(w, j, True).wait_recv()
            d2d(w, j, False).start()

    def last(_, dst, sems):
        ici, d2d = copies(dst, sems)
        for w, j in pairs:
            d2d(w, j, True).wait_recv()
        for w, j in pairs:
            ici(w, j, False).wait_send()
            d2d(w, j, False).wait_send()

    def middle_and_last(src, dst, sems):
        middle(src, dst, sems)
        last(src, dst, sems)

    phases = (first, None, middle_and_last) if short_host else (first, middle, last)
    return _Comm(arrays=list(bufs), out_shapes=[jax.ShapeDtypeStruct(b.shape, b.dtype) for b in bufs],
                 aliases={w: w for w in range(n)}, sems=[pltpu.SemaphoreType.DMA((3 * n,))] * 4, phases=phases)


def _run_comm(name, comm):
    nci, nco = len(comm.arrays), len(comm.out_shapes)

    def body(*refs):
        cin, cout, sems = refs[:nci], refs[nci:nci + nco], refs[nci + nco:]
        for fn in comm.phases:
            if fn is not None:
                fn(cin, cout, sems)

    return pl.pallas_call(
        body, name=name, in_specs=[ANY] * nci, out_specs=[ANY] * nco, out_shape=list(comm.out_shapes),
        input_output_aliases=dict(comm.aliases), scratch_shapes=list(comm.sems),
    )(*comm.arrays)


def _pair_comm(grads):
    n = len(grads)

    def copies(src, dst, sems):
        send, recv = sems
        x, y, c, _ = _place()
        out = []
        for w in range(n):
            h = grads[w].shape[1] // 2
            out.append(pltpu.make_async_remote_copy(
                src_ref=src[w].at[:, pl.ds((1 - c) * h, h), :], dst_ref=dst[w],
                send_sem=send.at[w], recv_sem=recv.at[w], device_id=(x, y, 1 - c), device_id_type=MESH))
        return out

    def first(src, dst, sems):
        for cp in copies(src, dst, sems):
            cp.start()

    def last(src, dst, sems):
        for cp in copies(src, dst, sems):
            cp.wait()

    return _Comm(arrays=list(grads),
                 out_shapes=[jax.ShapeDtypeStruct((N_CHIPS, g.shape[1] // 2, g.shape[2]), g.dtype) for g in grads],
                 aliases={}, sems=[pltpu.SemaphoreType.DMA((n,))] * 2, phases=(first, None, last))


def _pair_sum(name, own, got, c_arr, tr=256):
    nc, r, cc = own.shape
    h = r // 2
    tr = min(tr, h)
    nb = h // tr

    def body(c_ref, a_ref, b_ref, o_ref):
        o_ref[...] = (a_ref[...].astype(F32) + b_ref[...].astype(F32)).astype(BF)

    return pl.pallas_call(
        body, name=name,
        grid_spec=pltpu.PrefetchScalarGridSpec(
            num_scalar_prefetch=1, grid=(nc, nb),
            in_specs=[pl.BlockSpec((None, tr, cc), lambda s, i, c_ref: (s, c_ref[0] * nb + i, 0)),
                      pl.BlockSpec((None, tr, cc), lambda s, i, c_ref: (s, i, 0))],
            out_specs=pl.BlockSpec((None, tr, cc), lambda s, i, c_ref: (s, i, 0))),
        out_shape=jax.ShapeDtypeStruct((nc, h, cc), BF),
        compiler_params=_params(("parallel", "parallel")),
    )(c_arr, own, got)


def _chip_comm(parts):
    n = len(parts)

    def copies(src, dst, sems):
        send, recv = sems
        _, _, c, chips = _place()
        return [pltpu.make_async_remote_copy(
            src_ref=src[w].at[2 * chip[0] + chip[1]], dst_ref=dst[w].at[j],
            send_sem=send.at[3 * w + j], recv_sem=recv.at[3 * w + j], device_id=(*chip, c), device_id_type=MESH)
            for w in range(n) for j, chip in enumerate(chips)]

    def first(src, dst, sems):
        for cp in copies(src, dst, sems):
            cp.start()

    def last(src, dst, sems):
        for cp in copies(src, dst, sems):
            cp.wait()

    return _Comm(arrays=list(parts), out_shapes=[jax.ShapeDtypeStruct((N_CHIPS - 1,) + p.shape[1:], p.dtype) for p in parts],
                 aliases={}, sems=[pltpu.SemaphoreType.DMA((3 * n,))] * 2, phases=(first, None, last))


def _chip_sum(name, parts, got, chip_arr, c_arr, tr=256):
    _, h, cc = parts.shape
    tr = min(tr, h)
    nb = h // tr

    def body(chip_ref, c_ref, own_ref, got_ref, o_ref):
        acc = own_ref[...].astype(F32)
        for k in range(N_CHIPS - 1):
            acc = acc + got_ref[k].astype(F32)
        o_ref[...] = acc

    return pl.pallas_call(
        body, name=name,
        grid_spec=pltpu.PrefetchScalarGridSpec(
            num_scalar_prefetch=2, grid=(nb,),
            in_specs=[pl.BlockSpec((None, tr, cc), lambda i, chip_ref, c_ref: (chip_ref[0], i, 0)),
                      pl.BlockSpec((N_CHIPS - 1, tr, cc), lambda i, chip_ref, c_ref: (0, i, 0))],
            out_specs=pl.BlockSpec((tr, cc), lambda i, chip_ref, c_ref: (c_ref[0] * nb + i, 0))),
        out_shape=jax.ShapeDtypeStruct((2 * h, cc), F32),
        compiler_params=_params(("parallel",)),
    )(chip_arr, c_arr, parts, got)


def _pair_gather_comm(bufs):
    n = len(bufs)

    def copy(dst, sems, w, core_half):
        send, recv = sems
        x, y, c, _ = _place()
        h = bufs[w].shape[0] // 2
        rows = dst[w].at[pl.ds((1 - c if core_half == "theirs" else c) * h, h)]
        return pltpu.make_async_remote_copy(src_ref=rows, dst_ref=rows, send_sem=send.at[w], recv_sem=recv.at[w],
                                            device_id=(x, y, 1 - c), device_id_type=MESH)

    def first(_, dst, sems):
        for w in range(n):
            copy(dst, sems, w, "mine").start()

    def last(_, dst, sems):
        for w in range(n):
            copy(dst, sems, w, "theirs").wait_recv()
        for w in range(n):
            copy(dst, sems, w, "mine").wait_send()

    return _Comm(arrays=list(bufs), out_shapes=[jax.ShapeDtypeStruct(b.shape, b.dtype) for b in bufs],
                 aliases={w: w for w in range(n)}, sems=[pltpu.SemaphoreType.DMA((n,))] * 2, phases=(first, None, last))


def _merge_comms(a, b):
    nai, nao, nas = len(a.arrays), len(a.out_shapes), len(a.sems)

    def both(fa, fb):
        if fa is None and fb is None:
            return None

        def phase(cin, cout, sems):
            if fa is not None:
                fa(cin[:nai], cout[:nao], sems[:nas])
            if fb is not None:
                fb(cin[nai:], cout[nao:], sems[nas:])
        return phase

    return _Comm(arrays=a.arrays + b.arrays, out_shapes=a.out_shapes + b.out_shapes,
                 aliases={**a.aliases, **{nai + k: nao + v for k, v in b.aliases.items()}},
                 sems=a.sems + b.sems, phases=tuple(both(fa, fb) for fa, fb in zip(a.phases, b.phases)))


def _all_sum_small(name, v):
    p = v.shape[0]

    def body(v_ref, o_ref, slots, send, recv):
        x, y, c, _ = _place()
        me = 4 * x + 2 * y + c
        copies = []
        for k in range(1, N_DEV):
            peer = (x ^ (k >> 2), y ^ ((k >> 1) & 1), c ^ (k & 1))
            copies.append(pltpu.make_async_remote_copy(
                src_ref=v_ref, dst_ref=slots.at[me], send_sem=send.at[k - 1], recv_sem=recv.at[k - 1],
                device_id=peer, device_id_type=MESH))
        for cp in copies:
            cp.start()
        slots[me] = v_ref[...]
        for cp in copies:
            cp.wait()
        acc = slots[0]
        for s in range(1, N_DEV):
            acc = acc + slots[s]
        o_ref[...] = acc

    vm = pl.BlockSpec(memory_space=pltpu.VMEM)
    return pl.pallas_call(
        body, name=name,
        in_specs=[vm], out_specs=vm,
        out_shape=jax.ShapeDtypeStruct(v.shape, F32),
        scratch_shapes=[pltpu.VMEM((N_DEV, p, LANES), F32), pltpu.SemaphoreType.DMA((N_DEV - 1,)),
                        pltpu.SemaphoreType.DMA((N_DEV - 1,))],
    )(v)


def _adamw(name, w, g, m, v, tr=256):
    r, c = w.shape
    tr = min(tr, r)
    assert r % tr == 0
    bc1 = 1.0 - ADAM_B1 ** ADAM_STEP
    bc2 = 1.0 - ADAM_B2 ** ADAM_STEP

    def body(w_ref, g_ref, m_ref, v_ref, d_ref, nm_ref, nv_ref):
        gv = g_ref[...]
        nm = ADAM_B1 * m_ref[...] + (1.0 - ADAM_B1) * gv
        nv = ADAM_B2 * v_ref[...] + (1.0 - ADAM_B2) * (gv * gv)
        nm_ref[...] = nm
        nv_ref[...] = nv
        d_ref[...] = -ADAM_LR * ((nm / bc1) / (jnp.sqrt(nv / bc2) + ADAM_EPS) + ADAM_WD * w_ref[...])

    blk = pl.BlockSpec((tr, c), lambda i: (i, 0))
    return pl.pallas_call(
        body, name=name, grid=(r // tr,),
        in_specs=[blk] * 4, out_specs=[blk] * 3,
        out_shape=[jax.ShapeDtypeStruct((r, c), F32)] * 3,
        compiler_params=_params(("parallel",)),
    )(w, g, m, v)


def _pack_small(parts):
    flat = jnp.concatenate([a.reshape(-1) for a in parts])
    n = flat.shape[0]
    p = -(-n // (8 * LANES)) * 8
    packed = jnp.pad(flat, (0, p * LANES - n)).reshape(p, LANES)

    def unpack(q):
        out, off = [], 0
        f = q.reshape(-1)
        for a in parts:
            out.append(f[off:off + a.size].reshape(a.shape))
            off += a.size
        return out

    return packed, unpack


def kernel(x, p, norm_mix, w_in, w_dw, conv_ln_g, conv_ln_b, w_conv_proj, q_norm, k_norm, w_attn_proj, w_out, norm_ffn, w_ff1, w_ff2, norm_ple, w_ple_gate, w_ple_proj, norm_final, loss_target, m_norm_mix, m_w_in, m_w_dw, m_conv_ln_g, m_conv_ln_b, m_w_conv_proj, m_q_norm, m_k_norm, m_w_attn_proj, m_w_out, m_norm_ffn, m_w_ff1, m_w_ff2, m_norm_ple, m_w_ple_gate, m_w_ple_proj, m_norm_final, v_norm_mix, v_w_in, v_w_dw, v_conv_ln_g, v_conv_ln_b, v_w_conv_proj, v_q_norm, v_k_norm, v_w_attn_proj, v_w_out, v_norm_ffn, v_w_ff1, v_w_ff2, v_norm_ple, v_w_ple_gate, v_w_ple_proj, v_norm_final):
    s, d = x.shape[1], x.shape[2]
    cw = d // 2
    kvw = d // GROUP
    xs, ps, tgt = x[0], p[0, 0], loss_target[0]
    cx, cy, cc = lax.axis_index("x"), lax.axis_index("y"), lax.axis_index("c")
    chip = 2 * cx + cy
    c_arr = jnp.reshape(cc, (1,)).astype(jnp.int32)
    tm, tme = min(MM_TM, s), min(MM_TM_EPI, s)

    names = ["w_in", "w_conv_proj", "w_attn_proj", "w_out", "w_ff1", "w_ff2", "w_ple_gate", "w_ple_proj"]
    big = [w_in, w_conv_proj, w_attn_proj, w_out, w_ff1, w_ff2, w_ple_gate, w_ple_proj]
    chip_arr = jnp.reshape(chip, (1,)).astype(jnp.int32)
    placed = [_cast_place("cast_" + nm, w[0], chip_arr) for nm, w in zip(names, big)]
    cpc = cw // N_CHIPS
    taps_rows = 32
    my_taps = jnp.pad(w_dw[0], ((0, taps_rows - CONV_KERNEL), (0, 0)))[None]
    taps_buf = lax.dynamic_update_slice(jnp.zeros((N_CHIPS, taps_rows, cpc), F32), my_taps, (chip, 0, 0))
    h0, (win, taps_all) = _rms_fwd("rms_mix", xs, norm_mix, comm=_gather_comm([placed[0], taps_buf], short_host=True))
    wdw = taps_all.transpose(1, 0, 2).reshape(taps_rows, cw)

    cos, sin = _rope_tables(s)
    (z,) = _mm("z_proj", h0, win, b_cm=True, tm=tm, tn=win.shape[2] // 3, tk=d)
    uc, act = _conv_fwd(z, wdw, conv_ln_g, conv_ln_b, cw)
    qt, kt = _qk_fwd(z, cos, sin, q_norm, k_norm, d)
    o, lse, (wcp, wap, wout, w1, w2, wpg, wple) = _flash_fwd(qt, kt, z, d, comm=_gather_comm(placed[1:]))
    wap, wout, w2, wpg = (t.reshape(-1, t.shape[-1]) for t in (wap, wout, w2, wpg))
    (y_c,) = _mm("conv_proj", act, wcp, b_cm=True, tm=tm, tn=wcp.shape[2], tk=cw, out_dtypes=(F32,))
    tn = d // 2
    gcb = (2 * d + 2 * kvw) // tn

    def merge_epi(acc, yc, gc, ga):
        return acc, _sigmoid(gc.astype(F32)) * yc + _sigmoid(ga.astype(F32)) * acc

    y_a, merged = _mm("attn_proj", o, wap, tm=tme, tn=tn, tk=d, epi=merge_epi, out_dtypes=(BF, BF), b_resident=True,
                      extras=[_tile_extra(y_c, tme, tn), _tile_extra(z, tme, tn, gcb), _tile_extra(z, tme, tn, gcb + 2)])
    def residual_norm(acc, r, g):
        xn = r + acc
        return xn, xn * lax.rsqrt(jnp.mean(xn * xn, axis=-1, keepdims=True) + EPS) * g

    gain = lambda g: (g, (1, d), lambda i, j, k: (0, 0), True)
    x1, h1 = _mm("out_proj", merged, wout, tm=tme, tn=d, tk=d, epi=residual_norm, out_dtypes=(F32, BF), b_resident=True,
                 extras=[_tile_extra(xs, tme, d), gain(norm_ffn)])
    (a,) = _mm("ff1", h1, w1, b_cm=True, tm=tm, tn=tn, tk=d)

    def relu2(t):
        return jnp.square(jnp.maximum(t, 0.0))

    x2, h2 = _mm("ff2", a, w2, tm=tme, tn=d, tk=d, a_fn=relu2, epi=residual_norm, out_dtypes=(F32, BF),
                 extras=[_tile_extra(x1, tme, d), gain(norm_ple)])
    to_bf = lambda t: t.astype(BF)
    (e,) = _mm("ple_proj", ps, wple, b_cm=True, tm=tm, tn=wple.shape[2], tk=ps.shape[1], a_fn=to_bf)

    def ple_epi(acc, ev, r):
        gt = _sigmoid(acc)
        return r + gt * ev.astype(F32), gt

    x3, gate = _mm("ple_gate", h2, wpg, tm=tme, tn=tn, tk=d, epi=ple_epi, out_dtypes=(F32, BF), b_resident=True,
                   extras=[_tile_extra(e, tme, tn), _tile_extra(x2, tme, tn)])

    dx3, de, dgp, sq, d_fin = _loss_bwd(x3, tgt, norm_final.reshape(1, d), e, gate)
    tkt = min(2048, s)
    (g_wple,) = _mm("d_wple", ps, de, ta=True, out_cm=True, tm=ps.shape[1], tn=wple.shape[2], tk=tkt, a_fn=to_bf)
    (g_wpg,) = _mm("d_wpg", h2, dgp, ta=True, tm=tm, tn=tn, tk=tkt)
    (dh2,) = _mm("d_h2", dgp, wpg, tb=True, tm=tm, tn=tn, tk=d)
    dx2, dx2b, d_ple = _rms_bwd("rms_ple_bwd", dh2, x2, norm_ple, dx3)

    (da,) = _mm("d_a", dx2b, w2, tb=True, tm=tm, tn=tn, tk=d, out_dtypes=(BF,),
                epi=lambda acc, av: (acc * (2.0 * jnp.maximum(av.astype(F32), 0.0)),), extras=[_tile_extra(a, tm, tn)])
    (g_w2,) = _mm("d_w2", a, dx2b, ta=True, tm=tm, tn=tn, tk=tkt, a_fn=relu2)
    (g_w1,) = _mm("d_w1", h1, da, ta=True, out_cm=True, tm=tm, tn=tn, tk=tkt)
    (dh1,) = _mm("d_h1", da, w1, tb=True, b_cm=True, tm=tm, tn=tn, tk=w1.shape[2])
    dx1, dx1b, d_ffn = _rms_bwd("rms_ffn_bwd", dh1, x1, norm_ffn, dx2)

    def merge_bwd(acc, gc, ga, yc, ya):
        sc, sa = _sigmoid(gc.astype(F32)), _sigmoid(ga.astype(F32))
        return acc * sc, acc * sa, jnp.concatenate(
            [acc * yc * sc * (1.0 - sc), acc * ya.astype(F32) * sa * (1.0 - sa)], axis=1)

    tmd = min(MM_TM_DZ, s)
    gate0 = 2 * d + 2 * kvw
    z_cols = z.shape[1]

    def gate_window(width, col0):
        return (pl.Element(tmd), pl.Element(width)), lambda i, j, k: (i * tmd, col0)

    dy_c, dy_a, dz = _mm(
        "d_merged", dx1b, wout, tb=True, tm=tmd, tn=d, tk=d, epi=merge_bwd, out_dtypes=(BF, BF, BF), b_resident=True,
        extras=[(z, *gate_window(d, gate0)), (z, *gate_window(d, gate0 + d)), _tile_extra(y_c, tmd, d),
                _tile_extra(y_a, tmd, d)],
        out_overrides={2: ((s, z_cols), pl.BlockSpec(*gate_window(2 * d, gate0)))})
    (g_wout,) = _mm("d_wout", merged, dx1b, ta=True, tm=tm, tn=tn, tk=tkt)
    (g_wap,) = _mm("d_wap", o, dy_a, ta=True, tm=tm, tn=tn, tk=tkt)

    def slabs(g):
        return g if g.ndim == 3 else g.reshape(N_CHIPS, g.shape[0] // N_CHIPS, g.shape[1])

    grads_a = [slabs(g) for g in (g_wap, g_wout, g_w1, g_w2, g_wpg, g_wple)]
    (do,), got_a = _mm("d_o", dy_a, wap, tb=True, tm=tm, tn=tn, tk=d, comm=_pair_comm(grads_a))
    parts_a = [_pair_sum("pair_sum_" + nm, g, r, c_arr) for nm, g, r in zip(names[2:], grads_a, got_a)]
    dqt, dkt, dz, _ = _flash_bwd(qt, kt, z, o, do, lse, d, dz)
    dz, d_qn, d_kn = _qk_bwd(dqt, dkt, z, cos, sin, q_norm, k_norm, d, dz)
    (g_wcp,) = _mm("d_wcp", act, dy_c, ta=True, out_cm=True, tm=cw, tn=wcp.shape[2], tk=tkt)
    (dact,) = _mm("d_act", dy_c, wcp, tb=True, b_cm=True, tm=tm, tn=cw, tk=wcp.shape[2])
    p_wap, p_wout, p_w1, p_w2, p_wpg, p_wple = parts_a
    dz, d_taps, d_lng, d_lnb, (s_wap, s_wout, s_wpg, s_wple) = _conv_bwd(
        dact, uc, z, wdw, conv_ln_g, conv_ln_b, cw, dz, comm=_chip_comm([p_wap, p_wout, p_wpg, p_wple]))
    (g_win,), (s_w1, s_w2) = _mm("d_win", h0, dz, ta=True, out_cm=True, tm=tm, tn=win.shape[2] // 3, tk=tkt,
                                 comm=_chip_comm([p_w1, p_w2]))
    slots_a = [s_wap, s_wout, s_w1, s_w2, s_wpg, s_wple]
    grads_b = [slabs(g_win), slabs(g_wcp)]
    got_b = _run_comm("grad_pair_exchange_b", _pair_comm(grads_b))
    parts_b = [_pair_sum("pair_sum_" + nm, g, r, c_arr) for nm, g, r in zip(names[:2], grads_b, got_b)]
    halves_a = [_chip_sum("chip_sum_" + nm, cp, sl, chip_arr, c_arr) for nm, cp, sl in zip(names[2:], parts_a, slots_a)]
    (dh0,), hosted = _mm("d_h0", dz, win, tb=True, b_cm=True, tm=tm, tn=tn, tk=win.shape[2],
                         comm=_merge_comms(_chip_comm(parts_b), _pair_gather_comm(halves_a)))
    slots_b, grads_a_done = hosted[:2], hosted[2:]
    dx, _, d_mix = _rms_bwd("rms_mix_bwd", dh0, xs, norm_mix, dx1)
    halves_b = [_chip_sum("chip_sum_" + nm, cp, sl, chip_arr, c_arr) for nm, cp, sl in zip(names[:2], parts_b, slots_b)]
    big_grads = list(_run_comm("grad_pair_gather_b", _pair_gather_comm(halves_b))) + list(grads_a_done)

    small = [d_mix, d_taps[:CONV_KERNEL], d_lng, d_lnb, d_qn, d_kn, d_ffn, d_ple, d_fin]
    packed, unpack = _pack_small(small)
    g_mix, g_taps, g_lng, g_lnb, g_qn, g_kn, g_ffn, g_ple, g_fin = unpack(_all_sum_small("reduce_small", packed))
    g_dw = lax.dynamic_slice_in_dim(g_taps.reshape(CONV_KERNEL, N_CHIPS, cpc), chip, 1, axis=1).reshape(1, CONV_KERNEL, cpc)

    sq_local = lax.reduce_precision(sq[0, 0], 8, 23)
    loss = (0.5 / d) * lax.psum(sq_local, ("x", "y", "c"))

    grads = {
        "norm_mix": g_mix, "w_in": big_grads[0][None], "w_dw": g_dw, "conv_ln_g": g_lng, "conv_ln_b": g_lnb,
        "w_conv_proj": big_grads[1][None], "q_norm": g_qn, "k_norm": g_kn, "w_attn_proj": big_grads[2][None],
        "w_out": big_grads[3][None], "norm_ffn": g_ffn, "w_ff1": big_grads[4][None], "w_ff2": big_grads[5][None],
        "norm_ple": g_ple, "w_ple_gate": big_grads[6][None], "w_ple_proj": big_grads[7][None],
        "norm_final": g_fin.reshape(d),
    }
    weights = dict(norm_mix=norm_mix, w_in=w_in, w_dw=w_dw, conv_ln_g=conv_ln_g, conv_ln_b=conv_ln_b, w_conv_proj=w_conv_proj,
                   q_norm=q_norm, k_norm=k_norm, w_attn_proj=w_attn_proj, w_out=w_out, norm_ffn=norm_ffn, w_ff1=w_ff1,
                   w_ff2=w_ff2, norm_ple=norm_ple, w_ple_gate=w_ple_gate, w_ple_proj=w_ple_proj, norm_final=norm_final)
    m_in = dict(norm_mix=m_norm_mix, w_in=m_w_in, w_dw=m_w_dw, conv_ln_g=m_conv_ln_g, conv_ln_b=m_conv_ln_b,
                w_conv_proj=m_w_conv_proj, q_norm=m_q_norm, k_norm=m_k_norm, w_attn_proj=m_w_attn_proj, w_out=m_w_out,
                norm_ffn=m_norm_ffn, w_ff1=m_w_ff1, w_ff2=m_w_ff2, norm_ple=m_norm_ple, w_ple_gate=m_w_ple_gate,
                w_ple_proj=m_w_ple_proj, norm_final=m_norm_final)
    v_in = dict(norm_mix=v_norm_mix, w_in=v_w_in, w_dw=v_w_dw, conv_ln_g=v_conv_ln_g, conv_ln_b=v_conv_ln_b,
                w_conv_proj=v_w_conv_proj, q_norm=v_q_norm, k_norm=v_k_norm, w_attn_proj=v_w_attn_proj, w_out=v_w_out,
                norm_ffn=v_norm_ffn, w_ff1=v_w_ff1, w_ff2=v_w_ff2, norm_ple=v_norm_ple, w_ple_gate=v_w_ple_gate,
                w_ple_proj=v_w_ple_proj, norm_final=v_norm_final)
    order = list(weights)
    deltas, new_m, new_v, g_out = [], [], [], []
    for nm in order:
        w = weights[nm]
        shape = w.shape
        two_d = (-1, shape[-1])
        dl, mm_, vv_ = _adamw("adamw_" + nm, w.reshape(two_d), grads[nm].reshape(two_d), m_in[nm].reshape(two_d),
                              v_in[nm].reshape(two_d))
        g_out.append(grads[nm].reshape(shape))
        deltas.append(dl.reshape(shape))
        new_m.append(mm_.reshape(shape))
        new_v.append(vv_.reshape(shape))
    return (loss, dx[None], *g_out, *deltas, *new_m, *new_v)
```

```python
from typing import NamedTuple

import jax
import jax.numpy as jnp
from jax import lax
from jax.experimental import pallas as pl
from jax.experimental.pallas import tpu as pltpu

F32 = jnp.float32
BF = jnp.bfloat16

EPS = 1e-6
HEAD_DIM = 128
GROUP = 4
GRID_W = 64
ROPE_THETA = 10000.0
CONV_KERNEL = 31
HALO = 16
N_CHIPS = 4
N_DEV = 8
LANES = 128

ADAM_LR = 0.001
ADAM_B1 = 0.9
ADAM_B2 = 0.999
ADAM_EPS = 1e-08
ADAM_WD = 0.01
ADAM_STEP = 10

VMEM_LIMIT = 56 * 2 ** 20
LOG2E = 1.4426950408889634
LN2 = 0.6931471805599453
Q_SCALE = HEAD_DIM ** -0.5 * LOG2E
ROW_TILE = 256
FLASH_TQ_FWD = 512
FLASH_TQ_BWD = 512
FLASH_TK = 512
MM_TM = 1024
MM_TM_EPI = 512
MM_TM_DZ = 256
MESH = pl.DeviceIdType.MESH
ANY = pl.BlockSpec(memory_space=pl.ANY)


def _params(sem):
    return pltpu.CompilerParams(dimension_semantics=sem, vmem_limit_bytes=VMEM_LIMIT)


def _sigmoid(x):
    return 1.0 / (1.0 + jnp.exp(-x))


class _Comm(NamedTuple):
    arrays: list
    out_shapes: list
    aliases: dict
    sems: list
    phases: tuple


def _call(body, *, name, grid, in_specs, out_specs, out_shape, scratch_shapes, semantics, args, comm=None, aliases=None):
    n_in, n_out = len(in_specs), len(out_specs)
    aliases = dict(aliases or {})
    if comm is None:
        res = pl.pallas_call(body, name=name, grid=grid, in_specs=in_specs, out_specs=out_specs, out_shape=out_shape,
                             scratch_shapes=scratch_shapes, input_output_aliases=aliases,
                             compiler_params=_params(semantics))(*args)
        return res, []
    nci, nco, ncs = len(comm.arrays), len(comm.out_shapes), len(comm.sems)
    n_steps = 1
    for g in grid:
        n_steps *= g
    first, middle, last = comm.phases

    def hosted(*refs):
        ins, cin = refs[:n_in], refs[n_in:n_in + nci]
        outs = refs[n_in + nci:n_in + nci + n_out]
        cout = refs[n_in + nci + n_out:n_in + nci + n_out + nco]
        rest = refs[n_in + nci + n_out + nco:]
        scratch, sems = rest[:len(rest) - ncs], rest[len(rest) - ncs:]
        step = 0
        for ax, g in enumerate(grid):
            step = step * g + pl.program_id(ax)
        for at, fn in ((0, first), (n_steps // 2, middle)):
            if fn is not None:
                pl.when(step == at)(lambda fn=fn: fn(cin, cout, sems))
        body(*ins, *outs, *scratch)
        if last is not None:
            pl.when(step == n_steps - 1)(lambda: last(cin, cout, sems))

    res = pl.pallas_call(
        hosted, name=name, grid=grid,
        in_specs=list(in_specs) + [ANY] * nci, out_specs=list(out_specs) + [ANY] * nco,
        out_shape=list(out_shape) + list(comm.out_shapes),
        input_output_aliases={**aliases, **{n_in + a: n_out + b for a, b in comm.aliases.items()}},
        scratch_shapes=list(scratch_shapes) + list(comm.sems),
        compiler_params=_params(("arbitrary",) * len(grid)),
    )(*args, *comm.arrays)
    return res[:n_out], res[n_out:]


def _mm(name, a, b, *, tm, tn, tk, ta=False, tb=False, b_cm=False, out_cm=False,
        a_fn=None, extras=(), epi=None, out_dtypes=(BF,), epi_rows=256, comm=None, b_resident=False,
        out_overrides=None):
    if ta:
        kc, m = a.shape
    else:
        m, kc = a.shape
    if b_cm:
        nc, r, c = b.shape
        n, per = (r, c) if tb else (nc * c, c)
    else:
        n = b.shape[0] if tb else b.shape[1]
    tm, tn, tk = min(tm, m), min(tn, n), min(tk, kc)
    assert m % tm == 0 and n % tn == 0 and kc % tk == 0, (name, m, n, kc, tm, tn, tk)
    nk = kc // tk
    a_spec = pl.BlockSpec((tk, tm), lambda i, j, k: (k, i)) if ta else pl.BlockSpec((tm, tk), lambda i, j, k: (i, k))
    if b_cm and not tb:
        assert per % tn == 0
        npj = per // tn
        b_spec = pl.BlockSpec((None, tk, tn), lambda i, j, k: (j // npj, k, j % npj))
    elif b_cm:
        assert per % tk == 0
        npk = per // tk
        b_spec = pl.BlockSpec((None, tn, tk), lambda i, j, k: (k // npk, j, k % npk))
    elif b_resident:
        assert nk == 1
        b_spec = pl.BlockSpec(b.shape, lambda i, j, k: (0, 0))
    elif tb:
        b_spec = pl.BlockSpec((tn, tk), lambda i, j, k: (j, k))
    else:
        b_spec = pl.BlockSpec((tk, tn), lambda i, j, k: (k, j))
    if out_cm:
        assert (n // N_CHIPS) % tn == 0
        npo = (n // N_CHIPS) // tn
        o_spec = pl.BlockSpec((None, tm, tn), lambda i, j, k: (j // npo, i, j % npo))
        o_shape = (N_CHIPS, m, n // N_CHIPS)
    else:
        o_spec = pl.BlockSpec((tm, tn), lambda i, j, k: (i, j))
        o_shape = (m, n)
    ne, no = len(extras), len(out_dtypes)
    whole = [len(e) > 3 and e[3] for e in extras]
    extras = [e[:3] for e in extras]

    def epi_args(ex, rows):
        return [e[...] if w else e[rows, :] for e, w in zip(ex, whole)]

    dims = (((0 if ta else 1,), (1 if tb else 0,)), ((), ()))
    er = min(epi_rows, tm)
    chunked = nk == 1 and epi is not None and not ta
    use_acc = (nk > 1 or epi is not None) and not chunked
    assert chunked or not b_resident

    def body(*refs):
        a_ref, b_ref = refs[0], refs[1]
        ex = refs[2:2 + ne]
        outs = refs[2 + ne:2 + ne + no]
        if chunked:
            if b_resident:
                cols = pl.ds(pl.multiple_of(pl.program_id(1) * tn, tn), tn)
                bt = b_ref[cols, :] if tb else b_ref[:, cols]
            else:
                bt = b_ref[...]
            for r0 in range(0, tm, er):
                rows = slice(r0, r0 + er)
                at = a_ref[rows, :]
                if a_fn is not None:
                    at = a_fn(at)
                d = lax.dot_general(at, bt, dims, preferred_element_type=F32)
                vals = epi(d, *epi_args(ex, rows))
                for o, v, dt in zip(outs, vals, out_dtypes):
                    o[rows, :] = v.astype(dt)
            return
        at = a_ref[...]
        if a_fn is not None:
            at = a_fn(at)
        d = lax.dot_general(at, b_ref[...], dims, preferred_element_type=F32)
        if not use_acc:
            outs[0][...] = d.astype(out_dtypes[0])
            return
        acc = refs[-1]
        k = pl.program_id(2)

        @pl.when(k == 0)
        def _():
            acc[...] = d

        if nk > 1:
            @pl.when(k > 0)
            def _():
                acc[...] += d

        @pl.when(k == nk - 1)
        def _():
            for r0 in range(0, tm, er):
                rows = slice(r0, r0 + er)
                if epi is None:
                    vals = (acc[rows, :],)
                else:
                    vals = epi(acc[rows, :], *epi_args(ex, rows))
                for o, v, dt in zip(outs, vals, out_dtypes):
                    o[rows, :] = v.astype(dt)

    out_specs = [o_spec] * no
    out_shapes = [jax.ShapeDtypeStruct(o_shape, dt) for dt in out_dtypes]
    for idx, (shape, spec) in (out_overrides or {}).items():
        out_specs[idx], out_shapes[idx] = spec, jax.ShapeDtypeStruct(shape, out_dtypes[idx])
    res, cres = _call(
        body, name=name, grid=(m // tm, n // tn, nk),
        in_specs=[a_spec, b_spec] + [pl.BlockSpec(bs, im) for _, bs, im in extras],
        out_specs=out_specs,
        out_shape=out_shapes,
        scratch_shapes=[pltpu.VMEM((tm, tn), F32)] if use_acc else [],
        semantics=("parallel", "parallel", "arbitrary"),
        args=[a, b] + [e for e, _, _ in extras], comm=comm)
    return res if comm is None else (res, cres)


def _tile_extra(arr, tm, tn, col_block0=0):
    return (arr, (tm, tn), lambda i, j, k: (i, j + col_block0))


def _rms_fwd(name, x, g, ts=None, comm=None):
    s, d = x.shape
    ts = ts or ROW_TILE

    def body(x_ref, g_ref, h_ref):
        xv = x_ref[...]
        r = lax.rsqrt(jnp.mean(xv * xv, axis=-1, keepdims=True) + EPS)
        h_ref[...] = (xv * r * g_ref[...]).astype(BF)

    (h,), cres = _call(
        body, name=name, grid=(s // ts,),
        in_specs=[pl.BlockSpec((ts, d), lambda i: (i, 0)), pl.BlockSpec((1, d), lambda i: (0, 0))],
        out_specs=[pl.BlockSpec((ts, d), lambda i: (i, 0))],
        out_shape=[jax.ShapeDtypeStruct((s, d), BF)],
        scratch_shapes=[], semantics=("parallel",), args=[x, g], comm=comm)
    return h if comm is None else (h, cres)


def _rms_bwd(name, dh, x, g, dres, ts=None):
    s, d = x.shape
    ts = ts or ROW_TILE

    def body(dh_ref, x_ref, g_ref, dres_ref, dx_ref, dxb_ref, dg_ref):
        xv = x_ref[...]
        dhv = dh_ref[...].astype(F32)
        r = lax.rsqrt(jnp.mean(xv * xv, axis=-1, keepdims=True) + EPS)
        nrm = xv * r
        dn = dhv * g_ref[...]
        dx = dres_ref[...] + r * (dn - nrm * jnp.mean(dn * nrm, axis=-1, keepdims=True))
        dx_ref[...] = dx
        dxb_ref[...] = dx.astype(BF)
        part = jnp.sum(dhv * nrm, axis=0, keepdims=True)

        @pl.when(pl.program_id(0) == 0)
        def _():
            dg_ref[...] = part

        @pl.when(pl.program_id(0) > 0)
        def _():
            dg_ref[...] += part

    row = pl.BlockSpec((ts, d), lambda i: (i, 0))
    vec = pl.BlockSpec((1, d), lambda i: (0, 0))
    return pl.pallas_call(
        body, name=name, grid=(s // ts,),
        in_specs=[row, row, vec, row],
        out_specs=[row, row, vec],
        out_shape=[jax.ShapeDtypeStruct((s, d), F32), jax.ShapeDtypeStruct((s, d), BF), jax.ShapeDtypeStruct((1, d), F32)],
        compiler_params=_params(("arbitrary",)),
    )(dh, x, g, dres)


def _loss_bwd(x3, tgt, gfin, e, gate, ts=None):
    s, d = x3.shape
    ts = ts or ROW_TILE

    def body(x_ref, t_ref, g_ref, e_ref, gate_ref, dx_ref, de_ref, dgp_ref, sq_ref, dg_ref):
        xv = x_ref[...]
        gv = g_ref[...]
        r = lax.rsqrt(jnp.mean(xv * xv, axis=-1, keepdims=True) + EPS)
        nrm = xv * r
        err = nrm * gv - t_ref[...]
        dy = err * (1.0 / d)
        dn = dy * gv
        dx = r * (dn - nrm * jnp.mean(dn * nrm, axis=-1, keepdims=True))
        dx_ref[...] = dx
        ev = e_ref[...].astype(F32)
        gt = gate_ref[...].astype(F32)
        de_ref[...] = (dx * gt).astype(BF)
        dgp_ref[...] = (dx * ev * gt * (1.0 - gt)).astype(BF)
        sq = jnp.full((8, LANES), jnp.sum(err * err), F32)
        part = jnp.sum(dy * nrm, axis=0, keepdims=True)

        @pl.when(pl.program_id(0) == 0)
        def _():
            sq_ref[...] = sq
            dg_ref[...] = part

        @pl.when(pl.program_id(0) > 0)
        def _():
            sq_ref[...] += sq
            dg_ref[...] += part

    row = pl.BlockSpec((ts, d), lambda i: (i, 0))
    vec = pl.BlockSpec((1, d), lambda i: (0, 0))
    return pl.pallas_call(
        body, name="loss_bwd", grid=(s // ts,),
        in_specs=[row, row, vec, row, row],
        out_specs=[row, row, row, pl.BlockSpec((8, LANES), lambda i: (0, 0)), vec],
        out_shape=[jax.ShapeDtypeStruct((s, d), F32), jax.ShapeDtypeStruct((s, d), BF), jax.ShapeDtypeStruct((s, d), BF),
                   jax.ShapeDtypeStruct((8, LANES), F32), jax.ShapeDtypeStruct((1, d), F32)],
        compiler_params=_params(("arbitrary",)),
    )(x3, tgt, gfin, e, gate)


def _halo_specs(ts, s, width, col_block):
    per = ts // HALO
    last = s // HALO - 1
    return [
        pl.BlockSpec((HALO, width), lambda i: (jnp.maximum(i * per - 1, 0), col_block)),
        pl.BlockSpec((ts, width), lambda i: (i, col_block)),
        pl.BlockSpec((HALO, width), lambda i: (jnp.minimum((i + 1) * per, last), col_block)),
    ]


def _glu_ext(zp, zc, zn, ext, cw, ts, i, n_tiles):
    def glu(zr):
        zv = zr[...].astype(F32)
        return zv[:, :cw] * _sigmoid(zv[:, cw:])

    ext[0:HALO, :] = jnp.where(i > 0, glu(zp), 0.0)
    ext[HALO:HALO + ts, :] = glu(zc)
    ext[HALO + ts:, :] = jnp.where(i < n_tiles - 1, glu(zn), 0.0)


SUBLANES = 8


def _shift_scratch(ts):
    return pltpu.VMEM((SUBLANES, ts + 2 * HALO - SUBLANES, LANES), F32)


def _shifted_copies(ext, sh, cols, ts):
    n = ts + 2 * HALO - SUBLANES
    for r in range(SUBLANES):
        sh[r] = ext[r:r + n, cols]


def _tap_rows(sh, off, ts):
    q, r = divmod(off, SUBLANES)
    return sh[r, q * SUBLANES:q * SUBLANES + ts, :]


def _ln_stats(uc):
    mu = jnp.mean(uc, axis=-1, keepdims=True)
    xc = uc - mu
    rstd = lax.rsqrt(jnp.mean(xc * xc, axis=-1, keepdims=True) + EPS)
    return xc * rstd, rstd


def _conv_fwd(z, wdw, ln_g, ln_b, cw, ts=None):
    s = z.shape[0]
    ts = ts or ROW_TILE
    n_tiles = s // ts
    pad = CONV_KERNEL // 2

    def body(zp, zc, zn, w_ref, g_ref, b_ref, uc_ref, act_ref, ext, sh):
        i = pl.program_id(0)
        _glu_ext(zp, zc, zn, ext, cw, ts, i, n_tiles)

        def col_block(cb, carry):
            cols = pl.ds(pl.multiple_of(cb * LANES, LANES), LANES)
            _shifted_copies(ext, sh, cols, ts)
            acc = jnp.zeros((ts, LANES), F32)
            for j in range(CONV_KERNEL):
                acc = acc + _tap_rows(sh, HALO - pad + j, ts) * w_ref[j:j + 1, cols]
            uc_ref[:, cols] = acc
            return carry

        lax.fori_loop(0, cw // LANES, col_block, 0)
        xhat, _ = _ln_stats(uc_ref[...])
        ln = xhat * g_ref[...] + b_ref[...]
        act_ref[...] = (ln * _sigmoid(ln)).astype(BF)

    vec = pl.BlockSpec((1, cw), lambda i: (0, 0))
    row = pl.BlockSpec((ts, cw), lambda i: (i, 0))
    return pl.pallas_call(
        body, name="conv_fwd", grid=(n_tiles,),
        in_specs=_halo_specs(ts, s, 2 * cw, 0) + [pl.BlockSpec((32, cw), lambda i: (0, 0)), vec, vec],
        out_specs=[row, row],
        out_shape=[jax.ShapeDtypeStruct((s, cw), F32), jax.ShapeDtypeStruct((s, cw), BF)],
        scratch_shapes=[pltpu.VMEM((ts + 2 * HALO, cw), F32), _shift_scratch(ts)],
        compiler_params=_params(("parallel",)),
    )(z, z, z, wdw, ln_g, ln_b)


def _conv_bwd(ds, uc, z, wdw, ln_g, ln_b, cw, dz, ts=None, comm=None):
    s = z.shape[0]
    ts = ts or ROW_TILE
    n_tiles = s // ts
    pad = CONV_KERNEL // 2

    def body(zp, zc, zn, dsp, dsc, dsn, ucp, ucc, ucn, w_ref, g_ref, b_ref, _dz_in,
             dz_ref, dw_ref, dg_ref, db_ref, ext, dext, sh, dsh):
        i = pl.program_id(0)
        gv, bv = g_ref[...], b_ref[...]

        def ln_bwd(ds_r, uc_r):
            xhat, rstd = _ln_stats(uc_r[...])
            ln = xhat * gv + bv
            sg = _sigmoid(ln)
            dln = ds_r[...].astype(F32) * (sg * (1.0 + ln * (1.0 - sg)))
            dxh = dln * gv
            duc = rstd * (dxh - jnp.mean(dxh, axis=-1, keepdims=True) - xhat * jnp.mean(dxh * xhat, axis=-1, keepdims=True))
            return duc, dln, xhat

        duc_p, _, _ = ln_bwd(dsp, ucp)
        duc_c, dln_c, xhat_c = ln_bwd(dsc, ucc)
        duc_n, _, _ = ln_bwd(dsn, ucn)
        dext[0:HALO, :] = jnp.where(i > 0, duc_p, 0.0)
        dext[HALO:HALO + ts, :] = duc_c
        dext[HALO + ts:, :] = jnp.where(i < n_tiles - 1, duc_n, 0.0)
        _glu_ext(zp, zc, zn, ext, cw, ts, i, n_tiles)

        dg_part = jnp.sum(dln_c * xhat_c, axis=0, keepdims=True)
        db_part = jnp.sum(dln_c, axis=0, keepdims=True)

        @pl.when(i == 0)
        def _():
            dw_ref[...] = jnp.zeros_like(dw_ref)
            dg_ref[...] = dg_part
            db_ref[...] = db_part

        @pl.when(i > 0)
        def _():
            dg_ref[...] += dg_part
            db_ref[...] += db_part

        def col_block(cb, carry):
            c0 = pl.multiple_of(cb * LANES, LANES)
            cols, gate_cols = pl.ds(c0, LANES), pl.ds(cw + c0, LANES)
            _shifted_copies(dext, dsh, cols, ts)
            _shifted_copies(ext, sh, cols, ts)
            du = jnp.zeros((ts, LANES), F32)
            for j in range(CONV_KERNEL):
                du = du + _tap_rows(dsh, HALO + pad - j, ts) * w_ref[j:j + 1, cols]
            ca, sb = zc[:, cols].astype(F32), _sigmoid(zc[:, gate_cols].astype(F32))
            dz_ref[:, cols] = (du * sb).astype(BF)
            dz_ref[:, gate_cols] = (du * ca * sb * (1.0 - sb)).astype(BF)
            duc_blk = _tap_rows(dsh, HALO, ts)
            for j in range(CONV_KERNEL):
                dw_ref[j:j + 1, cols] += jnp.sum(_tap_rows(sh, HALO - pad + j, ts) * duc_blk, axis=0, keepdims=True)
            return carry

        lax.fori_loop(0, cw // LANES, col_block, 0)

    vec = pl.BlockSpec((1, cw), lambda i: (0, 0))
    wsp = pl.BlockSpec((32, cw), lambda i: (0, 0))
    res, cres = _call(
        body, name="conv_bwd", grid=(n_tiles,),
        in_specs=(_halo_specs(ts, s, 2 * cw, 0) + _halo_specs(ts, s, cw, 0) + _halo_specs(ts, s, cw, 0)
                  + [wsp, vec, vec, ANY]),
        out_specs=[pl.BlockSpec((ts, 2 * cw), lambda i: (i, 0)), wsp, vec, vec],
        out_shape=[jax.ShapeDtypeStruct(dz.shape, BF), jax.ShapeDtypeStruct((32, cw), F32),
                   jax.ShapeDtypeStruct((1, cw), F32), jax.ShapeDtypeStruct((1, cw), F32)],
        scratch_shapes=[pltpu.VMEM((ts + 2 * HALO, cw), F32), pltpu.VMEM((ts + 2 * HALO, cw), F32),
                        _shift_scratch(ts), _shift_scratch(ts)],
        semantics=("arbitrary",), args=[z, z, z, ds, ds, ds, uc, uc, uc, wdw, ln_g, ln_b, dz], comm=comm,
        aliases={12: 0})
    return (*res, cres)


def _rope_tables(s):
    axis_dim = HEAD_DIM // 2
    n_rows = s // GRID_W
    inv_freq = ROPE_THETA ** (-jnp.arange(0, axis_dim, 2, dtype=F32) / axis_dim)[None, :]
    ar = jnp.arange(n_rows, dtype=jnp.int32).astype(F32)[:, None] * inv_freq
    ac = jnp.arange(GRID_W, dtype=jnp.int32).astype(F32)[:, None] * inv_freq

    def table(fr, fc):
        by_row = jnp.broadcast_to(fr[:, None, :], (n_rows, GRID_W, axis_dim))
        by_col = jnp.broadcast_to(fc[None, :, :], (n_rows, GRID_W, axis_dim))
        return jnp.concatenate([by_row, by_col], axis=-1).reshape(s, HEAD_DIM)

    cos = table(jnp.concatenate([jnp.cos(ar), jnp.cos(ar)], axis=-1), jnp.concatenate([jnp.cos(ac), jnp.cos(ac)], axis=-1))
    sin = table(jnp.concatenate([-jnp.sin(ar), jnp.sin(ar)], axis=-1), jnp.concatenate([-jnp.sin(ac), jnp.sin(ac)], axis=-1))
    return cos, sin


def _lower_quarter_mask(shape):
    q = HEAD_DIM // 4
    return (lax.broadcasted_iota(jnp.int32, shape, 1) % (2 * q)) < q


def _swap_quarters(x, lower):
    q = HEAD_DIM // 4
    return jnp.where(lower, pltpu.roll(x, HEAD_DIM - q, 1), pltpu.roll(x, q, 1))


def _qk_fwd(z, cos, sin, qg, kg, d, ts=None):
    s = z.shape[0]
    ts = ts or ROW_TILE
    kvw = d // GROUP
    scale = Q_SCALE

    def body(q_ref, k_ref, c_ref, s_ref, qg_ref, kg_ref, qo_ref, ko_ref):
        cv, sv = c_ref[...], s_ref[...]
        lower = _lower_quarter_mask(cv.shape)

        def head(x_ref, g_ref, o_ref, h, mul):
            xv = x_ref[:, h * HEAD_DIM:(h + 1) * HEAD_DIM].astype(F32)
            r = lax.rsqrt(jnp.mean(xv * xv, axis=-1, keepdims=True) + EPS)
            nrm = xv * r * g_ref[...]
            out = nrm * cv + _swap_quarters(nrm, lower) * sv
            o_ref[:, h * HEAD_DIM:(h + 1) * HEAD_DIM] = (out * mul).astype(BF)

        for h in range(d // HEAD_DIM):
            head(q_ref, qg_ref, qo_ref, h, scale)
        for h in range(kvw // HEAD_DIM):
            head(k_ref, kg_ref, ko_ref, h, 1.0)

    cw2 = d
    tab = pl.BlockSpec((ts, HEAD_DIM), lambda i: (i, 0))
    vec = pl.BlockSpec((1, HEAD_DIM), lambda i: (0, 0))
    return pl.pallas_call(
        body, name="qk_fwd", grid=(s // ts,),
        in_specs=[pl.BlockSpec((ts, d), lambda i: (i, cw2 // d)),
                  pl.BlockSpec((ts, kvw), lambda i: (i, (cw2 + d) // kvw)), tab, tab, vec, vec],
        out_specs=[pl.BlockSpec((ts, d), lambda i: (i, 0)), pl.BlockSpec((ts, kvw), lambda i: (i, 0))],
        out_shape=[jax.ShapeDtypeStruct((s, d), BF), jax.ShapeDtypeStruct((s, kvw), BF)],
        compiler_params=_params(("parallel",)),
    )(z, z, cos, sin, qg, kg)


def _qk_bwd(dqt, dkt, z, cos, sin, qg, kg, d, dz, ts=None):
    s = z.shape[0]
    ts = ts or ROW_TILE
    kvw = d // GROUP
    scale = HEAD_DIM ** -0.5

    def body(dq_ref, dk_ref, q_ref, k_ref, c_ref, s_ref, qg_ref, kg_ref, _dz_in, dzo_ref, dqg_ref, dkg_ref):
        cv, sv = c_ref[...], s_ref[...]
        lower = _lower_quarter_mask(cv.shape)

        def head(dy_ref, x_ref, g_ref, col0, h, mul):
            dout = dy_ref[:, h * HEAD_DIM:(h + 1) * HEAD_DIM].astype(F32) * mul
            dn = dout * cv + _swap_quarters(dout * sv, lower)
            xv = x_ref[:, h * HEAD_DIM:(h + 1) * HEAD_DIM].astype(F32)
            r = lax.rsqrt(jnp.mean(xv * xv, axis=-1, keepdims=True) + EPS)
            nh = xv * r
            dnh = dn * g_ref[...]
            c0 = col0 + h * HEAD_DIM
            dzo_ref[:, c0:c0 + HEAD_DIM] = (r * (dnh - nh * jnp.mean(dnh * nh, axis=-1, keepdims=True))).astype(BF)
            return jnp.sum(dn * nh, axis=0, keepdims=True)

        dqg = jnp.zeros((1, HEAD_DIM), F32)
        for h in range(d // HEAD_DIM):
            dqg = dqg + head(dq_ref, q_ref, qg_ref, 0, h, scale)
        dkg = jnp.zeros((1, HEAD_DIM), F32)
        for h in range(kvw // HEAD_DIM):
            dkg = dkg + head(dk_ref, k_ref, kg_ref, d, h, LN2)

        @pl.when(pl.program_id(0) == 0)
        def _():
            dqg_ref[...] = dqg
            dkg_ref[...] = dkg

        @pl.when(pl.program_id(0) > 0)
        def _():
            dqg_ref[...] += dqg
            dkg_ref[...] += dkg

    cw2 = d
    tab = pl.BlockSpec((ts, HEAD_DIM), lambda i: (i, 0))
    vec = pl.BlockSpec((1, HEAD_DIM), lambda i: (0, 0))
    qrow = pl.BlockSpec((ts, d), lambda i: (i, 0))
    krow = pl.BlockSpec((ts, kvw), lambda i: (i, 0))
    window = pl.BlockSpec((pl.Element(ts), pl.Element(d + kvw)), lambda i: (i * ts, cw2))
    return pl.pallas_call(
        body, name="qk_bwd", grid=(s // ts,),
        in_specs=[qrow, krow, pl.BlockSpec((ts, d), lambda i: (i, cw2 // d)),
                  pl.BlockSpec((ts, kvw), lambda i: (i, (cw2 + d) // kvw)), tab, tab, vec, vec, ANY],
        out_specs=[window, vec, vec],
        out_shape=[jax.ShapeDtypeStruct(dz.shape, BF),
                   jax.ShapeDtypeStruct((1, HEAD_DIM), F32), jax.ShapeDtypeStruct((1, HEAD_DIM), F32)],
        input_output_aliases={8: 0},
        compiler_params=_params(("arbitrary",)),
    )(dqt, dkt, z, z, cos, sin, qg, kg, dz)


_NT = (((1,), (1,)), ((), ()))
_TN = (((0,), (0,)), ((), ()))


def _v_col_block(d):
    return (2 * d + d // GROUP) // HEAD_DIM


def _flash_fwd(qt, kt, z, d, tq=None, tk=None, comm=None):
    s = qt.shape[0]
    tq, tk = min(tq or FLASH_TQ_FWD, s), min(tk or FLASH_TK, s)
    ng, nq, nk = d // (GROUP * HEAD_DIM), s // tq, s // tk
    gw = GROUP * HEAD_DIM
    rows = GROUP * tq

    nt = tk // LANES
    assert nk % 2 == 0, (s, tk)

    def body(q_ref, k_ref, v_ref, o_ref, lse_ref, qs, v1, p_s, m_s, acc_s, sc_s):
        @pl.when(pl.program_id(1) == 0)
        def _():
            v1[:, :HEAD_DIM] = v_ref[...]
            v1[:, HEAD_DIM:] = jnp.ones((s, HEAD_DIM), BF)

        for h in range(GROUP):
            qs[h * tq:(h + 1) * tq, :] = q_ref[:, h * HEAD_DIM:(h + 1) * HEAD_DIM]
        m_s[...] = jnp.full((rows, LANES), -1e30, F32)
        acc_s[...] = jnp.zeros((rows, 2 * HEAD_DIM), F32)

        def scores(j):
            return lax.dot_general(qs[...], k_ref[pl.ds(pl.multiple_of(j * tk, tk), tk), :], _NT, preferred_element_type=F32)

        def softmax_pv(j, sc):
            kv_rows = pl.ds(pl.multiple_of(j * tk, tk), tk)
            mt = sc[:, :LANES]
            for c in range(1, nt):
                mt = jnp.maximum(mt, sc[:, c * LANES:(c + 1) * LANES])
            m_old = m_s[...]
            m_new = jnp.maximum(m_old, jnp.max(mt, axis=-1, keepdims=True))
            alpha = jnp.exp2(m_old - m_new)
            for c in range(nt):
                cs = slice(c * LANES, (c + 1) * LANES)
                p_s[:, cs] = jnp.exp2(sc[:, cs] - m_new).astype(BF)
            pv = jnp.dot(p_s[...], v1[kv_rows, :], preferred_element_type=F32)
            acc_s[:, :HEAD_DIM] = alpha * acc_s[:, :HEAD_DIM] + pv[:, :HEAD_DIM]
            acc_s[:, HEAD_DIM:] = alpha * acc_s[:, HEAD_DIM:] + pv[:, HEAD_DIM:]
            m_s[...] = m_new

        sc_s[0] = scores(0)

        def step(jj, carry):
            j = 2 * jj
            sc_s[1] = scores(j + 1)
            softmax_pv(j, sc_s[0])
            sc_s[0] = scores(jnp.minimum(j + 2, nk - 1))
            softmax_pv(j + 1, sc_s[1])
            return carry

        lax.fori_loop(0, nk // 2, step, 0)
        l = acc_s[:, HEAD_DIM:]
        o = acc_s[:, :HEAD_DIM] / l
        for h in range(GROUP):
            o_ref[:, h * HEAD_DIM:(h + 1) * HEAD_DIM] = o[h * tq:(h + 1) * tq, :].astype(BF)
        lse = m_s[...] + jnp.log2(l)
        for h in range(GROUP):
            lse_ref[h] = lse[h * tq:(h + 1) * tq, :]

    vb = _v_col_block(d)
    (o, lse), cres = _call(
        body, name="flash_fwd", grid=(ng, nq),
        in_specs=[pl.BlockSpec((tq, gw), lambda g, i: (i, g)),
                  pl.BlockSpec((s, HEAD_DIM), lambda g, i: (0, g)),
                  pl.BlockSpec((s, HEAD_DIM), lambda g, i: (0, vb + g))],
        out_specs=[pl.BlockSpec((tq, gw), lambda g, i: (i, g)),
                   pl.BlockSpec((GROUP, tq, LANES), lambda g, i: (g, i, 0))],
        out_shape=[jax.ShapeDtypeStruct((s, d), BF), jax.ShapeDtypeStruct((ng * GROUP, s, LANES), F32)],
        scratch_shapes=[pltpu.VMEM((rows, HEAD_DIM), BF), pltpu.VMEM((s, 2 * HEAD_DIM), BF), pltpu.VMEM((rows, tk), BF),
                        pltpu.VMEM((rows, LANES), F32), pltpu.VMEM((rows, 2 * HEAD_DIM), F32), pltpu.VMEM((2, rows, tk), F32)],
        semantics=("parallel", "arbitrary"), args=[qt, kt, z], comm=comm)
    return o, lse, cres


def _flash_bwd(qt, kt, z, o, do, lse, d, dz, tq=None, tk=None, comm=None):
    s = qt.shape[0]
    tq, tk = min(tq or FLASH_TQ_BWD, s), min(tk or FLASH_TK, s)
    ng, nq, nk = d // (GROUP * HEAD_DIM), s // tq, s // tk
    gw = GROUP * HEAD_DIM
    rows = GROUP * tq

    nt = tk // LANES

    def body(q_ref, k_ref, v_ref, o_ref, do_ref, lse_ref, _dz_in, dq_ref, dk_ref, dzv_ref,
             qs, dos, delta_s, dq_s, p_s, ds_s, lse_s, dv_ref):
        i = pl.program_id(1)
        for h in range(GROUP):
            cols = slice(h * HEAD_DIM, (h + 1) * HEAD_DIM)
            lse_s[h * tq:(h + 1) * tq, :] = lse_ref[h]
            qs[h * tq:(h + 1) * tq, :] = q_ref[:, cols]
            dov = do_ref[:, cols]
            dos[h * tq:(h + 1) * tq, :] = dov
            delta = jnp.sum(dov.astype(F32) * o_ref[:, cols].astype(F32), axis=-1, keepdims=True)
            delta_s[h * tq:(h + 1) * tq, :] = jnp.broadcast_to(delta, (tq, LANES))
        dq_s[...] = jnp.zeros((rows, HEAD_DIM), F32)

        @pl.when(i == 0)
        def _():
            dk_ref[...] = jnp.zeros_like(dk_ref)
            dv_ref[...] = jnp.zeros_like(dv_ref)

        def step(j, carry):
            kv_rows = pl.ds(pl.multiple_of(j * tk, tk), tk)
            kv, vv = k_ref[kv_rows, :], v_ref[kv_rows, :]
            sc = lax.dot_general(qs[...], kv, _NT, preferred_element_type=F32)
            dp = lax.dot_general(dos[...], vv, _NT, preferred_element_type=F32)
            lse, delta = lse_s[...], delta_s[...]
            for c in range(nt):
                cs = slice(c * LANES, (c + 1) * LANES)
                p = jnp.exp2(sc[:, cs] - lse)
                p_s[:, cs] = p.astype(BF)
                ds_s[:, cs] = (p * (dp[:, cs] - delta)).astype(BF)
            dv_ref[kv_rows, :] += lax.dot_general(p_s[...], dos[...], _TN, preferred_element_type=F32)
            dk_ref[kv_rows, :] += lax.dot_general(ds_s[...], qs[...], _TN, preferred_element_type=F32)
            dq_s[...] += jnp.dot(ds_s[...], kv, preferred_element_type=F32)
            return carry

        lax.fori_loop(0, nk, step, 0)
        for h in range(GROUP):
            dq_ref[:, h * HEAD_DIM:(h + 1) * HEAD_DIM] = dq_s[h * tq:(h + 1) * tq, :].astype(BF)

        @pl.when(i == nq - 1)
        def _():
            dzv_ref[...] = dv_ref[...].astype(BF)

    vb = _v_col_block(d)
    qspec = pl.BlockSpec((tq, gw), lambda g, i: (i, g))
    kspec = pl.BlockSpec((s, HEAD_DIM), lambda g, i: (0, g))
    vspec = pl.BlockSpec((s, HEAD_DIM), lambda g, i: (0, vb + g))
    (dq, dk, dz), cres = _call(
        body, name="flash_bwd", grid=(ng, nq),
        in_specs=[qspec, kspec, vspec, qspec, qspec, pl.BlockSpec((GROUP, tq, LANES), lambda g, i: (g, i, 0)), ANY],
        out_specs=[qspec, kspec, vspec],
        out_shape=[jax.ShapeDtypeStruct((s, d), BF), jax.ShapeDtypeStruct((s, d // GROUP), F32),
                   jax.ShapeDtypeStruct(dz.shape, BF)],
        scratch_shapes=[pltpu.VMEM((rows, HEAD_DIM), BF), pltpu.VMEM((rows, HEAD_DIM), BF), pltpu.VMEM((rows, LANES), F32),
                        pltpu.VMEM((rows, HEAD_DIM), F32), pltpu.VMEM((rows, tk), BF), pltpu.VMEM((rows, tk), BF),
                        pltpu.VMEM((rows, LANES), F32), pltpu.VMEM((s, HEAD_DIM), F32)],
        semantics=("parallel", "arbitrary"), args=[qt, kt, z, o, do, lse, dz], comm=comm, aliases={6: 2})
    return dq, dk, dz, cres


def _place():
    x, y, c = lax.axis_index("x"), lax.axis_index("y"), lax.axis_index("c")
    other_chips = [(1 - x, y), (x, 1 - y), (1 - x, 1 - y)]
    return x, y, c, other_chips


def _cast_place(name, w, chip_arr, tr=256):
    r, cc = w.shape
    tr = min(tr, r)

    def body(p_ref, w_ref, o_ref):
        o_ref[...] = w_ref[...].astype(BF)

    return pl.pallas_call(
        body, name=name,
        grid_spec=pltpu.PrefetchScalarGridSpec(
            num_scalar_prefetch=1, grid=(r // tr,),
            in_specs=[pl.BlockSpec((tr, cc), lambda i, p_ref: (i, 0))],
            out_specs=pl.BlockSpec((None, tr, cc), lambda i, p_ref: (p_ref[0], i, 0))),
        out_shape=jax.ShapeDtypeStruct((N_CHIPS, r, cc), BF),
        compiler_params=_params(("parallel",)),
    )(chip_arr, w)


def _gather_comm(bufs, short_host=False):
    n = len(bufs)
    pairs = [(w, j) for w in range(n) for j in range(N_CHIPS - 1)]

    def copies(dst, sems):
        send, recv, fsend, frecv = sems
        x, y, c, chips = _place()

        def part(w, chip, core_half):
            h = bufs[w].shape[1] // 2
            return dst[w].at[2 * chip[0] + chip[1], pl.ds(core_half * h, h)]

        def ici(w, j, incoming):
            slab = part(w, chips[j] if incoming else (x, y), c)
            return pltpu.make_async_remote_copy(
                src_ref=slab, dst_ref=slab, send_sem=send.at[3 * w + j], recv_sem=recv.at[3 * w + j],
                device_id=(*chips[j], c), device_id_type=MESH)

        def d2d(w, j, incoming):
            slab = part(w, chips[j], 1 - c if incoming else c)
            return pltpu.make_async_remote_copy(
                src_ref=slab, dst_ref=slab, send_sem=fsend.at[3 * w + j], recv_sem=frecv.at[3 * w + j],
                device_id=(x, y, 1 - c), device_id_type=MESH)

        return ici, d2d

    def first(_, dst, sems):
        ici, _d = copies(dst, sems)
        for w, j in pairs:
            ici(w, j, False).start()

    def middle(_, dst, sems):
        ici, d2d = copies(dst, sems)
        for w, j in pairs:
            ici(w, j, True).wait_recv()
            d2d(w, j, False).start()

    def last(_, dst, sems):
        ici, d2d = copies(dst, sems)
        for w, j in pairs:
            d2d(w, j, True).wait_recv()
        for w, j in pairs:
            ici(w, j, False).wait_send()
            d2d(w, j, False).wait_send()

    def middle_and_last(src, dst, sems):
        middle(src, dst, sems)
        last(src, dst, sems)

    phases = (first, None, middle_and_last) if short_host else (first, middle, last)
    return _Comm(arrays=list(bufs), out_shapes=[jax.ShapeDtypeStruct(b.shape, b.dtype) for b in bufs],
                 aliases={w: w for w in range(n)}, sems=[pltpu.SemaphoreType.DMA((3 * n,))] * 4, phases=phases)


def _run_comm(name, comm):
    nci, nco = len(comm.arrays), len(comm.out_shapes)

    def body(*refs):
        cin, cout, sems = refs[:nci], refs[nci:nci + nco], refs[nci + nco:]
        for fn in comm.phases:
            if fn is not None:
                fn(cin, cout, sems)

    return pl.pallas_call(
        body, name=name, in_specs=[ANY] * nci, out_specs=[ANY] * nco, out_shape=list(comm.out_shapes),
        input_output_aliases=dict(comm.aliases), scratch_shapes=list(comm.sems),
    )(*comm.arrays)


def _pair_comm(grads):
    n = len(grads)

    def copies(src, dst, sems):
        send, recv = sems
        x, y, c, _ = _place()
        out = []
        for w in range(n):
            h = grads[w].shape[1] // 2
            out.append(pltpu.make_async_remote_copy(
                src_ref=src[w].at[:, pl.ds((1 - c) * h, h), :], dst_ref=dst[w],
                send_sem=send.at[w], recv_sem=recv.at[w], device_id=(x, y, 1 - c), device_id_type=MESH))
        return out

    def first(src, dst, sems):
        for cp in copies(src, dst, sems):
            cp.start()

    def last(src, dst, sems):
        for cp in copies(src, dst, sems):
            cp.wait()

    return _Comm(arrays=list(grads),
                 out_shapes=[jax.ShapeDtypeStruct((N_CHIPS, g.shape[1] // 2, g.shape[2]), g.dtype) for g in grads],
                 aliases={}, sems=[pltpu.SemaphoreType.DMA((n,))] * 2, phases=(first, None, last))


def _pair_sum(name, own, got, c_arr, tr=256):
    nc, r, cc = own.shape
    h = r // 2
    tr = min(tr, h)
    nb = h // tr

    def body(c_ref, a_ref, b_ref, o_ref):
        o_ref[...] = (a_ref[...].astype(F32) + b_ref[...].astype(F32)).astype(BF)

    return pl.pallas_call(
        body, name=name,
        grid_spec=pltpu.PrefetchScalarGridSpec(
            num_scalar_prefetch=1, grid=(nc, nb),
            in_specs=[pl.BlockSpec((None, tr, cc), lambda s, i, c_ref: (s, c_ref[0] * nb + i, 0)),
                      pl.BlockSpec((None, tr, cc), lambda s, i, c_ref: (s, i, 0))],
            out_specs=pl.BlockSpec((None, tr, cc), lambda s, i, c_ref: (s, i, 0))),
        out_shape=jax.ShapeDtypeStruct((nc, h, cc), BF),
        compiler_params=_params(("parallel", "parallel")),
    )(c_arr, own, got)


def _chip_comm(parts):
    n = len(parts)

    def copies(src, dst, sems):
        send, recv = sems
        _, _, c, chips = _place()
        return [pltpu.make_async_remote_copy(
            src_ref=src[w].at[2 * chip[0] + chip[1]], dst_ref=dst[w].at[j],
            send_sem=send.at[3 * w + j], recv_sem=recv.at[3 * w + j], device_id=(*chip, c), device_id_type=MESH)
            for w in range(n) for j, chip in enumerate(chips)]

    def first(src, dst, sems):
        for cp in copies(src, dst, sems):
            cp.start()

    def last(src, dst, sems):
        for cp in copies(src, dst, sems):
            cp.wait()

    return _Comm(arrays=list(parts), out_shapes=[jax.ShapeDtypeStruct((N_CHIPS - 1,) + p.shape[1:], p.dtype) for p in parts],
                 aliases={}, sems=[pltpu.SemaphoreType.DMA((3 * n,))] * 2, phases=(first, None, last))


def _chip_sum(name, parts, got, chip_arr, c_arr, tr=256):
    _, h, cc = parts.shape
    tr = min(tr, h)
    nb = h // tr

    def body(chip_ref, c_ref, own_ref, got_ref, o_ref):
        acc = own_ref[...].astype(F32)
        for k in range(N_CHIPS - 1):
            acc = acc + got_ref[k].astype(F32)
        o_ref[...] = acc

    return pl.pallas_call(
        body, name=name,
        grid_spec=pltpu.PrefetchScalarGridSpec(
            num_scalar_prefetch=2, grid=(nb,),
            in_specs=[pl.BlockSpec((None, tr, cc), lambda i, chip_ref, c_ref: (chip_ref[0], i, 0)),
                      pl.BlockSpec((N_CHIPS - 1, tr, cc), lambda i, chip_ref, c_ref: (0, i, 0))],
            out_specs=pl.BlockSpec((tr, cc), lambda i, chip_ref, c_ref: (c_ref[0] * nb + i, 0))),
        out_shape=jax.ShapeDtypeStruct((2 * h, cc), F32),
        compiler_params=_params(("parallel",)),
    )(chip_arr, c_arr, parts, got)


def _pair_gather_comm(bufs):
    n = len(bufs)

    def copy(dst, sems, w, core_half):
        send, recv = sems
        x, y, c, _ = _place()
        h = bufs[w].shape[0] // 2
        rows = dst[w].at[pl.ds((1 - c if core_half == "theirs" else c) * h, h)]
        return pltpu.make_async_remote_copy(src_ref=rows, dst_ref=rows, send_sem=send.at[w], recv_sem=recv.at[w],
                                            device_id=(x, y, 1 - c), device_id_type=MESH)

    def first(_, dst, sems):
        for w in range(n):
            copy(dst, sems, w, "mine").start()

    def last(_, dst, sems):
        for w in range(n):
            copy(dst, sems, w, "theirs").wait_recv()
        for w in range(n):
            copy(dst, sems, w, "mine").wait_send()

    return _Comm(arrays=list(bufs), out_shapes=[jax.ShapeDtypeStruct(b.shape, b.dtype) for b in bufs],
                 aliases={w: w for w in range(n)}, sems=[pltpu.SemaphoreType.DMA((n,))] * 2, phases=(first, None, last))


def _merge_comms(a, b):
    nai, nao, nas = len(a.arrays), len(a.out_shapes), len(a.sems)

    def both(fa, fb):
        if fa is None and fb is None:
            return None

        def phase(cin, cout, sems):
            if fa is not None:
                fa(cin[:nai], cout[:nao], sems[:nas])
            if fb is not None:
                fb(cin[nai:], cout[nao:], sems[nas:])
        return phase

    return _Comm(arrays=a.arrays + b.arrays, out_shapes=a.out_shapes + b.out_shapes,
                 aliases={**a.aliases, **{nai + k: nao + v for k, v in b.aliases.items()}},
                 sems=a.sems + b.sems, phases=tuple(both(fa, fb) for fa, fb in zip(a.phases, b.phases)))


def _all_sum_small(name, v):
    p = v.shape[0]

    def body(v_ref, o_ref, slots, send, recv):
        x, y, c, _ = _place()
        me = 4 * x + 2 * y + c
        copies = []
        for k in range(1, N_DEV):
            peer = (x ^ (k >> 2), y ^ ((k >> 1) & 1), c ^ (k & 1))
            copies.append(pltpu.make_async_remote_copy(
                src_ref=v_ref, dst_ref=slots.at[me], send_sem=send.at[k - 1], recv_sem=recv.at[k - 1],
                device_id=peer, device_id_type=MESH))
        for cp in copies:
            cp.start()
        slots[me] = v_ref[...]
        for cp in copies:
            cp.wait()
        acc = slots[0]
        for s in range(1, N_DEV):
            acc = acc + slots[s]
        o_ref[...] = acc

    vm = pl.BlockSpec(memory_space=pltpu.VMEM)
    return pl.pallas_call(
        body, name=name,
        in_specs=[vm], out_specs=vm,
        out_shape=jax.ShapeDtypeStruct(v.shape, F32),
        scratch_shapes=[pltpu.VMEM((N_DEV, p, LANES), F32), pltpu.SemaphoreType.DMA((N_DEV - 1,)),
                        pltpu.SemaphoreType.DMA((N_DEV - 1,))],
    )(v)


def _adamw(name, w, g, m, v, tr=256):
    r, c = w.shape
    tr = min(tr, r)
    assert r % tr == 0
    bc1 = 1.0 - ADAM_B1 ** ADAM_STEP
    bc2 = 1.0 - ADAM_B2 ** ADAM_STEP

    def body(w_ref, g_ref, m_ref, v_ref, d_ref, nm_ref, nv_ref):
        gv = g_ref[...]
        nm = ADAM_B1 * m_ref[...] + (1.0 - ADAM_B1) * gv
        nv = ADAM_B2 * v_ref[...] + (1.0 - ADAM_B2) * (gv * gv)
        nm_ref[...] = nm
        nv_ref[...] = nv
        d_ref[...] = -ADAM_LR * ((nm / bc1) / (jnp.sqrt(nv / bc2) + ADAM_EPS) + ADAM_WD * w_ref[...])

    blk = pl.BlockSpec((tr, c), lambda i: (i, 0))
    return pl.pallas_call(
        body, name=name, grid=(r // tr,),
        in_specs=[blk] * 4, out_specs=[blk] * 3,
        out_shape=[jax.ShapeDtypeStruct((r, c), F32)] * 3,
        compiler_params=_params(("parallel",)),
    )(w, g, m, v)


def _pack_small(parts):
    flat = jnp.concatenate([a.reshape(-1) for a in parts])
    n = flat.shape[0]
    p = -(-n // (8 * LANES)) * 8
    packed = jnp.pad(flat, (0, p * LANES - n)).reshape(p, LANES)

    def unpack(q):
        out, off = [], 0
        f = q.reshape(-1)
        for a in parts:
            out.append(f[off:off + a.size].reshape(a.shape))
            off += a.size
        return out

    return packed, unpack


def kernel(x, p, norm_mix, w_in, w_dw, conv_ln_g, conv_ln_b, w_conv_proj, q_norm, k_norm, w_attn_proj, w_out, norm_ffn, w_ff1, w_ff2, norm_ple, w_ple_gate, w_ple_proj, norm_final, loss_target, m_norm_mix, m_w_in, m_w_dw, m_conv_ln_g, m_conv_ln_b, m_w_conv_proj, m_q_norm, m_k_norm, m_w_attn_proj, m_w_out, m_norm_ffn, m_w_ff1, m_w_ff2, m_norm_ple, m_w_ple_gate, m_w_ple_proj, m_norm_final, v_norm_mix, v_w_in, v_w_dw, v_conv_ln_g, v_conv_ln_b, v_w_conv_proj, v_q_norm, v_k_norm, v_w_attn_proj, v_w_out, v_norm_ffn, v_w_ff1, v_w_ff2, v_norm_ple, v_w_ple_gate, v_w_ple_proj, v_norm_final):
    s, d = x.shape[1], x.shape[2]
    cw = d // 2
    kvw = d // GROUP
    xs, ps, tgt = x[0], p[0, 0], loss_target[0]
    cx, cy, cc = lax.axis_index("x"), lax.axis_index("y"), lax.axis_index("c")
    chip = 2 * cx + cy
    c_arr = jnp.reshape(cc, (1,)).astype(jnp.int32)
    tm, tme = min(MM_TM, s), min(MM_TM_EPI, s)

    names = ["w_in", "w_conv_proj", "w_attn_proj", "w_out", "w_ff1", "w_ff2", "w_ple_gate", "w_ple_proj"]
    big = [w_in, w_conv_proj, w_attn_proj, w_out, w_ff1, w_ff2, w_ple_gate, w_ple_proj]
    chip_arr = jnp.reshape(chip, (1,)).astype(jnp.int32)
    placed = [_cast_place("cast_" + nm, w[0], chip_arr) for nm, w in zip(names, big)]
    cpc = cw // N_CHIPS
    taps_rows = 32
    my_taps = jnp.pad(w_dw[0], ((0, taps_rows - CONV_KERNEL), (0, 0)))[None]
    taps_buf = lax.dynamic_update_slice(jnp.zeros((N_CHIPS, taps_rows, cpc), F32), my_taps, (chip, 0, 0))
    h0, (win, taps_all) = _rms_fwd("rms_mix", xs, norm_mix, comm=_gather_comm([placed[0], taps_buf], short_host=True))
    wdw = taps_all.transpose(1, 0, 2).reshape(taps_rows, cw)

    cos, sin = _rope_tables(s)
    (z,) = _mm("z_proj", h0, win, b_cm=True, tm=tm, tn=win.shape[2] // 3, tk=d)
    uc, act = _conv_fwd(z, wdw, conv_ln_g, conv_ln_b, cw)
    qt, kt = _qk_fwd(z, cos, sin, q_norm, k_norm, d)
    o, lse, (wcp, wap, wout, w1, w2, wpg, wple) = _flash_fwd(qt, kt, z, d, comm=_gather_comm(placed[1:]))
    wap, wout, w2, wpg = (t.reshape(-1, t.shape[-1]) for t in (wap, wout, w2, wpg))
    (y_c,) = _mm("conv_proj", act, wcp, b_cm=True, tm=tm, tn=wcp.shape[2], tk=cw, out_dtypes=(F32,))
    tn = d // 2
    gcb = (2 * d + 2 * kvw) // tn

    def merge_epi(acc, yc, gc, ga):
        return acc, _sigmoid(gc.astype(F32)) * yc + _sigmoid(ga.astype(F32)) * acc

    y_a, merged = _mm("attn_proj", o, wap, tm=tme, tn=tn, tk=d, epi=merge_epi, out_dtypes=(BF, BF), b_resident=True,
                      extras=[_tile_extra(y_c, tme, tn), _tile_extra(z, tme, tn, gcb), _tile_extra(z, tme, tn, gcb + 2)])
    def residual_norm(acc, r, g):
        xn = r + acc
        return xn, xn * lax.rsqrt(jnp.mean(xn * xn, axis=-1, keepdims=True) + EPS) * g

    gain = lambda g: (g, (1, d), lambda i, j, k: (0, 0), True)
    x1, h1 = _mm("out_proj", merged, wout, tm=tme, tn=d, tk=d, epi=residual_norm, out_dtypes=(F32, BF), b_resident=True,
                 extras=[_tile_extra(xs, tme, d), gain(norm_ffn)])
    (a,) = _mm("ff1", h1, w1, b_cm=True, tm=tm, tn=tn, tk=d)

    def relu2(t):
        return jnp.square(jnp.maximum(t, 0.0))

    x2, h2 = _mm("ff2", a, w2, tm=tme, tn=d, tk=d, a_fn=relu2, epi=residual_norm, out_dtypes=(F32, BF),
                 extras=[_tile_extra(x1, tme, d), gain(norm_ple)])
    to_bf = lambda t: t.astype(BF)
    (e,) = _mm("ple_proj", ps, wple, b_cm=True, tm=tm, tn=wple.shape[2], tk=ps.shape[1], a_fn=to_bf)

    def ple_epi(acc, ev, r):
        gt = _sigmoid(acc)
        return r + gt * ev.astype(F32), gt

    x3, gate = _mm("ple_gate", h2, wpg, tm=tme, tn=tn, tk=d, epi=ple_epi, out_dtypes=(F32, BF), b_resident=True,
                   extras=[_tile_extra(e, tme, tn), _tile_extra(x2, tme, tn)])

    dx3, de, dgp, sq, d_fin = _loss_bwd(x3, tgt, norm_final.reshape(1, d), e, gate)
    tkt = min(2048, s)
    (g_wple,) = _mm("d_wple", ps, de, ta=True, out_cm=True, tm=ps.shape[1], tn=wple.shape[2], tk=tkt, a_fn=to_bf)
    (g_wpg,) = _mm("d_wpg", h2, dgp, ta=True, tm=tm, tn=tn, tk=tkt)
    (dh2,) = _mm("d_h2", dgp, wpg, tb=True, tm=tm, tn=tn, tk=d)
    dx2, dx2b, d_ple = _rms_bwd("rms_ple_bwd", dh2, x2, norm_ple, dx3)

    (da,) = _mm("d_a", dx2b, w2, tb=True, tm=tm, tn=tn, tk=d, out_dtypes=(BF,),
                epi=lambda acc, av: (acc * (2.0 * jnp.maximum(av.astype(F32), 0.0)),), extras=[_tile_extra(a, tm, tn)])
    (g_w2,) = _mm("d_w2", a, dx2b, ta=True, tm=tm, tn=tn, tk=tkt, a_fn=relu2)
    (g_w1,) = _mm("d_w1", h1, da, ta=True, out_cm=True, tm=tm, tn=tn, tk=tkt)
    (dh1,) = _mm("d_h1", da, w1, tb=True, b_cm=True, tm=tm, tn=tn, tk=w1.shape[2])
    dx1, dx1b, d_ffn = _rms_bwd("rms_ffn_bwd", dh1, x1, norm_ffn, dx2)

    def merge_bwd(acc, gc, ga, yc, ya):
        sc, sa = _sigmoid(gc.astype(F32)), _sigmoid(ga.astype(F32))
        return acc * sc, acc * sa, jnp.concatenate(
            [acc * yc * sc * (1.0 - sc), acc * ya.astype(F32) * sa * (1.0 - sa)], axis=1)

    tmd = min(MM_TM_DZ, s)
    gate0 = 2 * d + 2 * kvw
    z_cols = z.shape[1]

    def gate_window(width, col0):
        return (pl.Element(tmd), pl.Element(width)), lambda i, j, k: (i * tmd, col0)

    dy_c, dy_a, dz = _mm(
        "d_merged", dx1b, wout, tb=True, tm=tmd, tn=d, tk=d, epi=merge_bwd, out_dtypes=(BF, BF, BF), b_resident=True,
        extras=[(z, *gate_window(d, gate0)), (z, *gate_window(d, gate0 + d)), _tile_extra(y_c, tmd, d),
                _tile_extra(y_a, tmd, d)],
        out_overrides={2: ((s, z_cols), pl.BlockSpec(*gate_window(2 * d, gate0)))})
    (g_wout,) = _mm("d_wout", merged, dx1b, ta=True, tm=tm, tn=tn, tk=tkt)
    (g_wap,) = _mm("d_wap", o, dy_a, ta=True, tm=tm, tn=tn, tk=tkt)

    def slabs(g):
        return g if g.ndim == 3 else g.reshape(N_CHIPS, g.shape[0] // N_CHIPS, g.shape[1])

    grads_a = [slabs(g) for g in (g_wap, g_wout, g_w1, g_w2, g_wpg, g_wple)]
    (do,), got_a = _mm("d_o", dy_a, wap, tb=True, tm=tm, tn=tn, tk=d, comm=_pair_comm(grads_a))
    parts_a = [_pair_sum("pair_sum_" + nm, g, r, c_arr) for nm, g, r in zip(names[2:], grads_a, got_a)]
    dqt, dkt, dz, _ = _flash_bwd(qt, kt, z, o, do, lse, d, dz)
    dz, d_qn, d_kn = _qk_bwd(dqt, dkt, z, cos, sin, q_norm, k_norm, d, dz)
    (g_wcp,) = _mm("d_wcp", act, dy_c, ta=True, out_cm=True, tm=cw, tn=wcp.shape[2], tk=tkt)
    (dact,) = _mm("d_act", dy_c, wcp, tb=True, b_cm=True, tm=tm, tn=cw, tk=wcp.shape[2])
    p_wap, p_wout, p_w1, p_w2, p_wpg, p_wple = parts_a
    dz, d_taps, d_lng, d_lnb, (s_wap, s_wout, s_wpg, s_wple) = _conv_bwd(
        dact, uc, z, wdw, conv_ln_g, conv_ln_b, cw, dz, comm=_chip_comm([p_wap, p_wout, p_wpg, p_wple]))
    (g_win,), (s_w1, s_w2) = _mm("d_win", h0, dz, ta=True, out_cm=True, tm=tm, tn=win.shape[2] // 3, tk=tkt,
                                 comm=_chip_comm([p_w1, p_w2]))
    slots_a = [s_wap, s_wout, s_w1, s_w2, s_wpg, s_wple]
    grads_b = [slabs(g_win), slabs(g_wcp)]
    got_b = _run_comm("grad_pair_exchange_b", _pair_comm(grads_b))
    parts_b = [_pair_sum("pair_sum_" + nm, g, r, c_arr) for nm, g, r in zip(names[:2], grads_b, got_b)]
    halves_a = [_chip_sum("chip_sum_" + nm, cp, sl, chip_arr, c_arr) for nm, cp, sl in zip(names[2:], parts_a, slots_a)]
    (dh0,), hosted = _mm("d_h0", dz, win, tb=True, b_cm=True, tm=tm, tn=tn, tk=win.shape[2],
                         comm=_merge_comms(_chip_comm(parts_b), _pair_gather_comm(halves_a)))
    slots_b, grads_a_done = hosted[:2], hosted[2:]
    dx, _, d_mix = _rms_bwd("rms_mix_bwd", dh0, xs, norm_mix, dx1)
    halves_b = [_chip_sum("chip_sum_" + nm, cp, sl, chip_arr, c_arr) for nm, cp, sl in zip(names[:2], parts_b, slots_b)]
    big_grads = list(_run_comm("grad_pair_gather_b", _pair_gather_comm(halves_b))) + list(grads_a_done)

    small = [d_mix, d_taps[:CONV_KERNEL], d_lng, d_lnb, d_qn, d_kn, d_ffn, d_ple, d_fin]
    packed, unpack = _pack_small(small)
    g_mix, g_taps, g_lng, g_lnb, g_qn, g_kn, g_ffn, g_ple, g_fin = unpack(_all_sum_small("reduce_small", packed))
    g_dw = lax.dynamic_slice_in_dim(g_taps.reshape(CONV_KERNEL, N_CHIPS, cpc), chip, 1, axis=1).reshape(1, CONV_KERNEL, cpc)

    sq_local = lax.reduce_precision(sq[0, 0], 8, 23)
    loss = (0.5 / d) * lax.psum(sq_local, ("x", "y", "c"))

    grads = {
        "norm_mix": g_mix, "w_in": big_grads[0][None], "w_dw": g_dw, "conv_ln_g": g_lng, "conv_ln_b": g_lnb,
        "w_conv_proj": big_grads[1][None], "q_norm": g_qn, "k_norm": g_kn, "w_attn_proj": big_grads[2][None],
        "w_out": big_grads[3][None], "norm_ffn": g_ffn, "w_ff1": big_grads[4][None], "w_ff2": big_grads[5][None],
        "norm_ple": g_ple, "w_ple_gate": big_grads[6][None], "w_ple_proj": big_grads[7][None],
        "norm_final": g_fin.reshape(d),
    }
    weights = dict(norm_mix=norm_mix, w_in=w_in, w_dw=w_dw, conv_ln_g=conv_ln_g, conv_ln_b=conv_ln_b, w_conv_proj=w_conv_proj,
                   q_norm=q_norm, k_norm=k_norm, w_attn_proj=w_attn_proj, w_out=w_out, norm_ffn=norm_ffn, w_ff1=w_ff1,
                   w_ff2=w_ff2, norm_ple=norm_ple, w_ple_gate=w_ple_gate, w_ple_proj=w_ple_proj, norm_final=norm_final)
    m_in = dict(norm_mix=m_norm_mix, w_in=m_w_in, w_dw=m_w_dw, conv_ln_g=m_conv_ln_g, conv_ln_b=m_conv_ln_b,
                w_conv_proj=m_w_conv_proj, q_norm=m_q_norm, k_norm=m_k_norm, w_attn_proj=m_w_attn_proj, w_out=m_w_out,
                norm_ffn=m_norm_ffn, w_ff1=m_w_ff1, w_ff2=m_w_ff2, norm_ple=m_norm_ple, w_ple_gate=m_w_ple_gate,
                w_ple_proj=m_w_ple_proj, norm_final=m_norm_final)
    v_in = dict(norm_mix=v_norm_mix, w_in=v_w_in, w_dw=v_w_dw, conv_ln_g=v_conv_ln_g, conv_ln_b=v_conv_ln_b,
                w_conv_proj=v_w_conv_proj, q_norm=v_q_norm, k_norm=v_k_norm, w_attn_proj=v_w_attn_proj, w_out=v_w_out,
                norm_ffn=v_norm_ffn, w_ff1=v_w_ff1, w_ff2=v_w_ff2, norm_ple=v_norm_ple, w_ple_gate=v_w_ple_gate,
                w_ple_proj=v_w_ple_proj, norm_final=v_norm_final)
    order = list(weights)
    deltas, new_m, new_v, g_out = [], [], [], []
    for nm in order:
        w = weights[nm]
        shape = w.shape
        two_d = (-1, shape[-1])
        dl, mm_, vv_ = _adamw("adamw_" + nm, w.reshape(two_d), grads[nm].reshape(two_d), m_in[nm].reshape(two_d),
                              v_in[nm].reshape(two_d))
        g_out.append(grads[nm].reshape(shape))
        deltas.append(dl.reshape(shape))
        new_m.append(mm_.reshape(shape))
        new_v.append(vv_.reshape(shape))
    return (loss, dx[None], *g_out, *deltas, *new_m, *new_v)
```

```python
from typing import NamedTuple

import jax
import jax.numpy as jnp
from jax import lax
from jax.experimental import pallas as pl
from jax.experimental.pallas import tpu as pltpu

F32 = jnp.float32
BF = jnp.bfloat16

EPS = 1e-6
HEAD_DIM = 128
GROUP = 4
GRID_W = 64
ROPE_THETA = 10000.0
CONV_KERNEL = 31
HALO = 16
N_CHIPS = 4
N_DEV = 8
LANES = 128

ADAM_LR = 0.001
ADAM_B1 = 0.9
ADAM_B2 = 0.999
ADAM_EPS = 1e-08
ADAM_WD = 0.01
ADAM_STEP = 10

VMEM_LIMIT = 56 * 2 ** 20
LOG2E = 1.4426950408889634
LN2 = 0.6931471805599453
Q_SCALE = HEAD_DIM ** -0.5 * LOG2E
ROW_TILE = 256
FLASH_TQ_FWD = 512
FLASH_TQ_BWD = 512
FLASH_TK = 512
MM_TM = 1024
MM_TM_EPI = 512
MM_TM_DZ = 256
MESH = pl.DeviceIdType.MESH
ANY = pl.BlockSpec(memory_space=pl.ANY)


def _params(sem):
    return pltpu.CompilerParams(dimension_semantics=sem, vmem_limit_bytes=VMEM_LIMIT)


def _sigmoid(x):
    return 1.0 / (1.0 + jnp.exp(-x))


class _Comm(NamedTuple):
    arrays: list
    out_shapes: list
    aliases: dict
    sems: list
    phases: tuple


def _call(body, *, name, grid, in_specs, out_specs, out_shape, scratch_shapes, semantics, args, comm=None, aliases=None):
    n_in, n_out = len(in_specs), len(out_specs)
    aliases = dict(aliases or {})
    if comm is None:
        res = pl.pallas_call(body, name=name, grid=grid, in_specs=in_specs, out_specs=out_specs, out_shape=out_shape,
                             scratch_shapes=scratch_shapes, input_output_aliases=aliases,
                             compiler_params=_params(semantics))(*args)
        return res, []
    nci, nco, ncs = len(comm.arrays), len(comm.out_shapes), len(comm.sems)
    n_steps = 1
    for g in grid:
        n_steps *= g
    first, middle, last = comm.phases

    def hosted(*refs):
        ins, cin = refs[:n_in], refs[n_in:n_in + nci]
        outs = refs[n_in + nci:n_in + nci + n_out]
        cout = refs[n_in + nci + n_out:n_in + nci + n_out + nco]
        rest = refs[n_in + nci + n_out + nco:]
        scratch, sems = rest[:len(rest) - ncs], rest[len(rest) - ncs:]
        step = 0
        for ax, g in enumerate(grid):
            step = step * g + pl.program_id(ax)
        for at, fn in ((0, first), (n_steps // 2, middle)):
            if fn is not None:
                pl.when(step == at)(lambda fn=fn: fn(cin, cout, sems))
        body(*ins, *outs, *scratch)
        if last is not None:
            pl.when(step == n_steps - 1)(lambda: last(cin, cout, sems))

    res = pl.pallas_call(
        hosted, name=name, grid=grid,
        in_specs=list(in_specs) + [ANY] * nci, out_specs=list(out_specs) + [ANY] * nco,
        out_shape=list(out_shape) + list(comm.out_shapes),
        input_output_aliases={**aliases, **{n_in + a: n_out + b for a, b in comm.aliases.items()}},
        scratch_shapes=list(scratch_shapes) + list(comm.sems),
        compiler_params=_params(("arbitrary",) * len(grid)),
    )(*args, *comm.arrays)
    return res[:n_out], res[n_out:]


def _mm(name, a, b, *, tm, tn, tk, ta=False, tb=False, b_cm=False, out_cm=False,
        a_fn=None, extras=(), epi=None, out_dtypes=(BF,), epi_rows=256, comm=None, b_resident=False,
        out_overrides=None):
    if ta:
        kc, m = a.shape
    else:
        m, kc = a.shape
    if b_cm:
        nc, r, c = b.shape
        n, per = (r, c) if tb else (nc * c, c)
    else:
        n = b.shape[0] if tb else b.shape[1]
    tm, tn, tk = min(tm, m), min(tn, n), min(tk, kc)
    assert m % tm == 0 and n % tn == 0 and kc % tk == 0, (name, m, n, kc, tm, tn, tk)
    nk = kc // tk
    a_spec = pl.BlockSpec((tk, tm), lambda i, j, k: (k, i)) if ta else pl.BlockSpec((tm, tk), lambda i, j, k: (i, k))
    if b_cm and not tb:
        assert per % tn == 0
        npj = per // tn
        b_spec = pl.BlockSpec((None, tk, tn), lambda i, j, k: (j // npj, k, j % npj))
    elif b_cm:
        assert per % tk == 0
        npk = per // tk
        b_spec = pl.BlockSpec((None, tn, tk), lambda i, j, k: (k // npk, j, k % npk))
    elif b_resident:
        assert nk == 1
        b_spec = pl.BlockSpec(b.shape, lambda i, j, k: (0, 0))
    elif tb:
        b_spec = pl.BlockSpec((tn, tk), lambda i, j, k: (j, k))
    else:
        b_spec = pl.BlockSpec((tk, tn), lambda i, j, k: (k, j))
    if out_cm:
        assert (n // N_CHIPS) % tn == 0
        npo = (n // N_CHIPS) // tn
        o_spec = pl.BlockSpec((None, tm, tn), lambda i, j, k: (j // npo, i, j % npo))
        o_shape = (N_CHIPS, m, n // N_CHIPS)
    else:
        o_spec = pl.BlockSpec((tm, tn), lambda i, j, k: (i, j))
        o_shape = (m, n)
    ne, no = len(extras), len(out_dtypes)
    whole = [len(e) > 3 and e[3] for e in extras]
    extras = [e[:3] for e in extras]

    def epi_args(ex, rows):
        return [e[...] if w else e[rows, :] for e, w in zip(ex, whole)]

    dims = (((0 if ta else 1,), (1 if tb else 0,)), ((), ()))
    er = min(epi_rows, tm)
    chunked = nk == 1 and epi is not None and not ta
    use_acc = (nk > 1 or epi is not None) and not chunked
    assert chunked or not b_resident

    def body(*refs):
        a_ref, b_ref = refs[0], refs[1]
        ex = refs[2:2 + ne]
        outs = refs[2 + ne:2 + ne + no]
        if chunked:
            if b_resident:
                cols = pl.ds(pl.multiple_of(pl.program_id(1) * tn, tn), tn)
                bt = b_ref[cols, :] if tb else b_ref[:, cols]
            else:
                bt = b_ref[...]
            for r0 in range(0, tm, er):
                rows = slice(r0, r0 + er)
                at = a_ref[rows, :]
                if a_fn is not None:
                    at = a_fn(at)
                d = lax.dot_general(at, bt, dims, preferred_element_type=F32)
                vals = epi(d, *epi_args(ex, rows))
                for o, v, dt in zip(outs, vals, out_dtypes):
                    o[rows, :] = v.astype(dt)
            return
        at = a_ref[...]
        if a_fn is not None:
            at = a_fn(at)
        d = lax.dot_general(at, b_ref[...], dims, preferred_element_type=F32)
        if not use_acc:
            outs[0][...] = d.astype(out_dtypes[0])
            return
        acc = refs[-1]
        k = pl.program_id(2)

        @pl.when(k == 0)
        def _():
            acc[...] = d

        if nk > 1:
            @pl.when(k > 0)
            def _():
                acc[...] += d

        @pl.when(k == nk - 1)
        def _():
            for r0 in range(0, tm, er):
                rows = slice(r0, r0 + er)
                if epi is None:
                    vals = (acc[rows, :],)
                else:
                    vals = epi(acc[rows, :], *epi_args(ex, rows))
                for o, v, dt in zip(outs, vals, out_dtypes):
                    o[rows, :] = v.astype(dt)

    out_specs = [o_spec] * no
    out_shapes = [jax.ShapeDtypeStruct(o_shape, dt) for dt in out_dtypes]
    for idx, (shape, spec) in (out_overrides or {}).items():
        out_specs[idx], out_shapes[idx] = spec, jax.ShapeDtypeStruct(shape, out_dtypes[idx])
    res, cres = _call(
        body, name=name, grid=(m // tm, n // tn, nk),
        in_specs=[a_spec, b_spec] + [pl.BlockSpec(bs, im) for _, bs, im in extras],
        out_specs=out_specs,
        out_shape=out_shapes,
        scratch_shapes=[pltpu.VMEM((tm, tn), F32)] if use_acc else [],
        semantics=("parallel", "parallel", "arbitrary"),
        args=[a, b] + [e for e, _, _ in extras], comm=comm)
    return res if comm is None else (res, cres)


def _tile_extra(arr, tm, tn, col_block0=0):
    return (arr, (tm, tn), lambda i, j, k: (i, j + col_block0))


def _rms_fwd(name, x, g, ts=None, comm=None):
    s, d = x.shape
    ts = ts or ROW_TILE

    def body(x_ref, g_ref, h_ref):
        xv = x_ref[...]
        r = lax.rsqrt(jnp.mean(xv * xv, axis=-1, keepdims=True) + EPS)
        h_ref[...] = (xv * r * g_ref[...]).astype(BF)

    (h,), cres = _call(
        body, name=name, grid=(s // ts,),
        in_specs=[pl.BlockSpec((ts, d), lambda i: (i, 0)), pl.BlockSpec((1, d), lambda i: (0, 0))],
        out_specs=[pl.BlockSpec((ts, d), lambda i: (i, 0))],
        out_shape=[jax.ShapeDtypeStruct((s, d), BF)],
        scratch_shapes=[], semantics=("parallel",), args=[x, g], comm=comm)
    return h if comm is None else (h, cres)


def _rms_bwd(name, dh, x, g, dres, ts=None):
    s, d = x.shape
    ts = ts or ROW_TILE

    def body(dh_ref, x_ref, g_ref, dres_ref, dx_ref, dxb_ref, dg_ref):
        xv = x_ref[...]
        dhv = dh_ref[...].astype(F32)
        r = lax.rsqrt(jnp.mean(xv * xv, axis=-1, keepdims=True) + EPS)
        nrm = xv * r
        dn = dhv * g_ref[...]
        dx = dres_ref[...] + r * (dn - nrm * jnp.mean(dn * nrm, axis=-1, keepdims=True))
        dx_ref[...] = dx
        dxb_ref[...] = dx.astype(BF)
        part = jnp.sum(dhv * nrm, axis=0, keepdims=True)

        @pl.when(pl.program_id(0) == 0)
        def _():
            dg_ref[...] = part

        @pl.when(pl.program_id(0) > 0)
        def _():
            dg_ref[...] += part

    row = pl.BlockSpec((ts, d), lambda i: (i, 0))
    vec = pl.BlockSpec((1, d), lambda i: (0, 0))
    return pl.pallas_call(
        body, name=name, grid=(s // ts,),
        in_specs=[row, row, vec, row],
        out_specs=[row, row, vec],
        out_shape=[jax.ShapeDtypeStruct((s, d), F32), jax.ShapeDtypeStruct((s, d), BF), jax.ShapeDtypeStruct((1, d), F32)],
        compiler_params=_params(("arbitrary",)),
    )(dh, x, g, dres)


def _loss_bwd(x3, tgt, gfin, e, gate, ts=None):
    s, d = x3.shape
    ts = ts or ROW_TILE

    def body(x_ref, t_ref, g_ref, e_ref, gate_ref, dx_ref, de_ref, dgp_ref, sq_ref, dg_ref):
        xv = x_ref[...]
        gv = g_ref[...]
        r = lax.rsqrt(jnp.mean(xv * xv, axis=-1, keepdims=True) + EPS)
        nrm = xv * r
        err = nrm * gv - t_ref[...]
        dy = err * (1.0 / d)
        dn = dy * gv
        dx = r * (dn - nrm * jnp.mean(dn * nrm, axis=-1, keepdims=True))
        dx_ref[...] = dx
        ev = e_ref[...].astype(F32)
        gt = gate_ref[...].astype(F32)
        de_ref[...] = (dx * gt).astype(BF)
        dgp_ref[...] = (dx * ev * gt * (1.0 - gt)).astype(BF)
        sq = jnp.full((8, LANES), jnp.sum(err * err), F32)
        part = jnp.sum(dy * nrm, axis=0, keepdims=True)

        @pl.when(pl.program_id(0) == 0)
        def _():
            sq_ref[...] = sq
            dg_ref[...] = part

        @pl.when(pl.program_id(0) > 0)
        def _():
            sq_ref[...] += sq
            dg_ref[...] += part

    row = pl.BlockSpec((ts, d), lambda i: (i, 0))
    vec = pl.BlockSpec((1, d), lambda i: (0, 0))
    return pl.pallas_call(
        body, name="loss_bwd", grid=(s // ts,),
        in_specs=[row, row, vec, row, row],
        out_specs=[row, row, row, pl.BlockSpec((8, LANES), lambda i: (0, 0)), vec],
        out_shape=[jax.ShapeDtypeStruct((s, d), F32), jax.ShapeDtypeStruct((s, d), BF), jax.ShapeDtypeStruct((s, d), BF),
                   jax.ShapeDtypeStruct((8, LANES), F32), jax.ShapeDtypeStruct((1, d), F32)],
        compiler_params=_params(("arbitrary",)),
    )(x3, tgt, gfin, e, gate)


def _halo_specs(ts, s, width, col_block):
    per = ts // HALO
    last = s // HALO - 1
    return [
        pl.BlockSpec((HALO, width), lambda i: (jnp.maximum(i * per - 1, 0), col_block)),
        pl.BlockSpec((ts, width), lambda i: (i, col_block)),
        pl.BlockSpec((HALO, width), lambda i: (jnp.minimum((i + 1) * per, last), col_block)),
    ]


def _glu_ext(zp, zc, zn, ext, cw, ts, i, n_tiles):
    def glu(zr):
        zv = zr[...].astype(F32)
        return zv[:, :cw] * _sigmoid(zv[:, cw:])

    ext[0:HALO, :] = jnp.where(i > 0, glu(zp), 0.0)
    ext[HALO:HALO + ts, :] = glu(zc)
    ext[HALO + ts:, :] = jnp.where(i < n_tiles - 1, glu(zn), 0.0)


SUBLANES = 8


def _shift_scratch(ts):
    return pltpu.VMEM((SUBLANES, ts + 2 * HALO - SUBLANES, LANES), F32)


def _shifted_copies(ext, sh, cols, ts):
    n = ts + 2 * HALO - SUBLANES
    for r in range(SUBLANES):
        sh[r] = ext[r:r + n, cols]


def _tap_rows(sh, off, ts):
    q, r = divmod(off, SUBLANES)
    return sh[r, q * SUBLANES:q * SUBLANES + ts, :]


def _ln_stats(uc):
    mu = jnp.mean(uc, axis=-1, keepdims=True)
    xc = uc - mu
    rstd = lax.rsqrt(jnp.mean(xc * xc, axis=-1, keepdims=True) + EPS)
    return xc * rstd, rstd


def _conv_fwd(z, wdw, ln_g, ln_b, cw, ts=None):
    s = z.shape[0]
    ts = ts or ROW_TILE
    n_tiles = s // ts
    pad = CONV_KERNEL // 2

    def body(zp, zc, zn, w_ref, g_ref, b_ref, uc_ref, act_ref, ext, sh):
        i = pl.program_id(0)
        _glu_ext(zp, zc, zn, ext, cw, ts, i, n_tiles)

        def col_block(cb, carry):
            cols = pl.ds(pl.multiple_of(cb * LANES, LANES), LANES)
            _shifted_copies(ext, sh, cols, ts)
            acc = jnp.zeros((ts, LANES), F32)
            for j in range(CONV_KERNEL):
                acc = acc + _tap_rows(sh, HALO - pad + j, ts) * w_ref[j:j + 1, cols]
            uc_ref[:, cols] = acc
            return carry

        lax.fori_loop(0, cw // LANES, col_block, 0)
        xhat, _ = _ln_stats(uc_ref[...])
        ln = xhat * g_ref[...] + b_ref[...]
        act_ref[...] = (ln * _sigmoid(ln)).astype(BF)

    vec = pl.BlockSpec((1, cw), lambda i: (0, 0))
    row = pl.BlockSpec((ts, cw), lambda i: (i, 0))
    return pl.pallas_call(
        body, name="conv_fwd", grid=(n_tiles,),
        in_specs=_halo_specs(ts, s, 2 * cw, 0) + [pl.BlockSpec((32, cw), lambda i: (0, 0)), vec, vec],
        out_specs=[row, row],
        out_shape=[jax.ShapeDtypeStruct((s, cw), F32), jax.ShapeDtypeStruct((s, cw), BF)],
        scratch_shapes=[pltpu.VMEM((ts + 2 * HALO, cw), F32), _shift_scratch(ts)],
        compiler_params=_params(("parallel",)),
    )(z, z, z, wdw, ln_g, ln_b)


def _conv_bwd(ds, uc, z, wdw, ln_g, ln_b, cw, dz, ts=None, comm=None):
    s = z.shape[0]
    ts = ts or ROW_TILE
    n_tiles = s // ts
    pad = CONV_KERNEL // 2

    def body(zp, zc, zn, dsp, dsc, dsn, ucp, ucc, ucn, w_ref, g_ref, b_ref, _dz_in,
             dz_ref, dw_ref, dg_ref, db_ref, ext, dext, sh, dsh):
        i = pl.program_id(0)
        gv, bv = g_ref[...], b_ref[...]

        def ln_bwd(ds_r, uc_r):
            xhat, rstd = _ln_stats(uc_r[...])
            ln = xhat * gv + bv
            sg = _sigmoid(ln)
            dln = ds_r[...].astype(F32) * (sg * (1.0 + ln * (1.0 - sg)))
            dxh = dln * gv
            duc = rstd * (dxh - jnp.mean(dxh, axis=-1, keepdims=True) - xhat * jnp.mean(dxh * xhat, axis=-1, keepdims=True))
            return duc, dln, xhat

        duc_p, _, _ = ln_bwd(dsp, ucp)
        duc_c, dln_c, xhat_c = ln_bwd(dsc, ucc)
        duc_n, _, _ = ln_bwd(dsn, ucn)
        dext[0:HALO, :] = jnp.where(i > 0, duc_p, 0.0)
        dext[HALO:HALO + ts, :] = duc_c
        dext[HALO + ts:, :] = jnp.where(i < n_tiles - 1, duc_n, 0.0)
        _glu_ext(zp, zc, zn, ext, cw, ts, i, n_tiles)

        dg_part = jnp.sum(dln_c * xhat_c, axis=0, keepdims=True)
        db_part = jnp.sum(dln_c, axis=0, keepdims=True)

        @pl.when(i == 0)
        def _():
            dw_ref[...] = jnp.zeros_like(dw_ref)
            dg_ref[...] = dg_part
            db_ref[...] = db_part

        @pl.when(i > 0)
        def _():
            dg_ref[...] += dg_part
            db_ref[...] += db_part

        def col_block(cb, carry):
            c0 = pl.multiple_of(cb * LANES, LANES)
            cols, gate_cols = pl.ds(c0, LANES), pl.ds(cw + c0, LANES)
            _shifted_copies(dext, dsh, cols, ts)
            _shifted_copies(ext, sh, cols, ts)
            du = jnp.zeros((ts, LANES), F32)
            for j in range(CONV_KERNEL):
                du = du + _tap_rows(dsh, HALO + pad - j, ts) * w_ref[j:j + 1, cols]
            ca, sb = zc[:, cols].astype(F32), _sigmoid(zc[:, gate_cols].astype(F32))
            dz_ref[:, cols] = (du * sb).astype(BF)
            dz_ref[:, gate_cols] = (du * ca * sb * (1.0 - sb)).astype(BF)
            duc_blk = _tap_rows(dsh, HALO, ts)
            for j in range(CONV_KERNEL):
                dw_ref[j:j + 1, cols] += jnp.sum(_tap_rows(sh, HALO - pad + j, ts) * duc_blk, axis=0, keepdims=True)
            return carry

        lax.fori_loop(0, cw // LANES, col_block, 0)

    vec = pl.BlockSpec((1, cw), lambda i: (0, 0))
    wsp = pl.BlockSpec((32, cw), lambda i: (0, 0))
    res, cres = _call(
        body, name="conv_bwd", grid=(n_tiles,),
        in_specs=(_halo_specs(ts, s, 2 * cw, 0) + _halo_specs(ts, s, cw, 0) + _halo_specs(ts, s, cw, 0)
                  + [wsp, vec, vec, ANY]),
        out_specs=[pl.BlockSpec((ts, 2 * cw), lambda i: (i, 0)), wsp, vec, vec],
        out_shape=[jax.ShapeDtypeStruct(dz.shape, BF), jax.ShapeDtypeStruct((32, cw), F32),
                   jax.ShapeDtypeStruct((1, cw), F32), jax.ShapeDtypeStruct((1, cw), F32)],
        scratch_shapes=[pltpu.VMEM((ts + 2 * HALO, cw), F32), pltpu.VMEM((ts + 2 * HALO, cw), F32),
                        _shift_scratch(ts), _shift_scratch(ts)],
        semantics=("arbitrary",), args=[z, z, z, ds, ds, ds, uc, uc, uc, wdw, ln_g, ln_b, dz], comm=comm,
        aliases={12: 0})
    return (*res, cres)


def _rope_tables(s):
    axis_dim = HEAD_DIM // 2
    n_rows = s // GRID_W
    inv_freq = ROPE_THETA ** (-jnp.arange(0, axis_dim, 2, dtype=F32) / axis_dim)[None, :]
    ar = jnp.arange(n_rows, dtype=jnp.int32).astype(F32)[:, None] * inv_freq
    ac = jnp.arange(GRID_W, dtype=jnp.int32).astype(F32)[:, None] * inv_freq

    def table(fr, fc):
        by_row = jnp.broadcast_to(fr[:, None, :], (n_rows, GRID_W, axis_dim))
        by_col = jnp.broadcast_to(fc[None, :, :], (n_rows, GRID_W, axis_dim))
        return jnp.concatenate([by_row, by_col], axis=-1).reshape(s, HEAD_DIM)

    cos = table(jnp.concatenate([jnp.cos(ar), jnp.cos(ar)], axis=-1), jnp.concatenate([jnp.cos(ac), jnp.cos(ac)], axis=-1))
    sin = table(jnp.concatenate([-jnp.sin(ar), jnp.sin(ar)], axis=-1), jnp.concatenate([-jnp.sin(ac), jnp.sin(ac)], axis=-1))
    return cos, sin


def _swap_quarters(x):
    q = HEAD_DIM // 4
    lane = lax.broadcasted_iota(jnp.int32, x.shape, 1)
    return jnp.where((lane % (2 * q)) < q, pltpu.roll(x, HEAD_DIM - q, 1), pltpu.roll(x, q, 1))


def _qk_fwd(z, cos, sin, qg, kg, d, ts=None):
    s = z.shape[0]
    ts = ts or ROW_TILE
    kvw = d // GROUP
    scale = Q_SCALE

    def body(q_ref, k_ref, c_ref, s_ref, qg_ref, kg_ref, qo_ref, ko_ref):
        cv, sv = c_ref[...], s_ref[...]

        def head(x_ref, g_ref, o_ref, h, mul):
            xv = x_ref[:, h * HEAD_DIM:(h + 1) * HEAD_DIM].astype(F32)
            r = lax.rsqrt(jnp.mean(xv * xv, axis=-1, keepdims=True) + EPS)
            nrm = xv * r * g_ref[...]
            out = nrm * cv + _swap_quarters(nrm) * sv
            o_ref[:, h * HEAD_DIM:(h + 1) * HEAD_DIM] = (out * mul).astype(BF)

        for h in range(d // HEAD_DIM):
            head(q_ref, qg_ref, qo_ref, h, scale)
        for h in range(kvw // HEAD_DIM):
            head(k_ref, kg_ref, ko_ref, h, 1.0)

    cw2 = d
    tab = pl.BlockSpec((ts, HEAD_DIM), lambda i: (i, 0))
    vec = pl.BlockSpec((1, HEAD_DIM), lambda i: (0, 0))
    return pl.pallas_call(
        body, name="qk_fwd", grid=(s // ts,),
        in_specs=[pl.BlockSpec((ts, d), lambda i: (i, cw2 // d)),
                  pl.BlockSpec((ts, kvw), lambda i: (i, (cw2 + d) // kvw)), tab, tab, vec, vec],
        out_specs=[pl.BlockSpec((ts, d), lambda i: (i, 0)), pl.BlockSpec((ts, kvw), lambda i: (i, 0))],
        out_shape=[jax.ShapeDtypeStruct((s, d), BF), jax.ShapeDtypeStruct((s, kvw), BF)],
        compiler_params=_params(("parallel",)),
    )(z, z, cos, sin, qg, kg)


def _qk_bwd(dqt, dkt, z, cos, sin, qg, kg, d, dz, ts=None):
    s = z.shape[0]
    ts = ts or ROW_TILE
    kvw = d // GROUP
    scale = HEAD_DIM ** -0.5

    def body(dq_ref, dk_ref, q_ref, k_ref, c_ref, s_ref, qg_ref, kg_ref, _dz_in, dzo_ref, dqg_ref, dkg_ref):
        cv, sv = c_ref[...], s_ref[...]

        def head(dy_ref, x_ref, g_ref, col0, h, mul):
            dout = dy_ref[:, h * HEAD_DIM:(h + 1) * HEAD_DIM].astype(F32) * mul
            dn = dout * cv + _swap_quarters(dout * sv)
            xv = x_ref[:, h * HEAD_DIM:(h + 1) * HEAD_DIM].astype(F32)
            r = lax.rsqrt(jnp.mean(xv * xv, axis=-1, keepdims=True) + EPS)
            nh = xv * r
            dnh = dn * g_ref[...]
            c0 = col0 + h * HEAD_DIM
            dzo_ref[:, c0:c0 + HEAD_DIM] = (r * (dnh - nh * jnp.mean(dnh * nh, axis=-1, keepdims=True))).astype(BF)
            return jnp.sum(dn * nh, axis=0, keepdims=True)

        dqg = jnp.zeros((1, HEAD_DIM), F32)
        for h in range(d // HEAD_DIM):
            dqg = dqg + head(dq_ref, q_ref, qg_ref, 0, h, scale)
        dkg = jnp.zeros((1, HEAD_DIM), F32)
        for h in range(kvw // HEAD_DIM):
            dkg = dkg + head(dk_ref, k_ref, kg_ref, d, h, LN2)

        @pl.when(pl.program_id(0) == 0)
        def _():
            dqg_ref[...] = dqg
            dkg_ref[...] = dkg

        @pl.when(pl.program_id(0) > 0)
        def _():
            dqg_ref[...] += dqg
            dkg_ref[...] += dkg

    cw2 = d
    tab = pl.BlockSpec((ts, HEAD_DIM), lambda i: (i, 0))
    vec = pl.BlockSpec((1, HEAD_DIM), lambda i: (0, 0))
    qrow = pl.BlockSpec((ts, d), lambda i: (i, 0))
    krow = pl.BlockSpec((ts, kvw), lambda i: (i, 0))
    window = pl.BlockSpec((pl.Element(ts), pl.Element(d + kvw)), lambda i: (i * ts, cw2))
    return pl.pallas_call(
        body, name="qk_bwd", grid=(s // ts,),
        in_specs=[qrow, krow, pl.BlockSpec((ts, d), lambda i: (i, cw2 // d)),
                  pl.BlockSpec((ts, kvw), lambda i: (i, (cw2 + d) // kvw)), tab, tab, vec, vec, ANY],
        out_specs=[window, vec, vec],
        out_shape=[jax.ShapeDtypeStruct(dz.shape, BF),
                   jax.ShapeDtypeStruct((1, HEAD_DIM), F32), jax.ShapeDtypeStruct((1, HEAD_DIM), F32)],
        input_output_aliases={8: 0},
        compiler_params=_params(("arbitrary",)),
    )(dqt, dkt, z, z, cos, sin, qg, kg, dz)


_NT = (((1,), (1,)), ((), ()))
_TN = (((0,), (0,)), ((), ()))


def _v_col_block(d):
    return (2 * d + d // GROUP) // HEAD_DIM


def _flash_fwd(qt, kt, z, d, tq=None, tk=None, comm=None):
    s = qt.shape[0]
    tq, tk = min(tq or FLASH_TQ_FWD, s), min(tk or FLASH_TK, s)
    ng, nq, nk = d // (GROUP * HEAD_DIM), s // tq, s // tk
    gw = GROUP * HEAD_DIM
    rows = GROUP * tq

    nt = tk // LANES
    assert nk % 2 == 0, (s, tk)

    def body(q_ref, k_ref, v_ref, o_ref, lse_ref, qs, v1, p_s, m_s, acc_s, sc_s):
        @pl.when(pl.program_id(1) == 0)
        def _():
            v1[:, :HEAD_DIM] = v_ref[...]
            v1[:, HEAD_DIM:] = jnp.ones((s, HEAD_DIM), BF)

        for h in range(GROUP):
            qs[h * tq:(h + 1) * tq, :] = q_ref[:, h * HEAD_DIM:(h + 1) * HEAD_DIM]
        m_s[...] = jnp.full((rows, LANES), -1e30, F32)
        acc_s[...] = jnp.zeros((rows, 2 * HEAD_DIM), F32)

        def scores(j):
            return lax.dot_general(qs[...], k_ref[pl.ds(pl.multiple_of(j * tk, tk), tk), :], _NT, preferred_element_type=F32)

        def softmax_pv(j, slot):
            kv_rows = pl.ds(pl.multiple_of(j * tk, tk), tk)
            mt = sc_s[slot, :, :LANES]
            for c in range(1, nt):
                mt = jnp.maximum(mt, sc_s[slot, :, c * LANES:(c + 1) * LANES])
            m_old = m_s[...]
            m_new = jnp.maximum(m_old, jnp.max(mt, axis=-1, keepdims=True))
            alpha = jnp.exp2(m_old - m_new)
            for c in range(nt):
                cs = slice(c * LANES, (c + 1) * LANES)
                p_s[:, cs] = jnp.exp2(sc_s[slot, :, cs] - m_new).astype(BF)
            pv = jnp.dot(p_s[...], v1[kv_rows, :], preferred_element_type=F32)
            acc_s[:, :HEAD_DIM] = alpha * acc_s[:, :HEAD_DIM] + pv[:, :HEAD_DIM]
            acc_s[:, HEAD_DIM:] = alpha * acc_s[:, HEAD_DIM:] + pv[:, HEAD_DIM:]
            m_s[...] = m_new

        sc_s[0] = scores(0)

        def step(jj, carry):
            j = 2 * jj
            sc_s[1] = scores(j + 1)
            softmax_pv(j, 0)
            sc_s[0] = scores(jnp.minimum(j + 2, nk - 1))
            softmax_pv(j + 1, 1)
            return carry

        lax.fori_loop(0, nk // 2, step, 0)
        l = acc_s[:, HEAD_DIM:]
        o = acc_s[:, :HEAD_DIM] / l
        for h in range(GROUP):
            o_ref[:, h * HEAD_DIM:(h + 1) * HEAD_DIM] = o[h * tq:(h + 1) * tq, :].astype(BF)
        lse = m_s[...] + jnp.log2(l)
        for h in range(GROUP):
            lse_ref[h] = lse[h * tq:(h + 1) * tq, :]

    vb = _v_col_block(d)
    (o, lse), cres = _call(
        body, name="flash_fwd", grid=(ng, nq),
        in_specs=[pl.BlockSpec((tq, gw), lambda g, i: (i, g)),
                  pl.BlockSpec((s, HEAD_DIM), lambda g, i: (0, g)),
                  pl.BlockSpec((s, HEAD_DIM), lambda g, i: (0, vb + g))],
        out_specs=[pl.BlockSpec((tq, gw), lambda g, i: (i, g)),
                   pl.BlockSpec((GROUP, tq, LANES), lambda g, i: (g, i, 0))],
        out_shape=[jax.ShapeDtypeStruct((s, d), BF), jax.ShapeDtypeStruct((ng * GROUP, s, LANES), F32)],
        scratch_shapes=[pltpu.VMEM((rows, HEAD_DIM), BF), pltpu.VMEM((s, 2 * HEAD_DIM), BF), pltpu.VMEM((rows, tk), BF),
                        pltpu.VMEM((rows, LANES), F32), pltpu.VMEM((rows, 2 * HEAD_DIM), F32), pltpu.VMEM((2, rows, tk), F32)],
        semantics=("parallel", "arbitrary"), args=[qt, kt, z], comm=comm)
    return o, lse, cres


def _flash_bwd(qt, kt, z, o, do, lse, d, dz, tq=None, tk=None, comm=None):
    s = qt.shape[0]
    tq, tk = min(tq or FLASH_TQ_BWD, s), min(tk or FLASH_TK, s)
    ng, nq, nk = d // (GROUP * HEAD_DIM), s // tq, s // tk
    gw = GROUP * HEAD_DIM
    rows = GROUP * tq

    nt = tk // LANES

    def body(q_ref, k_ref, v_ref, o_ref, do_ref, lse_ref, _dz_in, dq_ref, dk_ref, dzv_ref,
             qs, dos, delta_s, dq_s, p_s, ds_s, lse_s, dv_ref):
        i = pl.program_id(1)
        for h in range(GROUP):
            cols = slice(h * HEAD_DIM, (h + 1) * HEAD_DIM)
            lse_s[h * tq:(h + 1) * tq, :] = lse_ref[h]
            qs[h * tq:(h + 1) * tq, :] = q_ref[:, cols]
            dov = do_ref[:, cols]
            dos[h * tq:(h + 1) * tq, :] = dov
            delta = jnp.sum(dov.astype(F32) * o_ref[:, cols].astype(F32), axis=-1, keepdims=True)
            delta_s[h * tq:(h + 1) * tq, :] = jnp.broadcast_to(delta, (tq, LANES))
        dq_s[...] = jnp.zeros((rows, HEAD_DIM), F32)

        @pl.when(i == 0)
        def _():
            dk_ref[...] = jnp.zeros_like(dk_ref)
            dv_ref[...] = jnp.zeros_like(dv_ref)

        def step(j, carry):
            kv_rows = pl.ds(pl.multiple_of(j * tk, tk), tk)
            kv, vv = k_ref[kv_rows, :], v_ref[kv_rows, :]
            sc = lax.dot_general(qs[...], kv, _NT, preferred_element_type=F32)
            dp = lax.dot_general(dos[...], vv, _NT, preferred_element_type=F32)
            lse, delta = lse_s[...], delta_s[...]
            for c in range(nt):
                cs = slice(c * LANES, (c + 1) * LANES)
                p = jnp.exp2(sc[:, cs] - lse)
                p_s[:, cs] = p.astype(BF)
                ds_s[:, cs] = (p * (dp[:, cs] - delta)).astype(BF)
            dv_ref[kv_rows, :] += lax.dot_general(p_s[...], dos[...], _TN, preferred_element_type=F32)
            dk_ref[kv_rows, :] += lax.dot_general(ds_s[...], qs[...], _TN, preferred_element_type=F32)
            dq_s[...] += jnp.dot(ds_s[...], kv, preferred_element_type=F32)
            return carry

        lax.fori_loop(0, nk, step, 0)
        for h in range(GROUP):
            dq_ref[:, h * HEAD_DIM:(h + 1) * HEAD_DIM] = dq_s[h * tq:(h + 1) * tq, :].astype(BF)

        @pl.when(i == nq - 1)
        def _():
            dzv_ref[...] = dv_ref[...].astype(BF)

    vb = _v_col_block(d)
    qspec = pl.BlockSpec((tq, gw), lambda g, i: (i, g))
    kspec = pl.BlockSpec((s, HEAD_DIM), lambda g, i: (0, g))
    vspec = pl.BlockSpec((s, HEAD_DIM), lambda g, i: (0, vb + g))
    (dq, dk, dz), cres = _call(
        body, name="flash_bwd", grid=(ng, nq),
        in_specs=[qspec, kspec, vspec, qspec, qspec, pl.BlockSpec((GROUP, tq, LANES), lambda g, i: (g, i, 0)), ANY],
        out_specs=[qspec, kspec, vspec],
        out_shape=[jax.ShapeDtypeStruct((s, d), BF), jax.ShapeDtypeStruct((s, d // GROUP), F32),
                   jax.ShapeDtypeStruct(dz.shape, BF)],
        scratch_shapes=[pltpu.VMEM((rows, HEAD_DIM), BF), pltpu.VMEM((rows, HEAD_DIM), BF), pltpu.VMEM((rows, LANES), F32),
                        pltpu.VMEM((rows, HEAD_DIM), F32), pltpu.VMEM((rows, tk), BF), pltpu.VMEM((rows, tk), BF),
                        pltpu.VMEM((rows, LANES), F32), pltpu.VMEM((s, HEAD_DIM), F32)],
        semantics=("parallel", "arbitrary"), args=[qt, kt, z, o, do, lse, dz], comm=comm, aliases={6: 2})
    return dq, dk, dz, cres


def _place():
    x, y, c = lax.axis_index("x"), lax.axis_index("y"), lax.axis_index("c")
    other_chips = [(1 - x, y), (x, 1 - y), (1 - x, 1 - y)]
    return x, y, c, other_chips


def _cast_place(name, w, chip_arr, tr=256):
    r, cc = w.shape
    tr = min(tr, r)

    def body(p_ref, w_ref, o_ref):
        o_ref[...] = w_ref[...].astype(BF)

    return pl.pallas_call(
        body, name=name,
        grid_spec=pltpu.PrefetchScalarGridSpec(
            num_scalar_prefetch=1, grid=(r // tr,),
            in_specs=[pl.BlockSpec((tr, cc), lambda i, p_ref: (i, 0))],
            out_specs=pl.BlockSpec((None, tr, cc), lambda i, p_ref: (p_ref[0], i, 0))),
        out_shape=jax.ShapeDtypeStruct((N_CHIPS, r, cc), BF),
        compiler_params=_params(("parallel",)),
    )(chip_arr, w)


def _gather_comm(bufs, short_host=False):
    n = len(bufs)
    pairs = [(w, j) for w in range(n) for j in range(N_CHIPS - 1)]

    def copies(dst, sems):
        send, recv, fsend, frecv = sems
        x, y, c, chips = _place()

        def part(w, chip, core_half):
            h = bufs[w].shape[1] // 2
            return dst[w].at[2 * chip[0] + chip[1], pl.ds(core_half * h, h)]

        def ici(w, j, incoming):
            slab = part(w, chips[j] if incoming else (x, y), c)
            return pltpu.make_async_remote_copy(
                src_ref=slab, dst_ref=slab, send_sem=send.at[3 * w + j], recv_sem=recv.at[3 * w + j],
                device_id=(*chips[j], c), device_id_type=MESH)

        def d2d(w, j, incoming):
            slab = part(w, chips[j], 1 - c if incoming else c)
            return pltpu.make_async_remote_copy(
                src_ref=slab, dst_ref=slab, send_sem=fsend.at[3 * w + j], recv_sem=frecv.at[3 * w + j],
                device_id=(x, y, 1 - c), device_id_type=MESH)

        return ici, d2d

    def first(_, dst, sems):
        ici, _d = copies(dst, sems)
        for w, j in pairs:
            ici(w, j, False).start()

    def middle(_, dst, sems):
        ici, d2d = copies(dst, sems)
        for w, j in pairs:
            ici(w, j, True).wait_recv()
            d2d(w, j, False).start()

    def last(_, dst, sems):
        ici, d2d = copies(dst, sems)
        for w, j in pairs:
            d2d(w, j, True).wait_recv()
        for w, j in pairs:
            ici(w, j, False).wait_send()
            d2d(w, j, False).wait_send()

    def middle_and_last(src, dst, sems):
        middle(src, dst, sems)
        last(src, dst, sems)

    phases = (first, None, middle_and_last) if short_host else (first, middle, last)
    return _Comm(arrays=list(bufs), out_shapes=[jax.ShapeDtypeStruct(b.shape, b.dtype) for b in bufs],
                 aliases={w: w for w in range(n)}, sems=[pltpu.SemaphoreType.DMA((3 * n,))] * 4, phases=phases)


def _run_comm(name, comm):
    nci, nco = len(comm.arrays), len(comm.out_shapes)

    def body(*refs):
        cin, cout, sems = refs[:nci], refs[nci:nci + nco], refs[nci + nco:]
        for fn in comm.phases:
            if fn is not None:
                fn(cin, cout, sems)

    return pl.pallas_call(
        body, name=name, in_specs=[ANY] * nci, out_specs=[ANY] * nco, out_shape=list(comm.out_shapes),
        input_output_aliases=dict(comm.aliases), scratch_shapes=list(comm.sems),
    )(*comm.arrays)


def _pair_comm(grads):
    n = len(grads)

    def copies(src, dst, sems):
        send, recv = sems
        x, y, c, _ = _place()
        out = []
        for w in range(n):
            h = grads[w].shape[1] // 2
            out.append(pltpu.make_async_remote_copy(
                src_ref=src[w].at[:, pl.ds((1 - c) * h, h), :], dst_ref=dst[w],
                send_sem=send.at[w], recv_sem=recv.at[w], device_id=(x, y, 1 - c), device_id_type=MESH))
        return out

    def first(src, dst, sems):
        for cp in copies(src, dst, sems):
            cp.start()

    def last(src, dst, sems):
        for cp in copies(src, dst, sems):
            cp.wait()

    return _Comm(arrays=list(grads),
                 out_shapes=[jax.ShapeDtypeStruct((N_CHIPS, g.shape[1] // 2, g.shape[2]), g.dtype) for g in grads],
                 aliases={}, sems=[pltpu.SemaphoreType.DMA((n,))] * 2, phases=(first, None, last))


def _pair_sum(name, own, got, c_arr, tr=256):
    nc, r, cc = own.shape
    h = r // 2
    tr = min(tr, h)
    nb = h // tr

    def body(c_ref, a_ref, b_ref, o_ref):
        o_ref[...] = (a_ref[...].astype(F32) + b_ref[...].astype(F32)).astype(BF)

    return pl.pallas_call(
        body, name=name,
        grid_spec=pltpu.PrefetchScalarGridSpec(
            num_scalar_prefetch=1, grid=(nc, nb),
            in_specs=[pl.BlockSpec((None, tr, cc), lambda s, i, c_ref: (s, c_ref[0] * nb + i, 0)),
                      pl.BlockSpec((None, tr, cc), lambda s, i, c_ref: (s, i, 0))],
            out_specs=pl.BlockSpec((None, tr, cc), lambda s, i, c_ref: (s, i, 0))),
        out_shape=jax.ShapeDtypeStruct((nc, h, cc), BF),
        compiler_params=_params(("parallel", "parallel")),
    )(c_arr, own, got)


def _chip_comm(parts):
    n = len(parts)

    def copies(src, dst, sems):
        send, recv = sems
        _, _, c, chips = _place()
        return [pltpu.make_async_remote_copy(
            src_ref=src[w].at[2 * chip[0] + chip[1]], dst_ref=dst[w].at[j],
            send_sem=send.at[3 * w + j], recv_sem=recv.at[3 * w + j], device_id=(*chip, c), device_id_type=MESH)
            for w in range(n) for j, chip in enumerate(chips)]

    def first(src, dst, sems):
        for cp in copies(src, dst, sems):
            cp.start()

    def last(src, dst, sems):
        for cp in copies(src, dst, sems):
            cp.wait()

    return _Comm(arrays=list(parts), out_shapes=[jax.ShapeDtypeStruct((N_CHIPS - 1,) + p.shape[1:], p.dtype) for p in parts],
                 aliases={}, sems=[pltpu.SemaphoreType.DMA((3 * n,))] * 2, phases=(first, None, last))


def _chip_sum(name, parts, got, chip_arr, c_arr, tr=256):
    _, h, cc = parts.shape
    tr = min(tr, h)
    nb = h // tr

    def body(chip_ref, c_ref, own_ref, got_ref, o_ref):
        acc = own_ref[...].astype(F32)
        for k in range(N_CHIPS - 1):
            acc = acc + got_ref[k].astype(F32)
        o_ref[...] = acc

    return pl.pallas_call(
        body, name=name,
        grid_spec=pltpu.PrefetchScalarGridSpec(
            num_scalar_prefetch=2, grid=(nb,),
            in_specs=[pl.BlockSpec((None, tr, cc), lambda i, chip_ref, c_ref: (chip_ref[0], i, 0)),
                      pl.BlockSpec((N_CHIPS - 1, tr, cc), lambda i, chip_ref, c_ref: (0, i, 0))],
            out_specs=pl.BlockSpec((tr, cc), lambda i, chip_ref, c_ref: (c_ref[0] * nb + i, 0))),
        out_shape=jax.ShapeDtypeStruct((2 * h, cc), F32),
        compiler_params=_params(("parallel",)),
    )(chip_arr, c_arr, parts, got)


def _pair_gather_comm(bufs):
    n = len(bufs)

    def copy(dst, sems, w, core_half):
        send, recv = sems
        x, y, c, _ = _place()
        h = bufs[w].shape[0] // 2
        rows = dst[w].at[pl.ds((1 - c if core_half == "theirs" else c) * h, h)]
        return pltpu.make_async_remote_copy(src_ref=rows, dst_ref=rows, send_sem=send.at[w], recv_sem=recv.at[w],
                                            device_id=(x, y, 1 - c), device_id_type=MESH)

    def first(_, dst, sems):
        for w in range(n):
            copy(dst, sems, w, "mine").start()

    def last(_, dst, sems):
        for w in range(n):
            copy(dst, sems, w, "theirs").wait_recv()
        for w in range(n):
            copy(dst, sems, w, "mine").wait_send()

    return _Comm(arrays=list(bufs), out_shapes=[jax.ShapeDtypeStruct(b.shape, b.dtype) for b in bufs],
                 aliases={w: w for w in range(n)}, sems=[pltpu.SemaphoreType.DMA((n,))] * 2, phases=(first, None, last))


def _merge_comms(a, b):
    nai, nao, nas = len(a.arrays), len(a.out_shapes), len(a.sems)

    def both(fa, fb):
        if fa is None and fb is None:
            return None

        def phase(cin, cout, sems):
            if fa is not None:
                fa(cin[:nai], cout[:nao], sems[:nas])
            if fb is not None:
                fb(cin[nai:], cout[nao:], sems[nas:])
        return phase

    return _Comm(arrays=a.arrays + b.arrays, out_shapes=a.out_shapes + b.out_shapes,
                 aliases={**a.aliases, **{nai + k: nao + v for k, v in b.aliases.items()}},
                 sems=a.sems + b.sems, phases=tuple(both(fa, fb) for fa, fb in zip(a.phases, b.phases)))


def _all_sum_small(name, v):
    p = v.shape[0]

    def body(v_ref, o_ref, slots, send, recv):
        x, y, c, _ = _place()
        me = 4 * x + 2 * y + c
        copies = []
        for k in range(1, N_DEV):
            peer = (x ^ (k >> 2), y ^ ((k >> 1) & 1), c ^ (k & 1))
            copies.append(pltpu.make_async_remote_copy(
                src_ref=v_ref, dst_ref=slots.at[me], send_sem=send.at[k - 1], recv_sem=recv.at[k - 1],
                device_id=peer, device_id_type=MESH))
        for cp in copies:
            cp.start()
        slots[me] = v_ref[...]
        for cp in copies:
            cp.wait()
        acc = slots[0]
        for s in range(1, N_DEV):
            acc = acc + slots[s]
        o_ref[...] = acc

    vm = pl.BlockSpec(memory_space=pltpu.VMEM)
    return pl.pallas_call(
        body, name=name,
        in_specs=[vm], out_specs=vm,
        out_shape=jax.ShapeDtypeStruct(v.shape, F32),
        scratch_shapes=[pltpu.VMEM((N_DEV, p, LANES), F32), pltpu.SemaphoreType.DMA((N_DEV - 1,)),
                        pltpu.SemaphoreType.DMA((N_DEV - 1,))],
    )(v)


def _adamw(name, w, g, m, v, tr=256):
    r, c = w.shape
    tr = min(tr, r)
    assert r % tr == 0
    bc1 = 1.0 - ADAM_B1 ** ADAM_STEP
    bc2 = 1.0 - ADAM_B2 ** ADAM_STEP

    def body(w_ref, g_ref, m_ref, v_ref, d_ref, nm_ref, nv_ref):
        gv = g_ref[...]
        nm = ADAM_B1 * m_ref[...] + (1.0 - ADAM_B1) * gv
        nv = ADAM_B2 * v_ref[...] + (1.0 - ADAM_B2) * (gv * gv)
        nm_ref[...] = nm
        nv_ref[...] = nv
        d_ref[...] = -ADAM_LR * ((nm / bc1) / (jnp.sqrt(nv / bc2) + ADAM_EPS) + ADAM_WD * w_ref[...])

    blk = pl.BlockSpec((tr, c), lambda i: (i, 0))
    return pl.pallas_call(
        body, name=name, grid=(r // tr,),
        in_specs=[blk] * 4, out_specs=[blk] * 3,
        out_shape=[jax.ShapeDtypeStruct((r, c), F32)] * 3,
        compiler_params=_params(("parallel",)),
    )(w, g, m, v)


def _pack_small(parts):
    flat = jnp.concatenate([a.reshape(-1) for a in parts])
    n = flat.shape[0]
    p = -(-n // (8 * LANES)) * 8
    packed = jnp.pad(flat, (0, p * LANES - n)).reshape(p, LANES)

    def unpack(q):
        out, off = [], 0
        f = q.reshape(-1)
        for a in parts:
            out.append(f[off:off + a.size].reshape(a.shape))
            off += a.size
        return out

    return packed, unpack


def kernel(x, p, norm_mix, w_in, w_dw, conv_ln_g, conv_ln_b, w_conv_proj, q_norm, k_norm, w_attn_proj, w_out, norm_ffn, w_ff1, w_ff2, norm_ple, w_ple_gate, w_ple_proj, norm_final, loss_target, m_norm_mix, m_w_in, m_w_dw, m_conv_ln_g, m_conv_ln_b, m_w_conv_proj, m_q_norm, m_k_norm, m_w_attn_proj, m_w_out, m_norm_ffn, m_w_ff1, m_w_ff2, m_norm_ple, m_w_ple_gate, m_w_ple_proj, m_norm_final, v_norm_mix, v_w_in, v_w_dw, v_conv_ln_g, v_conv_ln_b, v_w_conv_proj, v_q_norm, v_k_norm, v_w_attn_proj, v_w_out, v_norm_ffn, v_w_ff1, v_w_ff2, v_norm_ple, v_w_ple_gate, v_w_ple_proj, v_norm_final):
    s, d = x.shape[1], x.shape[2]
    cw = d // 2
    kvw = d // GROUP
    xs, ps, tgt = x[0], p[0, 0], loss_target[0]
    cx, cy, cc = lax.axis_index("x"), lax.axis_index("y"), lax.axis_index("c")
    chip = 2 * cx + cy
    c_arr = jnp.reshape(cc, (1,)).astype(jnp.int32)
    tm, tme = min(MM_TM, s), min(MM_TM_EPI, s)

    names = ["w_in", "w_conv_proj", "w_attn_proj", "w_out", "w_ff1", "w_ff2", "w_ple_gate", "w_ple_proj"]
    big = [w_in, w_conv_proj, w_attn_proj, w_out, w_ff1, w_ff2, w_ple_gate, w_ple_proj]
    chip_arr = jnp.reshape(chip, (1,)).astype(jnp.int32)
    placed = [_cast_place("cast_" + nm, w[0], chip_arr) for nm, w in zip(names, big)]
    cpc = cw // N_CHIPS
    taps_rows = 32
    my_taps = jnp.pad(w_dw[0], ((0, taps_rows - CONV_KERNEL), (0, 0)))[None]
    taps_buf = lax.dynamic_update_slice(jnp.zeros((N_CHIPS, taps_rows, cpc), F32), my_taps, (chip, 0, 0))
    h0, (win, taps_all) = _rms_fwd("rms_mix", xs, norm_mix, comm=_gather_comm([placed[0], taps_buf], short_host=True))
    wdw = taps_all.transpose(1, 0, 2).reshape(taps_rows, cw)

    cos, sin = _rope_tables(s)
    (z,) = _mm("z_proj", h0, win, b_cm=True, tm=tm, tn=win.shape[2] // 3, tk=d)
    uc, act = _conv_fwd(z, wdw, conv_ln_g, conv_ln_b, cw)
    qt, kt = _qk_fwd(z, cos, sin, q_norm, k_norm, d)
    o, lse, (wcp, wap, wout, w1, w2, wpg, wple) = _flash_fwd(qt, kt, z, d, comm=_gather_comm(placed[1:]))
    wap, wout, w2, wpg = (t.reshape(-1, t.shape[-1]) for t in (wap, wout, w2, wpg))
    (y_c,) = _mm("conv_proj", act, wcp, b_cm=True, tm=tm, tn=wcp.shape[2], tk=cw, out_dtypes=(F32,))
    tn = d // 2
    gcb = (2 * d + 2 * kvw) // tn

    def merge_epi(acc, yc, gc, ga):
        return acc, _sigmoid(gc.astype(F32)) * yc + _sigmoid(ga.astype(F32)) * acc

    y_a, merged = _mm("attn_proj", o, wap, tm=tme, tn=tn, tk=d, epi=merge_epi, out_dtypes=(BF, BF), b_resident=True,
                      extras=[_tile_extra(y_c, tme, tn), _tile_extra(z, tme, tn, gcb), _tile_extra(z, tme, tn, gcb + 2)])
    def residual_norm(acc, r, g):
        xn = r + acc
        return xn, xn * lax.rsqrt(jnp.mean(xn * xn, axis=-1, keepdims=True) + EPS) * g

    gain = lambda g: (g, (1, d), lambda i, j, k: (0, 0), True)
    x1, h1 = _mm("out_proj", merged, wout, tm=tme, tn=d, tk=d, epi=residual_norm, out_dtypes=(F32, BF), b_resident=True,
                 extras=[_tile_extra(xs, tme, d), gain(norm_ffn)])
    (a,) = _mm("ff1", h1, w1, b_cm=True, tm=tm, tn=tn, tk=d)

    def relu2(t):
        return jnp.square(jnp.maximum(t, 0.0))

    x2, h2 = _mm("ff2", a, w2, tm=tme, tn=d, tk=d, a_fn=relu2, epi=residual_norm, out_dtypes=(F32, BF),
                 extras=[_tile_extra(x1, tme, d), gain(norm_ple)])
    to_bf = lambda t: t.astype(BF)
    (e,) = _mm("ple_proj", ps, wple, b_cm=True, tm=tm, tn=wple.shape[2], tk=ps.shape[1], a_fn=to_bf)

    def ple_epi(acc, ev, r):
        gt = _sigmoid(acc)
        return r + gt * ev.astype(F32), gt

    x3, gate = _mm("ple_gate", h2, wpg, tm=tme, tn=tn, tk=d, epi=ple_epi, out_dtypes=(F32, BF), b_resident=True,
                   extras=[_tile_extra(e, tme, tn), _tile_extra(x2, tme, tn)])

    dx3, de, dgp, sq, d_fin = _loss_bwd(x3, tgt, norm_final.reshape(1, d), e, gate)
    tkt = min(2048, s)
    (g_wple,) = _mm("d_wple", ps, de, ta=True, out_cm=True, tm=ps.shape[1], tn=wple.shape[2], tk=tkt, a_fn=to_bf)
    (g_wpg,) = _mm("d_wpg", h2, dgp, ta=True, tm=tm, tn=tn, tk=tkt)
    (dh2,) = _mm("d_h2", dgp, wpg, tb=True, tm=tm, tn=tn, tk=d)
    dx2, dx2b, d_ple = _rms_bwd("rms_ple_bwd", dh2, x2, norm_ple, dx3)

    (da,) = _mm("d_a", dx2b, w2, tb=True, tm=tm, tn=tn, tk=d, out_dtypes=(BF,),
                epi=lambda acc, av: (acc * (2.0 * jnp.maximum(av.astype(F32), 0.0)),), extras=[_tile_extra(a, tm, tn)])
    (g_w2,) = _mm("d_w2", a, dx2b, ta=True, tm=tm, tn=tn, tk=tkt, a_fn=relu2)
    (g_w1,) = _mm("d_w1", h1, da, ta=True, out_cm=True, tm=tm, tn=tn, tk=tkt)
    (dh1,) = _mm("d_h1", da, w1, tb=True, b_cm=True, tm=tm, tn=tn, tk=w1.shape[2])
    dx1, dx1b, d_ffn = _rms_bwd("rms_ffn_bwd", dh1, x1, norm_ffn, dx2)

    def merge_bwd(acc, gc, ga, yc, ya):
        sc, sa = _sigmoid(gc.astype(F32)), _sigmoid(ga.astype(F32))
        return acc * sc, acc * sa, jnp.concatenate(
            [acc * yc * sc * (1.0 - sc), acc * ya.astype(F32) * sa * (1.0 - sa)], axis=1)

    tmd = min(MM_TM_DZ, s)
    gate0 = 2 * d + 2 * kvw
    z_cols = z.shape[1]

    def gate_window(width, col0):
        return (pl.Element(tmd), pl.Element(width)), lambda i, j, k: (i * tmd, col0)

    dy_c, dy_a, dz = _mm(
        "d_merged", dx1b, wout, tb=True, tm=tmd, tn=d, tk=d, epi=merge_bwd, out_dtypes=(BF, BF, BF), b_resident=True,
        extras=[(z, *gate_window(d, gate0)), (z, *gate_window(d, gate0 + d)), _tile_extra(y_c, tmd, d),
                _tile_extra(y_a, tmd, d)],
        out_overrides={2: ((s, z_cols), pl.BlockSpec(*gate_window(2 * d, gate0)))})
    (g_wout,) = _mm("d_wout", merged, dx1b, ta=True, tm=tm, tn=tn, tk=tkt)
    (g_wap,) = _mm("d_wap", o, dy_a, ta=True, tm=tm, tn=tn, tk=tkt)

    def slabs(g):
        return g if g.ndim == 3 else g.reshape(N_CHIPS, g.shape[0] // N_CHIPS, g.shape[1])

    grads_a = [slabs(g) for g in (g_wap, g_wout, g_w1, g_w2, g_wpg, g_wple)]
    (do,), got_a = _mm("d_o", dy_a, wap, tb=True, tm=tm, tn=tn, tk=d, comm=_pair_comm(grads_a))
    parts_a = [_pair_sum("pair_sum_" + nm, g, r, c_arr) for nm, g, r in zip(names[2:], grads_a, got_a)]
    dqt, dkt, dz, _ = _flash_bwd(qt, kt, z, o, do, lse, d, dz)
    dz, d_qn, d_kn = _qk_bwd(dqt, dkt, z, cos, sin, q_norm, k_norm, d, dz)
    (g_wcp,) = _mm("d_wcp", act, dy_c, ta=True, out_cm=True, tm=cw, tn=wcp.shape[2], tk=tkt)
    (dact,) = _mm("d_act", dy_c, wcp, tb=True, b_cm=True, tm=tm, tn=cw, tk=wcp.shape[2])
    p_wap, p_wout, p_w1, p_w2, p_wpg, p_wple = parts_a
    dz, d_taps, d_lng, d_lnb, (s_wap, s_wout, s_wpg, s_wple) = _conv_bwd(
        dact, uc, z, wdw, conv_ln_g, conv_ln_b, cw, dz, comm=_chip_comm([p_wap, p_wout, p_wpg, p_wple]))
    (g_win,), (s_w1, s_w2) = _mm("d_win", h0, dz, ta=True, out_cm=True, tm=tm, tn=win.shape[2] // 3, tk=tkt,
                                 comm=_chip_comm([p_w1, p_w2]))
    slots_a = [s_wap, s_wout, s_w1, s_w2, s_wpg, s_wple]
    grads_b = [slabs(g_win), slabs(g_wcp)]
    got_b = _run_comm("grad_pair_exchange_b", _pair_comm(grads_b))
    parts_b = [_pair_sum("pair_sum_" + nm, g, r, c_arr) for nm, g, r in zip(names[:2], grads_b, got_b)]
    halves_a = [_chip_sum("chip_sum_" + nm, cp, sl, chip_arr, c_arr) for nm, cp, sl in zip(names[2:], parts_a, slots_a)]
    (dh0,), hosted = _mm("d_h0", dz, win, tb=True, b_cm=True, tm=tm, tn=tn, tk=win.shape[2],
                         comm=_merge_comms(_chip_comm(parts_b), _pair_gather_comm(halves_a)))
    slots_b, grads_a_done = hosted[:2], hosted[2:]
    dx, _, d_mix = _rms_bwd("rms_mix_bwd", dh0, xs, norm_mix, dx1)
    halves_b = [_chip_sum("chip_sum_" + nm, cp, sl, chip_arr, c_arr) for nm, cp, sl in zip(names[:2], parts_b, slots_b)]
    big_grads = list(_run_comm("grad_pair_gather_b", _pair_gather_comm(halves_b))) + list(grads_a_done)

    small = [d_mix, d_taps[:CONV_KERNEL], d_lng, d_lnb, d_qn, d_kn, d_ffn, d_ple, d_fin]
    packed, unpack = _pack_small(small)
    g_mix, g_taps, g_lng, g_lnb, g_qn, g_kn, g_ffn, g_ple, g_fin = unpack(_all_sum_small("reduce_small", packed))
    g_dw = lax.dynamic_slice_in_dim(g_taps.reshape(CONV_KERNEL, N_CHIPS, cpc), chip, 1, axis=1).reshape(1, CONV_KERNEL, cpc)

    sq_local = lax.reduce_precision(sq[0, 0], 8, 23)
    loss = (0.5 / d) * lax.psum(sq_local, ("x", "y", "c"))

    grads = {
        "norm_mix": g_mix, "w_in": big_grads[0][None], "w_dw": g_dw, "conv_ln_g": g_lng, "conv_ln_b": g_lnb,
        "w_conv_proj": big_grads[1][None], "q_norm": g_qn, "k_norm": g_kn, "w_attn_proj": big_grads[2][None],
        "w_out": big_grads[3][None], "norm_ffn": g_ffn, "w_ff1": big_grads[4][None], "w_ff2": big_grads[5][None],
        "norm_ple": g_ple, "w_ple_gate": big_grads[6][None], "w_ple_proj": big_grads[7][None],
        "norm_final": g_fin.reshape(d),
    }
    weights = dict(norm_mix=norm_mix, w_in=w_in, w_dw=w_dw, conv_ln_g=conv_ln_g, conv_ln_b=conv_ln_b, w_conv_proj=w_conv_proj,
                   q_norm=q_norm, k_norm=k_norm, w_attn_proj=w_attn_proj, w_out=w_out, norm_ffn=norm_ffn, w_ff1=w_ff1,
                   w_ff2=w_ff2, norm_ple=norm_ple, w_ple_gate=w_ple_gate, w_ple_proj=w_ple_proj, norm_final=norm_final)
    m_in = dict(norm_mix=m_norm_mix, w_in=m_w_in, w_dw=m_w_dw, conv_ln_g=m_conv_ln_g, conv_ln_b=m_conv_ln_b,
                w_conv_proj=m_w_conv_proj, q_norm=m_q_norm, k_norm=m_k_norm, w_attn_proj=m_w_attn_proj, w_out=m_w_out,
                norm_ffn=m_norm_ffn, w_ff1=m_w_ff1, w_ff2=m_w_ff2, norm_ple=m_norm_ple, w_ple_gate=m_w_ple_gate,
                w_ple_proj=m_w_ple_proj, norm_final=m_norm_final)
    v_in = dict(norm_mix=v_norm_mix, w_in=v_w_in, w_dw=v_w_dw, conv_ln_g=v_conv_ln_g, conv_ln_b=v_conv_ln_b,
                w_conv_proj=v_w_conv_proj, q_norm=v_q_norm, k_norm=v_k_norm, w_attn_proj=v_w_attn_proj, w_out=v_w_out,
                norm_ffn=v_norm_ffn, w_ff1=v_w_ff1, w_ff2=v_w_ff2, norm_ple=v_norm_ple, w_ple_gate=v_w_ple_gate,
                w_ple_proj=v_w_ple_proj, norm_final=v_norm_final)
    order = list(weights)
    deltas, new_m, new_v, g_out = [], [], [], []
    for nm in order:
        w = weights[nm]
        shape = w.shape
        two_d = (-1, shape[-1])
        dl, mm_, vv_ = _adamw("adamw_" + nm, w.reshape(two_d), grads[nm].reshape(two_d), m_in[nm].reshape(two_d),
                              v_in[nm].reshape(two_d))
        g_out.append(grads[nm].reshape(shape))
        deltas.append(dl.reshape(shape))
        new_m.append(mm_.reshape(shape))
        new_v.append(vv_.reshape(shape))
    return (loss, dx[None], *g_out, *deltas, *new_m, *new_v)
```
